```python
import math
import jax, jax.numpy as jnp
from jax import lax
import numpy as np

D_MODEL = 2048
BATCH = 4
SEQ = 2048
DEPTH = 1
DEC_BATCH = 32
DEC_SEQ = 8
PAST_LEN = 8192
PAGE_SIZE = 128

N_HEADS = 16
HEAD_DIM = 64
KV_HEADS = 4
NSA_WIDTH = N_HEADS * HEAD_DIM
KV_WIDTH = KV_HEADS * HEAD_DIM
ROT_DIM = HEAD_DIM // 4
ROPE_THETA = 500000.0
L_CMP = 32
STRIDE = 16
CMP_R = L_CMP // STRIDE
CMP_HIDDEN = 128
L_SEL = 64
SEL_RATIO = L_SEL // STRIDE
SEL_INNER = (L_SEL - L_CMP) // STRIDE + 1
TOP_N = 16
WINDOW = 512
WIN_QB = 128
SEL_QB = 32
SGU_WIDTH = D_MODEL // 2
SGU_GROUPS = 8
SGU_GROUP_DIM = SGU_WIDTH // SGU_GROUPS
CHUNK = 128
MIX_WIDTH = NSA_WIDTH + SGU_WIDTH
IN_COLS = NSA_WIDTH + 6 * KV_WIDTH + 3 * N_HEADS + 2 * SGU_WIDTH
N_MEM = 256
MEM_HEADS = 4
MEM_HEAD_DIM = 128
MEM_WIDTH = MEM_HEADS * MEM_HEAD_DIM
N_GROUPS = 4
EXPERTS_PER_GROUP = 4
N_EXPERTS = N_GROUPS * EXPERTS_PER_GROUP
TOP_K_IN_GROUP = 2
EXPERT_FF = 512
EPS = 1e-6
NEG_INF = -1e30
BIG = 1e9
STATE_NAMES = ('cmp_k_p', 'cmp_v_p', 'sel_k_p', 'sel_v_p', 'win_k_p', 'win_v_p', 'mem_k_p', 'mem_v_p', 'cmp_k_s', 'cmp_v_s', 'sel_k_s', 'sel_v_s', 'win_k_s', 'win_v_s', 'sgu_v_s')

kernel_name = 'nsa_gmlp_hybrid_step'


def rmsnorm(x, g):
    xf = x.astype(jnp.float32)
    y = xf * lax.rsqrt(jnp.mean(xf * xf, axis=-1, keepdims=True) + EPS)
    return (y * g.astype(jnp.float32)).astype(x.dtype)


def partial_rope(x, pos):
    half = ROT_DIM // 2
    freqs = ROPE_THETA ** (-jnp.arange(half, dtype=jnp.float32) / half)
    ang = pos.astype(jnp.float32)[:, None] * freqs[None, :]
    cos = jnp.cos(ang)[None, :, None, :]
    sin = jnp.sin(ang)[None, :, None, :]
    xf = x.astype(jnp.float32)
    x1, x2 = xf[..., :half], xf[..., half:ROT_DIM]
    rot = jnp.concatenate([x1 * cos - x2 * sin, x2 * cos + x1 * sin], axis=-1).astype(x.dtype)
    return jnp.concatenate([rot, x[..., ROT_DIM:]], axis=-1)


def masked_softmax(s, mask):
    s = jnp.where(mask, s, NEG_INF)
    m = jnp.max(s, axis=-1, keepdims=True)
    e = jnp.where(mask, jnp.exp(s - m), 0.0)
    return e / jnp.maximum(jnp.sum(e, axis=-1, keepdims=True), 1e-30)


def gqa_attend(q, k, v, mask):
    n, tq, h, d = q.shape
    g = k.shape[2]
    qg = q.reshape(n, tq, g, h // g, d)
    s = jnp.einsum('ntgqd,nkgd->ngqtk', qg, k).astype(jnp.float32) * (d ** -0.5)
    p = masked_softmax(s, mask)
    o = jnp.einsum('ngqtk,nkgd->ntgqd', p.astype(v.dtype), v)
    return o.reshape(n, tq, h, d), p


def compress(raw, pe, w1, w2):
    b, t, g, d = raw.shape
    n_sub = t // STRIDE
    nc = n_sub - CMP_R + 1
    sub = raw[:, :n_sub * STRIDE].reshape(b, n_sub, STRIDE, g, d)
    w1r = w1.reshape(CMP_R, STRIDE, d, CMP_HIDDEN)
    part = jnp.einsum('bnsgd,rsdh->brngh', sub, w1r)
    part = part + jnp.einsum('rsd,rsdh->rh', pe.reshape(CMP_R, STRIDE, d), w1r)[None, :, None, None, :]
    hsum = part[:, 0, 0:nc]
    for r in range(1, CMP_R):
        hsum = hsum + part[:, r, r:r + nc]
    return jax.nn.silu(hsum) @ w2


def cmp_branch(q, kc, vc, q_pos):
    nc = kc.shape[1]
    end = jnp.arange(nc) * STRIDE + (L_CMP - 1)
    mask = (end[None, :] <= q_pos[:, None])[None, None, None]
    return gqa_attend(q, kc, vc, mask)


def select_blocks(p, q_pos, n_keys):
    b, g, qg, tq, nc = p.shape
    ns = -(-n_keys // L_SEL)
    pg = jnp.sum(p, axis=2)
    pg = jnp.pad(pg, ((0, 0), (0, 0), (0, 0), (0, ns * SEL_RATIO - nc)))
    ps = pg.reshape(b, g, tq, ns, SEL_RATIO)[..., :SEL_INNER].sum(-1)
    blk = jnp.arange(ns)[None, :]
    cur = (q_pos // L_SEL)[:, None]
    forced = (blk == 0) | (blk == cur) | (blk == cur - 1)
    future = blk * L_SEL > q_pos[:, None]
    score = jnp.where(forced, BIG, jnp.where(future, -BIG, ps))
    _, idx = lax.top_k(score, min(TOP_N, ns))
    return idx


def to_blocks(k, ns):
    b, t, g, d = k.shape
    k = jnp.pad(k, ((0, 0), (0, ns * L_SEL - t), (0, 0), (0, 0)))
    return k.reshape(b, ns, L_SEL, g, d).transpose(0, 3, 1, 2, 4)


def sel_branch(q_rot, kb, vb, idx, q_pos):
    b, tq, h, d = q_rot.shape
    g = kb.shape[1]
    n = idx.shape[-1]
    bi = jnp.arange(b)[:, None, None, None]
    gi = jnp.arange(g)[None, :, None, None]
    ksel = kb[bi, gi, idx].reshape(b, g, tq, n * L_SEL, d)
    vsel = vb[bi, gi, idx].reshape(b, g, tq, n * L_SEL, d)
    kpos = (idx[..., None] * L_SEL + jnp.arange(L_SEL)).reshape(b, g, tq, n * L_SEL)
    mask = (kpos <= q_pos[None, None, :, None])[:, :, None]
    qg = q_rot.reshape(b, tq, g, h // g, d)
    s = jnp.einsum('btgqd,bgtkd->bgqtk', qg, ksel).astype(jnp.float32) * (d ** -0.5)
    p = masked_softmax(s, mask)
    o = jnp.einsum('bgqtk,bgtkd->btgqd', p.astype(vsel.dtype), vsel)
    return o.reshape(b, tq, h, d)


def sel_attend_blocked(q_rot, kb, vb, idx, q_pos):
    b, tq, h, d = q_rot.shape
    g, n = idx.shape[1], idx.shape[-1]
    qb = SEL_QB if (tq > SEL_QB and tq % SEL_QB == 0) else tq
    nq = tq // qb
    qs = q_rot.reshape(b, nq, qb, h, d).transpose(1, 0, 2, 3, 4)
    ids = idx.reshape(b, g, nq, qb, n).transpose(2, 0, 1, 3, 4)
    pss = q_pos.reshape(nq, qb)
    o = lax.map(lambda a: sel_branch(a[0], kb, vb, a[1], a[2]), (qs, ids, pss))
    return o.transpose(1, 0, 2, 3, 4).reshape(b, tq, h, d)


def window_mask(q_pos, k_pos):
    qp = q_pos[..., :, None]
    kp = k_pos[..., None, :]
    return (kp <= qp) & (kp > qp - WINDOW) & (kp >= 0)


def win_banded(q_rot, kw, vw):
    b, t, h, d = q_rot.shape
    g = kw.shape[2]
    qb = min(WIN_QB, t)
    nb = t // qb
    wk = WINDOW + qb
    pad = ((0, 0), (WINDOW, 0), (0, 0), (0, 0))
    idx = jnp.arange(nb)[:, None] * qb + jnp.arange(wk)[None, :]
    kblk = jnp.pad(kw, pad)[:, idx].reshape(b * nb, wk, g, d)
    vblk = jnp.pad(vw, pad)[:, idx].reshape(b * nb, wk, g, d)
    mask = window_mask(jnp.arange(t).reshape(nb, qb), idx - WINDOW)
    mask = jnp.broadcast_to(mask[None], (b, nb, qb, wk)).reshape(b * nb, 1, 1, qb, wk)
    o, _ = gqa_attend(q_rot.reshape(b * nb, qb, h, d), kblk, vblk, mask)
    return o.reshape(b, t, h, d)


def win_cached(q_rot, buf_k, buf_v, kw, vw, q_pos, past_len):
    wb = buf_k.shape[1]
    k_all = jnp.concatenate([buf_k, kw], axis=1)
    v_all = jnp.concatenate([buf_v, vw], axis=1)
    k_pos = past_len - wb + jnp.arange(wb + kw.shape[1])
    mask = window_mask(q_pos, k_pos)[None, None, None]
    o, _ = gqa_attend(q_rot, k_all, v_all, mask)
    return o


def gather_pages(pool, page_table):
    b, n = page_table.shape
    rows = pool[page_table]
    return rows.reshape(b, n * pool.shape[1], pool.shape[2], pool.shape[3])


def sgu(u, v, w_s, b_s):
    b, t, _ = u.shape
    c = min(t, CHUNK)
    n = t // c
    tri = jnp.tril(jnp.ones((c, c), dtype=w_s.dtype))
    ws = w_s[:, :c, :c] * tri[None]
    vr = v.reshape(b, n, c, SGU_GROUPS, SGU_GROUP_DIM)
    mixed = jnp.einsum('gts,bnsgd->bntgd', ws, vr) + b_s[:, :c].T[None, None, :, :, None]
    return u * mixed.reshape(b, t, SGU_WIDTH)


def project(h, lw, q_pos):
    b, t, _ = h.shape
    z = h @ lw['w_in']
    sizes = (NSA_WIDTH, KV_WIDTH, KV_WIDTH, KV_WIDTH, KV_WIDTH, KV_WIDTH, KV_WIDTH, 3 * N_HEADS, SGU_WIDTH, SGU_WIDTH)
    parts = []
    off = 0
    for s in sizes:
        parts.append(z[..., off:off + s])
        off += s
    q, kc, vc, ks, vs, kw, vw, gt, u, v = parts
    heads = lambda a, nh: a.reshape(b, t, nh, HEAD_DIM)
    q = heads(q, N_HEADS)
    kc, vc, vs, vw = heads(kc, KV_HEADS), heads(vc, KV_HEADS), heads(vs, KV_HEADS), heads(vw, KV_HEADS)
    q_rot = partial_rope(q, q_pos)
    ks = partial_rope(heads(ks, KV_HEADS), q_pos)
    kw = partial_rope(heads(kw, KV_HEADS), q_pos)
    gates = jax.nn.sigmoid(gt.astype(jnp.float32)).reshape(b, t, N_HEADS, 3).astype(h.dtype)
    u = jax.nn.gelu(u)
    v = rmsnorm(jax.nn.gelu(v), lw['g_sgu_v'])
    return q, q_rot, kc, vc, ks, vs, kw, vw, gates, u, v


def nsa_cmp_sel(q, q_rot, kc_all, vc_all, ks_all, vs_all, q_pos, lw):
    kc = compress(kc_all, lw['pe_cmp_k'], lw['w_cmp_k1'], lw['w_cmp_k2'])
    vc = compress(vc_all, lw['pe_cmp_v'], lw['w_cmp_v1'], lw['w_cmp_v2'])
    o_cmp, p_cmp = cmp_branch(q, kc, vc, q_pos)
    n_keys = ks_all.shape[1]
    ns = -(-n_keys // L_SEL)
    idx = select_blocks(p_cmp, q_pos, n_keys)
    o_sel = sel_attend_blocked(q_rot, to_blocks(ks_all, ns), to_blocks(vs_all, ns), idx, q_pos)
    return o_cmp, o_sel


def memory_kv(mem, lw):
    b, m, _ = mem.shape
    mn = rmsnorm(mem, lw['g_mem_src'])
    mk = (mn @ lw['w_mem_k']).reshape(b, m, MEM_HEADS, MEM_HEAD_DIM)
    mv = (mn @ lw['w_mem_v']).reshape(b, m, MEM_HEADS, MEM_HEAD_DIM)
    return mk, mv


def hier_moe(h, w_rg, w_re, w_gate, w_up, w_down):
    b, t, dm = h.shape
    xf = h.reshape(b * t, dm)
    pg = jax.nn.softmax((xf @ w_rg).astype(jnp.float32), axis=-1)
    g_top = jnp.argmax(pg, axis=-1)
    pg_top = jnp.max(pg, axis=-1)
    fl = (xf @ w_re).astype(jnp.float32).reshape(-1, N_GROUPS, EXPERTS_PER_GROUP)
    fl = jnp.take_along_axis(fl, g_top[:, None, None], axis=1)[:, 0]
    pf = jax.nn.softmax(fl, axis=-1)
    top_p, top_i = lax.top_k(pf, TOP_K_IN_GROUP)
    wts = top_p / jnp.sum(top_p, axis=-1, keepdims=True) * pg_top[:, None]
    eid = g_top[:, None] * EXPERTS_PER_GROUP + top_i
    comb = jnp.sum(jax.nn.one_hot(eid, N_EXPERTS, dtype=jnp.float32) * wts[..., None], axis=1)
    act = jax.nn.silu(jnp.einsum('nd,edf->nef', xf, w_gate)) * jnp.einsum('nd,edf->nef', xf, w_up)
    act = act * comb.astype(act.dtype)[..., None]
    return jnp.einsum('nef,efd->nd', act, w_down).reshape(b, t, dm)


def finish_layer(x, o_cmp, o_sel, o_win, gates, u, v, mem_k, mem_v, lw):
    b, t, _ = x.shape
    o_nsa = gates[..., 0:1] * o_cmp + gates[..., 1:2] * o_sel + gates[..., 2:3] * o_win
    o_nsa = rmsnorm(o_nsa.reshape(b, t, NSA_WIDTH), lw['g_nsa_out'])
    o_sgu = rmsnorm(sgu(u, v, lw['w_sgu'], lw['b_sgu']), lw['g_sgu_out'])
    x = x + jnp.concatenate([o_nsa, o_sgu], axis=-1) @ lw['w_out']
    hq = (rmsnorm(x, lw['g_mem_norm']) @ lw['w_mem_q']).reshape(b, t, MEM_HEADS, MEM_HEAD_DIM)
    o_m, _ = gqa_attend(hq, mem_k, mem_v, jnp.ones((1, 1, 1, 1, 1), dtype=bool))
    x = x + o_m.reshape(b, t, MEM_WIDTH) @ lw['w_mem_o']
    x = x + hier_moe(rmsnorm(x, lw['g_moe_norm']), lw['w_router_group'], lw['w_router_expert'], lw['w_exp_gate'], lw['w_exp_up'], lw['w_exp_down'])
    return x


def setup_inputs(seed: int = 0) -> dict:
    key = jax.random.key(seed)
    keys = iter(jax.random.split(key, 64))

    def nrm(shape, scale):
        return scale * jax.random.normal(next(keys), shape, jnp.float32)

    def gain(shape):
        return 1.0 + nrm(shape, 0.02)

    n_pages = PAST_LEN // PAGE_SIZE
    n_used = DEC_BATCH * n_pages
    n_pool = n_used + max(1, n_used // 4)
    wb = min(WINDOW, PAST_LEN)
    pool = (DEPTH, n_pool, PAGE_SIZE, KV_HEADS, HEAD_DIM)
    win = (DEPTH, DEC_BATCH, wb, KV_HEADS, HEAD_DIM)
    memc = (DEPTH, DEC_BATCH, N_MEM, MEM_HEADS, MEM_HEAD_DIM)
    page_table = jax.random.permutation(next(keys), n_pool)[:n_used].reshape(DEC_BATCH, n_pages).astype(jnp.int32)
    return {
        'x_prompt': nrm((BATCH, SEQ, D_MODEL), 1.0),
        'x_sample': nrm((DEC_BATCH, DEC_SEQ, D_MODEL), 1.0),
        'cache_cmp_k': nrm(pool, 1.0),
        'cache_cmp_v': nrm(pool, 1.0),
        'cache_sel_k': nrm(pool, 1.0),
        'cache_sel_v': nrm(pool, 1.0),
        'cache_win_k': nrm(win, 1.0),
        'cache_win_v': nrm(win, 1.0),
        'cache_mem_k': nrm(memc, 1.0),
        'cache_mem_v': nrm(memc, 1.0),
        'page_table': page_table,
        'mem_prompt': nrm((BATCH, N_MEM, D_MODEL), 1.0),
        'w_in': nrm((DEPTH, D_MODEL, IN_COLS), D_MODEL ** -0.5),
        'g_attn_norm': gain((DEPTH, D_MODEL)),
        'pe_cmp_k': nrm((DEPTH, L_CMP, HEAD_DIM), 0.1),
        'w_cmp_k1': nrm((DEPTH, L_CMP * HEAD_DIM, CMP_HIDDEN), (L_CMP * HEAD_DIM) ** -0.5),
        'w_cmp_k2': nrm((DEPTH, CMP_HIDDEN, HEAD_DIM), CMP_HIDDEN ** -0.5),
        'pe_cmp_v': nrm((DEPTH, L_CMP, HEAD_DIM), 0.1),
        'w_cmp_v1': nrm((DEPTH, L_CMP * HEAD_DIM, CMP_HIDDEN), (L_CMP * HEAD_DIM) ** -0.5),
        'w_cmp_v2': nrm((DEPTH, CMP_HIDDEN, HEAD_DIM), CMP_HIDDEN ** -0.5),
        'g_sgu_v': gain((DEPTH, SGU_WIDTH)),
        'w_sgu': nrm((DEPTH, SGU_GROUPS, CHUNK, CHUNK), CHUNK ** -0.5),
        'b_sgu': gain((DEPTH, SGU_GROUPS, CHUNK)),
        'g_nsa_out': gain((DEPTH, NSA_WIDTH)),
        'g_sgu_out': gain((DEPTH, SGU_WIDTH)),
        'w_out': nrm((DEPTH, MIX_WIDTH, D_MODEL), MIX_WIDTH ** -0.5),
        'g_mem_norm': gain((DEPTH, D_MODEL)),
        'g_mem_src': gain((DEPTH, D_MODEL)),
        'w_mem_q': nrm((DEPTH, D_MODEL, MEM_WIDTH), D_MODEL ** -0.5),
        'w_mem_k': nrm((DEPTH, D_MODEL, MEM_WIDTH), D_MODEL ** -0.5),
        'w_mem_v': nrm((DEPTH, D_MODEL, MEM_WIDTH), D_MODEL ** -0.5),
        'w_mem_o': nrm((DEPTH, MEM_WIDTH, D_MODEL), MEM_WIDTH ** -0.5),
        'g_moe_norm': gain((DEPTH, D_MODEL)),
        'w_router_group': nrm((DEPTH, D_MODEL, N_GROUPS), D_MODEL ** -0.5),
        'w_router_expert': nrm((DEPTH, D_MODEL, N_EXPERTS), D_MODEL ** -0.5),
        'w_exp_gate': nrm((DEPTH, N_EXPERTS, D_MODEL, EXPERT_FF), D_MODEL ** -0.5),
        'w_exp_up': nrm((DEPTH, N_EXPERTS, D_MODEL, EXPERT_FF), D_MODEL ** -0.5),
        'w_exp_down': nrm((DEPTH, N_EXPERTS, EXPERT_FF, D_MODEL), EXPERT_FF ** -0.5),
        'g_final': gain((D_MODEL,)),
    }


def reference(x_prompt, x_sample, cache_cmp_k, cache_cmp_v, cache_sel_k, cache_sel_v, cache_win_k, cache_win_v, cache_mem_k, cache_mem_v, page_table, mem_prompt, w_in, g_attn_norm, pe_cmp_k, w_cmp_k1, w_cmp_k2, pe_cmp_v, w_cmp_v1, w_cmp_v2, g_sgu_v, w_sgu, b_sgu, g_nsa_out, g_sgu_out, w_out, g_mem_norm, g_mem_src, w_mem_q, w_mem_k, w_mem_v, w_mem_o, g_moe_norm, w_router_group, w_router_expert, w_exp_gate, w_exp_up, w_exp_down, g_final):
    past_len = page_table.shape[1] * PAGE_SIZE
    pos_p = jnp.arange(x_prompt.shape[1], dtype=jnp.int32)
    pos_s = past_len + jnp.arange(x_sample.shape[1], dtype=jnp.int32)
    xp, xs = x_prompt, x_sample
    out = {name: [] for name in STATE_NAMES}
    for l in range(DEPTH):
        lw = {
            'w_in': w_in[l], 'g_attn_norm': g_attn_norm[l],
            'pe_cmp_k': pe_cmp_k[l], 'w_cmp_k1': w_cmp_k1[l], 'w_cmp_k2': w_cmp_k2[l],
            'pe_cmp_v': pe_cmp_v[l], 'w_cmp_v1': w_cmp_v1[l], 'w_cmp_v2': w_cmp_v2[l],
            'g_sgu_v': g_sgu_v[l], 'w_sgu': w_sgu[l], 'b_sgu': b_sgu[l],
            'g_nsa_out': g_nsa_out[l], 'g_sgu_out': g_sgu_out[l], 'w_out': w_out[l],
            'g_mem_norm': g_mem_norm[l], 'g_mem_src': g_mem_src[l],
            'w_mem_q': w_mem_q[l], 'w_mem_k': w_mem_k[l], 'w_mem_v': w_mem_v[l], 'w_mem_o': w_mem_o[l],
            'g_moe_norm': g_moe_norm[l], 'w_router_group': w_router_group[l], 'w_router_expert': w_router_expert[l],
            'w_exp_gate': w_exp_gate[l], 'w_exp_up': w_exp_up[l], 'w_exp_down': w_exp_down[l],
        }
        hp = rmsnorm(xp, lw['g_attn_norm'])
        q, q_rot, kc, vc, ks, vs, kw, vw, gates, u, v = project(hp, lw, pos_p)
        o_cmp, o_sel = nsa_cmp_sel(q, q_rot, kc, vc, ks, vs, pos_p, lw)
        o_win = win_banded(q_rot, kw, vw)
        mk, mv = memory_kv(mem_prompt, lw)
        xp = finish_layer(xp, o_cmp, o_sel, o_win, gates, u, v, mk, mv, lw)
        wbp = min(WINDOW, xp.shape[1])
        out['cmp_k_p'].append(kc)
        out['cmp_v_p'].append(vc)
        out['sel_k_p'].append(ks)
        out['sel_v_p'].append(vs)
        out['win_k_p'].append(kw[:, -wbp:])
        out['win_v_p'].append(vw[:, -wbp:])
        out['mem_k_p'].append(mk)
        out['mem_v_p'].append(mv)
        hs = rmsnorm(xs, lw['g_attn_norm'])
        q, q_rot, kc, vc, ks, vs, kw, vw, gates, u, v = project(hs, lw, pos_s)
        kc_all = jnp.concatenate([gather_pages(cache_cmp_k[l], page_table), kc], axis=1)
        vc_all = jnp.concatenate([gather_pages(cache_cmp_v[l], page_table), vc], axis=1)
        ks_all = jnp.concatenate([gather_pages(cache_sel_k[l], page_table), ks], axis=1)
        vs_all = jnp.concatenate([gather_pages(cache_sel_v[l], page_table), vs], axis=1)
        o_cmp, o_sel = nsa_cmp_sel(q, q_rot, kc_all, vc_all, ks_all, vs_all, pos_s, lw)
        o_win = win_cached(q_rot, cache_win_k[l], cache_win_v[l], kw, vw, pos_s, past_len)
        xs = finish_layer(xs, o_cmp, o_sel, o_win, gates, u, v, cache_mem_k[l], cache_mem_v[l], lw)
        wb = cache_win_k.shape[2]
        out['cmp_k_s'].append(kc)
        out['cmp_v_s'].append(vc)
        out['sel_k_s'].append(ks)
        out['sel_v_s'].append(vs)
        out['win_k_s'].append(jnp.concatenate([cache_win_k[l], kw], axis=1)[:, -wb:])
        out['win_v_s'].append(jnp.concatenate([cache_win_v[l], vw], axis=1)[:, -wb:])
        out['sgu_v_s'].append(v)
    y_prompt = rmsnorm(xp, g_final)
    y_sample = rmsnorm(xs, g_final)
    return (y_prompt, y_sample, jnp.stack(out['cmp_k_p']), jnp.stack(out['cmp_v_p']), jnp.stack(out['sel_k_p']), jnp.stack(out['sel_v_p']), jnp.stack(out['win_k_p']), jnp.stack(out['win_v_p']), jnp.stack(out['mem_k_p']), jnp.stack(out['mem_v_p']), jnp.stack(out['cmp_k_s']), jnp.stack(out['cmp_v_s']), jnp.stack(out['sel_k_s']), jnp.stack(out['sel_v_s']), jnp.stack(out['win_k_s']), jnp.stack(out['win_v_s']), jnp.stack(out['sgu_v_s']))
```

```python
import functools

import jax
import jax.numpy as jnp
from jax import lax
from jax.experimental import pallas as pl
from jax.experimental.pallas import tpu as pltpu

F32 = jnp.float32
BF16 = jnp.bfloat16

N_HEADS = 16
HEAD_DIM = 64
KV_HEADS = 4
Q_PER_KV = N_HEADS // KV_HEADS
NSA_WIDTH = N_HEADS * HEAD_DIM
KV_WIDTH = KV_HEADS * HEAD_DIM
ROT_DIM = HEAD_DIM // 4
ROPE_THETA = 500000.0
L_CMP = 32
STRIDE = 16
CMP_R = L_CMP // STRIDE
CMP_HIDDEN = 128
L_SEL = 64
SEL_RATIO = L_SEL // STRIDE
SEL_INNER = (L_SEL - L_CMP) // STRIDE + 1
TOP_N = 16
WINDOW = 512
SGU_GROUPS = 8
SGU_GROUP_DIM = 128
CHUNK = 128
MEM_HEADS = 4
MEM_HEAD_DIM = 128
N_GROUPS = 4
EXPERTS_PER_GROUP = 4
N_EXPERTS = N_GROUPS * EXPERTS_PER_GROUP
EXPERT_FF = 512
EPS = 1e-6
NEG_INF = -1e30
BIG = 1e9
PAD_SCORE = -3e38
ATTN_SCALE = HEAD_DIM ** -0.5
MEM_SCALE = MEM_HEAD_DIM ** -0.5

LANES = 128
SUBLANES = 8
V7X_VMEM_LIMIT_BYTES = 60000 * 1024

GATE_COLS = LANES
_SEG = {}
_off = 0
for _name, _w in (("q", NSA_WIDTH), ("kc", KV_WIDTH), ("vc", KV_WIDTH), ("ks", KV_WIDTH), ("vs", KV_WIDTH),
                  ("kw", KV_WIDTH), ("vw", KV_WIDTH), ("gt", GATE_COLS), ("u", 1024), ("v", 1024)):
    _SEG[_name] = (_off, _w)
    _off += _w
IN_COLS_PADDED = _off


def _cparams(sem, vmem=V7X_VMEM_LIMIT_BYTES):
    return pltpu.CompilerParams(dimension_semantics=sem, vmem_limit_bytes=vmem)


def _dot(a, b):
    return jnp.dot(a.astype(BF16), b.astype(BF16), preferred_element_type=F32)


def _dot_nt(a, b):
    return lax.dot_general(a.astype(BF16), b.astype(BF16), (((1,), (1,)), ((), ())), preferred_element_type=F32)


def _dot_tn(a, b):
    return lax.dot_general(a.astype(BF16), b.astype(BF16), (((0,), (0,)), ((), ())), preferred_element_type=F32)


def _rms(x, g):
    return x * lax.rsqrt(jnp.mean(x * x, axis=-1, keepdims=True) + EPS) * g


def _masked_softmax(s, mask, axis=-1):
    s = jnp.where(mask, s, NEG_INF)
    m = jnp.max(s, axis=axis, keepdims=True)
    e = jnp.where(mask, jnp.exp(s - m), 0.0)
    return e * (1.0 / jnp.maximum(jnp.sum(e, axis=axis, keepdims=True), 1e-30))


def _const_spec(shape):
    nd = len(shape)
    return pl.BlockSpec(shape, lambda *_: (0,) * nd, pipeline_mode=pl.Buffered(1))


def _proj_kernel(x_ref, g_ref, w_ref, rc_ref, rs1_ref, rs2_ref, gsv_ref,
                 q_ref, qrot_ref, kc_ref, vc_ref, ks_ref, vs_ref, kw_ref, vw_ref, gates_ref, u_ref, v_ref):
    hb = _rms(x_ref[...], g_ref[...]).astype(BF16)

    def seg(name):
        lo, width = _SEG[name]
        return jnp.dot(hb, w_ref[:, lo:lo + width], preferred_element_type=F32)

    def rope(z):
        rc, rs1, rs2 = rc_ref[...], rs1_ref[...], rs2_ref[...]
        half = ROT_DIM // 2
        outs = []
        for c in range(z.shape[1] // LANES):
            zc = z[:, c * LANES:(c + 1) * LANES]
            outs.append(zc * rc + pltpu.roll(zc, LANES - half, 1) * rs1 + pltpu.roll(zc, half, 1) * rs2)
        return jnp.concatenate(outs, axis=1)

    q = seg("q")
    q_ref[...] = q.astype(BF16)
    qrot_ref[...] = rope(q).astype(BF16)
    kc_ref[...] = seg("kc")
    vc_ref[...] = seg("vc")
    ks_ref[...] = rope(seg("ks"))
    vs_ref[...] = seg("vs")
    kw_ref[...] = rope(seg("kw"))
    vw_ref[...] = seg("vw")
    gates_ref[...] = jax.nn.sigmoid(seg("gt"))
    u_ref[...] = jax.nn.gelu(seg("u"))
    v_ref[...] = _rms(jax.nn.gelu(seg("v")), gsv_ref[...])


def _project(x2d, g_attn, w_in_p, tables, g_sgu_v, tm):
    n, d = x2d.shape
    rc, rs1, rs2 = tables
    tt = rc.shape[0]
    nt = tt // tm
    row = lambda w: pl.BlockSpec((tm, w), lambda i: (i, 0))
    tab = pl.BlockSpec((tm, LANES), lambda i: (i % nt, 0))
    out_shapes = [jax.ShapeDtypeStruct((n, NSA_WIDTH), BF16), jax.ShapeDtypeStruct((n, NSA_WIDTH), BF16)]
    out_shapes += [jax.ShapeDtypeStruct((n, KV_WIDTH), F32)] * 6
    out_shapes += [jax.ShapeDtypeStruct((n, GATE_COLS), F32), jax.ShapeDtypeStruct((n, 1024), F32),
                   jax.ShapeDtypeStruct((n, 1024), F32)]
    out_specs = [row(NSA_WIDTH), row(NSA_WIDTH)] + [row(KV_WIDTH)] * 6 + [row(GATE_COLS), row(1024), row(1024)]
    return pl.pallas_call(
        _proj_kernel,
        grid=(n // tm,),
        in_specs=[row(d), _const_spec((1, d)), _const_spec(w_in_p.shape), tab, tab, tab, _const_spec((1, 1024))],
        out_specs=out_specs,
        out_shape=out_shapes,
        compiler_params=_cparams(("parallel",)),
        name="in_proj",
    )(x2d, g_attn, w_in_p, rc, rs1, rs2, g_sgu_v)


def _rope_tables(pos):
    half = ROT_DIM // 2
    freqs = ROPE_THETA ** (-jnp.arange(half, dtype=F32) / half)
    ang = pos.astype(F32)[:, None] * freqs[None, :]
    cos, sin = jnp.cos(ang), jnp.sin(ang)
    t = pos.shape[0]
    ones = jnp.ones((t, HEAD_DIM - ROT_DIM), F32)
    zeros = jnp.zeros((t, HEAD_DIM - ROT_DIM), F32)
    zh = jnp.zeros((t, half), F32)
    rc = jnp.concatenate([cos, cos, ones], axis=1)
    rs1 = jnp.concatenate([-sin, zh, zeros], axis=1)
    rs2 = jnp.concatenate([zh, sin, zeros], axis=1)
    rep = LANES // HEAD_DIM
    return tuple(jnp.tile(a, (1, rep)) for a in (rc, rs1, rs2))


def _compress_kernel(pt_ref, pool_ref, pe_ref, w1_ref, w2_ref, out_ref, buf, a_ref, sem, *, n_pages, page):
    b = pl.program_id(0)
    nb = pl.num_programs(0)
    n_rows = n_pages * page
    n_sub = n_rows // STRIDE
    slot = b % 2

    n_ct = KV_WIDTH // LANES

    def page_copy(bb, sl, p, c):
        src_tile = pool_ref.at[pt_ref[bb, p], :, pl.ds(c * LANES, LANES)]
        dst = buf.at[sl, c, pl.ds(pl.multiple_of(p * page, page), page), :]
        return pltpu.make_async_copy(src_tile, dst, sem.at[sl])

    def start_all(bb, sl):
        def body(p, carry):
            for c in range(n_ct):
                page_copy(bb, sl, p, c).start()
            return carry
        lax.fori_loop(0, n_pages, body, 0)

    def wait_all(bb, sl):
        def body(p, carry):
            for c in range(n_ct):
                page_copy(bb, sl, p, c).wait()
            return carry
        lax.fori_loop(0, n_pages, body, 0)

    @pl.when(b == 0)
    def _():
        start_all(0, 0)

    @pl.when(b + 1 < nb)
    def _():
        start_all(b + 1, 1 - slot)

    wait_all(b, slot)

    pev =_dot(pe_ref[...], w1_ref[...])
    low = lax.broadcasted_iota(jnp.int32, (n_sub, LANES), 1) < HEAD_DIM
    row = lax.broadcasted_iota(jnp.int32, (n_sub, 4 * HEAD_DIM), 0)
    for c in range(n_ct):
        src = buf.at[slot, c]
        for sp in range(STRIDE // 2):
            x0 = src[pl.ds(2 * sp, n_sub, stride=STRIDE), :]
            x1 = src[pl.ds(2 * sp + 1, n_sub, stride=STRIDE), :]
            a_ref[0, :, sp * LANES:(sp + 1) * LANES] = jnp.where(low, x0, pltpu.roll(x1, HEAD_DIM, 1)).astype(BF16)
            a_ref[1, :, sp * LANES:(sp + 1) * LANES] = jnp.where(low, pltpu.roll(x0, HEAD_DIM, 1), x1).astype(BF16)
        for k in range(2):
            pm = jnp.dot(a_ref[k], w1_ref[...], preferred_element_type=F32)
            part0 = pm[:, :CMP_HIDDEN] + pev[0:1, :CMP_HIDDEN]
            part1 = pm[:, CMP_HIDDEN:] + pev[1:2, CMP_HIDDEN:]
            hsum = part0 + pltpu.roll(part1, n_sub - 1, 0)
            o = _dot(jax.nn.silu(hsum), w2_ref[...])
            out_ref[0, 2 * c + k] = jnp.where(row < n_sub - 1, o, 0.0).astype(BF16)


def _compress(pool, page_table, pe, w1, w2):
    bsz, n_pages = page_table.shape
    page = pool.shape[1]
    n_rows = n_pages * page
    n_sub = n_rows // STRIDE
    kdim = STRIDE * HEAD_DIM
    pe8 = jnp.zeros((SUBLANES, kdim), F32).at[:CMP_R].set(pe.reshape(CMP_R, kdim))
    w1c = w1.reshape(CMP_R, kdim, CMP_HIDDEN).transpose(1, 0, 2).reshape(kdim, CMP_R * CMP_HIDDEN).astype(BF16)
    w2t = jnp.tile(w2, (1, 4)).astype(BF16)
    grid_spec = pltpu.PrefetchScalarGridSpec(
        num_scalar_prefetch=1,
        grid=(bsz,),
        in_specs=[pl.BlockSpec(memory_space=pl.ANY),
                  pl.BlockSpec((SUBLANES, kdim), lambda b, pt: (0, 0)),
                  pl.BlockSpec(w1c.shape, lambda b, pt: (0, 0)),
                  pl.BlockSpec(w2t.shape, lambda b, pt: (0, 0))],
        out_specs=pl.BlockSpec((1, KV_HEADS, n_sub, 4 * HEAD_DIM), lambda b, pt: (b, 0, 0, 0)),
        scratch_shapes=[pltpu.VMEM((2, KV_WIDTH // LANES, n_rows, LANES), F32),
                        pltpu.VMEM((2, n_sub, kdim), BF16),
                        pltpu.SemaphoreType.DMA((2,))],
    )
    return pl.pallas_call(
        functools.partial(_compress_kernel, n_pages=n_pages, page=page),
        grid_spec=grid_spec,
        out_shape=jax.ShapeDtypeStruct((bsz, KV_HEADS, n_sub, 4 * HEAD_DIM), BF16),
        compiler_params=_cparams(("arbitrary",)),
        name="compress",
    )(page_table, pool, pe8, w1c, w2t)


def _head_blockdiag(x, n_heads_in_lanes=Q_PER_KV):
    n = x.shape[0]
    lane_head = lax.broadcasted_iota(jnp.int32, (n, n_heads_in_lanes * HEAD_DIM), 1) // HEAD_DIM
    xf = x.astype(F32)
    return jnp.concatenate([jnp.where(lane_head == qi, xf, 0.0) for qi in range(n_heads_in_lanes)], axis=0).astype(BF16)


def _cmpsel_kernel(q_ref, k_ref, v_ref, st_ref, o_ref, sel_ref, sc_scr, *, pos0, nc, ns, topn, tq):
    i = pl.program_id(2)
    q = q_ref[0]
    n_sub = k_ref.shape[2]
    ns_pad = st_ref.shape[0]
    kbd = _head_blockdiag(k_ref[0, 0])
    vbd = _head_blockdiag(v_ref[0, 0])
    s_all = _dot_nt(q, kbd) * ATTN_SCALE
    qpos = pos0 + i * tq + lax.broadcasted_iota(jnp.int32, (tq, 1), 0)
    kidx = lax.broadcasted_iota(jnp.int32, (1, n_sub), 1)
    mask = (kidx * STRIDE + (L_CMP - 1) <= qpos) & (kidx < nc)
    probs = []
    for qi in range(Q_PER_KV):
        probs.append(_masked_softmax(s_all[:, qi * n_sub:(qi + 1) * n_sub], mask))
    o_ref[0] = _dot(jnp.concatenate(probs, axis=1), vbd)
    pg = probs[0]
    for p in probs[1:]:
        pg = pg + p
    pg_hi = pg.astype(BF16)
    pg_lo = (pg - pg_hi.astype(F32)).astype(BF16)
    st = st_ref[...]
    ps_t = _dot_nt(st, pg_hi) + _dot_nt(st, pg_lo)
    blk = lax.broadcasted_iota(jnp.int32, (ns_pad, tq), 0)
    qpos_t = pos0 + i * tq + lax.broadcasted_iota(jnp.int32, (ns_pad, tq), 1)
    cur = qpos_t // L_SEL
    forced = (blk == 0) | (blk == cur) | (blk == cur - 1)
    future = blk * L_SEL > qpos_t
    sc = jnp.where(forced, BIG, jnp.where(future, -BIG, ps_t))
    sc = jnp.where(blk < ns, sc, PAD_SCORE)
    sc_scr[...] = sc
    sel_ref[0, 0] = jnp.zeros((ns_pad, tq), F32)

    def body(j, c):
        row = sc_scr[pl.ds(j, 1), :]
        ahead = (sc > row) | ((sc == row) & (blk < j))
        rank = jnp.sum(ahead.astype(F32), axis=0, keepdims=True)
        sel_ref[0, 0, pl.ds(j, 1), :] = (rank < topn).astype(F32)
        return c

    lax.fori_loop(0, ns, body, 0)


def _cmp_select(q, kcmp, vcmp, pos0, n_keys):
    bsz, t, _ = q.shape
    n_sub = kcmp.shape[2]
    nc = n_sub - CMP_R + 1
    ns = -(-n_keys // L_SEL)
    ns_pad = -(-ns // SUBLANES) * SUBLANES
    topn = min(TOP_N, ns)
    tq = min(t, 512)
    gw = Q_PER_KV * HEAD_DIM
    cidx = jnp.arange(n_sub)
    st = ((cidx[None, :] // SEL_RATIO == jnp.arange(ns_pad)[:, None]) & (cidx[None, :] % SEL_RATIO < SEL_INNER)
          & (cidx[None, :] < nc)).astype(BF16)
    kern = functools.partial(_cmpsel_kernel, pos0=pos0, nc=nc, ns=ns, topn=topn, tq=tq)
    return pl.pallas_call(
        kern,
        grid=(bsz, KV_HEADS, t // tq),
        in_specs=[pl.BlockSpec((1, tq, gw), lambda b, g, i: (b, i, g)),
                  pl.BlockSpec((1, 1, n_sub, gw), lambda b, g, i: (b, g, 0, 0)),
                  pl.BlockSpec((1, 1, n_sub, gw), lambda b, g, i: (b, g, 0, 0)),
                  pl.BlockSpec((ns_pad, n_sub), lambda b, g, i: (0, 0))],
        out_specs=[pl.BlockSpec((1, tq, gw), lambda b, g, i: (b, i, g)),
                   pl.BlockSpec((1, 1, ns_pad, tq), lambda b, g, i: (b, g, 0, i))],
        out_shape=[jax.ShapeDtypeStruct((bsz, t, NSA_WIDTH), F32),
                   jax.ShapeDtypeStruct((bsz, KV_HEADS, ns_pad, t), F32)],
        scratch_shapes=[pltpu.VMEM((ns_pad, tq), F32)],
        compiler_params=_cparams(("parallel", "parallel", "arbitrary")),
        name="cmp_select",
    )(q, kcmp, vcmp, st)


def _pattn_kernel(*refs, mode, t, tq):
    if mode == "sel":
        q_ref, k_ref, v_ref, r_ref, sel_ref, e_ref, o_ref, kbd, vbd = refs
    else:
        q_ref, k_ref, v_ref, r_ref, o_ref, kbd, vbd = refs
    i = pl.program_id(2)

    @pl.when(i == 0)
    def _():
        lane_head = lax.broadcasted_iota(jnp.int32, (t, Q_PER_KV * HEAD_DIM), 1) // HEAD_DIM
        for src, dst in ((k_ref, kbd), (v_ref, vbd)):
            xt = _dot(src[0], r_ref[0])
            for qi in range(Q_PER_KV):
                dst[qi * t:(qi + 1) * t, :] = jnp.where(lane_head == qi, xt, 0.0).astype(BF16)

    q = q_ref[0]
    qpos = i * tq + lax.broadcasted_iota(jnp.int32, (tq, 1), 0)
    if mode == "sel":
        start, length = 0, t
        kpos = lax.broadcasted_iota(jnp.int32, (1, length), 1)
        chosen = _dot_tn(sel_ref[0, 0], e_ref[...])
        allowed = (chosen > 0.5) & (kpos <= qpos)
    else:
        length = WINDOW + tq
        start = pl.multiple_of(jnp.maximum(i * tq - WINDOW, 0), tq)
        kpos = start + lax.broadcasted_iota(jnp.int32, (1, length), 1)
        allowed = (kpos <= qpos) & (kpos > qpos - WINDOW)
    acc = jnp.zeros((tq, Q_PER_KV * HEAD_DIM), F32)
    for qi in range(Q_PER_KV):
        kb = kbd[pl.ds(qi * t + start, length), :]
        vb = vbd[pl.ds(qi * t + start, length), :]
        p = _masked_softmax(_dot_nt(q, kb) * ATTN_SCALE, allowed)
        acc = acc + _dot(p, vb)
    o_ref[0] = acc


def _head_pick_matrices():
    rows = jnp.arange(KV_WIDTH)
    cols = jnp.arange(Q_PER_KV * HEAD_DIM)
    g = jnp.arange(KV_HEADS)
    m = (rows[None, :, None] // HEAD_DIM == g[:, None, None]) & (rows[None, :, None] % HEAD_DIM == cols[None, None, :] % HEAD_DIM)
    return m.astype(BF16)


def _prompt_attention(qrot, k, v, sel_t=None):
    bsz, t, _ = qrot.shape
    tq = 128
    gw = Q_PER_KV * HEAD_DIM
    mode = "win" if sel_t is None else "sel"
    if mode == "win":
        assert t >= WINDOW + tq and t % tq == 0
    in_specs = [pl.BlockSpec((1, tq, gw), lambda b, g, i: (b, i, g)),
                pl.BlockSpec((1, t, KV_WIDTH), lambda b, g, i: (b, 0, 0)),
                pl.BlockSpec((1, t, KV_WIDTH), lambda b, g, i: (b, 0, 0)),
                pl.BlockSpec((1, KV_WIDTH, gw), lambda b, g, i: (g, 0, 0))]
    args = [qrot, k, v, _head_pick_matrices()]
    if mode == "sel":
        ns_pad = sel_t.shape[2]
        e = (jnp.arange(t)[None, :] // L_SEL == jnp.arange(ns_pad)[:, None]).astype(BF16)
        in_specs += [pl.BlockSpec((1, 1, ns_pad, tq), lambda b, g, i: (b, g, 0, i)),
                     pl.BlockSpec((ns_pad, t), lambda b, g, i: (0, 0))]
        args += [sel_t, e]
    return pl.pallas_call(
        functools.partial(_pattn_kernel, mode=mode, t=t, tq=tq),
        grid=(bsz, KV_HEADS, t // tq),
        in_specs=in_specs,
        out_specs=pl.BlockSpec((1, tq, gw), lambda b, g, i: (b, i, g)),
        out_shape=jax.ShapeDtypeStruct((bsz, t, NSA_WIDTH), F32),
        scratch_shapes=[pltpu.VMEM((Q_PER_KV * t, gw), BF16), pltpu.VMEM((Q_PER_KV * t, gw), BF16)],
        compiler_params=_cparams(("parallel", "parallel", "arbitrary")),
        name="prompt_attn_" + mode,
    )(*args)


def _sattn_kernel(pt_ref, qbd_ref, kpool_ref, vpool_ref, ksn_ref, vsn_ref, selc_ref, wk_ref, wv_ref, kwn_ref, vwn_ref,
                  osel_ref, owin_ref, kbuf, vbuf, m_scr, l_scr, acc_scr, sem, *, n_pages, ppc, page, tdec):
    b = pl.program_id(0)
    c = pl.program_id(1)
    nb = pl.num_programs(0)
    nch = n_pages // ppc
    step = b * nch + c
    slot = step % 2
    rows = ppc * page
    ncol = qbd_ref.shape[2]

    def copies(bb, cc, sl, p):
        dst_k = kbuf.at[sl, pl.ds(pl.multiple_of(p * page, page), page), :]
        dst_v = vbuf.at[sl, pl.ds(pl.multiple_of(p * page, page), page), :]
        pid = pt_ref[bb, cc * ppc + p]
        return (pltpu.make_async_copy(kpool_ref.at[pid], dst_k, sem.at[0, sl]),
                pltpu.make_async_copy(vpool_ref.at[pid], dst_v, sem.at[1, sl]))

    def start_all(bb, cc, sl):
        def body(p, carry):
            ck, cv = copies(bb, cc, sl, p)
            ck.start()
            cv.start()
            return carry
        lax.fori_loop(0, ppc, body, 0)

    def wait_all(bb, cc, sl):
        def body(p, carry):
            ck, cv = copies(bb, cc, sl, p)
            ck.wait()
            cv.wait()
            return carry
        lax.fori_loop(0, ppc, body, 0)

    @pl.when(step == 0)
    def _():
        start_all(0, 0, 0)

    @pl.when(step + 1 < nb * nch)
    def _():
        nxt = step + 1
        start_all(nxt // nch, nxt % nch, 1 - slot)

    wait_all(b, c, slot)

    @pl.when(c == 0)
    def _():
        m_scr[...] = jnp.full(m_scr.shape, NEG_INF, F32)
        l_scr[...] = jnp.zeros(l_scr.shape, F32)
        acc_scr[...] = jnp.zeros(acc_scr.shape, F32)

    qbd = qbd_ref[0]
    nblk = rows // L_SEL
    s3 = (_dot(kbuf[slot], qbd) * ATTN_SCALE).reshape(nblk, L_SEL, ncol)
    blk0 = pl.multiple_of(c * nblk, SUBLANES)
    chosen = (selc_ref[0, pl.ds(blk0, nblk), :] > 0.5)[:, None, :]
    s3 = jnp.where(chosen, s3, NEG_INF)
    m_old = m_scr[...]
    m_new = jnp.maximum(m_old, jnp.max(jnp.max(s3, axis=0), axis=0, keepdims=True))
    e3 = jnp.where(chosen, jnp.exp(s3 - m_new[None]), 0.0)
    alpha = jnp.exp(m_old - m_new)
    l_scr[...] = l_scr[...] * alpha + jnp.sum(jnp.sum(e3, axis=0), axis=0, keepdims=True)
    m_scr[...] = m_new
    acc_scr[...] = acc_scr[...] * alpha + _dot_tn(vbuf[slot], e3.reshape(rows, ncol))

    @pl.when(c == nch - 1)
    def _():
        tcol = lax.broadcasted_iota(jnp.int32, (tdec, ncol), 1) % tdec
        jrow = lax.broadcasted_iota(jnp.int32, (tdec, ncol), 0)
        causal_new = jrow <= tcol
        sel_last = selc_ref[0, pl.ds(nch * nblk, 1), :] > 0.5
        ok_new = causal_new & sel_last
        s_new = jnp.where(ok_new, _dot(ksn_ref[0], qbd) * ATTN_SCALE, NEG_INF)
        m_old2 = m_scr[...]
        m_fin = jnp.maximum(m_old2, jnp.max(s_new, axis=0, keepdims=True))
        e_new = jnp.where(ok_new, jnp.exp(s_new - m_fin), 0.0)
        alpha2 = jnp.exp(m_old2 - m_fin)
        l_fin = l_scr[...] * alpha2 + jnp.sum(e_new, axis=0, keepdims=True)
        inv = 1.0 / jnp.maximum(l_fin, 1e-30)
        acc = acc_scr[...] * alpha2 + _dot_tn(vsn_ref[0], e_new)
        osel_ref[0] = acc * inv

        wb = wk_ref.shape[1]
        jw = lax.broadcasted_iota(jnp.int32, (wb, ncol), 0)
        tw = lax.broadcasted_iota(jnp.int32, (wb, ncol), 1) % tdec
        ok_c = jw + (WINDOW - wb) > tw
        s_c = jnp.where(ok_c, _dot(wk_ref[0], qbd) * ATTN_SCALE, NEG_INF)
        s_n = jnp.where(causal_new, _dot(kwn_ref[0], qbd) * ATTN_SCALE, NEG_INF)
        m_w = jnp.maximum(jnp.max(s_c, axis=0, keepdims=True), jnp.max(s_n, axis=0, keepdims=True))
        e_c = jnp.where(ok_c, jnp.exp(s_c - m_w), 0.0)
        e_n = jnp.where(causal_new, jnp.exp(s_n - m_w), 0.0)
        inv_w = 1.0 / jnp.maximum(jnp.sum(e_c, axis=0, keepdims=True) + jnp.sum(e_n, axis=0, keepdims=True), 1e-30)
        ow = _dot_tn(wv_ref[0], e_c) + _dot_tn(vwn_ref[0], e_n)
        owin_ref[0] = ow * inv_w


def _sample_attention(qrot, kpool, vpool, page_table, ks_new, vs_new, sel_t, wk, wv, kw_new, vw_new):
    bsz, tdec, _ = qrot.shape
    n_pages = page_table.shape[1]
    page = kpool.shape[1]
    ppc = min(n_pages, 32)
    assert n_pages % ppc == 0 and page % L_SEL == 0
    nch = n_pages // ppc
    rows = ppc * page
    ncol = KV_HEADS * Q_PER_KV * tdec
    ns_pad = sel_t.shape[2]
    wb = wk.shape[1]
    q5 = qrot.reshape(bsz, tdec, KV_HEADS, Q_PER_KV, HEAD_DIM).transpose(0, 2, 4, 3, 1)
    eye = jnp.eye(KV_HEADS, dtype=qrot.dtype)
    qbd = (q5[:, :, :, None] * eye[None, :, None, :, None, None]).reshape(bsz, KV_WIDTH, ncol)
    selc = jnp.broadcast_to(sel_t.transpose(0, 2, 1, 3)[:, :, :, None, :], (bsz, ns_pad, KV_HEADS, Q_PER_KV, tdec))
    selc = selc.reshape(bsz, ns_pad, ncol)
    per_b = lambda shape: pl.BlockSpec((1,) + shape, lambda b, c, pt: (b, 0, 0))
    grid_spec = pltpu.PrefetchScalarGridSpec(
        num_scalar_prefetch=1,
        grid=(bsz, nch),
        in_specs=[per_b((KV_WIDTH, ncol)),
                  pl.BlockSpec(memory_space=pl.ANY), pl.BlockSpec(memory_space=pl.ANY),
                  per_b((tdec, KV_WIDTH)), per_b((tdec, KV_WIDTH)),
                  per_b((ns_pad, ncol)),
                  per_b((wb, KV_WIDTH)), per_b((wb, KV_WIDTH)),
                  per_b((tdec, KV_WIDTH)), per_b((tdec, KV_WIDTH))],
        out_specs=[per_b((KV_WIDTH, ncol)), per_b((KV_WIDTH, ncol))],
        scratch_shapes=[pltpu.VMEM((2, rows, KV_WIDTH), F32), pltpu.VMEM((2, rows, KV_WIDTH), F32),
                        pltpu.VMEM((1, ncol), F32), pltpu.VMEM((1, ncol), F32), pltpu.VMEM((KV_WIDTH, ncol), F32),
                        pltpu.SemaphoreType.DMA((2, 2))],
    )
    o_sel, o_win = pl.pallas_call(
        functools.partial(_sattn_kernel, n_pages=n_pages, ppc=ppc, page=page, tdec=tdec),
        grid_spec=grid_spec,
        out_shape=[jax.ShapeDtypeStruct((bsz, KV_WIDTH, ncol), F32)] * 2,
        compiler_params=_cparams(("arbitrary", "arbitrary")),
        name="sample_attn",
    )(page_table, qbd, kpool, vpool, ks_new, vs_new, selc, wk, wv, kw_new, vw_new)

    def unpack(o):
        o6 = o.reshape(bsz, KV_HEADS, HEAD_DIM, KV_HEADS, Q_PER_KV, tdec)
        diag = jnp.stack([o6[:, g, :, g] for g in range(KV_HEADS)], axis=1)
        return diag.transpose(0, 4, 1, 3, 2).reshape(bsz, tdec, NSA_WIDTH)

    return unpack(o_sel), unpack(o_win)


def _mix_kernel(x_ref, oc_ref, os_ref, ow_ref, gt_ref, u_ref, v_ref, wsm_ref, bs_ref, gn_ref, gs_ref, eg_ref, wout_ref,
                o_ref, *, chunk):
    r = x_ref.shape[0]
    g = gt_ref[...]
    g_hi = g.astype(BF16)
    g_lo = (g - g_hi.astype(F32)).astype(BF16)
    onsa = None
    for j, branch in enumerate((oc_ref, os_ref, ow_ref)):
        ge = _dot(g_hi, eg_ref[j]) + _dot(g_lo, eg_ref[j])
        term = ge * branch[...]
        onsa = term if onsa is None else onsa + term
    onsa = _rms(onsa, gn_ref[...])
    ii = lax.broadcasted_iota(jnp.int32, (r, r), 0)
    jj = lax.broadcasted_iota(jnp.int32, (r, r), 1)
    tri = (ii // chunk == jj // chunk) & (jj % chunk <= ii % chunk)
    cols = []
    for gi in range(SGU_GROUPS):
        sl = slice(gi * SGU_GROUP_DIM, (gi + 1) * SGU_GROUP_DIM)
        ws = jnp.where(tri, wsm_ref[gi], 0.0)
        mixed = _dot(ws, v_ref[:, sl]) + bs_ref[:, gi:gi + 1]
        cols.append(u_ref[:, sl] * mixed)
    osgu = _rms(jnp.concatenate(cols, axis=1), gs_ref[...])
    o_ref[...] = x_ref[...] + _dot(jnp.concatenate([onsa, osgu], axis=1), wout_ref[...])


def _gate_expanders():
    c = jnp.arange(GATE_COLS)[None, :, None]
    lane = jnp.arange(NSA_WIDTH)[None, None, :]
    j = jnp.arange(3)[:, None, None]
    return (c == (lane // HEAD_DIM) * 3 + j).astype(BF16)


def _mix(x2d, o_cmp, o_sel, o_win, gates, u, v, w_sgu, b_sgu, g_nsa_out, g_sgu_out, w_out_b, chunk, r):
    n, d = x2d.shape
    rep = r // chunk
    wsm = jnp.tile(w_sgu[:, :chunk, :chunk], (1, rep, rep))
    bs = jnp.tile(b_sgu[:, :chunk].T, (rep, 1))
    row = lambda w: pl.BlockSpec((r, w), lambda i: (i, 0))
    return pl.pallas_call(
        functools.partial(_mix_kernel, chunk=chunk),
        grid=(n // r,),
        in_specs=[row(d), row(NSA_WIDTH), row(NSA_WIDTH), row(NSA_WIDTH), row(GATE_COLS), row(1024), row(1024),
                  _const_spec(wsm.shape), _const_spec(bs.shape), _const_spec((1, NSA_WIDTH)), _const_spec((1, 1024)),
                  _const_spec((3, GATE_COLS, NSA_WIDTH)), _const_spec(w_out_b.shape)],
        out_specs=row(d),
        out_shape=jax.ShapeDtypeStruct((n, d), F32),
        compiler_params=_cparams(("parallel",)),
        name="mix_out_proj",
    )(x2d, o_cmp, o_sel, o_win, gates, u, v, wsm, bs, g_nsa_out, g_sgu_out, _gate_expanders(), w_out_b)


def _rms_matmul_kernel(x_ref, g_ref, w_ref, o_ref):
    o_ref[...] = _dot(_rms(x_ref[...], g_ref[...]), w_ref[...])


def _rms_matmul(x2d, g, w_b, tm):
    n, d = x2d.shape
    m = w_b.shape[1]
    return pl.pallas_call(
        _rms_matmul_kernel,
        grid=(n // tm,),
        in_specs=[pl.BlockSpec((tm, d), lambda i: (i, 0)), _const_spec((1, d)), _const_spec(w_b.shape)],
        out_specs=pl.BlockSpec((tm, m), lambda i: (i, 0)),
        out_shape=jax.ShapeDtypeStruct((n, m), F32),
        compiler_params=_cparams(("parallel",)),
        name="rms_matmul",
    )(x2d, g, w_b)


def _matmul_res_kernel(a_ref, w_ref, r_ref, o_ref):
    o_ref[...] = r_ref[...] + _dot(a_ref[...], w_ref[...])


def _matmul_res(a2d, w_b, res, tm):
    n, k = a2d.shape
    m = w_b.shape[1]
    return pl.pallas_call(
        _matmul_res_kernel,
        grid=(n // tm,),
        in_specs=[pl.BlockSpec((tm, k), lambda i: (i, 0)), _const_spec(w_b.shape), pl.BlockSpec((tm, m), lambda i: (i, 0))],
        out_specs=pl.BlockSpec((tm, m), lambda i: (i, 0)),
        out_shape=jax.ShapeDtypeStruct((n, m), F32),
        compiler_params=_cparams(("parallel",)),
        name="matmul_residual",
    )(a2d, w_b, res)


def _memattn_kernel(q_ref, k_ref, v_ref, o_ref):
    q = q_ref[0]
    k = k_ref[0]
    v = v_ref[0]
    outs = []
    for h in range(MEM_HEADS):
        sl = slice(h * MEM_HEAD_DIM, (h + 1) * MEM_HEAD_DIM)
        s = _dot_nt(q[:, sl], k[:, sl]) * MEM_SCALE
        p = _masked_softmax(s, jnp.ones(s.shape, dtype=jnp.bool_))
        outs.append(_dot(p, v[:, sl]))
    o_ref[0] = jnp.concatenate(outs, axis=1)


def _mem_attention(hq, mk, mv):
    bsz, t, w = hq.shape
    m = mk.shape[1]
    tq = min(t, 512)
    return pl.pallas_call(
        _memattn_kernel,
        grid=(bsz, t // tq),
        in_specs=[pl.BlockSpec((1, tq, w), lambda b, i: (b, i, 0)),
                  pl.BlockSpec((1, m, w), lambda b, i: (b, 0, 0)),
                  pl.BlockSpec((1, m, w), lambda b, i: (b, 0, 0))],
        out_specs=pl.BlockSpec((1, tq, w), lambda b, i: (b, i, 0)),
        out_shape=jax.ShapeDtypeStruct((bsz, t, w), F32),
        compiler_params=_cparams(("parallel", "parallel")),
        name="mem_attn",
    )(hq, mk, mv)


def _router_logits(h, wr_ref):
    return jnp.dot(h, wr_ref[...], precision=lax.Precision.HIGHEST, preferred_element_type=F32)


def _router_kernel(x_ref, g_ref, wr_ref, gtop_ref):
    z = _router_logits(_rms(x_ref[...], g_ref[...]), wr_ref)
    lane = lax.broadcasted_iota(jnp.int32, z.shape, 1)
    zg = jnp.where(lane < N_GROUPS, z, -jnp.inf)
    m = jnp.max(zg, axis=1, keepdims=True)
    first = jnp.min(jnp.where(zg == m, lane.astype(F32), float(LANES)), axis=1, keepdims=True)
    gtop_ref[...] = first.astype(jnp.int32)


def _route_groups(x2d, g_moe, wr, tm):
    n, d = x2d.shape
    return pl.pallas_call(
        _router_kernel,
        grid=(n // tm,),
        in_specs=[pl.BlockSpec((tm, d), lambda i: (i, 0)), _const_spec((1, d)), _const_spec(wr.shape)],
        out_specs=pl.BlockSpec((tm, 1), lambda i: (i, 0)),
        out_shape=jax.ShapeDtypeStruct((n, 1), jnp.int32),
        compiler_params=_cparams(("parallel",)),
        name="route_groups",
    )(x2d, g_moe, wr)


def _moe_kernel(src_ref, tg_ref, tv_ref, widx_ref, x_hbm, gm_ref, gf_ref, wr_ref, wg_ref, wu_ref, wd_ref, y_hbm,
                xbuf, hb_scr, w4_scr, acc_scr, ybuf, gsem, ssem, *, tm):
    t = pl.program_id(0)
    e = pl.program_id(1)
    nt = pl.num_programs(0)
    slot = t % 2
    valid = tv_ref[t] == 1

    def gather_copy(tt, sl, r):
        idx = jnp.maximum(src_ref[tt * tm + r], 0)
        return pltpu.make_async_copy(x_hbm.at[pl.ds(idx, 1), :], xbuf.at[sl, pl.ds(r, 1), :], gsem.at[sl])

    def gather_start(tt, sl):
        def body(r, c):
            gather_copy(tt, sl, r).start()
            return c
        lax.fori_loop(0, tm, body, 0)

    def gather_wait(tt, sl):
        def body(r, c):
            gather_copy(tt, sl, r).wait()
            return c
        lax.fori_loop(0, tm, body, 0)

    @pl.when(e == 0)
    def _():
        @pl.when(t == 0)
        def _():
            gather_start(0, 0)

        @pl.when((t + 1 < nt) & (tv_ref[jnp.minimum(t + 1, nt - 1)] == 1))
        def _():
            gather_start(t + 1, 1 - slot)

        @pl.when(valid)
        def _():
            gather_wait(t, slot)
            h = _rms(xbuf[slot], gm_ref[...])
            hb_scr[...] = h.astype(BF16)
            z = _router_logits(h, wr_ref)
            lane = lax.broadcasted_iota(jnp.int32, z.shape, 1)
            lanef = lane.astype(F32)
            grp = lane < N_GROUPS
            zg = jnp.where(grp, z, -jnp.inf)
            pg_top = 1.0 / jnp.sum(jnp.where(grp, jnp.exp(zg - jnp.max(zg, axis=1, keepdims=True)), 0.0), axis=1, keepdims=True)
            lo = N_GROUPS + tg_ref[t] * EXPERTS_PER_GROUP
            ing = (lane >= lo) & (lane < lo + EXPERTS_PER_GROUP)
            pf = _masked_softmax(z, ing)
            big = float(2 * LANES)
            m1 = jnp.max(jnp.where(ing, pf, -1.0), axis=1, keepdims=True)
            i1 = jnp.min(jnp.where(ing & (pf == m1), lanef, big), axis=1, keepdims=True)
            rest = ing & (lanef != i1)
            m2 = jnp.max(jnp.where(rest, pf, -1.0), axis=1, keepdims=True)
            i2 = jnp.min(jnp.where(rest & (pf == m2), lanef, big), axis=1, keepdims=True)
            tot = m1 + m2
            w4_scr[...] = jnp.where(lanef == i1, m1 / tot * pg_top, jnp.where(lanef == i2, m2 / tot * pg_top, 0.0))
            acc_scr[...] = jnp.zeros(acc_scr.shape, F32)

    @pl.when(valid)
    def _():
        hb = hb_scr[...]
        lane = lax.broadcasted_iota(jnp.int32, w4_scr.shape, 1)
        col = N_GROUPS + tg_ref[t] * EXPERTS_PER_GROUP + e
        we = jnp.sum(jnp.where(lane == col, w4_scr[...], 0.0), axis=1, keepdims=True)
        act = jax.nn.silu(_dot(hb, wg_ref[0])) * _dot(hb, wu_ref[0])
        acc_scr[...] += _dot(act * we, wd_ref[0])

    @pl.when(valid & (e == EXPERTS_PER_GROUP - 1))
    def _():
        ybuf[...] = _rms(xbuf[slot] + acc_scr[...], gf_ref[...])

        def scatter_copy(r):
            idx = src_ref[t * tm + r]
            return idx, pltpu.make_async_copy(ybuf.at[pl.ds(r, 1), :], y_hbm.at[pl.ds(jnp.maximum(idx, 0), 1), :], ssem.at[0])

        def start_body(r, c):
            idx, cp = scatter_copy(r)

            @pl.when(idx >= 0)
            def _():
                cp.start()
            return c

        def wait_body(r, c):
            idx, cp = scatter_copy(r)

            @pl.when(idx >= 0)
            def _():
                cp.wait()
            return c

        lax.fori_loop(0, tm, start_body, 0)
        lax.fori_loop(0, tm, wait_body, 0)


def _moe_final(x2d, g_moe, g_final, wr, w_gate, w_up, w_down, tm):
    n, d = x2d.shape
    g_top = _route_groups(x2d, g_moe, wr, min(n, 512))[:, 0]
    n_tiles = n // tm + N_GROUPS
    onehot = (g_top[:, None] == jnp.arange(N_GROUPS)[None, :]).astype(jnp.int32)
    counts = jnp.sum(onehot, axis=0)
    rank = jnp.sum((jnp.cumsum(onehot, axis=0) - onehot) * onehot, axis=1)
    padded = (counts + tm - 1) // tm * tm
    ends = jnp.cumsum(padded)
    base = ends - padded
    pos = base[g_top] + rank
    src = jnp.full((n_tiles * tm,), -1, jnp.int32).at[pos].set(jnp.arange(n, dtype=jnp.int32))
    tile_start = jnp.arange(n_tiles, dtype=jnp.int32) * tm
    tile_valid = (tile_start < ends[-1]).astype(jnp.int32)
    tile_group = jnp.minimum(jnp.sum((tile_start[:, None] >= ends[None, :]).astype(jnp.int32), axis=1), N_GROUPS - 1)
    n_valid = ends[-1] // tm
    last_group = tile_group[jnp.maximum(n_valid - 1, 0)]
    eidx = tile_group[:, None] * EXPERTS_PER_GROUP + jnp.arange(EXPERTS_PER_GROUP, dtype=jnp.int32)[None, :]
    widx = jnp.where(tile_valid[:, None] == 1, eidx, last_group * EXPERTS_PER_GROUP + EXPERTS_PER_GROUP - 1)
    widx = widx.reshape(-1).astype(jnp.int32)

    wmap = lambda t, e, src, tg, tv, wi: (wi[t * EXPERTS_PER_GROUP + e], 0, 0)
    cmap = lambda t, e, src, tg, tv, wi: (0, 0)
    grid_spec = pltpu.PrefetchScalarGridSpec(
        num_scalar_prefetch=4,
        grid=(n_tiles, EXPERTS_PER_GROUP),
        in_specs=[pl.BlockSpec(memory_space=pl.ANY),
                  pl.BlockSpec((1, d), cmap), pl.BlockSpec((1, d), cmap), pl.BlockSpec(wr.shape, cmap),
                  pl.BlockSpec((1, d, EXPERT_FF), wmap), pl.BlockSpec((1, d, EXPERT_FF), wmap),
                  pl.BlockSpec((1, EXPERT_FF, d), wmap)],
        out_specs=pl.BlockSpec(memory_space=pl.ANY),
        scratch_shapes=[pltpu.VMEM((2, tm, d), F32), pltpu.VMEM((tm, d), BF16), pltpu.VMEM((tm, LANES), F32),
                        pltpu.VMEM((tm, d), F32), pltpu.VMEM((tm, d), F32),
                        pltpu.SemaphoreType.DMA((2,)), pltpu.SemaphoreType.DMA((1,))],
    )
    return pl.pallas_call(
        functools.partial(_moe_kernel, tm=tm),
        grid_spec=grid_spec,
        out_shape=jax.ShapeDtypeStruct((n, d), F32),
        compiler_params=_cparams(("arbitrary", "arbitrary")),
        name="moe_final_norm",
    )(src, tile_group, tile_valid, widx, x2d, g_moe, g_final, wr, w_gate, w_up, w_down)


def _finish(x2d, o_cmp, o_sel, o_win, gates, u, v, mk, mv, bsz, lw, chunk, moe_tm):
    n, d = x2d.shape
    t = n // bsz
    x1 = _mix(x2d, o_cmp, o_sel, o_win, gates, u, v, lw["w_sgu"], lw["b_sgu"], lw["g_nsa_out"], lw["g_sgu_out"],
              lw["w_out"], chunk, min(n, 256))
    hq = _rms_matmul(x1, lw["g_mem_norm"], lw["w_mem_q"], min(n, 512))
    o_m = _mem_attention(hq.reshape(bsz, t, -1), mk, mv)
    x2 = _matmul_res(o_m.reshape(n, -1), lw["w_mem_o"], x1, min(n, 512))
    return _moe_final(x2, lw["g_moe_norm"], lw["g_final"], lw["w_router"], lw["w_exp_gate"], lw["w_exp_up"],
                      lw["w_exp_down"], moe_tm)


def kernel(x_prompt, x_sample, cache_cmp_k, cache_cmp_v, cache_sel_k, cache_sel_v, cache_win_k, cache_win_v, cache_mem_k, cache_mem_v, page_table, mem_prompt, w_in, g_attn_norm, pe_cmp_k, w_cmp_k1, w_cmp_k2, pe_cmp_v, w_cmp_v1, w_cmp_v2, g_sgu_v, w_sgu, b_sgu, g_nsa_out, g_sgu_out, w_out, g_mem_norm, g_mem_src, w_mem_q, w_mem_k, w_mem_v, w_mem_o, g_moe_norm, w_router_group, w_router_expert, w_exp_gate, w_exp_up, w_exp_down, g_final):
    depth = w_in.shape[0]
    assert depth == 1, "single-layer trunk"
    bp, tp, d = x_prompt.shape
    bs, ts, _ = x_sample.shape
    n_pages = page_table.shape[1]
    page = cache_cmp_k.shape[2]
    past = n_pages * page
    assert ts < STRIDE and tp % 256 == 0 and page % STRIDE == 0
    row = lambda a: a[0].reshape(1, -1)

    wi = w_in[0]
    c_q = NSA_WIDTH + 6 * KV_WIDTH
    n_gate = 3 * N_HEADS
    w_in_p = jnp.concatenate([wi[:, :c_q + n_gate], jnp.zeros((d, GATE_COLS - n_gate), F32), wi[:, c_q + n_gate:]],
                             axis=1).astype(BF16)
    wr = jnp.concatenate([w_router_group[0], w_router_expert[0],
                          jnp.zeros((d, LANES - N_GROUPS - N_EXPERTS), F32)], axis=1)
    lw = {
        "w_sgu": w_sgu[0], "b_sgu": b_sgu[0], "g_nsa_out": row(g_nsa_out), "g_sgu_out": row(g_sgu_out),
        "w_out": w_out[0].astype(BF16), "g_mem_norm": row(g_mem_norm), "w_mem_q": w_mem_q[0].astype(BF16),
        "w_mem_o": w_mem_o[0].astype(BF16), "g_moe_norm": row(g_moe_norm), "g_final": g_final.reshape(1, -1),
        "w_router": wr, "w_exp_gate": w_exp_gate[0], "w_exp_up": w_exp_up[0], "w_exp_down": w_exp_down[0],
    }
    g_attn = row(g_attn_norm)
    gsv = row(g_sgu_v)
    kv5 = lambda a, b, t: a.reshape(1, b, t, KV_HEADS, HEAD_DIM)

    np_ = bp * tp
    tabs_p = _rope_tables(jnp.arange(tp, dtype=jnp.int32))
    (q, qrot, kc, vc, ks, vs, kw, vw, gates, u, v) = _project(x_prompt.reshape(np_, d), g_attn, w_in_p, tabs_p, gsv, 256)
    pt_p = jnp.arange(np_ // page, dtype=jnp.int32).reshape(bp, tp // page)
    kcmp = _compress(kc.reshape(np_ // page, page, KV_WIDTH), pt_p, pe_cmp_k[0], w_cmp_k1[0], w_cmp_k2[0])
    vcmp = _compress(vc.reshape(np_ // page, page, KV_WIDTH), pt_p, pe_cmp_v[0], w_cmp_v1[0], w_cmp_v2[0])
    o_cmp, sel_t = _cmp_select(q.reshape(bp, tp, -1), kcmp, vcmp, 0, tp)
    qrot3 = qrot.reshape(bp, tp, -1)
    ks3, vs3, kw3, vw3 = (a.reshape(bp, tp, KV_WIDTH) for a in (ks, vs, kw, vw))
    o_sel = _prompt_attention(qrot3, ks3, vs3, sel_t)
    o_win = _prompt_attention(qrot3, kw3, vw3)
    n_mem = mem_prompt.shape[1]
    w_mem_kv = jnp.concatenate([w_mem_k[0], w_mem_v[0]], axis=1).astype(BF16)
    mkv = _rms_matmul(mem_prompt.reshape(bp * n_mem, d), row(g_mem_src), w_mem_kv, min(bp * n_mem, 512))
    mem_w = MEM_HEADS * MEM_HEAD_DIM
    mk_p = mkv[:, :mem_w].reshape(bp, n_mem, mem_w)
    mv_p = mkv[:, mem_w:].reshape(bp, n_mem, mem_w)
    y_p = _finish(x_prompt.reshape(np_, d), o_cmp.reshape(np_, -1), o_sel.reshape(np_, -1), o_win.reshape(np_, -1),
                  gates, u, v, mk_p, mv_p, bp, lw, CHUNK, min(np_, 512))
    wbp = min(WINDOW, tp)
    outs_p = (kv5(kc, bp, tp), kv5(vc, bp, tp), kv5(ks, bp, tp), kv5(vs, bp, tp),
              kv5(kw, bp, tp)[:, :, -wbp:], kv5(vw, bp, tp)[:, :, -wbp:],
              mk_p.reshape(1, bp, n_mem, MEM_HEADS, MEM_HEAD_DIM), mv_p.reshape(1, bp, n_mem, MEM_HEADS, MEM_HEAD_DIM))

    ns_ = bs * ts
    pos_s = past + jnp.arange(ts, dtype=jnp.int32)
    tabs_s = tuple(jnp.tile(a, (bs, 1)) for a in _rope_tables(pos_s))
    (q, qrot, kc, vc, ks, vs, kw, vw, gates, u, v) = _project(x_sample.reshape(ns_, d), g_attn, w_in_p, tabs_s, gsv, ns_)
    pool = lambda c: c[0].reshape(c.shape[1], page, KV_WIDTH)
    kcmp = _compress(pool(cache_cmp_k), page_table, pe_cmp_k[0], w_cmp_k1[0], w_cmp_k2[0])
    vcmp = _compress(pool(cache_cmp_v), page_table, pe_cmp_v[0], w_cmp_v1[0], w_cmp_v2[0])
    o_cmp, sel_t = _cmp_select(q.reshape(bs, ts, -1).astype(F32), kcmp, vcmp, past, past + ts)
    wb = cache_win_k.shape[2]
    wk = cache_win_k[0].reshape(bs, wb, KV_WIDTH)
    wv = cache_win_v[0].reshape(bs, wb, KV_WIDTH)
    ks3, vs3, kw3, vw3 = (a.reshape(bs, ts, KV_WIDTH) for a in (ks, vs, kw, vw))
    o_sel, o_win = _sample_attention(qrot.reshape(bs, ts, -1), pool(cache_sel_k), pool(cache_sel_v), page_table,
                                     ks3, vs3, sel_t, wk, wv, kw3, vw3)
    mem_ks = cache_mem_k[0].reshape(bs, cache_mem_k.shape[2], mem_w)
    mem_vs = cache_mem_v[0].reshape(bs, cache_mem_v.shape[2], mem_w)
    y_s = _finish(x_sample.reshape(ns_, d), o_cmp.reshape(ns_, -1), o_sel.reshape(ns_, -1), o_win.reshape(ns_, -1),
                  gates, u, v, mem_ks, mem_vs, bs, lw, ts, min(ns_, 128))
    win_k_s = jnp.concatenate([wk, kw3], axis=1)[:, -wb:].reshape(1, bs, wb, KV_HEADS, HEAD_DIM)
    win_v_s = jnp.concatenate([wv, vw3], axis=1)[:, -wb:].reshape(1, bs, wb, KV_HEADS, HEAD_DIM)
    outs_s = (kv5(kc, bs, ts), kv5(vc, bs, ts), kv5(ks, bs, ts), kv5(vs, bs, ts), win_k_s, win_v_s,
              v.reshape(1, bs, ts, -1))

    return (y_p.reshape(bp, tp, d), y_s.reshape(bs, ts, d)) + outs_p + outs_s
```

```python
import functools

import jax
import jax.numpy as jnp
from jax import lax
from jax.experimental import pallas as pl
from jax.experimental.pallas import tpu as pltpu

F32 = jnp.float32
BF16 = jnp.bfloat16

N_HEADS = 16
HEAD_DIM = 64
KV_HEADS = 4
Q_PER_KV = N_HEADS // KV_HEADS
NSA_WIDTH = N_HEADS * HEAD_DIM
KV_WIDTH = KV_HEADS * HEAD_DIM
ROT_DIM = HEAD_DIM // 4
ROPE_THETA = 500000.0
L_CMP = 32
STRIDE = 16
CMP_R = L_CMP // STRIDE
CMP_HIDDEN = 128
L_SEL = 64
SEL_RATIO = L_SEL // STRIDE
SEL_INNER = (L_SEL - L_CMP) // STRIDE + 1
TOP_N = 16
WINDOW = 512
SGU_GROUPS = 8
SGU_GROUP_DIM = 128
CHUNK = 128
MEM_HEADS = 4
MEM_HEAD_DIM = 128
N_GROUPS = 4
EXPERTS_PER_GROUP = 4
N_EXPERTS = N_GROUPS * EXPERTS_PER_GROUP
EXPERT_FF = 512
EPS = 1e-6
NEG_INF = -1e30
BIG = 1e9
PAD_SCORE = -3e38
ATTN_SCALE = HEAD_DIM ** -0.5
MEM_SCALE = MEM_HEAD_DIM ** -0.5

LANES = 128
SUBLANES = 8
V7X_VMEM_LIMIT_BYTES = 60000 * 1024

GATE_COLS = LANES
_SEG = {}
_off = 0
for _name, _w in (("q", NSA_WIDTH), ("kc", KV_WIDTH), ("vc", KV_WIDTH), ("ks", KV_WIDTH), ("vs", KV_WIDTH),
                  ("kw", KV_WIDTH), ("vw", KV_WIDTH), ("gt", GATE_COLS), ("u", 1024), ("v", 1024)):
    _SEG[_name] = (_off, _w)
    _off += _w
IN_COLS_PADDED = _off


def _cparams(sem, vmem=V7X_VMEM_LIMIT_BYTES):
    return pltpu.CompilerParams(dimension_semantics=sem, vmem_limit_bytes=vmem)


def _dot(a, b):
    return jnp.dot(a.astype(BF16), b.astype(BF16), preferred_element_type=F32)


def _dot_nt(a, b):
    return lax.dot_general(a.astype(BF16), b.astype(BF16), (((1,), (1,)), ((), ())), preferred_element_type=F32)


def _dot_tn(a, b):
    return lax.dot_general(a.astype(BF16), b.astype(BF16), (((0,), (0,)), ((), ())), preferred_element_type=F32)


def _rms(x, g):
    return x * lax.rsqrt(jnp.mean(x * x, axis=-1, keepdims=True) + EPS) * g


def _masked_softmax(s, mask, axis=-1):
    s = jnp.where(mask, s, NEG_INF)
    m = jnp.max(s, axis=axis, keepdims=True)
    e = jnp.where(mask, jnp.exp(s - m), 0.0)
    return e * (1.0 / jnp.maximum(jnp.sum(e, axis=axis, keepdims=True), 1e-30))


def _const_spec(shape):
    nd = len(shape)
    return pl.BlockSpec(shape, lambda *_: (0,) * nd, pipeline_mode=pl.Buffered(1))


def _proj_kernel(x_ref, g_ref, w_ref, rc_ref, rs1_ref, rs2_ref, gsv_ref,
                 q_ref, qrot_ref, kc_ref, vc_ref, ks_ref, vs_ref, kw_ref, vw_ref, gates_ref, u_ref, v_ref):
    hb = _rms(x_ref[...], g_ref[...]).astype(BF16)

    def seg(name):
        lo, width = _SEG[name]
        return jnp.dot(hb, w_ref[:, lo:lo + width], preferred_element_type=F32)

    def rope(z):
        rc, rs1, rs2 = rc_ref[...], rs1_ref[...], rs2_ref[...]
        half = ROT_DIM // 2
        outs = []
        for c in range(z.shape[1] // LANES):
            zc = z[:, c * LANES:(c + 1) * LANES]
            outs.append(zc * rc + pltpu.roll(zc, LANES - half, 1) * rs1 + pltpu.roll(zc, half, 1) * rs2)
        return jnp.concatenate(outs, axis=1)

    q = seg("q")
    q_ref[...] = q.astype(BF16)
    qrot_ref[...] = rope(q).astype(BF16)
    kc_ref[...] = seg("kc")
    vc_ref[...] = seg("vc")
    ks_ref[...] = rope(seg("ks"))
    vs_ref[...] = seg("vs")
    kw_ref[...] = rope(seg("kw"))
    vw_ref[...] = seg("vw")
    gates_ref[...] = jax.nn.sigmoid(seg("gt"))
    u_ref[...] = jax.nn.gelu(seg("u"))
    v_ref[...] = _rms(jax.nn.gelu(seg("v")), gsv_ref[...])


def _project(x2d, g_attn, w_in_p, tables, g_sgu_v, tm):
    n, d = x2d.shape
    rc, rs1, rs2 = tables
    tt = rc.shape[0]
    nt = tt // tm
    row = lambda w: pl.BlockSpec((tm, w), lambda i: (i, 0))
    tab = pl.BlockSpec((tm, LANES), lambda i: (i % nt, 0))
    out_shapes = [jax.ShapeDtypeStruct((n, NSA_WIDTH), BF16), jax.ShapeDtypeStruct((n, NSA_WIDTH), BF16)]
    out_shapes += [jax.ShapeDtypeStruct((n, KV_WIDTH), F32)] * 6
    out_shapes += [jax.ShapeDtypeStruct((n, GATE_COLS), F32), jax.ShapeDtypeStruct((n, 1024), F32),
                   jax.ShapeDtypeStruct((n, 1024), F32)]
    out_specs = [row(NSA_WIDTH), row(NSA_WIDTH)] + [row(KV_WIDTH)] * 6 + [row(GATE_COLS), row(1024), row(1024)]
    return pl.pallas_call(
        _proj_kernel,
        grid=(n // tm,),
        in_specs=[row(d), _const_spec((1, d)), _const_spec(w_in_p.shape), tab, tab, tab, _const_spec((1, 1024))],
        out_specs=out_specs,
        out_shape=out_shapes,
        compiler_params=_cparams(("parallel",)),
        name="in_proj",
    )(x2d, g_attn, w_in_p, rc, rs1, rs2, g_sgu_v)


def _rope_tables(pos):
    half = ROT_DIM // 2
    freqs = ROPE_THETA ** (-jnp.arange(half, dtype=F32) / half)
    ang = pos.astype(F32)[:, None] * freqs[None, :]
    cos, sin = jnp.cos(ang), jnp.sin(ang)
    t = pos.shape[0]
    ones = jnp.ones((t, HEAD_DIM - ROT_DIM), F32)
    zeros = jnp.zeros((t, HEAD_DIM - ROT_DIM), F32)
    zh = jnp.zeros((t, half), F32)
    rc = jnp.concatenate([cos, cos, ones], axis=1)
    rs1 = jnp.concatenate([-sin, zh, zeros], axis=1)
    rs2 = jnp.concatenate([zh, sin, zeros], axis=1)
    rep = LANES // HEAD_DIM
    return tuple(jnp.tile(a, (1, rep)) for a in (rc, rs1, rs2))


def _compress_kernel(pt_ref, pool_ref, pe_ref, w1_ref, w2_ref, out_ref, buf, a_ref, sem, *, n_pages, page):
    b = pl.program_id(0)
    nb = pl.num_programs(0)
    n_rows = n_pages * page
    n_sub = n_rows // STRIDE
    slot = b % 2

    n_ct = KV_WIDTH // LANES

    def page_copy(bb, sl, p, c):
        src_tile = pool_ref.at[pt_ref[bb, p], :, pl.ds(c * LANES, LANES)]
        dst = buf.at[sl, c, pl.ds(pl.multiple_of(p * page, page), page), :]
        return pltpu.make_async_copy(src_tile, dst, sem.at[sl])

    def start_all(bb, sl):
        def body(p, carry):
            for c in range(n_ct):
                page_copy(bb, sl, p, c).start()
            return carry
        lax.fori_loop(0, n_pages, body, 0)

    def wait_all(bb, sl):
        def body(p, carry):
            for c in range(n_ct):
                page_copy(bb, sl, p, c).wait()
            return carry
        lax.fori_loop(0, n_pages, body, 0)

    @pl.when(b == 0)
    def _():
        start_all(0, 0)

    @pl.when(b + 1 < nb)
    def _():
        start_all(b + 1, 1 - slot)

    wait_all(b, slot)

    pev =_dot(pe_ref[...], w1_ref[...])
    low = lax.broadcasted_iota(jnp.int32, (n_sub, LANES), 1) < HEAD_DIM
    row = lax.broadcasted_iota(jnp.int32, (n_sub, 4 * HEAD_DIM), 0)
    for c in range(n_ct):
        src = buf.at[slot, c]
        for sp in range(STRIDE // 2):
            x0 = src[pl.ds(2 * sp, n_sub, stride=STRIDE), :]
            x1 = src[pl.ds(2 * sp + 1, n_sub, stride=STRIDE), :]
            a_ref[0, :, sp * LANES:(sp + 1) * LANES] = jnp.where(low, x0, pltpu.roll(x1, HEAD_DIM, 1)).astype(BF16)
            a_ref[1, :, sp * LANES:(sp + 1) * LANES] = jnp.where(low, pltpu.roll(x0, HEAD_DIM, 1), x1).astype(BF16)
        for k in range(2):
            pm = jnp.dot(a_ref[k], w1_ref[...], preferred_element_type=F32)
            part0 = pm[:, :CMP_HIDDEN] + pev[0:1, :CMP_HIDDEN]
            part1 = pm[:, CMP_HIDDEN:] + pev[1:2, CMP_HIDDEN:]
            hsum = part0 + pltpu.roll(part1, n_sub - 1, 0)
            o = _dot(jax.nn.silu(hsum), w2_ref[...])
            out_ref[0, 2 * c + k] = jnp.where(row < n_sub - 1, o, 0.0).astype(BF16)


def _compress(pool, page_table, pe, w1, w2):
    bsz, n_pages = page_table.shape
    page = pool.shape[1]
    n_rows = n_pages * page
    n_sub = n_rows // STRIDE
    kdim = STRIDE * HEAD_DIM
    pe8 = jnp.zeros((SUBLANES, kdim), F32).at[:CMP_R].set(pe.reshape(CMP_R, kdim))
    w1c = w1.reshape(CMP_R, kdim, CMP_HIDDEN).transpose(1, 0, 2).reshape(kdim, CMP_R * CMP_HIDDEN).astype(BF16)
    w2t = jnp.tile(w2, (1, 4)).astype(BF16)
    grid_spec = pltpu.PrefetchScalarGridSpec(
        num_scalar_prefetch=1,
        grid=(bsz,),
        in_specs=[pl.BlockSpec(memory_space=pl.ANY),
                  pl.BlockSpec((SUBLANES, kdim), lambda b, pt: (0, 0)),
                  pl.BlockSpec(w1c.shape, lambda b, pt: (0, 0)),
                  pl.BlockSpec(w2t.shape, lambda b, pt: (0, 0))],
        out_specs=pl.BlockSpec((1, KV_HEADS, n_sub, 4 * HEAD_DIM), lambda b, pt: (b, 0, 0, 0)),
        scratch_shapes=[pltpu.VMEM((2, KV_WIDTH // LANES, n_rows, LANES), F32),
                        pltpu.VMEM((2, n_sub, kdim), BF16),
                        pltpu.SemaphoreType.DMA((2,))],
    )
    return pl.pallas_call(
        functools.partial(_compress_kernel, n_pages=n_pages, page=page),
        grid_spec=grid_spec,
        out_shape=jax.ShapeDtypeStruct((bsz, KV_HEADS, n_sub, 4 * HEAD_DIM), BF16),
        compiler_params=_cparams(("arbitrary",)),
        name="compress",
    )(page_table, pool, pe8, w1c, w2t)


def _head_blockdiag(x, n_heads_in_lanes=Q_PER_KV):
    n = x.shape[0]
    lane_head = lax.broadcasted_iota(jnp.int32, (n, n_heads_in_lanes * HEAD_DIM), 1) // HEAD_DIM
    xf = x.astype(F32)
    return jnp.concatenate([jnp.where(lane_head == qi, xf, 0.0) for qi in range(n_heads_in_lanes)], axis=0).astype(BF16)


def _cmpsel_kernel(q_ref, k_ref, v_ref, st_ref, o_ref, sc_ref, *, pos0, nc, ns, tq):
    i = pl.program_id(2)
    q = q_ref[0]
    n_sub = k_ref.shape[2]
    ns_pad = st_ref.shape[0]
    kbd = _head_blockdiag(k_ref[0, 0])
    vbd = _head_blockdiag(v_ref[0, 0])
    s_all = _dot_nt(q, kbd) * ATTN_SCALE
    qpos = pos0 + i * tq + lax.broadcasted_iota(jnp.int32, (tq, 1), 0)
    kidx = lax.broadcasted_iota(jnp.int32, (1, n_sub), 1)
    mask = (kidx * STRIDE + (L_CMP - 1) <= qpos) & (kidx < nc)
    probs = []
    for qi in range(Q_PER_KV):
        probs.append(_masked_softmax(s_all[:, qi * n_sub:(qi + 1) * n_sub], mask))
    o_ref[0] = _dot(jnp.concatenate(probs, axis=1), vbd)
    pg = probs[0]
    for p in probs[1:]:
        pg = pg + p
    pg_hi = pg.astype(BF16)
    pg_lo = (pg - pg_hi.astype(F32)).astype(BF16)
    st = st_ref[...]
    ps_t = _dot_nt(st, pg_hi) + _dot_nt(st, pg_lo)
    blk = lax.broadcasted_iota(jnp.int32, (ns_pad, tq), 0)
    qpos_t = pos0 + i * tq + lax.broadcasted_iota(jnp.int32, (ns_pad, tq), 1)
    cur = qpos_t // L_SEL
    forced = (blk == 0) | (blk == cur) | (blk == cur - 1)
    future = blk * L_SEL > qpos_t
    sc = jnp.where(forced, BIG, jnp.where(future, -BIG, ps_t))
    sc_ref[0, 0] = jnp.where(blk < ns, sc, PAD_SCORE)


def _rank_kernel(sc_ref, sel_ref, *, ns, topn):
    sc = sc_ref[0, 0]
    blk = lax.broadcasted_iota(jnp.int32, sc.shape, 0)
    sel_ref[0, 0] = jnp.zeros(sc.shape, F32)

    def body(j, c):
        row = sc_ref[0, 0, pl.ds(j, 1), :]
        ahead = (sc > row) | ((sc == row) & (blk < j))
        rank = jnp.sum(ahead.astype(F32), axis=0, keepdims=True)
        sel_ref[0, 0, pl.ds(j, 1), :] = (rank < topn).astype(F32)
        return c

    lax.fori_loop(0, ns, body, 0)


def _cmp_select(q, kcmp, vcmp, pos0, n_keys):
    bsz, t, _ = q.shape
    n_sub = kcmp.shape[2]
    nc = n_sub - CMP_R + 1
    ns = -(-n_keys // L_SEL)
    ns_pad = -(-ns // SUBLANES) * SUBLANES
    topn = min(TOP_N, ns)
    tq = min(t, 512)
    gw = Q_PER_KV * HEAD_DIM
    cidx = jnp.arange(n_sub)
    st = ((cidx[None, :] // SEL_RATIO == jnp.arange(ns_pad)[:, None]) & (cidx[None, :] % SEL_RATIO < SEL_INNER)
          & (cidx[None, :] < nc)).astype(BF16)
    o_cmp, sc = pl.pallas_call(
        functools.partial(_cmpsel_kernel, pos0=pos0, nc=nc, ns=ns, tq=tq),
        grid=(bsz, KV_HEADS, t // tq),
        in_specs=[pl.BlockSpec((1, tq, gw), lambda b, g, i: (b, i, g)),
                  pl.BlockSpec((1, 1, n_sub, gw), lambda b, g, i: (b, g, 0, 0)),
                  pl.BlockSpec((1, 1, n_sub, gw), lambda b, g, i: (b, g, 0, 0)),
                  pl.BlockSpec((ns_pad, n_sub), lambda b, g, i: (0, 0))],
        out_specs=[pl.BlockSpec((1, tq, gw), lambda b, g, i: (b, i, g)),
                   pl.BlockSpec((1, 1, ns_pad, tq), lambda b, g, i: (b, g, 0, i))],
        out_shape=[jax.ShapeDtypeStruct((bsz, t, NSA_WIDTH), F32),
                   jax.ShapeDtypeStruct((bsz, KV_HEADS, ns_pad, t), F32)],
        compiler_params=_cparams(("parallel", "parallel", "parallel")),
        name="cmp_scores",
    )(q, kcmp, vcmp, st)
    fold = t < LANES
    if fold:
        sc = sc.transpose(2, 0, 1, 3).reshape(1, 1, ns_pad, bsz * KV_HEADS * t)
    nb, ng, _, width = sc.shape
    tl = min(width, 512)
    sel = pl.pallas_call(
        functools.partial(_rank_kernel, ns=ns, topn=topn),
        grid=(nb, ng, width // tl),
        in_specs=[pl.BlockSpec((1, 1, ns_pad, tl), lambda b, g, i: (b, g, 0, i))],
        out_specs=pl.BlockSpec((1, 1, ns_pad, tl), lambda b, g, i: (b, g, 0, i)),
        out_shape=jax.ShapeDtypeStruct(sc.shape, F32),
        compiler_params=_cparams(("parallel", "parallel", "parallel")),
        name="rank_select",
    )(sc)
    if fold:
        sel = sel.reshape(ns_pad, bsz, KV_HEADS, t).transpose(1, 2, 0, 3)
    return o_cmp, sel


def _pattn_kernel(*refs, mode, t, tq, ck):
    if mode == "sel":
        q_ref, k_ref, v_ref, rk_ref, rv_ref, sel_ref, et_ref, o_ref, kc_scr, vt_scr, m_scr, l_scr, acc_scr = refs
    else:
        q_ref, k_ref, v_ref, rk_ref, rv_ref, o_ref, kc_scr, vt_scr, m_scr, l_scr, acc_scr = refs
    i = pl.program_id(2)
    n_chunks = t // ck
    last = (i + 1) * (tq // ck) - 1

    @pl.when(i == 0)
    def _():
        kg = _dot(k_ref[0], rk_ref[0]).astype(BF16)
        vg = _dot_nt(rv_ref[0], v_ref[0]).astype(BF16)
        for c in range(n_chunks):
            kc_scr[c] = kg[c * ck:(c + 1) * ck, :]
            vt_scr[c] = vg[:, c * ck:(c + 1) * ck]

    qf = q_ref[0].astype(F32) * ATTN_SCALE
    q_heads = []
    for qi in range(Q_PER_KV):
        tile = qf[:, (qi // 2) * LANES:(qi // 2 + 1) * LANES]
        q_heads.append((tile if qi % 2 == 0 else pltpu.roll(tile, HEAD_DIM, 1)).astype(BF16))
    qpos = i * tq + lax.broadcasted_iota(jnp.int32, (1, tq), 1)
    m_scr[...] = jnp.full(m_scr.shape, NEG_INF, F32)
    l_scr[...] = jnp.zeros(l_scr.shape, F32)
    acc_scr[...] = jnp.zeros(acc_scr.shape, F32)
    if mode == "sel":
        sel_b = sel_ref[0, 0].astype(BF16)
        n_steps = last + 1
    else:
        n_steps = last - jnp.maximum(i * tq - WINDOW, 0) // ck + 1

    def body(step, carry):
        c = last - step
        kpos = c * ck + lax.broadcasted_iota(jnp.int32, (ck, 1), 0)
        if mode == "sel":
            chosen = jnp.dot(et_ref[c], sel_b, preferred_element_type=F32)
            ok = (chosen > 0.5) & (kpos <= qpos)
        else:
            ok = (kpos <= qpos) & (kpos > qpos - WINDOW)
        bias = jnp.where(ok, 0.0, NEG_INF)
        kc = kc_scr[c]
        vt = vt_scr[c]
        for qi in range(Q_PER_KV):
            s = _dot_nt(kc, q_heads[qi]) + bias
            m_old = m_scr[qi]
            m_new = jnp.maximum(m_old, jnp.max(s, axis=0, keepdims=True))
            e = jnp.exp(s - m_new)
            alpha = jnp.exp(m_old - m_new)
            l_scr[qi] = l_scr[qi] * alpha + jnp.sum(e, axis=0, keepdims=True)
            m_scr[qi] = m_new
            acc_scr[qi] = acc_scr[qi] * alpha + jnp.dot(vt, e.astype(BF16), preferred_element_type=F32)
        return carry

    lax.fori_loop(0, n_steps, body, 0)
    outs = [acc_scr[qi] * (1.0 / jnp.maximum(l_scr[qi], 1e-30)) for qi in range(Q_PER_KV)]
    o_ref[0] = jnp.concatenate(outs, axis=0).T


def _head_pick_matrices():
    rows = jnp.arange(KV_WIDTH)[None, :, None]
    cols = jnp.arange(LANES)[None, None, :]
    g = jnp.arange(KV_HEADS)[:, None, None]
    rk = ((rows // HEAD_DIM == g) & (rows % HEAD_DIM == cols)).astype(BF16)
    rv = rk[:, :, :HEAD_DIM].transpose(0, 2, 1)
    return rk, rv


def _prompt_attention(qrot, k, v, sel_t=None):
    bsz, t, _ = qrot.shape
    tq, ck = 256, 256
    gw = Q_PER_KV * HEAD_DIM
    mode = "win" if sel_t is None else "sel"
    assert t % tq == 0 and tq % ck == 0 and WINDOW % ck == 0 and ck % L_SEL == 0
    rk, rv = _head_pick_matrices()
    in_specs = [pl.BlockSpec((1, tq, gw), lambda b, g, i: (b, i, g)),
                pl.BlockSpec((1, t, KV_WIDTH), lambda b, g, i: (b, 0, 0)),
                pl.BlockSpec((1, t, KV_WIDTH), lambda b, g, i: (b, 0, 0)),
                pl.BlockSpec((1, KV_WIDTH, LANES), lambda b, g, i: (g, 0, 0)),
                pl.BlockSpec((1, HEAD_DIM, KV_WIDTH), lambda b, g, i: (g, 0, 0))]
    args = [qrot, k, v, rk, rv]
    if mode == "sel":
        ns_pad = sel_t.shape[2]
        et = (jnp.arange(t)[:, None] // L_SEL == jnp.arange(ns_pad)[None, :]).astype(BF16).reshape(t // ck, ck, ns_pad)
        in_specs += [pl.BlockSpec((1, 1, ns_pad, tq), lambda b, g, i: (b, g, 0, i)),
                     pl.BlockSpec((t // ck, ck, ns_pad), lambda b, g, i: (0, 0, 0))]
        args += [sel_t, et]
    return pl.pallas_call(
        functools.partial(_pattn_kernel, mode=mode, t=t, tq=tq, ck=ck),
        grid=(bsz, KV_HEADS, t // tq),
        in_specs=in_specs,
        out_specs=pl.BlockSpec((1, tq, gw), lambda b, g, i: (b, i, g)),
        out_shape=jax.ShapeDtypeStruct((bsz, t, NSA_WIDTH), F32),
        scratch_shapes=[pltpu.VMEM((t // ck, ck, LANES), BF16), pltpu.VMEM((t // ck, HEAD_DIM, ck), BF16),
                        pltpu.VMEM((Q_PER_KV, 1, tq), F32), pltpu.VMEM((Q_PER_KV, 1, tq), F32),
                        pltpu.VMEM((Q_PER_KV, HEAD_DIM, tq), F32)],
        compiler_params=_cparams(("parallel", "parallel", "arbitrary")),
        name="prompt_attn_" + mode,
    )(*args)


def _sattn_kernel(pt_ref, qbd_ref, kpool_ref, vpool_ref, ksn_ref, vsn_ref, selc_ref, wk_ref, wv_ref, kwn_ref, vwn_ref,
                  osel_ref, owin_ref, kbuf, vbuf, m_scr, l_scr, acc_scr, sem, *, n_pages, ppc, page, tdec):
    b = pl.program_id(0)
    c = pl.program_id(1)
    nb = pl.num_programs(0)
    nch = n_pages // ppc
    step = b * nch + c
    slot = step % 2
    rows = ppc * page
    ncol = qbd_ref.shape[2]

    def copies(bb, cc, sl, p):
        dst_k = kbuf.at[sl, pl.ds(pl.multiple_of(p * page, page), page), :]
        dst_v = vbuf.at[sl, pl.ds(pl.multiple_of(p * page, page), page), :]
        pid = pt_ref[bb, cc * ppc + p]
        return (pltpu.make_async_copy(kpool_ref.at[pid], dst_k, sem.at[0, sl]),
                pltpu.make_async_copy(vpool_ref.at[pid], dst_v, sem.at[1, sl]))

    def start_all(bb, cc, sl):
        def body(p, carry):
            ck, cv = copies(bb, cc, sl, p)
            ck.start()
            cv.start()
            return carry
        lax.fori_loop(0, ppc, body, 0)

    def wait_all(bb, cc, sl):
        def body(p, carry):
            ck, cv = copies(bb, cc, sl, p)
            ck.wait()
            cv.wait()
            return carry
        lax.fori_loop(0, ppc, body, 0)

    @pl.when(step == 0)
    def _():
        start_all(0, 0, 0)

    @pl.when(step + 1 < nb * nch)
    def _():
        nxt = step + 1
        start_all(nxt // nch, nxt % nch, 1 - slot)

    wait_all(b, c, slot)

    @pl.when(c == 0)
    def _():
        m_scr[...] = jnp.full(m_scr.shape, NEG_INF, F32)
        l_scr[...] = jnp.zeros(l_scr.shape, F32)
        acc_scr[...] = jnp.zeros(acc_scr.shape, F32)

    qbd = qbd_ref[0]
    nblk = rows // L_SEL
    s3 = (_dot(kbuf[slot], qbd) * ATTN_SCALE).reshape(nblk, L_SEL, ncol)
    blk0 = pl.multiple_of(c * nblk, SUBLANES)
    chosen = (selc_ref[0, pl.ds(blk0, nblk), :] > 0.5)[:, None, :]
    s3 = jnp.where(chosen, s3, NEG_INF)
    m_old = m_scr[...]
    m_new = jnp.maximum(m_old, jnp.max(jnp.max(s3, axis=0), axis=0, keepdims=True))
    e3 = jnp.where(chosen, jnp.exp(s3 - m_new[None]), 0.0)
    alpha = jnp.exp(m_old - m_new)
    l_scr[...] = l_scr[...] * alpha + jnp.sum(jnp.sum(e3, axis=0), axis=0, keepdims=True)
    m_scr[...] = m_new
    acc_scr[...] = acc_scr[...] * alpha + _dot_tn(vbuf[slot], e3.reshape(rows, ncol))

    @pl.when(c == nch - 1)
    def _():
        tcol = lax.broadcasted_iota(jnp.int32, (tdec, ncol), 1) % tdec
        jrow = lax.broadcasted_iota(jnp.int32, (tdec, ncol), 0)
        causal_new = jrow <= tcol
        sel_last = selc_ref[0, pl.ds(nch * nblk, 1), :] > 0.5
        ok_new = causal_new & sel_last
        s_new = jnp.where(ok_new, _dot(ksn_ref[0], qbd) * ATTN_SCALE, NEG_INF)
        m_old2 = m_scr[...]
        m_fin = jnp.maximum(m_old2, jnp.max(s_new, axis=0, keepdims=True))
        e_new = jnp.where(ok_new, jnp.exp(s_new - m_fin), 0.0)
        alpha2 = jnp.exp(m_old2 - m_fin)
        l_fin = l_scr[...] * alpha2 + jnp.sum(e_new, axis=0, keepdims=True)
        inv = 1.0 / jnp.maximum(l_fin, 1e-30)
        acc = acc_scr[...] * alpha2 + _dot_tn(vsn_ref[0], e_new)
        osel_ref[0] = acc * inv

        wb = wk_ref.shape[1]
        jw = lax.broadcasted_iota(jnp.int32, (wb, ncol), 0)
        tw = lax.broadcasted_iota(jnp.int32, (wb, ncol), 1) % tdec
        ok_c = jw + (WINDOW - wb) > tw
        s_c = jnp.where(ok_c, _dot(wk_ref[0], qbd) * ATTN_SCALE, NEG_INF)
        s_n = jnp.where(causal_new, _dot(kwn_ref[0], qbd) * ATTN_SCALE, NEG_INF)
        m_w = jnp.maximum(jnp.max(s_c, axis=0, keepdims=True), jnp.max(s_n, axis=0, keepdims=True))
        e_c = jnp.where(ok_c, jnp.exp(s_c - m_w), 0.0)
        e_n = jnp.where(causal_new, jnp.exp(s_n - m_w), 0.0)
        inv_w = 1.0 / jnp.maximum(jnp.sum(e_c, axis=0, keepdims=True) + jnp.sum(e_n, axis=0, keepdims=True), 1e-30)
        ow = _dot_tn(wv_ref[0], e_c) + _dot_tn(vwn_ref[0], e_n)
        owin_ref[0] = ow * inv_w


def _sample_attention(qrot, kpool, vpool, page_table, ks_new, vs_new, sel_t, wk, wv, kw_new, vw_new):
    bsz, tdec, _ = qrot.shape
    n_pages = page_table.shape[1]
    page = kpool.shape[1]
    ppc = min(n_pages, 32)
    assert n_pages % ppc == 0 and page % L_SEL == 0
    nch = n_pages // ppc
    rows = ppc * page
    ncol = KV_HEADS * Q_PER_KV * tdec
    ns_pad = sel_t.shape[2]
    wb = wk.shape[1]
    q5 = qrot.reshape(bsz, tdec, KV_HEADS, Q_PER_KV, HEAD_DIM).transpose(0, 2, 4, 3, 1)
    eye = jnp.eye(KV_HEADS, dtype=qrot.dtype)
    qbd = (q5[:, :, :, None] * eye[None, :, None, :, None, None]).reshape(bsz, KV_WIDTH, ncol)
    selc = jnp.broadcast_to(sel_t.transpose(0, 2, 1, 3)[:, :, :, None, :], (bsz, ns_pad, KV_HEADS, Q_PER_KV, tdec))
    selc = selc.reshape(bsz, ns_pad, ncol)
    per_b = lambda shape: pl.BlockSpec((1,) + shape, lambda b, c, pt: (b, 0, 0))
    grid_spec = pltpu.PrefetchScalarGridSpec(
        num_scalar_prefetch=1,
        grid=(bsz, nch),
        in_specs=[per_b((KV_WIDTH, ncol)),
                  pl.BlockSpec(memory_space=pl.ANY), pl.BlockSpec(memory_space=pl.ANY),
                  per_b((tdec, KV_WIDTH)), per_b((tdec, KV_WIDTH)),
                  per_b((ns_pad, ncol)),
                  per_b((wb, KV_WIDTH)), per_b((wb, KV_WIDTH)),
                  per_b((tdec, KV_WIDTH)), per_b((tdec, KV_WIDTH))],
        out_specs=[per_b((KV_WIDTH, ncol)), per_b((KV_WIDTH, ncol))],
        scratch_shapes=[pltpu.VMEM((2, rows, KV_WIDTH), F32), pltpu.VMEM((2, rows, KV_WIDTH), F32),
                        pltpu.VMEM((1, ncol), F32), pltpu.VMEM((1, ncol), F32), pltpu.VMEM((KV_WIDTH, ncol), F32),
                        pltpu.SemaphoreType.DMA((2, 2))],
    )
    o_sel, o_win = pl.pallas_call(
        functools.partial(_sattn_kernel, n_pages=n_pages, ppc=ppc, page=page, tdec=tdec),
        grid_spec=grid_spec,
        out_shape=[jax.ShapeDtypeStruct((bsz, KV_WIDTH, ncol), F32)] * 2,
        compiler_params=_cparams(("arbitrary", "arbitrary")),
        name="sample_attn",
    )(page_table, qbd, kpool, vpool, ks_new, vs_new, selc, wk, wv, kw_new, vw_new)

    def unpack(o):
        o6 = o.reshape(bsz, KV_HEADS, HEAD_DIM, KV_HEADS, Q_PER_KV, tdec)
        diag = jnp.stack([o6[:, g, :, g] for g in range(KV_HEADS)], axis=1)
        return diag.transpose(0, 4, 1, 3, 2).reshape(bsz, tdec, NSA_WIDTH)

    return unpack(o_sel), unpack(o_win)


def _mix_kernel(x_ref, oc_ref, os_ref, ow_ref, gt_ref, u_ref, v_ref, wsm_ref, bs_ref, gn_ref, gs_ref, eg_ref, wout_ref,
                o_ref, *, chunk):
    r = x_ref.shape[0]
    g = gt_ref[...]
    g_hi = g.astype(BF16)
    g_lo = (g - g_hi.astype(F32)).astype(BF16)
    onsa = None
    for j, branch in enumerate((oc_ref, os_ref, ow_ref)):
        ge = _dot(g_hi, eg_ref[j]) + _dot(g_lo, eg_ref[j])
        term = ge * branch[...]
        onsa = term if onsa is None else onsa + term
    onsa = _rms(onsa, gn_ref[...])
    ii = lax.broadcasted_iota(jnp.int32, (r, r), 0)
    jj = lax.broadcasted_iota(jnp.int32, (r, r), 1)
    tri = (ii // chunk == jj // chunk) & (jj % chunk <= ii % chunk)
    cols = []
    for gi in range(SGU_GROUPS):
        sl = slice(gi * SGU_GROUP_DIM, (gi + 1) * SGU_GROUP_DIM)
        ws = jnp.where(tri, wsm_ref[gi], 0.0)
        mixed = _dot(ws, v_ref[:, sl]) + bs_ref[:, gi:gi + 1]
        cols.append(u_ref[:, sl] * mixed)
    osgu = _rms(jnp.concatenate(cols, axis=1), gs_ref[...])
    o_ref[...] = x_ref[...] + _dot(jnp.concatenate([onsa, osgu], axis=1), wout_ref[...])


def _gate_expanders():
    c = jnp.arange(GATE_COLS)[None, :, None]
    lane = jnp.arange(NSA_WIDTH)[None, None, :]
    j = jnp.arange(3)[:, None, None]
    return (c == (lane // HEAD_DIM) * 3 + j).astype(BF16)


def _mix(x2d, o_cmp, o_sel, o_win, gates, u, v, w_sgu, b_sgu, g_nsa_out, g_sgu_out, w_out_b, chunk, r):
    n, d = x2d.shape
    rep = r // chunk
    wsm = jnp.tile(w_sgu[:, :chunk, :chunk], (1, rep, rep))
    bs = jnp.tile(b_sgu[:, :chunk].T, (rep, 1))
    row = lambda w: pl.BlockSpec((r, w), lambda i: (i, 0))
    return pl.pallas_call(
        functools.partial(_mix_kernel, chunk=chunk),
        grid=(n // r,),
        in_specs=[row(d), row(NSA_WIDTH), row(NSA_WIDTH), row(NSA_WIDTH), row(GATE_COLS), row(1024), row(1024),
                  _const_spec(wsm.shape), _const_spec(bs.shape), _const_spec((1, NSA_WIDTH)), _const_spec((1, 1024)),
                  _const_spec((3, GATE_COLS, NSA_WIDTH)), _const_spec(w_out_b.shape)],
        out_specs=row(d),
        out_shape=jax.ShapeDtypeStruct((n, d), F32),
        compiler_params=_cparams(("parallel",)),
        name="mix_out_proj",
    )(x2d, o_cmp, o_sel, o_win, gates, u, v, wsm, bs, g_nsa_out, g_sgu_out, _gate_expanders(), w_out_b)


def _rms_matmul_kernel(x_ref, g_ref, w_ref, o_ref):
    o_ref[...] = _dot(_rms(x_ref[...], g_ref[...]), w_ref[...])


def _rms_matmul(x2d, g, w_b, tm):
    n, d = x2d.shape
    m = w_b.shape[1]
    return pl.pallas_call(
        _rms_matmul_kernel,
        grid=(n // tm,),
        in_specs=[pl.BlockSpec((tm, d), lambda i: (i, 0)), _const_spec((1, d)), _const_spec(w_b.shape)],
        out_specs=pl.BlockSpec((tm, m), lambda i: (i, 0)),
        out_shape=jax.ShapeDtypeStruct((n, m), F32),
        compiler_params=_cparams(("parallel",)),
        name="rms_matmul",
    )(x2d, g, w_b)


def _matmul_res_kernel(a_ref, w_ref, r_ref, o_ref):
    o_ref[...] = r_ref[...] + _dot(a_ref[...], w_ref[...])


def _matmul_res(a2d, w_b, res, tm):
    n, k = a2d.shape
    m = w_b.shape[1]
    return pl.pallas_call(
        _matmul_res_kernel,
        grid=(n // tm,),
        in_specs=[pl.BlockSpec((tm, k), lambda i: (i, 0)), _const_spec(w_b.shape), pl.BlockSpec((tm, m), lambda i: (i, 0))],
        out_specs=pl.BlockSpec((tm, m), lambda i: (i, 0)),
        out_shape=jax.ShapeDtypeStruct((n, m), F32),
        compiler_params=_cparams(("parallel",)),
        name="matmul_residual",
    )(a2d, w_b, res)


def _memattn_kernel(q_ref, k_ref, v_ref, o_ref):
    q = q_ref[0]
    k = k_ref[0]
    v = v_ref[0]
    outs = []
    for h in range(MEM_HEADS):
        sl = slice(h * MEM_HEAD_DIM, (h + 1) * MEM_HEAD_DIM)
        s = _dot_nt(q[:, sl], k[:, sl]) * MEM_SCALE
        p = _masked_softmax(s, jnp.ones(s.shape, dtype=jnp.bool_))
        outs.append(_dot(p, v[:, sl]))
    o_ref[0] = jnp.concatenate(outs, axis=1)


def _mem_attention(hq, mk, mv):
    bsz, t, w = hq.shape
    m = mk.shape[1]
    tq = min(t, 512)
    return pl.pallas_call(
        _memattn_kernel,
        grid=(bsz, t // tq),
        in_specs=[pl.BlockSpec((1, tq, w), lambda b, i: (b, i, 0)),
                  pl.BlockSpec((1, m, w), lambda b, i: (b, 0, 0)),
                  pl.BlockSpec((1, m, w), lambda b, i: (b, 0, 0))],
        out_specs=pl.BlockSpec((1, tq, w), lambda b, i: (b, i, 0)),
        out_shape=jax.ShapeDtypeStruct((bsz, t, w), F32),
        compiler_params=_cparams(("parallel", "parallel")),
        name="mem_attn",
    )(hq, mk, mv)


def _router_logits(h, wr_ref):
    return jnp.dot(h, wr_ref[...], precision=lax.Precision.HIGHEST, preferred_element_type=F32)


def _router_kernel(x_ref, g_ref, wr_ref, gtop_ref):
    z = _router_logits(_rms(x_ref[...], g_ref[...]), wr_ref)
    lane = lax.broadcasted_iota(jnp.int32, z.shape, 1)
    zg = jnp.where(lane < N_GROUPS, z, -jnp.inf)
    m = jnp.max(zg, axis=1, keepdims=True)
    first = jnp.min(jnp.where(zg == m, lane.astype(F32), float(LANES)), axis=1, keepdims=True)
    gtop_ref[...] = first.astype(jnp.int32)


def _route_groups(x2d, g_moe, wr, tm):
    n, d = x2d.shape
    return pl.pallas_call(
        _router_kernel,
        grid=(n // tm,),
        in_specs=[pl.BlockSpec((tm, d), lambda i: (i, 0)), _const_spec((1, d)), _const_spec(wr.shape)],
        out_specs=pl.BlockSpec((tm, 1), lambda i: (i, 0)),
        out_shape=jax.ShapeDtypeStruct((n, 1), jnp.int32),
        compiler_params=_cparams(("parallel",)),
        name="route_groups",
    )(x2d, g_moe, wr)


def _moe_kernel(src_ref, tg_ref, tv_ref, nv_ref, widx_ref, x_hbm, gm_ref, gf_ref, wr_ref, wg_ref, wu_ref, wd_ref, y_hbm,
                buf, hb_scr, w4_scr, gsem, ssem, *, tm):
    t = pl.program_id(0)
    e = pl.program_id(1)
    nt = pl.num_programs(0)
    slot = t % 2
    other = 1 - slot
    valid = tv_ref[t] == 1
    prev_valid = (t >= 1) & (tv_ref[jnp.maximum(t - 1, 0)] == 1)
    next_valid = (t + 1 < nt) & (tv_ref[jnp.minimum(t + 1, nt - 1)] == 1)

    def gather_start(tt, sl):
        def body(r, c):
            idx = jnp.maximum(src_ref[tt * tm + r], 0)
            pltpu.make_async_copy(x_hbm.at[pl.ds(idx, 1), :], buf.at[sl, pl.ds(r, 1), :], gsem.at[sl]).start()
            return c
        lax.fori_loop(0, tm, body, 0, unroll=8)

    def gather_wait(sl):
        pltpu.make_async_copy(x_hbm.at[pl.ds(0, tm), :], buf.at[sl], gsem.at[sl]).wait()

    def scatter_copy(tt, sl, r):
        return pltpu.make_async_copy(buf.at[sl, pl.ds(r, 1), :], y_hbm.at[pl.ds(src_ref[tt * tm + r], 1), :], ssem.at[0])

    def scatter_start(tt, sl):
        def body(r, c):
            scatter_copy(tt, sl, r).start()
            return c
        lax.fori_loop(0, nv_ref[tt], body, 0)

    def scatter_wait(tt, sl):
        def body(r, c):
            scatter_copy(tt, sl, r).wait()
            return c
        lax.fori_loop(0, nv_ref[tt], body, 0)

    @pl.when(e == 0)
    def _():
        @pl.when(t == 0)
        def _():
            gather_start(0, 0)

        @pl.when(prev_valid)
        def _():
            scatter_start(t - 1, other)

        @pl.when(valid)
        def _():
            gather_wait(slot)
            h = _rms(buf[slot], gm_ref[...])
            hb_scr[...] = h.astype(BF16)
            z = _router_logits(h, wr_ref)
            lane = lax.broadcasted_iota(jnp.int32, z.shape, 1)
            lanef = lane.astype(F32)
            grp = lane < N_GROUPS
            zg = jnp.where(grp, z, -jnp.inf)
            pg_top = 1.0 / jnp.sum(jnp.where(grp, jnp.exp(zg - jnp.max(zg, axis=1, keepdims=True)), 0.0), axis=1, keepdims=True)
            lo = N_GROUPS + tg_ref[t] * EXPERTS_PER_GROUP
            ing = (lane >= lo) & (lane < lo + EXPERTS_PER_GROUP)
            pf = _masked_softmax(z, ing)
            big = float(2 * LANES)
            m1 = jnp.max(jnp.where(ing, pf, -1.0), axis=1, keepdims=True)
            i1 = jnp.min(jnp.where(ing & (pf == m1), lanef, big), axis=1, keepdims=True)
            rest = ing & (lanef != i1)
            m2 = jnp.max(jnp.where(rest, pf, -1.0), axis=1, keepdims=True)
            i2 = jnp.min(jnp.where(rest & (pf == m2), lanef, big), axis=1, keepdims=True)
            tot = m1 + m2
            w4_scr[...] = jnp.where(lanef == i1, m1 / tot * pg_top, jnp.where(lanef == i2, m2 / tot * pg_top, 0.0))

    @pl.when(e == EXPERTS_PER_GROUP // 2)
    def _():
        @pl.when(prev_valid)
        def _():
            scatter_wait(t - 1, other)

        @pl.when(next_valid)
        def _():
            gather_start(t + 1, other)

    @pl.when(valid)
    def _():
        hb = hb_scr[...]
        lane = lax.broadcasted_iota(jnp.int32, w4_scr.shape, 1)
        col = N_GROUPS + tg_ref[t] * EXPERTS_PER_GROUP + e
        we = jnp.sum(jnp.where(lane == col, w4_scr[...], 0.0), axis=1, keepdims=True)
        act = jax.nn.silu(_dot(hb, wg_ref[0])) * _dot(hb, wu_ref[0])
        buf[slot] += _dot(act * we, wd_ref[0])

    @pl.when(valid & (e == EXPERTS_PER_GROUP - 1))
    def _():
        buf[slot] = _rms(buf[slot], gf_ref[...])


def _moe_final(x2d, g_moe, g_final, wr, w_gate, w_up, w_down, tm):
    n, d = x2d.shape
    g_top = _route_groups(x2d, g_moe, wr, min(n, 512))[:, 0]
    n_tiles = n // tm + N_GROUPS
    onehot = (g_top[:, None] == jnp.arange(N_GROUPS)[None, :]).astype(jnp.int32)
    counts = jnp.sum(onehot, axis=0)
    rank = jnp.sum((jnp.cumsum(onehot, axis=0) - onehot) * onehot, axis=1)
    padded = (counts + tm - 1) // tm * tm
    ends = jnp.cumsum(padded)
    base = ends - padded
    pos = base[g_top] + rank
    src = jnp.full((n_tiles * tm,), -1, jnp.int32).at[pos].set(jnp.arange(n, dtype=jnp.int32))
    tile_start = jnp.arange(n_tiles, dtype=jnp.int32) * tm
    tile_valid = (tile_start < ends[-1]).astype(jnp.int32)
    tile_group = jnp.minimum(jnp.sum((tile_start[:, None] >= ends[None, :]).astype(jnp.int32), axis=1), N_GROUPS - 1)
    tile_rows = jnp.clip(ends[tile_group] - padded[tile_group] + counts[tile_group] - tile_start, 0, tm)
    tile_rows = (tile_rows * tile_valid).astype(jnp.int32)
    n_valid = ends[-1] // tm
    last_group = tile_group[jnp.maximum(n_valid - 1, 0)]
    eidx = tile_group[:, None] * EXPERTS_PER_GROUP + jnp.arange(EXPERTS_PER_GROUP, dtype=jnp.int32)[None, :]
    widx = jnp.where(tile_valid[:, None] == 1, eidx, last_group * EXPERTS_PER_GROUP + EXPERTS_PER_GROUP - 1)
    widx = widx.reshape(-1).astype(jnp.int32)

    wmap = lambda t, e, src, tg, tv, nv, wi: (wi[t * EXPERTS_PER_GROUP + e], 0, 0)
    cmap = lambda t, e, src, tg, tv, nv, wi: (0, 0)
    grid_spec = pltpu.PrefetchScalarGridSpec(
        num_scalar_prefetch=5,
        grid=(n_tiles, EXPERTS_PER_GROUP),
        in_specs=[pl.BlockSpec(memory_space=pl.ANY),
                  pl.BlockSpec((1, d), cmap), pl.BlockSpec((1, d), cmap), pl.BlockSpec(wr.shape, cmap),
                  pl.BlockSpec((1, d, EXPERT_FF), wmap), pl.BlockSpec((1, d, EXPERT_FF), wmap),
                  pl.BlockSpec((1, EXPERT_FF, d), wmap)],
        out_specs=pl.BlockSpec(memory_space=pl.ANY),
        scratch_shapes=[pltpu.VMEM((2, tm, d), F32), pltpu.VMEM((tm, d), BF16), pltpu.VMEM((tm, LANES), F32),
                        pltpu.SemaphoreType.DMA((2,)), pltpu.SemaphoreType.DMA((1,))],
    )
    return pl.pallas_call(
        functools.partial(_moe_kernel, tm=tm),
        grid_spec=grid_spec,
        out_shape=jax.ShapeDtypeStruct((n, d), F32),
        compiler_params=_cparams(("arbitrary", "arbitrary")),
        name="moe_final_norm",
    )(src, tile_group, tile_valid, tile_rows, widx, x2d, g_moe, g_final, wr, w_gate, w_up, w_down)


def _finish(x2d, o_cmp, o_sel, o_win, gates, u, v, mk, mv, bsz, lw, chunk, moe_tm):
    n, d = x2d.shape
    t = n // bsz
    x1 = _mix(x2d, o_cmp, o_sel, o_win, gates, u, v, lw["w_sgu"], lw["b_sgu"], lw["g_nsa_out"], lw["g_sgu_out"],
              lw["w_out"], chunk, min(n, 256))
    hq = _rms_matmul(x1, lw["g_mem_norm"], lw["w_mem_q"], min(n, 512))
    o_m = _mem_attention(hq.reshape(bsz, t, -1), mk, mv)
    x2 = _matmul_res(o_m.reshape(n, -1), lw["w_mem_o"], x1, min(n, 512))
    return _moe_final(x2, lw["g_moe_norm"], lw["g_final"], lw["w_router"], lw["w_exp_gate"], lw["w_exp_up"],
                      lw["w_exp_down"], moe_tm)


def kernel(x_prompt, x_sample, cache_cmp_k, cache_cmp_v, cache_sel_k, cache_sel_v, cache_win_k, cache_win_v, cache_mem_k, cache_mem_v, page_table, mem_prompt, w_in, g_attn_norm, pe_cmp_k, w_cmp_k1, w_cmp_k2, pe_cmp_v, w_cmp_v1, w_cmp_v2, g_sgu_v, w_sgu, b_sgu, g_nsa_out, g_sgu_out, w_out, g_mem_norm, g_mem_src, w_mem_q, w_mem_k, w_mem_v, w_mem_o, g_moe_norm, w_router_group, w_router_expert, w_exp_gate, w_exp_up, w_exp_down, g_final):
    depth = w_in.shape[0]
    assert depth == 1, "single-layer trunk"
    bp, tp, d = x_prompt.shape
    bs, ts, _ = x_sample.shape
    n_pages = page_table.shape[1]
    page = cache_cmp_k.shape[2]
    past = n_pages * page
    assert ts < STRIDE and tp % 256 == 0 and page % STRIDE == 0
    row = lambda a: a[0].reshape(1, -1)

    wi = w_in[0]
    c_q = NSA_WIDTH + 6 * KV_WIDTH
    n_gate = 3 * N_HEADS
    w_in_p = jnp.concatenate([wi[:, :c_q + n_gate], jnp.zeros((d, GATE_COLS - n_gate), F32), wi[:, c_q + n_gate:]],
                             axis=1).astype(BF16)
    wr = jnp.concatenate([w_router_group[0], w_router_expert[0],
                          jnp.zeros((d, LANES - N_GROUPS - N_EXPERTS), F32)], axis=1)
    lw = {
        "w_sgu": w_sgu[0], "b_sgu": b_sgu[0], "g_nsa_out": row(g_nsa_out), "g_sgu_out": row(g_sgu_out),
        "w_out": w_out[0].astype(BF16), "g_mem_norm": row(g_mem_norm), "w_mem_q": w_mem_q[0].astype(BF16),
        "w_mem_o": w_mem_o[0].astype(BF16), "g_moe_norm": row(g_moe_norm), "g_final": g_final.reshape(1, -1),
        "w_router": wr, "w_exp_gate": w_exp_gate[0], "w_exp_up": w_exp_up[0], "w_exp_down": w_exp_down[0],
    }
    g_attn = row(g_attn_norm)
    gsv = row(g_sgu_v)
    kv5 = lambda a, b, t: a.reshape(1, b, t, KV_HEADS, HEAD_DIM)

    np_ = bp * tp
    tabs_p = _rope_tables(jnp.arange(tp, dtype=jnp.int32))
    (q, qrot, kc, vc, ks, vs, kw, vw, gates, u, v) = _project(x_prompt.reshape(np_, d), g_attn, w_in_p, tabs_p, gsv, 256)
    pt_p = jnp.arange(np_ // page, dtype=jnp.int32).reshape(bp, tp // page)
    kcmp = _compress(kc.reshape(np_ // page, page, KV_WIDTH), pt_p, pe_cmp_k[0], w_cmp_k1[0], w_cmp_k2[0])
    vcmp = _compress(vc.reshape(np_ // page, page, KV_WIDTH), pt_p, pe_cmp_v[0], w_cmp_v1[0], w_cmp_v2[0])
    o_cmp, sel_t = _cmp_select(q.reshape(bp, tp, -1), kcmp, vcmp, 0, tp)
    qrot3 = qrot.reshape(bp, tp, -1)
    ks3, vs3, kw3, vw3 = (a.reshape(bp, tp, KV_WIDTH) for a in (ks, vs, kw, vw))
    o_sel = _prompt_attention(qrot3, ks3, vs3, sel_t)
    o_win = _prompt_attention(qrot3, kw3, vw3)
    n_mem = mem_prompt.shape[1]
    w_mem_kv = jnp.concatenate([w_mem_k[0], w_mem_v[0]], axis=1).astype(BF16)
    mkv = _rms_matmul(mem_prompt.reshape(bp * n_mem, d), row(g_mem_src), w_mem_kv, min(bp * n_mem, 512))
    mem_w = MEM_HEADS * MEM_HEAD_DIM
    mk_p = mkv[:, :mem_w].reshape(bp, n_mem, mem_w)
    mv_p = mkv[:, mem_w:].reshape(bp, n_mem, mem_w)
    y_p = _finish(x_prompt.reshape(np_, d), o_cmp.reshape(np_, -1), o_sel.reshape(np_, -1), o_win.reshape(np_, -1),
                  gates, u, v, mk_p, mv_p, bp, lw, CHUNK, min(np_, 512))
    wbp = min(WINDOW, tp)
    outs_p = (kv5(kc, bp, tp), kv5(vc, bp, tp), kv5(ks, bp, tp), kv5(vs, bp, tp),
              kv5(kw, bp, tp)[:, :, -wbp:], kv5(vw, bp, tp)[:, :, -wbp:],
              mk_p.reshape(1, bp, n_mem, MEM_HEADS, MEM_HEAD_DIM), mv_p.reshape(1, bp, n_mem, MEM_HEADS, MEM_HEAD_DIM))

    ns_ = bs * ts
    pos_s = past + jnp.arange(ts, dtype=jnp.int32)
    tabs_s = tuple(jnp.tile(a, (bs, 1)) for a in _rope_tables(pos_s))
    (q, qrot, kc, vc, ks, vs, kw, vw, gates, u, v) = _project(x_sample.reshape(ns_, d), g_attn, w_in_p, tabs_s, gsv, ns_)
    pool = lambda c: c[0].reshape(c.shape[1], page, KV_WIDTH)
    kcmp = _compress(pool(cache_cmp_k), page_table, pe_cmp_k[0], w_cmp_k1[0], w_cmp_k2[0])
    vcmp = _compress(pool(cache_cmp_v), page_table, pe_cmp_v[0], w_cmp_v1[0], w_cmp_v2[0])
    o_cmp, sel_t = _cmp_select(q.reshape(bs, ts, -1).astype(F32), kcmp, vcmp, past, past + ts)
    wb = cache_win_k.shape[2]
    wk = cache_win_k[0].reshape(bs, wb, KV_WIDTH)
    wv = cache_win_v[0].reshape(bs, wb, KV_WIDTH)
    ks3, vs3, kw3, vw3 = (a.reshape(bs, ts, KV_WIDTH) for a in (ks, vs, kw, vw))
    o_sel, o_win = _sample_attention(qrot.reshape(bs, ts, -1), pool(cache_sel_k), pool(cache_sel_v), page_table,
                                     ks3, vs3, sel_t, wk, wv, kw3, vw3)
    mem_ks = cache_mem_k[0].reshape(bs, cache_mem_k.shape[2], mem_w)
    mem_vs = cache_mem_v[0].reshape(bs, cache_mem_v.shape[2], mem_w)
    y_s = _finish(x_sample.reshape(ns_, d), o_cmp.reshape(ns_, -1), o_sel.reshape(ns_, -1), o_win.reshape(ns_, -1),
                  gates, u, v, mem_ks, mem_vs, bs, lw, ts, min(ns_, 128))
    win_k_s = jnp.concatenate([wk, kw3], axis=1)[:, -wb:].reshape(1, bs, wb, KV_HEADS, HEAD_DIM)
    win_v_s = jnp.concatenate([wv, vw3], axis=1)[:, -wb:].reshape(1, bs, wb, KV_HEADS, HEAD_DIM)
    outs_s = (kv5(kc, bs, ts), kv5(vc, bs, ts), kv5(ks, bs, ts), kv5(vs, bs, ts), win_k_s, win_v_s,
              v.reshape(1, bs, ts, -1))

    return (y_p.reshape(bp, tp, d), y_s.reshape(bs, ts, d)) + outs_p + outs_s
```

```python
import functools

import jax
import jax.numpy as jnp
from jax import lax
from jax.experimental import pallas as pl
from jax.experimental.pallas import tpu as pltpu

F32 = jnp.float32
BF16 = jnp.bfloat16

N_HEADS = 16
HEAD_DIM = 64
KV_HEADS = 4
Q_PER_KV = N_HEADS // KV_HEADS
NSA_WIDTH = N_HEADS * HEAD_DIM
KV_WIDTH = KV_HEADS * HEAD_DIM
ROT_DIM = HEAD_DIM // 4
ROPE_THETA = 500000.0
L_CMP = 32
STRIDE = 16
CMP_R = L_CMP // STRIDE
CMP_HIDDEN = 128
L_SEL = 64
SEL_RATIO = L_SEL // STRIDE
SEL_INNER = (L_SEL - L_CMP) // STRIDE + 1
TOP_N = 16
WINDOW = 512
SGU_GROUPS = 8
SGU_GROUP_DIM = 128
CHUNK = 128
MEM_HEADS = 4
MEM_HEAD_DIM = 128
N_GROUPS = 4
EXPERTS_PER_GROUP = 4
N_EXPERTS = N_GROUPS * EXPERTS_PER_GROUP
EXPERT_FF = 512
EPS = 1e-6
NEG_INF = -1e30
BIG = 1e9
PAD_SCORE = -3e38
ATTN_SCALE = HEAD_DIM ** -0.5
MEM_SCALE = MEM_HEAD_DIM ** -0.5

LANES = 128
SUBLANES = 8
V7X_VMEM_LIMIT_BYTES = 60000 * 1024

GATE_COLS = LANES
_SEG = {}
_off = 0
for _name, _w in (("q", NSA_WIDTH), ("kc", KV_WIDTH), ("vc", KV_WIDTH), ("ks", KV_WIDTH), ("vs", KV_WIDTH),
                  ("kw", KV_WIDTH), ("vw", KV_WIDTH), ("gt", GATE_COLS), ("u", 1024), ("v", 1024)):
    _SEG[_name] = (_off, _w)
    _off += _w
IN_COLS_PADDED = _off


def _cparams(sem, vmem=V7X_VMEM_LIMIT_BYTES):
    return pltpu.CompilerParams(dimension_semantics=sem, vmem_limit_bytes=vmem)


def _dot(a, b):
    return jnp.dot(a.astype(BF16), b.astype(BF16), preferred_element_type=F32)


def _dot_nt(a, b):
    return lax.dot_general(a.astype(BF16), b.astype(BF16), (((1,), (1,)), ((), ())), preferred_element_type=F32)


def _dot_tn(a, b):
    return lax.dot_general(a.astype(BF16), b.astype(BF16), (((0,), (0,)), ((), ())), preferred_element_type=F32)


def _rms(x, g):
    return x * lax.rsqrt(jnp.mean(x * x, axis=-1, keepdims=True) + EPS) * g


def _masked_softmax(s, mask, axis=-1):
    s = jnp.where(mask, s, NEG_INF)
    m = jnp.max(s, axis=axis, keepdims=True)
    e = jnp.where(mask, jnp.exp(s - m), 0.0)
    return e * (1.0 / jnp.maximum(jnp.sum(e, axis=axis, keepdims=True), 1e-30))


def _const_spec(shape):
    nd = len(shape)
    return pl.BlockSpec(shape, lambda *_: (0,) * nd, pipeline_mode=pl.Buffered(1))


def _proj_kernel(x_ref, g_ref, w_ref, rc_ref, rs1_ref, rs2_ref, gsv_ref,
                 q_ref, qrot_ref, kc_ref, vc_ref, ks_ref, vs_ref, kw_ref, vw_ref, gates_ref, u_ref, v_ref):
    hb = _rms(x_ref[...], g_ref[...]).astype(BF16)

    def seg(name):
        lo, width = _SEG[name]
        return jnp.dot(hb, w_ref[:, lo:lo + width], preferred_element_type=F32)

    def rope(z):
        rc, rs1, rs2 = rc_ref[...], rs1_ref[...], rs2_ref[...]
        half = ROT_DIM // 2
        outs = []
        for c in range(z.shape[1] // LANES):
            zc = z[:, c * LANES:(c + 1) * LANES]
            outs.append(zc * rc + pltpu.roll(zc, LANES - half, 1) * rs1 + pltpu.roll(zc, half, 1) * rs2)
        return jnp.concatenate(outs, axis=1)

    q = seg("q")
    q_ref[...] = q.astype(BF16)
    qrot_ref[...] = rope(q).astype(BF16)
    kc_ref[...] = seg("kc")
    vc_ref[...] = seg("vc")
    ks_ref[...] = rope(seg("ks"))
    vs_ref[...] = seg("vs")
    kw_ref[...] = rope(seg("kw"))
    vw_ref[...] = seg("vw")
    gates_ref[...] = jax.nn.sigmoid(seg("gt"))
    u_ref[...] = jax.nn.gelu(seg("u"))
    v_ref[...] = _rms(jax.nn.gelu(seg("v")), gsv_ref[...])


def _project(x2d, g_attn, w_in_p, tables, g_sgu_v, tm):
    n, d = x2d.shape
    rc, rs1, rs2 = tables
    tt = rc.shape[0]
    nt = tt // tm
    row = lambda w: pl.BlockSpec((tm, w), lambda i: (i, 0))
    tab = pl.BlockSpec((tm, LANES), lambda i: (i % nt, 0))
    out_shapes = [jax.ShapeDtypeStruct((n, NSA_WIDTH), BF16), jax.ShapeDtypeStruct((n, NSA_WIDTH), BF16)]
    out_shapes += [jax.ShapeDtypeStruct((n, KV_WIDTH), F32)] * 6
    out_shapes += [jax.ShapeDtypeStruct((n, GATE_COLS), F32), jax.ShapeDtypeStruct((n, 1024), F32),
                   jax.ShapeDtypeStruct((n, 1024), F32)]
    out_specs = [row(NSA_WIDTH), row(NSA_WIDTH)] + [row(KV_WIDTH)] * 6 + [row(GATE_COLS), row(1024), row(1024)]
    return pl.pallas_call(
        _proj_kernel,
        grid=(n // tm,),
        in_specs=[row(d), _const_spec((1, d)), _const_spec(w_in_p.shape), tab, tab, tab, _const_spec((1, 1024))],
        out_specs=out_specs,
        out_shape=out_shapes,
        compiler_params=_cparams(("parallel",)),
        name="in_proj",
    )(x2d, g_attn, w_in_p, rc, rs1, rs2, g_sgu_v)


def _rope_tables(pos):
    half = ROT_DIM // 2
    freqs = ROPE_THETA ** (-jnp.arange(half, dtype=F32) / half)
    ang = pos.astype(F32)[:, None] * freqs[None, :]
    cos, sin = jnp.cos(ang), jnp.sin(ang)
    t = pos.shape[0]
    ones = jnp.ones((t, HEAD_DIM - ROT_DIM), F32)
    zeros = jnp.zeros((t, HEAD_DIM - ROT_DIM), F32)
    zh = jnp.zeros((t, half), F32)
    rc = jnp.concatenate([cos, cos, ones], axis=1)
    rs1 = jnp.concatenate([-sin, zh, zeros], axis=1)
    rs2 = jnp.concatenate([zh, sin, zeros], axis=1)
    rep = LANES // HEAD_DIM
    return tuple(jnp.tile(a, (1, rep)) for a in (rc, rs1, rs2))


def _compress_kernel(pt_ref, pool_ref, pe_ref, w1_ref, w2_ref, out_ref, *scratch, n_pages, page, transposed):
    if transposed:
        stage, buf, a_ref, sem = scratch
    else:
        buf, a_ref, sem = scratch
    b = pl.program_id(0)
    nb = pl.num_programs(0)
    n_rows = n_pages * page
    n_sub = n_rows // STRIDE
    slot = b % 2
    n_ct = KV_WIDTH // LANES

    def page_copies(bb, sl, p):
        row0 = pl.multiple_of(p * page, page)
        pid = pt_ref[bb, p]
        if transposed:
            return [pltpu.make_async_copy(pool_ref.at[pid], stage.at[sl, :, pl.ds(row0, page)], sem.at[sl])]
        return [pltpu.make_async_copy(pool_ref.at[pid, :, pl.ds(c * LANES, LANES)],
                                      buf.at[sl, c, pl.ds(row0, page), :], sem.at[sl]) for c in range(n_ct)]

    def start_all(bb, sl):
        def body(p, carry):
            for cp in page_copies(bb, sl, p):
                cp.start()
            return carry
        lax.fori_loop(0, n_pages, body, 0)

    def wait_all(bb, sl):
        def body(p, carry):
            for cp in page_copies(bb, sl, p):
                cp.wait()
            return carry
        lax.fori_loop(0, n_pages, body, 0)

    @pl.when(b == 0)
    def _():
        start_all(0, 0)

    @pl.when(b + 1 < nb)
    def _():
        start_all(b + 1, 1 - slot)

    wait_all(b, slot)

    if transposed:
        tch = min(n_rows, 512)
        for c in range(n_ct):
            for j in range(n_rows // tch):
                buf[c, j * tch:(j + 1) * tch, :] = stage[slot, c * LANES:(c + 1) * LANES, j * tch:(j + 1) * tch].T

    pev =_dot(pe_ref[...], w1_ref[...])
    low = lax.broadcasted_iota(jnp.int32, (n_sub, LANES), 1) < HEAD_DIM
    row = lax.broadcasted_iota(jnp.int32, (n_sub, 4 * HEAD_DIM), 0)
    for c in range(n_ct):
        src = buf.at[c] if transposed else buf.at[slot, c]
        for sp in range(STRIDE // 2):
            x0 = src[pl.ds(2 * sp, n_sub, stride=STRIDE), :]
            x1 = src[pl.ds(2 * sp + 1, n_sub, stride=STRIDE), :]
            a_ref[0, :, sp * LANES:(sp + 1) * LANES] = jnp.where(low, x0, pltpu.roll(x1, HEAD_DIM, 1)).astype(BF16)
            a_ref[1, :, sp * LANES:(sp + 1) * LANES] = jnp.where(low, pltpu.roll(x0, HEAD_DIM, 1), x1).astype(BF16)
        for k in range(2):
            pm = jnp.dot(a_ref[k], w1_ref[...], preferred_element_type=F32)
            part0 = pm[:, :CMP_HIDDEN] + pev[0:1, :CMP_HIDDEN]
            part1 = pm[:, CMP_HIDDEN:] + pev[1:2, CMP_HIDDEN:]
            hsum = part0 + pltpu.roll(part1, n_sub - 1, 0)
            o = _dot(jax.nn.silu(hsum), w2_ref[...])
            out_ref[0, 2 * c + k] = jnp.where(row < n_sub - 1, o, 0.0).astype(BF16)


def _compress(pool, page_table, pe, w1, w2, transposed):
    bsz, n_pages = page_table.shape
    page = pool.shape[2] if transposed else pool.shape[1]
    n_rows = n_pages * page
    n_sub = n_rows // STRIDE
    kdim = STRIDE * HEAD_DIM
    n_ct = KV_WIDTH // LANES
    pe8 = jnp.zeros((SUBLANES, kdim), F32).at[:CMP_R].set(pe.reshape(CMP_R, kdim))
    w1c = w1.reshape(CMP_R, kdim, CMP_HIDDEN).transpose(1, 0, 2).reshape(kdim, CMP_R * CMP_HIDDEN).astype(BF16)
    w2t = jnp.tile(w2, (1, 4)).astype(BF16)
    if transposed:
        bufs = [pltpu.VMEM((2, KV_WIDTH, n_rows), F32), pltpu.VMEM((n_ct, n_rows, LANES), F32)]
    else:
        bufs = [pltpu.VMEM((2, n_ct, n_rows, LANES), F32)]
    grid_spec = pltpu.PrefetchScalarGridSpec(
        num_scalar_prefetch=1,
        grid=(bsz,),
        in_specs=[pl.BlockSpec(memory_space=pl.ANY),
                  pl.BlockSpec((SUBLANES, kdim), lambda b, pt: (0, 0)),
                  pl.BlockSpec(w1c.shape, lambda b, pt: (0, 0)),
                  pl.BlockSpec(w2t.shape, lambda b, pt: (0, 0))],
        out_specs=pl.BlockSpec((1, KV_HEADS, n_sub, 4 * HEAD_DIM), lambda b, pt: (b, 0, 0, 0)),
        scratch_shapes=bufs + [pltpu.VMEM((2, n_sub, kdim), BF16), pltpu.SemaphoreType.DMA((2,))],
    )
    return pl.pallas_call(
        functools.partial(_compress_kernel, n_pages=n_pages, page=page, transposed=transposed),
        grid_spec=grid_spec,
        out_shape=jax.ShapeDtypeStruct((bsz, KV_HEADS, n_sub, 4 * HEAD_DIM), BF16),
        compiler_params=_cparams(("arbitrary",)),
        name="compress",
    )(page_table, pool, pe8, w1c, w2t)


def _head_blockdiag(x, n_heads_in_lanes=Q_PER_KV):
    n = x.shape[0]
    lane_head = lax.broadcasted_iota(jnp.int32, (n, n_heads_in_lanes * HEAD_DIM), 1) // HEAD_DIM
    xf = x.astype(F32)
    return jnp.concatenate([jnp.where(lane_head == qi, xf, 0.0) for qi in range(n_heads_in_lanes)], axis=0).astype(BF16)


def _cmpsel_kernel(q_ref, k_ref, v_ref, st_ref, o_ref, sc_ref, *, pos0, nc, ns, tq):
    i = pl.program_id(2)
    q = q_ref[0]
    n_sub = k_ref.shape[2]
    ns_pad = st_ref.shape[0]
    kbd = _head_blockdiag(k_ref[0, 0])
    vbd = _head_blockdiag(v_ref[0, 0])
    s_all = _dot_nt(q, kbd) * ATTN_SCALE
    qpos = pos0 + i * tq + lax.broadcasted_iota(jnp.int32, (tq, 1), 0)
    kidx = lax.broadcasted_iota(jnp.int32, (1, n_sub), 1)
    mask = (kidx * STRIDE + (L_CMP - 1) <= qpos) & (kidx < nc)
    probs = []
    for qi in range(Q_PER_KV):
        probs.append(_masked_softmax(s_all[:, qi * n_sub:(qi + 1) * n_sub], mask))
    o_ref[0] = _dot(jnp.concatenate(probs, axis=1), vbd)
    pg = probs[0]
    for p in probs[1:]:
        pg = pg + p
    pg_hi = pg.astype(BF16)
    pg_lo = (pg - pg_hi.astype(F32)).astype(BF16)
    st = st_ref[...]
    ps_t = _dot_nt(st, pg_hi) + _dot_nt(st, pg_lo)
    blk = lax.broadcasted_iota(jnp.int32, (ns_pad, tq), 0)
    qpos_t = pos0 + i * tq + lax.broadcasted_iota(jnp.int32, (ns_pad, tq), 1)
    cur = qpos_t // L_SEL
    forced = (blk == 0) | (blk == cur) | (blk == cur - 1)
    future = blk * L_SEL > qpos_t
    sc = jnp.where(forced, BIG, jnp.where(future, -BIG, ps_t))
    sc_ref[0, 0] = jnp.where(blk < ns, sc, PAD_SCORE)


def _rank_kernel(sc_ref, sel_ref, *, ns, topn):
    sc = sc_ref[0, 0]
    blk = lax.broadcasted_iota(jnp.int32, sc.shape, 0)
    sel_ref[0, 0] = jnp.zeros(sc.shape, F32)

    def body(j, c):
        row = sc_ref[0, 0, pl.ds(j, 1), :]
        ahead = (sc > row) | ((sc == row) & (blk < j))
        rank = jnp.sum(ahead.astype(F32), axis=0, keepdims=True)
        sel_ref[0, 0, pl.ds(j, 1), :] = (rank < topn).astype(F32)
        return c

    lax.fori_loop(0, ns, body, 0)


def _cmp_select(q, kcmp, vcmp, pos0, n_keys):
    bsz, t, _ = q.shape
    n_sub = kcmp.shape[2]
    nc = n_sub - CMP_R + 1
    ns = -(-n_keys // L_SEL)
    ns_pad = -(-ns // SUBLANES) * SUBLANES
    topn = min(TOP_N, ns)
    tq = min(t, 512)
    gw = Q_PER_KV * HEAD_DIM
    cidx = jnp.arange(n_sub)
    st = ((cidx[None, :] // SEL_RATIO == jnp.arange(ns_pad)[:, None]) & (cidx[None, :] % SEL_RATIO < SEL_INNER)
          & (cidx[None, :] < nc)).astype(BF16)
    o_cmp, sc = pl.pallas_call(
        functools.partial(_cmpsel_kernel, pos0=pos0, nc=nc, ns=ns, tq=tq),
        grid=(bsz, KV_HEADS, t // tq),
        in_specs=[pl.BlockSpec((1, tq, gw), lambda b, g, i: (b, i, g)),
                  pl.BlockSpec((1, 1, n_sub, gw), lambda b, g, i: (b, g, 0, 0)),
                  pl.BlockSpec((1, 1, n_sub, gw), lambda b, g, i: (b, g, 0, 0)),
                  pl.BlockSpec((ns_pad, n_sub), lambda b, g, i: (0, 0))],
        out_specs=[pl.BlockSpec((1, tq, gw), lambda b, g, i: (b, i, g)),
                   pl.BlockSpec((1, 1, ns_pad, tq), lambda b, g, i: (b, g, 0, i))],
        out_shape=[jax.ShapeDtypeStruct((bsz, t, NSA_WIDTH), F32),
                   jax.ShapeDtypeStruct((bsz, KV_HEADS, ns_pad, t), F32)],
        compiler_params=_cparams(("parallel", "parallel", "parallel")),
        name="cmp_scores",
    )(q, kcmp, vcmp, st)
    fold = t < LANES
    if fold:
        sc = sc.transpose(2, 0, 1, 3).reshape(1, 1, ns_pad, bsz * KV_HEADS * t)
    nb, ng, _, width = sc.shape
    tl = min(width, 512)
    sel = pl.pallas_call(
        functools.partial(_rank_kernel, ns=ns, topn=topn),
        grid=(nb, ng, width // tl),
        in_specs=[pl.BlockSpec((1, 1, ns_pad, tl), lambda b, g, i: (b, g, 0, i))],
        out_specs=pl.BlockSpec((1, 1, ns_pad, tl), lambda b, g, i: (b, g, 0, i)),
        out_shape=jax.ShapeDtypeStruct(sc.shape, F32),
        compiler_params=_cparams(("parallel", "parallel", "parallel")),
        name="rank_select",
    )(sc)
    if fold:
        sel = sel.reshape(ns_pad, bsz, KV_HEADS, t).transpose(1, 2, 0, 3)
    return o_cmp, sel


def _pattn_kernel(*refs, mode, t, tq, ck):
    if mode == "sel":
        q_ref, k_ref, v_ref, rk_ref, rv_ref, sel_ref, et_ref, o_ref, kc_scr, vt_scr, m_scr, l_scr, acc_scr = refs
    else:
        q_ref, k_ref, v_ref, rk_ref, rv_ref, o_ref, kc_scr, vt_scr, m_scr, l_scr, acc_scr = refs
    i = pl.program_id(2)
    n_chunks = t // ck
    last = (i + 1) * (tq // ck) - 1

    @pl.when(i == 0)
    def _():
        kg = _dot(k_ref[0], rk_ref[0]).astype(BF16)
        vg = _dot_nt(rv_ref[0], v_ref[0]).astype(BF16)
        for c in range(n_chunks):
            kc_scr[c] = kg[c * ck:(c + 1) * ck, :]
            vt_scr[c] = vg[:, c * ck:(c + 1) * ck]

    qf = q_ref[0].astype(F32) * ATTN_SCALE
    q_heads = []
    for qi in range(Q_PER_KV):
        tile = qf[:, (qi // 2) * LANES:(qi // 2 + 1) * LANES]
        q_heads.append((tile if qi % 2 == 0 else pltpu.roll(tile, HEAD_DIM, 1)).astype(BF16))
    qpos = i * tq + lax.broadcasted_iota(jnp.int32, (1, tq), 1)
    m_scr[...] = jnp.full(m_scr.shape, NEG_INF, F32)
    l_scr[...] = jnp.zeros(l_scr.shape, F32)
    acc_scr[...] = jnp.zeros(acc_scr.shape, F32)
    if mode == "sel":
        sel_b = sel_ref[0, 0].astype(BF16)
        n_steps = last + 1
    else:
        n_steps = last - jnp.maximum(i * tq - WINDOW, 0) // ck + 1

    def body(step, carry):
        c = last - step
        kpos = c * ck + lax.broadcasted_iota(jnp.int32, (ck, 1), 0)
        if mode == "sel":
            chosen = jnp.dot(et_ref[c], sel_b, preferred_element_type=F32)
            ok = (chosen > 0.5) & (kpos <= qpos)
        else:
            ok = (kpos <= qpos) & (kpos > qpos - WINDOW)
        bias = jnp.where(ok, 0.0, NEG_INF)
        kc = kc_scr[c]
        vt = vt_scr[c]
        for qi in range(Q_PER_KV):
            s = _dot_nt(kc, q_heads[qi]) + bias
            m_old = m_scr[qi]
            m_new = jnp.maximum(m_old, jnp.max(s, axis=0, keepdims=True))
            e = jnp.exp(s - m_new)
            alpha = jnp.exp(m_old - m_new)
            l_scr[qi] = l_scr[qi] * alpha + jnp.sum(e, axis=0, keepdims=True)
            m_scr[qi] = m_new
            acc_scr[qi] = acc_scr[qi] * alpha + jnp.dot(vt, e.astype(BF16), preferred_element_type=F32)
        return carry

    lax.fori_loop(0, n_steps, body, 0)
    outs = [acc_scr[qi] * (1.0 / jnp.maximum(l_scr[qi], 1e-30)) for qi in range(Q_PER_KV)]
    o_ref[0] = jnp.concatenate(outs, axis=0).T


def _head_pick_matrices():
    rows = jnp.arange(KV_WIDTH)[None, :, None]
    cols = jnp.arange(LANES)[None, None, :]
    g = jnp.arange(KV_HEADS)[:, None, None]
    rk = ((rows // HEAD_DIM == g) & (rows % HEAD_DIM == cols)).astype(BF16)
    rv = rk[:, :, :HEAD_DIM].transpose(0, 2, 1)
    return rk, rv


def _prompt_attention(qrot, k, v, sel_t=None):
    bsz, t, _ = qrot.shape
    tq, ck = 256, 256
    gw = Q_PER_KV * HEAD_DIM
    mode = "win" if sel_t is None else "sel"
    assert t % tq == 0 and tq % ck == 0 and WINDOW % ck == 0 and ck % L_SEL == 0
    rk, rv = _head_pick_matrices()
    in_specs = [pl.BlockSpec((1, tq, gw), lambda b, g, i: (b, i, g)),
                pl.BlockSpec((1, t, KV_WIDTH), lambda b, g, i: (b, 0, 0)),
                pl.BlockSpec((1, t, KV_WIDTH), lambda b, g, i: (b, 0, 0)),
                pl.BlockSpec((1, KV_WIDTH, LANES), lambda b, g, i: (g, 0, 0)),
                pl.BlockSpec((1, HEAD_DIM, KV_WIDTH), lambda b, g, i: (g, 0, 0))]
    args = [qrot, k, v, rk, rv]
    if mode == "sel":
        ns_pad = sel_t.shape[2]
        et = (jnp.arange(t)[:, None] // L_SEL == jnp.arange(ns_pad)[None, :]).astype(BF16).reshape(t // ck, ck, ns_pad)
        in_specs += [pl.BlockSpec((1, 1, ns_pad, tq), lambda b, g, i: (b, g, 0, i)),
                     pl.BlockSpec((t // ck, ck, ns_pad), lambda b, g, i: (0, 0, 0))]
        args += [sel_t, et]
    return pl.pallas_call(
        functools.partial(_pattn_kernel, mode=mode, t=t, tq=tq, ck=ck),
        grid=(bsz, KV_HEADS, t // tq),
        in_specs=in_specs,
        out_specs=pl.BlockSpec((1, tq, gw), lambda b, g, i: (b, i, g)),
        out_shape=jax.ShapeDtypeStruct((bsz, t, NSA_WIDTH), F32),
        scratch_shapes=[pltpu.VMEM((t // ck, ck, LANES), BF16), pltpu.VMEM((t // ck, HEAD_DIM, ck), BF16),
                        pltpu.VMEM((Q_PER_KV, 1, tq), F32), pltpu.VMEM((Q_PER_KV, 1, tq), F32),
                        pltpu.VMEM((Q_PER_KV, HEAD_DIM, tq), F32)],
        compiler_params=_cparams(("parallel", "parallel", "arbitrary")),
        name="prompt_attn_" + mode,
    )(*args)


def _sattn_kernel(pt_ref, qbd_ref, kpool_ref, vpool_ref, ksn_ref, vsn_ref, selc_ref, wk_ref, wv_ref, kwn_ref, vwn_ref,
                  osel_ref, owin_ref, kbuf, vbuf, m_scr, l_scr, acc_scr, sem, *, n_pages, ppc, page, tdec):
    b = pl.program_id(0)
    c = pl.program_id(1)
    nb = pl.num_programs(0)
    nch = n_pages // ppc
    step = b * nch + c
    slot = step % 2
    rows = ppc * page
    ncol = qbd_ref.shape[2]

    def copies(bb, cc, sl, p):
        dst_k = kbuf.at[sl, :, pl.ds(pl.multiple_of(p * page, page), page)]
        dst_v = vbuf.at[sl, :, pl.ds(pl.multiple_of(p * page, page), page)]
        pid = pt_ref[bb, cc * ppc + p]
        return (pltpu.make_async_copy(kpool_ref.at[pid], dst_k, sem.at[0, sl]),
                pltpu.make_async_copy(vpool_ref.at[pid], dst_v, sem.at[1, sl]))

    def start_all(bb, cc, sl):
        def body(p, carry):
            ck, cv = copies(bb, cc, sl, p)
            ck.start()
            cv.start()
            return carry
        lax.fori_loop(0, ppc, body, 0)

    def wait_all(bb, cc, sl):
        def body(p, carry):
            ck, cv = copies(bb, cc, sl, p)
            ck.wait()
            cv.wait()
            return carry
        lax.fori_loop(0, ppc, body, 0)

    @pl.when(step == 0)
    def _():
        start_all(0, 0, 0)

    @pl.when(step + 1 < nb * nch)
    def _():
        nxt = step + 1
        start_all(nxt // nch, nxt % nch, 1 - slot)

    wait_all(b, c, slot)

    @pl.when(c == 0)
    def _():
        m_scr[...] = jnp.full(m_scr.shape, NEG_INF, F32)
        l_scr[...] = jnp.zeros(l_scr.shape, F32)
        acc_scr[...] = jnp.zeros(acc_scr.shape, F32)

    qbd = qbd_ref[0]
    nblk = rows // L_SEL
    s3 = (_dot_tn(kbuf[slot], qbd) * ATTN_SCALE).reshape(nblk, L_SEL, ncol)
    blk0 = pl.multiple_of(c * nblk, SUBLANES)
    chosen = (selc_ref[0, pl.ds(blk0, nblk), :] > 0.5)[:, None, :]
    s3 = jnp.where(chosen, s3, NEG_INF)
    m_old = m_scr[...]
    m_new = jnp.maximum(m_old, jnp.max(jnp.max(s3, axis=0), axis=0, keepdims=True))
    e3 = jnp.where(chosen, jnp.exp(s3 - m_new[None]), 0.0)
    alpha = jnp.exp(m_old - m_new)
    l_scr[...] = l_scr[...] * alpha + jnp.sum(jnp.sum(e3, axis=0), axis=0, keepdims=True)
    m_scr[...] = m_new
    acc_scr[...] = acc_scr[...] * alpha + _dot(vbuf[slot], e3.reshape(rows, ncol))

    @pl.when(c == nch - 1)
    def _():
        tcol = lax.broadcasted_iota(jnp.int32, (tdec, ncol), 1) % tdec
        jrow = lax.broadcasted_iota(jnp.int32, (tdec, ncol), 0)
        causal_new = jrow <= tcol
        sel_last = selc_ref[0, pl.ds(nch * nblk, 1), :] > 0.5
        ok_new = causal_new & sel_last
        s_new = jnp.where(ok_new, _dot(ksn_ref[0], qbd) * ATTN_SCALE, NEG_INF)
        m_old2 = m_scr[...]
        m_fin = jnp.maximum(m_old2, jnp.max(s_new, axis=0, keepdims=True))
        e_new = jnp.where(ok_new, jnp.exp(s_new - m_fin), 0.0)
        alpha2 = jnp.exp(m_old2 - m_fin)
        l_fin = l_scr[...] * alpha2 + jnp.sum(e_new, axis=0, keepdims=True)
        inv = 1.0 / jnp.maximum(l_fin, 1e-30)
        acc = acc_scr[...] * alpha2 + _dot_tn(vsn_ref[0], e_new)
        osel_ref[0] = acc * inv

        wb = wk_ref.shape[2]
        jw = lax.broadcasted_iota(jnp.int32, (wb, ncol), 0)
        tw = lax.broadcasted_iota(jnp.int32, (wb, ncol), 1) % tdec
        ok_c = jw + (WINDOW - wb) > tw
        s_c = jnp.where(ok_c, _dot_tn(wk_ref[0], qbd) * ATTN_SCALE, NEG_INF)
        s_n = jnp.where(causal_new, _dot(kwn_ref[0], qbd) * ATTN_SCALE, NEG_INF)
        m_w = jnp.maximum(jnp.max(s_c, axis=0, keepdims=True), jnp.max(s_n, axis=0, keepdims=True))
        e_c = jnp.where(ok_c, jnp.exp(s_c - m_w), 0.0)
        e_n = jnp.where(causal_new, jnp.exp(s_n - m_w), 0.0)
        inv_w = 1.0 / jnp.maximum(jnp.sum(e_c, axis=0, keepdims=True) + jnp.sum(e_n, axis=0, keepdims=True), 1e-30)
        ow = _dot(wv_ref[0], e_c) + _dot_tn(vwn_ref[0], e_n)
        owin_ref[0] = ow * inv_w


def _sample_attention(qrot, kpool, vpool, page_table, ks_new, vs_new, sel_t, wk, wv, kw_new, vw_new):
    bsz, tdec, _ = qrot.shape
    n_pages = page_table.shape[1]
    page = kpool.shape[2]
    ppc = min(n_pages, 32)
    assert n_pages % ppc == 0 and page % L_SEL == 0
    nch = n_pages // ppc
    rows = ppc * page
    ncol = KV_HEADS * Q_PER_KV * tdec
    ns_pad = sel_t.shape[2]
    wb = wk.shape[2]
    q5 = qrot.reshape(bsz, tdec, KV_HEADS, Q_PER_KV, HEAD_DIM).transpose(0, 2, 4, 3, 1)
    eye = jnp.eye(KV_HEADS, dtype=qrot.dtype)
    qbd = (q5[:, :, :, None] * eye[None, :, None, :, None, None]).reshape(bsz, KV_WIDTH, ncol)
    selc = jnp.broadcast_to(sel_t.transpose(0, 2, 1, 3)[:, :, :, None, :], (bsz, ns_pad, KV_HEADS, Q_PER_KV, tdec))
    selc = selc.reshape(bsz, ns_pad, ncol)
    per_b = lambda shape: pl.BlockSpec((1,) + shape, lambda b, c, pt: (b, 0, 0))
    grid_spec = pltpu.PrefetchScalarGridSpec(
        num_scalar_prefetch=1,
        grid=(bsz, nch),
        in_specs=[per_b((KV_WIDTH, ncol)),
                  pl.BlockSpec(memory_space=pl.ANY), pl.BlockSpec(memory_space=pl.ANY),
                  per_b((tdec, KV_WIDTH)), per_b((tdec, KV_WIDTH)),
                  per_b((ns_pad, ncol)),
                  per_b((KV_WIDTH, wb)), per_b((KV_WIDTH, wb)),
                  per_b((tdec, KV_WIDTH)), per_b((tdec, KV_WIDTH))],
        out_specs=[per_b((KV_WIDTH, ncol)), per_b((KV_WIDTH, ncol))],
        scratch_shapes=[pltpu.VMEM((2, KV_WIDTH, rows), F32), pltpu.VMEM((2, KV_WIDTH, rows), F32),
                        pltpu.VMEM((1, ncol), F32), pltpu.VMEM((1, ncol), F32), pltpu.VMEM((KV_WIDTH, ncol), F32),
                        pltpu.SemaphoreType.DMA((2, 2))],
    )
    o_sel, o_win = pl.pallas_call(
        functools.partial(_sattn_kernel, n_pages=n_pages, ppc=ppc, page=page, tdec=tdec),
        grid_spec=grid_spec,
        out_shape=[jax.ShapeDtypeStruct((bsz, KV_WIDTH, ncol), F32)] * 2,
        compiler_params=_cparams(("arbitrary", "arbitrary")),
        name="sample_attn",
    )(page_table, qbd, kpool, vpool, ks_new, vs_new, selc, wk, wv, kw_new, vw_new)

    def unpack(o):
        o6 = o.reshape(bsz, KV_HEADS, HEAD_DIM, KV_HEADS, Q_PER_KV, tdec)
        diag = jnp.stack([o6[:, g, :, g] for g in range(KV_HEADS)], axis=1)
        return diag.transpose(0, 4, 1, 3, 2).reshape(bsz, tdec, NSA_WIDTH)

    return unpack(o_sel), unpack(o_win)


def _mix_kernel(x_ref, oc_ref, os_ref, ow_ref, gt_ref, u_ref, v_ref, wsm_ref, bs_ref, gn_ref, gs_ref, eg_ref, wout_ref,
                o_ref, *, chunk):
    r = x_ref.shape[0]
    g = gt_ref[...]
    g_hi = g.astype(BF16)
    g_lo = (g - g_hi.astype(F32)).astype(BF16)
    onsa = None
    for j, branch in enumerate((oc_ref, os_ref, ow_ref)):
        ge = _dot(g_hi, eg_ref[j]) + _dot(g_lo, eg_ref[j])
        term = ge * branch[...]
        onsa = term if onsa is None else onsa + term
    onsa = _rms(onsa, gn_ref[...])
    ii = lax.broadcasted_iota(jnp.int32, (r, r), 0)
    jj = lax.broadcasted_iota(jnp.int32, (r, r), 1)
    tri = (ii // chunk == jj // chunk) & (jj % chunk <= ii % chunk)
    cols = []
    for gi in range(SGU_GROUPS):
        sl = slice(gi * SGU_GROUP_DIM, (gi + 1) * SGU_GROUP_DIM)
        ws = jnp.where(tri, wsm_ref[gi], 0.0)
        mixed = _dot(ws, v_ref[:, sl]) + bs_ref[:, gi:gi + 1]
        cols.append(u_ref[:, sl] * mixed)
    osgu = _rms(jnp.concatenate(cols, axis=1), gs_ref[...])
    o_ref[...] = x_ref[...] + _dot(jnp.concatenate([onsa, osgu], axis=1), wout_ref[...])


def _gate_expanders():
    c = jnp.arange(GATE_COLS)[None, :, None]
    lane = jnp.arange(NSA_WIDTH)[None, None, :]
    j = jnp.arange(3)[:, None, None]
    return (c == (lane // HEAD_DIM) * 3 + j).astype(BF16)


def _mix(x2d, o_cmp, o_sel, o_win, gates, u, v, w_sgu, b_sgu, g_nsa_out, g_sgu_out, w_out_b, chunk, r):
    n, d = x2d.shape
    rep = r // chunk
    wsm = jnp.tile(w_sgu[:, :chunk, :chunk], (1, rep, rep))
    bs = jnp.tile(b_sgu[:, :chunk].T, (rep, 1))
    row = lambda w: pl.BlockSpec((r, w), lambda i: (i, 0))
    return pl.pallas_call(
        functools.partial(_mix_kernel, chunk=chunk),
        grid=(n // r,),
        in_specs=[row(d), row(NSA_WIDTH), row(NSA_WIDTH), row(NSA_WIDTH), row(GATE_COLS), row(1024), row(1024),
                  _const_spec(wsm.shape), _const_spec(bs.shape), _const_spec((1, NSA_WIDTH)), _const_spec((1, 1024)),
                  _const_spec((3, GATE_COLS, NSA_WIDTH)), _const_spec(w_out_b.shape)],
        out_specs=row(d),
        out_shape=jax.ShapeDtypeStruct((n, d), F32),
        compiler_params=_cparams(("parallel",)),
        name="mix_out_proj",
    )(x2d, o_cmp, o_sel, o_win, gates, u, v, wsm, bs, g_nsa_out, g_sgu_out, _gate_expanders(), w_out_b)


def _rms_matmul_kernel(x_ref, g_ref, w_ref, o_ref):
    o_ref[...] = _dot(_rms(x_ref[...], g_ref[...]), w_ref[...])


def _rms_matmul(x2d, g, w_b, tm):
    n, d = x2d.shape
    m = w_b.shape[1]
    return pl.pallas_call(
        _rms_matmul_kernel,
        grid=(n // tm,),
        in_specs=[pl.BlockSpec((tm, d), lambda i: (i, 0)), _const_spec((1, d)), _const_spec(w_b.shape)],
        out_specs=pl.BlockSpec((tm, m), lambda i: (i, 0)),
        out_shape=jax.ShapeDtypeStruct((n, m), F32),
        compiler_params=_cparams(("parallel",)),
        name="rms_matmul",
    )(x2d, g, w_b)


def _matmul_res_kernel(a_ref, w_ref, r_ref, o_ref):
    o_ref[...] = r_ref[...] + _dot(a_ref[...], w_ref[...])


def _matmul_res(a2d, w_b, res, tm):
    n, k = a2d.shape
    m = w_b.shape[1]
    return pl.pallas_call(
        _matmul_res_kernel,
        grid=(n // tm,),
        in_specs=[pl.BlockSpec((tm, k), lambda i: (i, 0)), _const_spec(w_b.shape), pl.BlockSpec((tm, m), lambda i: (i, 0))],
        out_specs=pl.BlockSpec((tm, m), lambda i: (i, 0)),
        out_shape=jax.ShapeDtypeStruct((n, m), F32),
        compiler_params=_cparams(("parallel",)),
        name="matmul_residual",
    )(a2d, w_b, res)


def _memattn_kernel(q_ref, k_ref, v_ref, o_ref):
    q = q_ref[0]
    k = k_ref[0]
    v = v_ref[0]
    outs = []
    for h in range(MEM_HEADS):
        sl = slice(h * MEM_HEAD_DIM, (h + 1) * MEM_HEAD_DIM)
        s = _dot_nt(q[:, sl], k[:, sl]) * MEM_SCALE
        p = _masked_softmax(s, jnp.ones(s.shape, dtype=jnp.bool_))
        outs.append(_dot(p, v[:, sl]))
    o_ref[0] = jnp.concatenate(outs, axis=1)


def _mem_attention(hq, mk, mv):
    bsz, t, w = hq.shape
    m = mk.shape[1]
    tq = min(t, 512)
    return pl.pallas_call(
        _memattn_kernel,
        grid=(bsz, t // tq),
        in_specs=[pl.BlockSpec((1, tq, w), lambda b, i: (b, i, 0)),
                  pl.BlockSpec((1, m, w), lambda b, i: (b, 0, 0)),
                  pl.BlockSpec((1, m, w), lambda b, i: (b, 0, 0))],
        out_specs=pl.BlockSpec((1, tq, w), lambda b, i: (b, i, 0)),
        out_shape=jax.ShapeDtypeStruct((bsz, t, w), F32),
        compiler_params=_cparams(("parallel", "parallel")),
        name="mem_attn",
    )(hq, mk, mv)


def _router_logits(h, wr_ref):
    return jnp.dot(h, wr_ref[...], precision=lax.Precision.HIGHEST, preferred_element_type=F32)


def _router_kernel(x_ref, g_ref, wr_ref, gtop_ref):
    z = _router_logits(_rms(x_ref[...], g_ref[...]), wr_ref)
    lane = lax.broadcasted_iota(jnp.int32, z.shape, 1)
    zg = jnp.where(lane < N_GROUPS, z, -jnp.inf)
    m = jnp.max(zg, axis=1, keepdims=True)
    first = jnp.min(jnp.where(zg == m, lane.astype(F32), float(LANES)), axis=1, keepdims=True)
    gtop_ref[...] = first.astype(jnp.int32)


def _route_groups(x2d, g_moe, wr, tm):
    n, d = x2d.shape
    return pl.pallas_call(
        _router_kernel,
        grid=(n // tm,),
        in_specs=[pl.BlockSpec((tm, d), lambda i: (i, 0)), _const_spec((1, d)), _const_spec(wr.shape)],
        out_specs=pl.BlockSpec((tm, 1), lambda i: (i, 0)),
        out_shape=jax.ShapeDtypeStruct((n, 1), jnp.int32),
        compiler_params=_cparams(("parallel",)),
        name="route_groups",
    )(x2d, g_moe, wr)


def _moe_kernel(src_ref, tg_ref, tv_ref, nv_ref, widx_ref, x_hbm, gm_ref, gf_ref, wr_ref, wg_ref, wu_ref, wd_ref, y_hbm,
                buf, hb_scr, w4_scr, gsem, ssem, *, tm):
    t = pl.program_id(0)
    e = pl.program_id(1)
    nt = pl.num_programs(0)
    slot = t % 2
    other = 1 - slot
    valid = tv_ref[t] == 1
    prev_valid = (t >= 1) & (tv_ref[jnp.maximum(t - 1, 0)] == 1)
    next_valid = (t + 1 < nt) & (tv_ref[jnp.minimum(t + 1, nt - 1)] == 1)

    def gather_start(tt, sl):
        def body(r, c):
            idx = jnp.maximum(src_ref[tt * tm + r], 0)
            pltpu.make_async_copy(x_hbm.at[pl.ds(idx, 1), :], buf.at[sl, pl.ds(r, 1), :], gsem.at[sl]).start()
            return c
        lax.fori_loop(0, tm, body, 0, unroll=8)

    def gather_wait(sl):
        pltpu.make_async_copy(x_hbm.at[pl.ds(0, tm), :], buf.at[sl], gsem.at[sl]).wait()

    def scatter_copy(tt, sl, r):
        return pltpu.make_async_copy(buf.at[sl, pl.ds(r, 1), :], y_hbm.at[pl.ds(src_ref[tt * tm + r], 1), :], ssem.at[0])

    def scatter_start(tt, sl):
        def body(r, c):
            scatter_copy(tt, sl, r).start()
            return c
        lax.fori_loop(0, nv_ref[tt], body, 0)

    def scatter_wait(tt, sl):
        def body(r, c):
            scatter_copy(tt, sl, r).wait()
            return c
        lax.fori_loop(0, nv_ref[tt], body, 0)

    @pl.when(e == 0)
    def _():
        @pl.when(t == 0)
        def _():
            gather_start(0, 0)

        @pl.when(prev_valid)
        def _():
            scatter_start(t - 1, other)

        @pl.when(valid)
        def _():
            gather_wait(slot)
            h = _rms(buf[slot], gm_ref[...])
            hb_scr[...] = h.astype(BF16)
            z = _router_logits(h, wr_ref)
            lane = lax.broadcasted_iota(jnp.int32, z.shape, 1)
            lanef = lane.astype(F32)
            grp = lane < N_GROUPS
            zg = jnp.where(grp, z, -jnp.inf)
            pg_top = 1.0 / jnp.sum(jnp.where(grp, jnp.exp(zg - jnp.max(zg, axis=1, keepdims=True)), 0.0), axis=1, keepdims=True)
            lo = N_GROUPS + tg_ref[t] * EXPERTS_PER_GROUP
            ing = (lane >= lo) & (lane < lo + EXPERTS_PER_GROUP)
            pf = _masked_softmax(z, ing)
            big = float(2 * LANES)
            m1 = jnp.max(jnp.where(ing, pf, -1.0), axis=1, keepdims=True)
            i1 = jnp.min(jnp.where(ing & (pf == m1), lanef, big), axis=1, keepdims=True)
            rest = ing & (lanef != i1)
            m2 = jnp.max(jnp.where(rest, pf, -1.0), axis=1, keepdims=True)
            i2 = jnp.min(jnp.where(rest & (pf == m2), lanef, big), axis=1, keepdims=True)
            tot = m1 + m2
            w4_scr[...] = jnp.where(lanef == i1, m1 / tot * pg_top, jnp.where(lanef == i2, m2 / tot * pg_top, 0.0))

    @pl.when(e == EXPERTS_PER_GROUP // 2)
    def _():
        @pl.when(prev_valid)
        def _():
            scatter_wait(t - 1, other)

        @pl.when(next_valid)
        def _():
            gather_start(t + 1, other)

    @pl.when(valid)
    def _():
        hb = hb_scr[...]
        lane = lax.broadcasted_iota(jnp.int32, w4_scr.shape, 1)
        col = N_GROUPS + tg_ref[t] * EXPERTS_PER_GROUP + e
        we = jnp.sum(jnp.where(lane == col, w4_scr[...], 0.0), axis=1, keepdims=True)
        act = jax.nn.silu(_dot(hb, wg_ref[0])) * _dot(hb, wu_ref[0])
        buf[slot] += _dot(act * we, wd_ref[0])

    @pl.when(valid & (e == EXPERTS_PER_GROUP - 1))
    def _():
        buf[slot] = _rms(buf[slot], gf_ref[...])


def _moe_final(x2d, g_moe, g_final, wr, w_gate, w_up, w_down, tm):
    n, d = x2d.shape
    g_top = _route_groups(x2d, g_moe, wr, min(n, 512))[:, 0]
    n_tiles = n // tm + N_GROUPS
    onehot = (g_top[:, None] == jnp.arange(N_GROUPS)[None, :]).astype(jnp.int32)
    counts = jnp.sum(onehot, axis=0)
    rank = jnp.sum((jnp.cumsum(onehot, axis=0) - onehot) * onehot, axis=1)
    padded = (counts + tm - 1) // tm * tm
    ends = jnp.cumsum(padded)
    base = ends - padded
    pos = base[g_top] + rank
    src = jnp.full((n_tiles * tm,), -1, jnp.int32).at[pos].set(jnp.arange(n, dtype=jnp.int32))
    tile_start = jnp.arange(n_tiles, dtype=jnp.int32) * tm
    tile_valid = (tile_start < ends[-1]).astype(jnp.int32)
    tile_group = jnp.minimum(jnp.sum((tile_start[:, None] >= ends[None, :]).astype(jnp.int32), axis=1), N_GROUPS - 1)
    tile_rows = jnp.clip(ends[tile_group] - padded[tile_group] + counts[tile_group] - tile_start, 0, tm)
    tile_rows = (tile_rows * tile_valid).astype(jnp.int32)
    n_valid = ends[-1] // tm
    last_group = tile_group[jnp.maximum(n_valid - 1, 0)]
    eidx = tile_group[:, None] * EXPERTS_PER_GROUP + jnp.arange(EXPERTS_PER_GROUP, dtype=jnp.int32)[None, :]
    widx = jnp.where(tile_valid[:, None] == 1, eidx, last_group * EXPERTS_PER_GROUP + EXPERTS_PER_GROUP - 1)
    widx = widx.reshape(-1).astype(jnp.int32)

    wmap = lambda t, e, src, tg, tv, nv, wi: (wi[t * EXPERTS_PER_GROUP + e], 0, 0)
    cmap = lambda t, e, src, tg, tv, nv, wi: (0, 0)
    grid_spec = pltpu.PrefetchScalarGridSpec(
        num_scalar_prefetch=5,
        grid=(n_tiles, EXPERTS_PER_GROUP),
        in_specs=[pl.BlockSpec(memory_space=pl.ANY),
                  pl.BlockSpec((1, d), cmap), pl.BlockSpec((1, d), cmap), pl.BlockSpec(wr.shape, cmap),
                  pl.BlockSpec((1, d, EXPERT_FF), wmap), pl.BlockSpec((1, d, EXPERT_FF), wmap),
                  pl.BlockSpec((1, EXPERT_FF, d), wmap)],
        out_specs=pl.BlockSpec(memory_space=pl.ANY),
        scratch_shapes=[pltpu.VMEM((2, tm, d), F32), pltpu.VMEM((tm, d), BF16), pltpu.VMEM((tm, LANES), F32),
                        pltpu.SemaphoreType.DMA((2,)), pltpu.SemaphoreType.DMA((1,))],
    )
    return pl.pallas_call(
        functools.partial(_moe_kernel, tm=tm),
        grid_spec=grid_spec,
        out_shape=jax.ShapeDtypeStruct((n, d), F32),
        compiler_params=_cparams(("arbitrary", "arbitrary")),
        name="moe_final_norm",
    )(src, tile_group, tile_valid, tile_rows, widx, x2d, g_moe, g_final, wr, w_gate, w_up, w_down)


def _finish(x2d, o_cmp, o_sel, o_win, gates, u, v, mk, mv, bsz, lw, chunk, moe_tm):
    n, d = x2d.shape
    t = n // bsz
    x1 = _mix(x2d, o_cmp, o_sel, o_win, gates, u, v, lw["w_sgu"], lw["b_sgu"], lw["g_nsa_out"], lw["g_sgu_out"],
              lw["w_out"], chunk, min(n, 256))
    hq = _rms_matmul(x1, lw["g_mem_norm"], lw["w_mem_q"], min(n, 512))
    o_m = _mem_attention(hq.reshape(bsz, t, -1), mk, mv)
    x2 = _matmul_res(o_m.reshape(n, -1), lw["w_mem_o"], x1, min(n, 512))
    return _moe_final(x2, lw["g_moe_norm"], lw["g_final"], lw["w_router"], lw["w_exp_gate"], lw["w_exp_up"],
                      lw["w_exp_down"], moe_tm)


def kernel(x_prompt, x_sample, cache_cmp_k, cache_cmp_v, cache_sel_k, cache_sel_v, cache_win_k, cache_win_v, cache_mem_k, cache_mem_v, page_table, mem_prompt, w_in, g_attn_norm, pe_cmp_k, w_cmp_k1, w_cmp_k2, pe_cmp_v, w_cmp_v1, w_cmp_v2, g_sgu_v, w_sgu, b_sgu, g_nsa_out, g_sgu_out, w_out, g_mem_norm, g_mem_src, w_mem_q, w_mem_k, w_mem_v, w_mem_o, g_moe_norm, w_router_group, w_router_expert, w_exp_gate, w_exp_up, w_exp_down, g_final):
    depth = w_in.shape[0]
    assert depth == 1, "single-layer trunk"
    bp, tp, d = x_prompt.shape
    bs, ts, _ = x_sample.shape
    n_pages = page_table.shape[1]
    page = cache_cmp_k.shape[2]
    past = n_pages * page
    assert ts < STRIDE and tp % 256 == 0 and page % STRIDE == 0
    row = lambda a: a[0].reshape(1, -1)

    wi = w_in[0]
    c_q = NSA_WIDTH + 6 * KV_WIDTH
    n_gate = 3 * N_HEADS
    w_in_p = jnp.concatenate([wi[:, :c_q + n_gate], jnp.zeros((d, GATE_COLS - n_gate), F32), wi[:, c_q + n_gate:]],
                             axis=1).astype(BF16)
    wr = jnp.concatenate([w_router_group[0], w_router_expert[0],
                          jnp.zeros((d, LANES - N_GROUPS - N_EXPERTS), F32)], axis=1)
    lw = {
        "w_sgu": w_sgu[0], "b_sgu": b_sgu[0], "g_nsa_out": row(g_nsa_out), "g_sgu_out": row(g_sgu_out),
        "w_out": w_out[0].astype(BF16), "g_mem_norm": row(g_mem_norm), "w_mem_q": w_mem_q[0].astype(BF16),
        "w_mem_o": w_mem_o[0].astype(BF16), "g_moe_norm": row(g_moe_norm), "g_final": g_final.reshape(1, -1),
        "w_router": wr, "w_exp_gate": w_exp_gate[0], "w_exp_up": w_exp_up[0], "w_exp_down": w_exp_down[0],
    }
    g_attn = row(g_attn_norm)
    gsv = row(g_sgu_v)
    kv5 = lambda a, b, t: a.reshape(1, b, t, KV_HEADS, HEAD_DIM)

    np_ = bp * tp
    tabs_p = _rope_tables(jnp.arange(tp, dtype=jnp.int32))
    (q, qrot, kc, vc, ks, vs, kw, vw, gates, u, v) = _project(x_prompt.reshape(np_, d), g_attn, w_in_p, tabs_p, gsv, 256)
    pt_p = jnp.arange(np_ // page, dtype=jnp.int32).reshape(bp, tp // page)
    kcmp = _compress(kc.reshape(np_ // page, page, KV_WIDTH), pt_p, pe_cmp_k[0], w_cmp_k1[0], w_cmp_k2[0], False)
    vcmp = _compress(vc.reshape(np_ // page, page, KV_WIDTH), pt_p, pe_cmp_v[0], w_cmp_v1[0], w_cmp_v2[0], False)
    o_cmp, sel_t = _cmp_select(q.reshape(bp, tp, -1), kcmp, vcmp, 0, tp)
    qrot3 = qrot.reshape(bp, tp, -1)
    ks3, vs3, kw3, vw3 = (a.reshape(bp, tp, KV_WIDTH) for a in (ks, vs, kw, vw))
    o_sel = _prompt_attention(qrot3, ks3, vs3, sel_t)
    o_win = _prompt_attention(qrot3, kw3, vw3)
    n_mem = mem_prompt.shape[1]
    w_mem_kv = jnp.concatenate([w_mem_k[0], w_mem_v[0]], axis=1).astype(BF16)
    mkv = _rms_matmul(mem_prompt.reshape(bp * n_mem, d), row(g_mem_src), w_mem_kv, min(bp * n_mem, 512))
    mem_w = MEM_HEADS * MEM_HEAD_DIM
    mk_p = mkv[:, :mem_w].reshape(bp, n_mem, mem_w)
    mv_p = mkv[:, mem_w:].reshape(bp, n_mem, mem_w)
    y_p = _finish(x_prompt.reshape(np_, d), o_cmp.reshape(np_, -1), o_sel.reshape(np_, -1), o_win.reshape(np_, -1),
                  gates, u, v, mk_p, mv_p, bp, lw, CHUNK, min(np_, 512))
    wbp = min(WINDOW, tp)
    outs_p = (kv5(kc, bp, tp), kv5(vc, bp, tp), kv5(ks, bp, tp), kv5(vs, bp, tp),
              kv5(kw, bp, tp)[:, :, -wbp:], kv5(vw, bp, tp)[:, :, -wbp:],
              mk_p.reshape(1, bp, n_mem, MEM_HEADS, MEM_HEAD_DIM), mv_p.reshape(1, bp, n_mem, MEM_HEADS, MEM_HEAD_DIM))

    ns_ = bs * ts
    pos_s = past + jnp.arange(ts, dtype=jnp.int32)
    tabs_s = tuple(jnp.tile(a, (bs, 1)) for a in _rope_tables(pos_s))
    (q, qrot, kc, vc, ks, vs, kw, vw, gates, u, v) = _project(x_sample.reshape(ns_, d), g_attn, w_in_p, tabs_s, gsv, ns_)
    pool = lambda c: c[0].transpose(0, 2, 3, 1).reshape(c.shape[1], KV_WIDTH, page)
    kcmp = _compress(pool(cache_cmp_k), page_table, pe_cmp_k[0], w_cmp_k1[0], w_cmp_k2[0], True)
    vcmp = _compress(pool(cache_cmp_v), page_table, pe_cmp_v[0], w_cmp_v1[0], w_cmp_v2[0], True)
    o_cmp, sel_t = _cmp_select(q.reshape(bs, ts, -1).astype(F32), kcmp, vcmp, past, past + ts)
    wb = cache_win_k.shape[2]
    wk = cache_win_k[0].transpose(0, 2, 3, 1).reshape(bs, KV_WIDTH, wb)
    wv = cache_win_v[0].transpose(0, 2, 3, 1).reshape(bs, KV_WIDTH, wb)
    ks3, vs3, kw3, vw3 = (a.reshape(bs, ts, KV_WIDTH) for a in (ks, vs, kw, vw))
    o_sel, o_win = _sample_attention(qrot.reshape(bs, ts, -1), pool(cache_sel_k), pool(cache_sel_v), page_table,
                                     ks3, vs3, sel_t, wk, wv, kw3, vw3)
    mem_ks = cache_mem_k[0].reshape(bs, cache_mem_k.shape[2], mem_w)
    mem_vs = cache_mem_v[0].reshape(bs, cache_mem_v.shape[2], mem_w)
    y_s = _finish(x_sample.reshape(ns_, d), o_cmp.reshape(ns_, -1), o_sel.reshape(ns_, -1), o_win.reshape(ns_, -1),
                  gates, u, v, mem_ks, mem_vs, bs, lw, ts, min(ns_, 128))
    def slide(cache_t, new):
        win = jnp.concatenate([cache_t, new.transpose(0, 2, 1)], axis=2)[:, :, -wb:]
        return win.reshape(bs, KV_HEADS, HEAD_DIM, wb).transpose(0, 3, 1, 2)[None]

    win_k_s = slide(wk, kw3)
    win_v_s = slide(wv, vw3)
    outs_s = (kv5(kc, bs, ts), kv5(vc, bs, ts), kv5(ks, bs, ts), kv5(vs, bs, ts), win_k_s, win_v_s,
              v.reshape(1, bs, ts, -1))

    return (y_p.reshape(bp, tp, d), y_s.reshape(bs, ts, d)) + outs_p + outs_s
```

```python
import functools

import jax
import jax.numpy as jnp
from jax import lax
from jax.experimental import pallas as pl
from jax.experimental.pallas import tpu as pltpu

F32 = jnp.float32
BF16 = jnp.bfloat16

N_HEADS = 16
HEAD_DIM = 64
KV_HEADS = 4
Q_PER_KV = N_HEADS // KV_HEADS
NSA_WIDTH = N_HEADS * HEAD_DIM
KV_WIDTH = KV_HEADS * HEAD_DIM
ROT_DIM = HEAD_DIM // 4
ROPE_THETA = 500000.0
L_CMP = 32
STRIDE = 16
CMP_R = L_CMP // STRIDE
CMP_HIDDEN = 128
L_SEL = 64
SEL_RATIO = L_SEL // STRIDE
SEL_INNER = (L_SEL - L_CMP) // STRIDE + 1
TOP_N = 16
WINDOW = 512
SGU_GROUPS = 8
SGU_GROUP_DIM = 128
CHUNK = 128
MEM_HEADS = 4
MEM_HEAD_DIM = 128
N_GROUPS = 4
EXPERTS_PER_GROUP = 4
N_EXPERTS = N_GROUPS * EXPERTS_PER_GROUP
EXPERT_FF = 512
EPS = 1e-6
NEG_INF = -1e30
BIG = 1e9
PAD_SCORE = -3e38
CMP_ROW_PITCH = STRIDE + 4
ATTN_SCALE = HEAD_DIM ** -0.5
MEM_SCALE = MEM_HEAD_DIM ** -0.5

LANES = 128
SUBLANES = 8
V7X_VMEM_LIMIT_BYTES = 60000 * 1024

GATE_COLS = LANES
N_GATES = 3 * N_HEADS
_SEG = {}
_off = 0
for _name, _w in (("q", NSA_WIDTH), ("kc", KV_WIDTH), ("vc", KV_WIDTH), ("ks", KV_WIDTH), ("vs", KV_WIDTH),
                  ("kw", KV_WIDTH), ("vw", KV_WIDTH), ("gt", N_GATES), ("u", 1024), ("v", 1024)):
    _SEG[_name] = (_off, _w)
    _off += _w
IN_COLS = _off


def _cparams(sem, vmem=V7X_VMEM_LIMIT_BYTES):
    return pltpu.CompilerParams(dimension_semantics=sem, vmem_limit_bytes=vmem)


def _dot(a, b):
    return jnp.dot(a.astype(BF16), b.astype(BF16), preferred_element_type=F32)


def _dot_nt(a, b):
    return lax.dot_general(a.astype(BF16), b.astype(BF16), (((1,), (1,)), ((), ())), preferred_element_type=F32)


def _dot_tn(a, b):
    return lax.dot_general(a.astype(BF16), b.astype(BF16), (((0,), (0,)), ((), ())), preferred_element_type=F32)


def _rms(x, g):
    return x * lax.rsqrt(jnp.mean(x * x, axis=-1, keepdims=True) + EPS) * g


def _masked_softmax(s, mask, axis=-1):
    s = jnp.where(mask, s, NEG_INF)
    m = jnp.max(s, axis=axis, keepdims=True)
    e = jnp.where(mask, jnp.exp(s - m), 0.0)
    return e * (1.0 / jnp.maximum(jnp.sum(e, axis=axis, keepdims=True), 1e-30))


def _const_spec(shape):
    nd = len(shape)
    return pl.BlockSpec(shape, lambda *_: (0,) * nd, pipeline_mode=pl.Buffered(1))


def _proj_kernel(x_ref, g_ref, w_ref, rc_ref, rs1_ref, rs2_ref, gsv_ref,
                 q_ref, qrot_ref, kc_ref, vc_ref, ks_ref, vs_ref, kw_ref, vw_ref, gates_ref, u_ref, v_ref):
    hb = _rms(x_ref[...], g_ref[...]).astype(BF16)

    def seg(name):
        lo, width = _SEG[name]
        return jnp.dot(hb, w_ref[:, lo:lo + width], preferred_element_type=F32)

    def rope(z):
        rc, rs1, rs2 = rc_ref[...], rs1_ref[...], rs2_ref[...]
        half = ROT_DIM // 2
        outs = []
        for c in range(z.shape[1] // LANES):
            zc = z[:, c * LANES:(c + 1) * LANES]
            outs.append(zc * rc + pltpu.roll(zc, LANES - half, 1) * rs1 + pltpu.roll(zc, half, 1) * rs2)
        return jnp.concatenate(outs, axis=1)

    q = seg("q")
    q_ref[...] = q.astype(BF16)
    qrot_ref[...] = rope(q).astype(BF16)
    kc_ref[...] = seg("kc")
    vc_ref[...] = seg("vc")
    ks_ref[...] = rope(seg("ks"))
    vs_ref[...] = seg("vs")
    kw_ref[...] = rope(seg("kw"))
    vw_ref[...] = seg("vw")
    g0 = _SEG["gt"][0]
    zt = jnp.dot(hb, w_ref[:, g0:IN_COLS], preferred_element_type=F32)
    u0, v0 = _SEG["u"][0] - g0, _SEG["v"][0] - g0
    gates_ref[...] = jax.nn.sigmoid(zt[:, :GATE_COLS])
    u_ref[...] = jax.nn.gelu(zt[:, u0:u0 + 1024])
    v_ref[...] = _rms(jax.nn.gelu(zt[:, v0:v0 + 1024]), gsv_ref[...])


def _project(x2d, g_attn, w_in_p, tables, g_sgu_v, tm):
    n, d = x2d.shape
    rc, rs1, rs2 = tables
    tt = rc.shape[0]
    nt = tt // tm
    row = lambda w: pl.BlockSpec((tm, w), lambda i: (i, 0))
    tab = pl.BlockSpec((tm, LANES), lambda i: (i % nt, 0))
    out_shapes = [jax.ShapeDtypeStruct((n, NSA_WIDTH), BF16), jax.ShapeDtypeStruct((n, NSA_WIDTH), BF16)]
    out_shapes += [jax.ShapeDtypeStruct((n, KV_WIDTH), F32)] * 6
    out_shapes += [jax.ShapeDtypeStruct((n, GATE_COLS), F32), jax.ShapeDtypeStruct((n, 1024), F32),
                   jax.ShapeDtypeStruct((n, 1024), F32)]
    out_specs = [row(NSA_WIDTH), row(NSA_WIDTH)] + [row(KV_WIDTH)] * 6 + [row(GATE_COLS), row(1024), row(1024)]
    return pl.pallas_call(
        _proj_kernel,
        grid=(n // tm,),
        in_specs=[row(d), _const_spec((1, d)), _const_spec(w_in_p.shape), tab, tab, tab, _const_spec((1, 1024))],
        out_specs=out_specs,
        out_shape=out_shapes,
        compiler_params=_cparams(("parallel",)),
        name="in_proj",
    )(x2d, g_attn, w_in_p, rc, rs1, rs2, g_sgu_v)


def _rope_tables(pos):
    half = ROT_DIM // 2
    freqs = ROPE_THETA ** (-jnp.arange(half, dtype=F32) / half)
    ang = pos.astype(F32)[:, None] * freqs[None, :]
    cos, sin = jnp.cos(ang), jnp.sin(ang)
    t = pos.shape[0]
    ones = jnp.ones((t, HEAD_DIM - ROT_DIM), F32)
    zeros = jnp.zeros((t, HEAD_DIM - ROT_DIM), F32)
    zh = jnp.zeros((t, half), F32)
    rc = jnp.concatenate([cos, cos, ones], axis=1)
    rs1 = jnp.concatenate([-sin, zh, zeros], axis=1)
    rs2 = jnp.concatenate([zh, sin, zeros], axis=1)
    rep = LANES // HEAD_DIM
    return tuple(jnp.tile(a, (1, rep)) for a in (rc, rs1, rs2))


def _compress_kernel(pt_ref, pool_ref, pe_ref, w1_ref, w2_ref, out_ref, *scratch, n_pages, page, transposed):
    if transposed:
        stage, buf, a_ref, sem = scratch
    else:
        buf, a_ref, sem = scratch
    b = pl.program_id(0)
    nb = pl.num_programs(0)
    n_rows = n_pages * page
    n_sub = n_rows // STRIDE
    slot = b % 2
    n_ct = KV_WIDTH // LANES

    def page_copies(bb, sl, p):
        row0 = pl.multiple_of(p * page, page)
        pid = pt_ref[bb, p]
        if transposed:
            return [pltpu.make_async_copy(pool_ref.at[pid], stage.at[sl, :, pl.ds(row0, page)], sem.at[sl])]
        return [pltpu.make_async_copy(pool_ref.at[pid, :, pl.ds(c * LANES, LANES)],
                                      buf.at[sl, c, pl.ds(row0, page), :], sem.at[sl]) for c in range(n_ct)]

    def start_all(bb, sl):
        def body(p, carry):
            for cp in page_copies(bb, sl, p):
                cp.start()
            return carry
        lax.fori_loop(0, n_pages, body, 0)

    def wait_all(bb, sl):
        def body(p, carry):
            for cp in page_copies(bb, sl, p):
                cp.wait()
            return carry
        lax.fori_loop(0, n_pages, body, 0)

    @pl.when(b == 0)
    def _():
        start_all(0, 0)

    @pl.when(b + 1 < nb)
    def _():
        start_all(b + 1, 1 - slot)

    wait_all(b, slot)

    pitch = CMP_ROW_PITCH if transposed else STRIDE
    if transposed:
        tch = min(n_rows, 512)
        for c in range(n_ct):
            for j in range(n_rows // tch):
                rows_t = stage[slot, c * LANES:(c + 1) * LANES, j * tch:(j + 1) * tch].T
                for n in range(tch // STRIDE):
                    r0 = (j * (tch // STRIDE) + n) * pitch
                    buf[c, r0:r0 + STRIDE, :] = rows_t[n * STRIDE:(n + 1) * STRIDE, :]

    pev =_dot(pe_ref[...], w1_ref[...])
    low = lax.broadcasted_iota(jnp.int32, (n_sub, LANES), 1) < HEAD_DIM
    row = lax.broadcasted_iota(jnp.int32, (n_sub, 4 * HEAD_DIM), 0)
    for c in range(n_ct):
        src = buf.at[c] if transposed else buf.at[slot, c]
        for sp in range(STRIDE // 2):
            x0 = src[pl.ds(2 * sp, n_sub, stride=pitch), :]
            x1 = src[pl.ds(2 * sp + 1, n_sub, stride=pitch), :]
            a_ref[0, :, sp * LANES:(sp + 1) * LANES] = jnp.where(low, x0, pltpu.roll(x1, HEAD_DIM, 1)).astype(BF16)
            a_ref[1, :, sp * LANES:(sp + 1) * LANES] = jnp.where(low, pltpu.roll(x0, HEAD_DIM, 1), x1).astype(BF16)
        for k in range(2):
            pm = jnp.dot(a_ref[k], w1_ref[...], preferred_element_type=F32)
            part0 = pm[:, :CMP_HIDDEN] + pev[0:1, :CMP_HIDDEN]
            part1 = pm[:, CMP_HIDDEN:] + pev[1:2, CMP_HIDDEN:]
            hsum = part0 + pltpu.roll(part1, n_sub - 1, 0)
            o = _dot(jax.nn.silu(hsum), w2_ref[...])
            out_ref[0, 2 * c + k] = jnp.where(row < n_sub - 1, o, 0.0).astype(BF16)


def _compress(pool, page_table, pe, w1, w2, transposed):
    bsz, n_pages = page_table.shape
    page = pool.shape[2] if transposed else pool.shape[1]
    n_rows = n_pages * page
    n_sub = n_rows // STRIDE
    kdim = STRIDE * HEAD_DIM
    n_ct = KV_WIDTH // LANES
    pe8 = jnp.zeros((SUBLANES, kdim), F32).at[:CMP_R].set(pe.reshape(CMP_R, kdim))
    w1c = w1.reshape(CMP_R, kdim, CMP_HIDDEN).transpose(1, 0, 2).reshape(kdim, CMP_R * CMP_HIDDEN).astype(BF16)
    w2t = jnp.tile(w2, (1, 4)).astype(BF16)
    if transposed:
        bufs = [pltpu.VMEM((2, KV_WIDTH, n_rows), F32), pltpu.VMEM((n_ct, n_sub * CMP_ROW_PITCH, LANES), F32)]
    else:
        bufs = [pltpu.VMEM((2, n_ct, n_rows, LANES), F32)]
    grid_spec = pltpu.PrefetchScalarGridSpec(
        num_scalar_prefetch=1,
        grid=(bsz,),
        in_specs=[pl.BlockSpec(memory_space=pl.ANY),
                  pl.BlockSpec((SUBLANES, kdim), lambda b, pt: (0, 0)),
                  pl.BlockSpec(w1c.shape, lambda b, pt: (0, 0)),
                  pl.BlockSpec(w2t.shape, lambda b, pt: (0, 0))],
        out_specs=pl.BlockSpec((1, KV_HEADS, n_sub, 4 * HEAD_DIM), lambda b, pt: (b, 0, 0, 0)),
        scratch_shapes=bufs + [pltpu.VMEM((2, n_sub, kdim), BF16), pltpu.SemaphoreType.DMA((2,))],
    )
    return pl.pallas_call(
        functools.partial(_compress_kernel, n_pages=n_pages, page=page, transposed=transposed),
        grid_spec=grid_spec,
        out_shape=jax.ShapeDtypeStruct((bsz, KV_HEADS, n_sub, 4 * HEAD_DIM), BF16),
        compiler_params=_cparams(("arbitrary",)),
        name="compress",
    )(page_table, pool, pe8, w1c, w2t)


def _head_blockdiag(x, n_heads_in_lanes=Q_PER_KV):
    n = x.shape[0]
    lane_head = lax.broadcasted_iota(jnp.int32, (n, n_heads_in_lanes * HEAD_DIM), 1) // HEAD_DIM
    xf = x.astype(F32)
    return jnp.concatenate([jnp.where(lane_head == qi, xf, 0.0) for qi in range(n_heads_in_lanes)], axis=0).astype(BF16)


def _cmpsel_kernel(q_ref, k_ref, v_ref, st_ref, o_ref, sc_ref, *, pos0, nc, ns, tq):
    i = pl.program_id(2)
    q = q_ref[0]
    n_sub = k_ref.shape[2]
    ns_pad = st_ref.shape[0]
    kbd = _head_blockdiag(k_ref[0, 0])
    vbd = _head_blockdiag(v_ref[0, 0])
    s_all = _dot_nt(q, kbd) * ATTN_SCALE
    qpos = pos0 + i * tq + lax.broadcasted_iota(jnp.int32, (tq, 1), 0)
    kidx = lax.broadcasted_iota(jnp.int32, (1, n_sub), 1)
    mask = (kidx * STRIDE + (L_CMP - 1) <= qpos) & (kidx < nc)
    probs = []
    for qi in range(Q_PER_KV):
        probs.append(_masked_softmax(s_all[:, qi * n_sub:(qi + 1) * n_sub], mask))
    o_ref[0] = _dot(jnp.concatenate(probs, axis=1), vbd)
    pg = probs[0]
    for p in probs[1:]:
        pg = pg + p
    pg_hi = pg.astype(BF16)
    pg_lo = (pg - pg_hi.astype(F32)).astype(BF16)
    st = st_ref[...]
    ps_t = _dot_nt(st, pg_hi) + _dot_nt(st, pg_lo)
    blk = lax.broadcasted_iota(jnp.int32, (ns_pad, tq), 0)
    qpos_t = pos0 + i * tq + lax.broadcasted_iota(jnp.int32, (ns_pad, tq), 1)
    cur = qpos_t // L_SEL
    forced = (blk == 0) | (blk == cur) | (blk == cur - 1)
    future = blk * L_SEL > qpos_t
    sc = jnp.where(forced, BIG, jnp.where(future, -BIG, ps_t))
    sc_ref[0, 0] = jnp.where(blk < ns, sc, PAD_SCORE)


def _rank_kernel(sc_ref, sel_ref, *, ns, topn):
    sc = sc_ref[0, 0]
    blk = lax.broadcasted_iota(jnp.int32, sc.shape, 0)
    sel_ref[0, 0] = jnp.zeros(sc.shape, F32)

    def body(j, c):
        row = sc_ref[0, 0, pl.ds(j, 1), :]
        ahead = (sc > row) | ((sc == row) & (blk < j))
        rank = jnp.sum(ahead.astype(F32), axis=0, keepdims=True)
        sel_ref[0, 0, pl.ds(j, 1), :] = (rank < topn).astype(F32)
        return c

    lax.fori_loop(0, ns, body, 0)


def _cmp_select(q, kcmp, vcmp, pos0, n_keys):
    bsz, t, _ = q.shape
    n_sub = kcmp.shape[2]
    nc = n_sub - CMP_R + 1
    ns = -(-n_keys // L_SEL)
    ns_pad = -(-ns // SUBLANES) * SUBLANES
    topn = min(TOP_N, ns)
    tq = min(t, 512)
    gw = Q_PER_KV * HEAD_DIM
    cidx = jnp.arange(n_sub)
    st = ((cidx[None, :] // SEL_RATIO == jnp.arange(ns_pad)[:, None]) & (cidx[None, :] % SEL_RATIO < SEL_INNER)
          & (cidx[None, :] < nc)).astype(BF16)
    o_cmp, sc = pl.pallas_call(
        functools.partial(_cmpsel_kernel, pos0=pos0, nc=nc, ns=ns, tq=tq),
        grid=(bsz, KV_HEADS, t // tq),
        in_specs=[pl.BlockSpec((1, tq, gw), lambda b, g, i: (b, i, g)),
                  pl.BlockSpec((1, 1, n_sub, gw), lambda b, g, i: (b, g, 0, 0)),
                  pl.BlockSpec((1, 1, n_sub, gw), lambda b, g, i: (b, g, 0, 0)),
                  pl.BlockSpec((ns_pad, n_sub), lambda b, g, i: (0, 0))],
        out_specs=[pl.BlockSpec((1, tq, gw), lambda b, g, i: (b, i, g)),
                   pl.BlockSpec((1, 1, ns_pad, tq), lambda b, g, i: (b, g, 0, i))],
        out_shape=[jax.ShapeDtypeStruct((bsz, t, NSA_WIDTH), F32),
                   jax.ShapeDtypeStruct((bsz, KV_HEADS, ns_pad, t), F32)],
        compiler_params=_cparams(("parallel", "parallel", "parallel")),
        name="cmp_scores",
    )(q, kcmp, vcmp, st)
    fold = t < LANES
    if fold:
        sc = sc.transpose(2, 0, 1, 3).reshape(1, 1, ns_pad, bsz * KV_HEADS * t)
    nb, ng, _, width = sc.shape
    tl = min(width, 512)
    sel = pl.pallas_call(
        functools.partial(_rank_kernel, ns=ns, topn=topn),
        grid=(nb, ng, width // tl),
        in_specs=[pl.BlockSpec((1, 1, ns_pad, tl), lambda b, g, i: (b, g, 0, i))],
        out_specs=pl.BlockSpec((1, 1, ns_pad, tl), lambda b, g, i: (b, g, 0, i)),
        out_shape=jax.ShapeDtypeStruct(sc.shape, F32),
        compiler_params=_cparams(("parallel", "parallel", "parallel")),
        name="rank_select",
    )(sc)
    if fold:
        sel = sel.reshape(ns_pad, bsz, KV_HEADS, t).transpose(1, 2, 0, 3)
    return o_cmp, sel


def _pattn_kernel(*refs, mode, t, tq, ck):
    if mode == "sel":
        q_ref, k_ref, v_ref, rk_ref, rv_ref, sel_ref, et_ref, o_ref, kc_scr, vt_scr, m_scr, l_scr, acc_scr = refs
    else:
        q_ref, k_ref, v_ref, rk_ref, rv_ref, o_ref, kc_scr, vt_scr, m_scr, l_scr, acc_scr = refs
    i = pl.program_id(2)
    n_chunks = t // ck
    last = (i + 1) * (tq // ck) - 1

    @pl.when(i == 0)
    def _():
        kg = _dot(k_ref[0], rk_ref[0]).astype(BF16)
        vg = _dot_nt(rv_ref[0], v_ref[0]).astype(BF16)
        for c in range(n_chunks):
            kc_scr[c] = kg[c * ck:(c + 1) * ck, :]
            vt_scr[c] = vg[:, c * ck:(c + 1) * ck]

    qf = q_ref[0].astype(F32) * ATTN_SCALE
    q_heads = []
    for qi in range(Q_PER_KV):
        tile = qf[:, (qi // 2) * LANES:(qi // 2 + 1) * LANES]
        q_heads.append((tile if qi % 2 == 0 else pltpu.roll(tile, HEAD_DIM, 1)).astype(BF16))
    q_all = jnp.concatenate(q_heads, axis=0)
    qpos = i * tq + lax.broadcasted_iota(jnp.int32, (1, tq), 1)
    m_scr[...] = jnp.full(m_scr.shape, NEG_INF, F32)
    l_scr[...] = jnp.zeros(l_scr.shape, F32)
    acc_scr[...] = jnp.zeros(acc_scr.shape, F32)
    if mode == "sel":
        sel_b = sel_ref[0, 0].astype(BF16)
        n_steps = last + 1
    else:
        n_steps = last - jnp.maximum(i * tq - WINDOW, 0) // ck + 1

    def chunk(c):
        kpos = c * ck + lax.broadcasted_iota(jnp.int32, (ck, 1), 0)
        if mode == "sel":
            chosen = jnp.dot(et_ref[c], sel_b, preferred_element_type=F32)
            ok = (chosen > 0.5) & (kpos <= qpos)
        else:
            ok = (kpos <= qpos) & (kpos > qpos - WINDOW)
        bias = jnp.where(ok, 0.0, NEG_INF)
        s = _dot_nt(kc_scr[c], q_all) + jnp.concatenate([bias] * Q_PER_KV, axis=1)
        m_old = m_scr[...]
        m_new = jnp.maximum(m_old, jnp.max(s, axis=0, keepdims=True))
        e = jnp.exp(s - m_new)
        alpha = jnp.exp(m_old - m_new)
        l_scr[...] = l_scr[...] * alpha + jnp.sum(e, axis=0, keepdims=True)
        m_scr[...] = m_new
        acc_scr[...] = acc_scr[...] * alpha + jnp.dot(vt_scr[c], e.astype(BF16), preferred_element_type=F32)

    def body(step, carry):
        chunk(last - step)
        return carry

    lax.fori_loop(0, n_steps, body, 0)
    out = acc_scr[...] * (1.0 / jnp.maximum(l_scr[...], 1e-30))
    o_ref[0] = jnp.concatenate([out[:, qi * tq:(qi + 1) * tq] for qi in range(Q_PER_KV)], axis=0).T


def _head_pick_matrices():
    rows = jnp.arange(KV_WIDTH)[None, :, None]
    cols = jnp.arange(LANES)[None, None, :]
    g = jnp.arange(KV_HEADS)[:, None, None]
    rk = ((rows // HEAD_DIM == g) & (rows % HEAD_DIM == cols)).astype(BF16)
    rv = rk[:, :, :HEAD_DIM].transpose(0, 2, 1)
    return rk, rv


def _prompt_attention(qrot, k, v, sel_t=None):
    bsz, t, _ = qrot.shape
    tq, ck = 256, 256
    gw = Q_PER_KV * HEAD_DIM
    mode = "win" if sel_t is None else "sel"
    assert t % tq == 0 and tq % ck == 0 and WINDOW % ck == 0 and ck % L_SEL == 0
    rk, rv = _head_pick_matrices()
    in_specs = [pl.BlockSpec((1, tq, gw), lambda b, g, i: (b, i, g)),
                pl.BlockSpec((1, t, KV_WIDTH), lambda b, g, i: (b, 0, 0)),
                pl.BlockSpec((1, t, KV_WIDTH), lambda b, g, i: (b, 0, 0)),
                pl.BlockSpec((1, KV_WIDTH, LANES), lambda b, g, i: (g, 0, 0)),
                pl.BlockSpec((1, HEAD_DIM, KV_WIDTH), lambda b, g, i: (g, 0, 0))]
    args = [qrot, k, v, rk, rv]
    if mode == "sel":
        ns_pad = sel_t.shape[2]
        et = (jnp.arange(t)[:, None] // L_SEL == jnp.arange(ns_pad)[None, :]).astype(BF16).reshape(t // ck, ck, ns_pad)
        in_specs += [pl.BlockSpec((1, 1, ns_pad, tq), lambda b, g, i: (b, g, 0, i)),
                     pl.BlockSpec((t // ck, ck, ns_pad), lambda b, g, i: (0, 0, 0))]
        args += [sel_t, et]
    return pl.pallas_call(
        functools.partial(_pattn_kernel, mode=mode, t=t, tq=tq, ck=ck),
        grid=(bsz, KV_HEADS, t // tq),
        in_specs=in_specs,
        out_specs=pl.BlockSpec((1, tq, gw), lambda b, g, i: (b, i, g)),
        out_shape=jax.ShapeDtypeStruct((bsz, t, NSA_WIDTH), F32),
        scratch_shapes=[pltpu.VMEM((t // ck, ck, LANES), BF16), pltpu.VMEM((t // ck, HEAD_DIM, ck), BF16),
                        pltpu.VMEM((1, Q_PER_KV * tq), F32), pltpu.VMEM((1, Q_PER_KV * tq), F32),
                        pltpu.VMEM((HEAD_DIM, Q_PER_KV * tq), F32)],
        compiler_params=_cparams(("parallel", "parallel", "arbitrary")),
        name="prompt_attn_" + mode,
    )(*args)


def _sattn_kernel(pt_ref, qbd_ref, kpool_ref, vpool_ref, ksn_ref, vsn_ref, selc_ref, wk_ref, wv_ref, kwn_ref, vwn_ref,
                  osel_ref, owin_ref, kbuf, vbuf, m_scr, l_scr, acc_scr, sem, *, n_pages, ppc, page, tdec):
    b = pl.program_id(0)
    c = pl.program_id(1)
    nb = pl.num_programs(0)
    nch = n_pages // ppc
    step = b * nch + c
    slot = step % 2
    rows = ppc * page
    ncol = qbd_ref.shape[2]

    def copies(bb, cc, sl, p):
        dst_k = kbuf.at[sl, :, pl.ds(pl.multiple_of(p * page, page), page)]
        dst_v = vbuf.at[sl, :, pl.ds(pl.multiple_of(p * page, page), page)]
        pid = pt_ref[bb, cc * ppc + p]
        return (pltpu.make_async_copy(kpool_ref.at[pid], dst_k, sem.at[0, sl]),
                pltpu.make_async_copy(vpool_ref.at[pid], dst_v, sem.at[1, sl]))

    def start_all(bb, cc, sl):
        def body(p, carry):
            ck, cv = copies(bb, cc, sl, p)
            ck.start()
            cv.start()
            return carry
        lax.fori_loop(0, ppc, body, 0)

    def wait_all(bb, cc, sl):
        def body(p, carry):
            ck, cv = copies(bb, cc, sl, p)
            ck.wait()
            cv.wait()
            return carry
        lax.fori_loop(0, ppc, body, 0)

    @pl.when(step == 0)
    def _():
        start_all(0, 0, 0)

    @pl.when(step + 1 < nb * nch)
    def _():
        nxt = step + 1
        start_all(nxt // nch, nxt % nch, 1 - slot)

    wait_all(b, c, slot)

    @pl.when(c == 0)
    def _():
        m_scr[...] = jnp.full(m_scr.shape, NEG_INF, F32)
        l_scr[...] = jnp.zeros(l_scr.shape, F32)
        acc_scr[...] = jnp.zeros(acc_scr.shape, F32)

    qbd = qbd_ref[0]
    nblk = rows // L_SEL
    s3 = (_dot_tn(kbuf[slot], qbd) * ATTN_SCALE).reshape(nblk, L_SEL, ncol)
    blk0 = pl.multiple_of(c * nblk, SUBLANES)
    chosen = (selc_ref[0, pl.ds(blk0, nblk), :] > 0.5)[:, None, :]
    s3 = jnp.where(chosen, s3, NEG_INF)
    m_old = m_scr[...]
    m_new = jnp.maximum(m_old, jnp.max(jnp.max(s3, axis=0), axis=0, keepdims=True))
    e3 = jnp.where(chosen, jnp.exp(s3 - m_new[None]), 0.0)
    alpha = jnp.exp(m_old - m_new)
    l_scr[...] = l_scr[...] * alpha + jnp.sum(jnp.sum(e3, axis=0), axis=0, keepdims=True)
    m_scr[...] = m_new
    acc_scr[...] = acc_scr[...] * alpha + _dot(vbuf[slot], e3.reshape(rows, ncol))

    @pl.when(c == nch - 1)
    def _():
        tcol = lax.broadcasted_iota(jnp.int32, (tdec, ncol), 1) % tdec
        jrow = lax.broadcasted_iota(jnp.int32, (tdec, ncol), 0)
        causal_new = jrow <= tcol
        sel_last = selc_ref[0, pl.ds(nch * nblk, 1), :] > 0.5
        ok_new = causal_new & sel_last
        s_new = jnp.where(ok_new, _dot(ksn_ref[0], qbd) * ATTN_SCALE, NEG_INF)
        m_old2 = m_scr[...]
        m_fin = jnp.maximum(m_old2, jnp.max(s_new, axis=0, keepdims=True))
        e_new = jnp.where(ok_new, jnp.exp(s_new - m_fin), 0.0)
        alpha2 = jnp.exp(m_old2 - m_fin)
        l_fin = l_scr[...] * alpha2 + jnp.sum(e_new, axis=0, keepdims=True)
        inv = 1.0 / jnp.maximum(l_fin, 1e-30)
        acc = acc_scr[...] * alpha2 + _dot_tn(vsn_ref[0], e_new)
        osel_ref[0] = acc * inv

        wb = wk_ref.shape[2]
        jw = lax.broadcasted_iota(jnp.int32, (wb, ncol), 0)
        tw = lax.broadcasted_iota(jnp.int32, (wb, ncol), 1) % tdec
        ok_c = jw + (WINDOW - wb) > tw
        s_c = jnp.where(ok_c, _dot_tn(wk_ref[0], qbd) * ATTN_SCALE, NEG_INF)
        s_n = jnp.where(causal_new, _dot(kwn_ref[0], qbd) * ATTN_SCALE, NEG_INF)
        m_w = jnp.maximum(jnp.max(s_c, axis=0, keepdims=True), jnp.max(s_n, axis=0, keepdims=True))
        e_c = jnp.where(ok_c, jnp.exp(s_c - m_w), 0.0)
        e_n = jnp.where(causal_new, jnp.exp(s_n - m_w), 0.0)
        inv_w = 1.0 / jnp.maximum(jnp.sum(e_c, axis=0, keepdims=True) + jnp.sum(e_n, axis=0, keepdims=True), 1e-30)
        ow = _dot(wv_ref[0], e_c) + _dot_tn(vwn_ref[0], e_n)
        owin_ref[0] = ow * inv_w


def _sample_attention(qrot, kpool, vpool, page_table, ks_new, vs_new, sel_t, wk, wv, kw_new, vw_new):
    bsz, tdec, _ = qrot.shape
    n_pages = page_table.shape[1]
    page = kpool.shape[2]
    ppc = min(n_pages, 32)
    assert n_pages % ppc == 0 and page % L_SEL == 0
    nch = n_pages // ppc
    rows = ppc * page
    ncol = KV_HEADS * Q_PER_KV * tdec
    ns_pad = sel_t.shape[2]
    wb = wk.shape[2]
    q5 = qrot.reshape(bsz, tdec, KV_HEADS, Q_PER_KV, HEAD_DIM).transpose(0, 2, 4, 3, 1)
    eye = jnp.eye(KV_HEADS, dtype=qrot.dtype)
    qbd = (q5[:, :, :, None] * eye[None, :, None, :, None, None]).reshape(bsz, KV_WIDTH, ncol)
    selc = jnp.broadcast_to(sel_t.transpose(0, 2, 1, 3)[:, :, :, None, :], (bsz, ns_pad, KV_HEADS, Q_PER_KV, tdec))
    selc = selc.reshape(bsz, ns_pad, ncol)
    per_b = lambda shape: pl.BlockSpec((1,) + shape, lambda b, c, pt: (b, 0, 0))
    grid_spec = pltpu.PrefetchScalarGridSpec(
        num_scalar_prefetch=1,
        grid=(bsz, nch),
        in_specs=[per_b((KV_WIDTH, ncol)),
                  pl.BlockSpec(memory_space=pl.ANY), pl.BlockSpec(memory_space=pl.ANY),
                  per_b((tdec, KV_WIDTH)), per_b((tdec, KV_WIDTH)),
                  per_b((ns_pad, ncol)),
                  per_b((KV_WIDTH, wb)), per_b((KV_WIDTH, wb)),
                  per_b((tdec, KV_WIDTH)), per_b((tdec, KV_WIDTH))],
        out_specs=[per_b((KV_WIDTH, ncol)), per_b((KV_WIDTH, ncol))],
        scratch_shapes=[pltpu.VMEM((2, KV_WIDTH, rows), F32), pltpu.VMEM((2, KV_WIDTH, rows), F32),
                        pltpu.VMEM((1, ncol), F32), pltpu.VMEM((1, ncol), F32), pltpu.VMEM((KV_WIDTH, ncol), F32),
                        pltpu.SemaphoreType.DMA((2, 2))],
    )
    o_sel, o_win = pl.pallas_call(
        functools.partial(_sattn_kernel, n_pages=n_pages, ppc=ppc, page=page, tdec=tdec),
        grid_spec=grid_spec,
        out_shape=[jax.ShapeDtypeStruct((bsz, KV_WIDTH, ncol), F32)] * 2,
        compiler_params=_cparams(("arbitrary", "arbitrary")),
        name="sample_attn",
    )(page_table, qbd, kpool, vpool, ks_new, vs_new, selc, wk, wv, kw_new, vw_new)

    def unpack(o):
        o6 = o.reshape(bsz, KV_HEADS, HEAD_DIM, KV_HEADS, Q_PER_KV, tdec)
        diag = jnp.stack([o6[:, g, :, g] for g in range(KV_HEADS)], axis=1)
        return diag.transpose(0, 4, 1, 3, 2).reshape(bsz, tdec, NSA_WIDTH)

    return unpack(o_sel), unpack(o_win)


def _mix_kernel(x_ref, oc_ref, os_ref, ow_ref, gt_ref, u_ref, v_ref, wsm_ref, bs_ref, gn_ref, gs_ref, eg_ref, wout_ref,
                o_ref, *, chunk):
    r = x_ref.shape[0]
    g = gt_ref[...]
    g_hi = g.astype(BF16)
    g_lo = (g - g_hi.astype(F32)).astype(BF16)
    onsa = None
    for j, branch in enumerate((oc_ref, os_ref, ow_ref)):
        ge = _dot(g_hi, eg_ref[j]) + _dot(g_lo, eg_ref[j])
        term = ge * branch[...]
        onsa = term if onsa is None else onsa + term
    onsa = _rms(onsa, gn_ref[...])
    ii = lax.broadcasted_iota(jnp.int32, (r, r), 0)
    jj = lax.broadcasted_iota(jnp.int32, (r, r), 1)
    tri = (ii // chunk == jj // chunk) & (jj % chunk <= ii % chunk)
    cols = []
    for gi in range(SGU_GROUPS):
        sl = slice(gi * SGU_GROUP_DIM, (gi + 1) * SGU_GROUP_DIM)
        ws = jnp.where(tri, wsm_ref[gi], 0.0)
        mixed = _dot(ws, v_ref[:, sl]) + bs_ref[:, gi:gi + 1]
        cols.append(u_ref[:, sl] * mixed)
    osgu = _rms(jnp.concatenate(cols, axis=1), gs_ref[...])
    o_ref[...] = x_ref[...] + _dot(jnp.concatenate([onsa, osgu], axis=1), wout_ref[...])


def _gate_expanders():
    c = jnp.arange(GATE_COLS)[None, :, None]
    lane = jnp.arange(NSA_WIDTH)[None, None, :]
    j = jnp.arange(3)[:, None, None]
    return (c == (lane // HEAD_DIM) * 3 + j).astype(BF16)


def _mix(x2d, o_cmp, o_sel, o_win, gates, u, v, w_sgu, b_sgu, g_nsa_out, g_sgu_out, w_out_b, chunk, r):
    n, d = x2d.shape
    rep = r // chunk
    wsm = jnp.tile(w_sgu[:, :chunk, :chunk], (1, rep, rep))
    bs = jnp.tile(b_sgu[:, :chunk].T, (rep, 1))
    row = lambda w: pl.BlockSpec((r, w), lambda i: (i, 0))
    return pl.pallas_call(
        functools.partial(_mix_kernel, chunk=chunk),
        grid=(n // r,),
        in_specs=[row(d), row(NSA_WIDTH), row(NSA_WIDTH), row(NSA_WIDTH), row(GATE_COLS), row(1024), row(1024),
                  _const_spec(wsm.shape), _const_spec(bs.shape), _const_spec((1, NSA_WIDTH)), _const_spec((1, 1024)),
                  _const_spec((3, GATE_COLS, NSA_WIDTH)), _const_spec(w_out_b.shape)],
        out_specs=row(d),
        out_shape=jax.ShapeDtypeStruct((n, d), F32),
        compiler_params=_cparams(("parallel",)),
        name="mix_out_proj",
    )(x2d, o_cmp, o_sel, o_win, gates, u, v, wsm, bs, g_nsa_out, g_sgu_out, _gate_expanders(), w_out_b)


def _rms_matmul_kernel(x_ref, g_ref, w_ref, o_ref):
    o_ref[...] = _dot(_rms(x_ref[...], g_ref[...]), w_ref[...])


def _rms_matmul(x2d, g, w_b, tm):
    n, d = x2d.shape
    m = w_b.shape[1]
    return pl.pallas_call(
        _rms_matmul_kernel,
        grid=(n // tm,),
        in_specs=[pl.BlockSpec((tm, d), lambda i: (i, 0)), _const_spec((1, d)), _const_spec(w_b.shape)],
        out_specs=pl.BlockSpec((tm, m), lambda i: (i, 0)),
        out_shape=jax.ShapeDtypeStruct((n, m), F32),
        compiler_params=_cparams(("parallel",)),
        name="rms_matmul",
    )(x2d, g, w_b)


def _matmul_res_kernel(a_ref, w_ref, r_ref, o_ref):
    o_ref[...] = r_ref[...] + _dot(a_ref[...], w_ref[...])


def _matmul_res(a2d, w_b, res, tm):
    n, k = a2d.shape
    m = w_b.shape[1]
    return pl.pallas_call(
        _matmul_res_kernel,
        grid=(n // tm,),
        in_specs=[pl.BlockSpec((tm, k), lambda i: (i, 0)), _const_spec(w_b.shape), pl.BlockSpec((tm, m), lambda i: (i, 0))],
        out_specs=pl.BlockSpec((tm, m), lambda i: (i, 0)),
        out_shape=jax.ShapeDtypeStruct((n, m), F32),
        compiler_params=_cparams(("parallel",)),
        name="matmul_residual",
    )(a2d, w_b, res)


def _memattn_kernel(q_ref, k_ref, v_ref, o_ref):
    q = q_ref[0]
    k = k_ref[0]
    v = v_ref[0]
    outs = []
    for h in range(MEM_HEADS):
        sl = slice(h * MEM_HEAD_DIM, (h + 1) * MEM_HEAD_DIM)
        s = _dot_nt(q[:, sl], k[:, sl]) * MEM_SCALE
        p = _masked_softmax(s, jnp.ones(s.shape, dtype=jnp.bool_))
        outs.append(_dot(p, v[:, sl]))
    o_ref[0] = jnp.concatenate(outs, axis=1)


def _mem_attention(hq, mk, mv):
    bsz, t, w = hq.shape
    m = mk.shape[1]
    tq = min(t, 512)
    return pl.pallas_call(
        _memattn_kernel,
        grid=(bsz, t // tq),
        in_specs=[pl.BlockSpec((1, tq, w), lambda b, i: (b, i, 0)),
                  pl.BlockSpec((1, m, w), lambda b, i: (b, 0, 0)),
                  pl.BlockSpec((1, m, w), lambda b, i: (b, 0, 0))],
        out_specs=pl.BlockSpec((1, tq, w), lambda b, i: (b, i, 0)),
        out_shape=jax.ShapeDtypeStruct((bsz, t, w), F32),
        compiler_params=_cparams(("parallel", "parallel")),
        name="mem_attn",
    )(hq, mk, mv)


def _router_logits(h, wr_ref):
    h_hi = h.astype(BF16)
    h_lo = (h - h_hi.astype(F32)).astype(BF16)
    w_hi, w_lo = wr_ref[0], wr_ref[1]
    return _dot(h_hi, w_hi) + _dot(h_hi, w_lo) + _dot(h_lo, w_hi)


def _router_kernel(x_ref, g_ref, wr_ref, gtop_ref):
    z = _router_logits(_rms(x_ref[...], g_ref[...]), wr_ref)
    lane = lax.broadcasted_iota(jnp.int32, z.shape, 1)
    zg = jnp.where(lane < N_GROUPS, z, -jnp.inf)
    m = jnp.max(zg, axis=1, keepdims=True)
    first = jnp.min(jnp.where(zg == m, lane.astype(F32), float(LANES)), axis=1, keepdims=True)
    gtop_ref[...] = first.astype(jnp.int32)


def _route_groups(x2d, g_moe, wr, tm):
    n, d = x2d.shape
    return pl.pallas_call(
        _router_kernel,
        grid=(n // tm,),
        in_specs=[pl.BlockSpec((tm, d), lambda i: (i, 0)), _const_spec((1, d)), _const_spec(wr.shape)],
        out_specs=pl.BlockSpec((tm, 1), lambda i: (i, 0)),
        out_shape=jax.ShapeDtypeStruct((n, 1), jnp.int32),
        compiler_params=_cparams(("parallel",)),
        name="route_groups",
    )(x2d, g_moe, wr)


def _moe_kernel(src_ref, tg_ref, tv_ref, nv_ref, widx_ref, x_hbm, gm_ref, gf_ref, wr_ref, wg_ref, wu_ref, wd_ref, y_hbm,
                buf, hb_scr, w4_scr, gsem, ssem, *, tm):
    t = pl.program_id(0)
    e = pl.program_id(1)
    nt = pl.num_programs(0)
    slot = t % 2
    other = 1 - slot
    valid = tv_ref[t] == 1
    prev_valid = (t >= 1) & (tv_ref[jnp.maximum(t - 1, 0)] == 1)
    next_valid = (t + 1 < nt) & (tv_ref[jnp.minimum(t + 1, nt - 1)] == 1)

    def gather_start(tt, sl):
        def body(r, c):
            idx = jnp.maximum(src_ref[tt * tm + r], 0)
            pltpu.make_async_copy(x_hbm.at[pl.ds(idx, 1), :], buf.at[sl, pl.ds(r, 1), :], gsem.at[sl]).start()
            return c
        lax.fori_loop(0, tm, body, 0, unroll=8)

    def gather_wait(sl):
        pltpu.make_async_copy(x_hbm.at[pl.ds(0, tm), :], buf.at[sl], gsem.at[sl]).wait()

    def scatter_copy(tt, sl, r):
        return pltpu.make_async_copy(buf.at[sl, pl.ds(r, 1), :], y_hbm.at[pl.ds(src_ref[tt * tm + r], 1), :], ssem.at[0])

    def scatter_start(tt, sl):
        def body(r, c):
            scatter_copy(tt, sl, r).start()
            return c
        lax.fori_loop(0, nv_ref[tt], body, 0)

    def scatter_wait(tt, sl):
        def body(r, c):
            scatter_copy(tt, sl, r).wait()
            return c
        lax.fori_loop(0, nv_ref[tt], body, 0)

    @pl.when(e == 0)
    def _():
        @pl.when(t == 0)
        def _():
            gather_start(0, 0)

        @pl.when(prev_valid)
        def _():
            scatter_start(t - 1, other)

        @pl.when(valid)
        def _():
            gather_wait(slot)
            h = _rms(buf[slot], gm_ref[...])
            hb_scr[...] = h.astype(BF16)
            z = _router_logits(h, wr_ref)
            lane = lax.broadcasted_iota(jnp.int32, z.shape, 1)
            lanef = lane.astype(F32)
            grp = lane < N_GROUPS
            zg = jnp.where(grp, z, -jnp.inf)
            pg_top = 1.0 / jnp.sum(jnp.where(grp, jnp.exp(zg - jnp.max(zg, axis=1, keepdims=True)), 0.0), axis=1, keepdims=True)
            lo = N_GROUPS + tg_ref[t] * EXPERTS_PER_GROUP
            ing = (lane >= lo) & (lane < lo + EXPERTS_PER_GROUP)
            pf = _masked_softmax(z, ing)
            big = float(2 * LANES)
            m1 = jnp.max(jnp.where(ing, pf, -1.0), axis=1, keepdims=True)
            i1 = jnp.min(jnp.where(ing & (pf == m1), lanef, big), axis=1, keepdims=True)
            rest = ing & (lanef != i1)
            m2 = jnp.max(jnp.where(rest, pf, -1.0), axis=1, keepdims=True)
            i2 = jnp.min(jnp.where(rest & (pf == m2), lanef, big), axis=1, keepdims=True)
            tot = m1 + m2
            w4_scr[...] = jnp.where(lanef == i1, m1 / tot * pg_top, jnp.where(lanef == i2, m2 / tot * pg_top, 0.0))

    @pl.when(e == EXPERTS_PER_GROUP // 2)
    def _():
        @pl.when(prev_valid)
        def _():
            scatter_wait(t - 1, other)

        @pl.when(next_valid)
        def _():
            gather_start(t + 1, other)

    @pl.when(valid)
    def _():
        hb = hb_scr[...]
        lane = lax.broadcasted_iota(jnp.int32, w4_scr.shape, 1)
        col = N_GROUPS + tg_ref[t] * EXPERTS_PER_GROUP + e
        we = jnp.sum(jnp.where(lane == col, w4_scr[...], 0.0), axis=1, keepdims=True)
        act = jax.nn.silu(_dot(hb, wg_ref[0])) * _dot(hb, wu_ref[0])
        buf[slot] += _dot(act * we, wd_ref[0])

    @pl.when(valid & (e == EXPERTS_PER_GROUP - 1))
    def _():
        buf[slot] = _rms(buf[slot], gf_ref[...])


def _moe_final(x2d, g_moe, g_final, wr, w_gate, w_up, w_down, tm):
    n, d = x2d.shape
    g_top = _route_groups(x2d, g_moe, wr, min(n, 512))[:, 0]
    n_tiles = n // tm + N_GROUPS
    onehot = (g_top[:, None] == jnp.arange(N_GROUPS)[None, :]).astype(jnp.int32)
    counts = jnp.sum(onehot, axis=0)
    rank = jnp.sum((jnp.cumsum(onehot, axis=0) - onehot) * onehot, axis=1)
    padded = (counts + tm - 1) // tm * tm
    ends = jnp.cumsum(padded)
    base = ends - padded
    pos = base[g_top] + rank
    src = jnp.full((n_tiles * tm,), -1, jnp.int32).at[pos].set(jnp.arange(n, dtype=jnp.int32))
    tile_start = jnp.arange(n_tiles, dtype=jnp.int32) * tm
    tile_valid = (tile_start < ends[-1]).astype(jnp.int32)
    tile_group = jnp.minimum(jnp.sum((tile_start[:, None] >= ends[None, :]).astype(jnp.int32), axis=1), N_GROUPS - 1)
    tile_rows = jnp.clip(ends[tile_group] - padded[tile_group] + counts[tile_group] - tile_start, 0, tm)
    tile_rows = (tile_rows * tile_valid).astype(jnp.int32)
    n_valid = ends[-1] // tm
    last_group = tile_group[jnp.maximum(n_valid - 1, 0)]
    eidx = tile_group[:, None] * EXPERTS_PER_GROUP + jnp.arange(EXPERTS_PER_GROUP, dtype=jnp.int32)[None, :]
    widx = jnp.where(tile_valid[:, None] == 1, eidx, last_group * EXPERTS_PER_GROUP + EXPERTS_PER_GROUP - 1)
    widx = widx.reshape(-1).astype(jnp.int32)

    wmap = lambda t, e, src, tg, tv, nv, wi: (wi[t * EXPERTS_PER_GROUP + e], 0, 0)
    cmap = lambda t, e, src, tg, tv, nv, wi: (0, 0)
    grid_spec = pltpu.PrefetchScalarGridSpec(
        num_scalar_prefetch=5,
        grid=(n_tiles, EXPERTS_PER_GROUP),
        in_specs=[pl.BlockSpec(memory_space=pl.ANY),
                  pl.BlockSpec((1, d), cmap), pl.BlockSpec((1, d), cmap),
                  pl.BlockSpec(wr.shape, lambda t, e, src, tg, tv, nv, wi: (0, 0, 0)),
                  pl.BlockSpec((1, d, EXPERT_FF), wmap), pl.BlockSpec((1, d, EXPERT_FF), wmap),
                  pl.BlockSpec((1, EXPERT_FF, d), wmap)],
        out_specs=pl.BlockSpec(memory_space=pl.ANY),
        scratch_shapes=[pltpu.VMEM((2, tm, d), F32), pltpu.VMEM((tm, d), BF16), pltpu.VMEM((tm, LANES), F32),
                        pltpu.SemaphoreType.DMA((2,)), pltpu.SemaphoreType.DMA((1,))],
    )
    return pl.pallas_call(
        functools.partial(_moe_kernel, tm=tm),
        grid_spec=grid_spec,
        out_shape=jax.ShapeDtypeStruct((n, d), F32),
        compiler_params=_cparams(("arbitrary", "arbitrary")),
        name="moe_final_norm",
    )(src, tile_group, tile_valid, tile_rows, widx, x2d, g_moe, g_final, wr, w_gate, w_up, w_down)


def _finish(x2d, o_cmp, o_sel, o_win, gates, u, v, mk, mv, bsz, lw, chunk, moe_tm):
    n, d = x2d.shape
    t = n // bsz
    x1 = _mix(x2d, o_cmp, o_sel, o_win, gates, u, v, lw["w_sgu"], lw["b_sgu"], lw["g_nsa_out"], lw["g_sgu_out"],
              lw["w_out"], chunk, min(n, 256))
    hq = _rms_matmul(x1, lw["g_mem_norm"], lw["w_mem_q"], min(n, 512))
    o_m = _mem_attention(hq.reshape(bsz, t, -1), mk, mv)
    x2 = _matmul_res(o_m.reshape(n, -1), lw["w_mem_o"], x1, min(n, 512))
    return _moe_final(x2, lw["g_moe_norm"], lw["g_final"], lw["w_router"], lw["w_exp_gate"], lw["w_exp_up"],
                      lw["w_exp_down"], moe_tm)


def kernel(x_prompt, x_sample, cache_cmp_k, cache_cmp_v, cache_sel_k, cache_sel_v, cache_win_k, cache_win_v, cache_mem_k, cache_mem_v, page_table, mem_prompt, w_in, g_attn_norm, pe_cmp_k, w_cmp_k1, w_cmp_k2, pe_cmp_v, w_cmp_v1, w_cmp_v2, g_sgu_v, w_sgu, b_sgu, g_nsa_out, g_sgu_out, w_out, g_mem_norm, g_mem_src, w_mem_q, w_mem_k, w_mem_v, w_mem_o, g_moe_norm, w_router_group, w_router_expert, w_exp_gate, w_exp_up, w_exp_down, g_final):
    depth = w_in.shape[0]
    assert depth == 1, "single-layer trunk"
    bp, tp, d = x_prompt.shape
    bs, ts, _ = x_sample.shape
    n_pages = page_table.shape[1]
    page = cache_cmp_k.shape[2]
    past = n_pages * page
    assert ts < STRIDE and tp % 256 == 0 and page % STRIDE == 0
    row = lambda a: a[0].reshape(1, -1)

    w_in_p = w_in[0].astype(BF16)
    wr = jnp.concatenate([w_router_group[0], w_router_expert[0],
                          jnp.zeros((d, LANES - N_GROUPS - N_EXPERTS), F32)], axis=1)
    wr_hi = wr.astype(BF16)
    wr = jnp.stack([wr_hi, (wr - wr_hi.astype(F32)).astype(BF16)])
    lw = {
        "w_sgu": w_sgu[0], "b_sgu": b_sgu[0], "g_nsa_out": row(g_nsa_out), "g_sgu_out": row(g_sgu_out),
        "w_out": w_out[0].astype(BF16), "g_mem_norm": row(g_mem_norm), "w_mem_q": w_mem_q[0].astype(BF16),
        "w_mem_o": w_mem_o[0].astype(BF16), "g_moe_norm": row(g_moe_norm), "g_final": g_final.reshape(1, -1),
        "w_router": wr, "w_exp_gate": w_exp_gate[0], "w_exp_up": w_exp_up[0], "w_exp_down": w_exp_down[0],
    }
    g_attn = row(g_attn_norm)
    gsv = row(g_sgu_v)
    kv5 = lambda a, b, t: a.reshape(1, b, t, KV_HEADS, HEAD_DIM)

    np_ = bp * tp
    tabs_p = _rope_tables(jnp.arange(tp, dtype=jnp.int32))
    (q, qrot, kc, vc, ks, vs, kw, vw, gates, u, v) = _project(x_prompt.reshape(np_, d), g_attn, w_in_p, tabs_p, gsv, 256)
    pt_p = jnp.arange(np_ // page, dtype=jnp.int32).reshape(bp, tp // page)
    kcmp = _compress(kc.reshape(np_ // page, page, KV_WIDTH), pt_p, pe_cmp_k[0], w_cmp_k1[0], w_cmp_k2[0], False)
    vcmp = _compress(vc.reshape(np_ // page, page, KV_WIDTH), pt_p, pe_cmp_v[0], w_cmp_v1[0], w_cmp_v2[0], False)
    o_cmp, sel_t = _cmp_select(q.reshape(bp, tp, -1), kcmp, vcmp, 0, tp)
    qrot3 = qrot.reshape(bp, tp, -1)
    ks3, vs3, kw3, vw3 = (a.reshape(bp, tp, KV_WIDTH) for a in (ks, vs, kw, vw))
    o_sel = _prompt_attention(qrot3, ks3, vs3, sel_t)
    o_win = _prompt_attention(qrot3, kw3, vw3)
    n_mem = mem_prompt.shape[1]
    mem_w = MEM_HEADS * MEM_HEAD_DIM
    mem2d = mem_prompt.reshape(bp * n_mem, d)
    mk_p = _rms_matmul(mem2d, row(g_mem_src), w_mem_k[0].astype(BF16), min(bp * n_mem, 512)).reshape(bp, n_mem, mem_w)
    mv_p = _rms_matmul(mem2d, row(g_mem_src), w_mem_v[0].astype(BF16), min(bp * n_mem, 512)).reshape(bp, n_mem, mem_w)
    y_p = _finish(x_prompt.reshape(np_, d), o_cmp.reshape(np_, -1), o_sel.reshape(np_, -1), o_win.reshape(np_, -1),
                  gates, u, v, mk_p, mv_p, bp, lw, CHUNK, min(np_, 1024))
    wbp = min(WINDOW, tp)
    outs_p = (kv5(kc, bp, tp), kv5(vc, bp, tp), kv5(ks, bp, tp), kv5(vs, bp, tp),
              kv5(kw, bp, tp)[:, :, -wbp:], kv5(vw, bp, tp)[:, :, -wbp:],
              mk_p.reshape(1, bp, n_mem, MEM_HEADS, MEM_HEAD_DIM), mv_p.reshape(1, bp, n_mem, MEM_HEADS, MEM_HEAD_DIM))

    ns_ = bs * ts
    pos_s = past + jnp.arange(ts, dtype=jnp.int32)
    tabs_s = tuple(jnp.tile(a, (bs, 1)) for a in _rope_tables(pos_s))
    (q, qrot, kc, vc, ks, vs, kw, vw, gates, u, v) = _project(x_sample.reshape(ns_, d), g_attn, w_in_p, tabs_s, gsv, ns_)
    pool = lambda c: c[0].transpose(0, 2, 3, 1).reshape(c.shape[1], KV_WIDTH, page)
    kcmp = _compress(pool(cache_cmp_k), page_table, pe_cmp_k[0], w_cmp_k1[0], w_cmp_k2[0], True)
    vcmp = _compress(pool(cache_cmp_v), page_table, pe_cmp_v[0], w_cmp_v1[0], w_cmp_v2[0], True)
    o_cmp, sel_t = _cmp_select(q.reshape(bs, ts, -1).astype(F32), kcmp, vcmp, past, past + ts)
    wb = cache_win_k.shape[2]
    wk = cache_win_k[0].transpose(0, 2, 3, 1).reshape(bs, KV_WIDTH, wb)
    wv = cache_win_v[0].transpose(0, 2, 3, 1).reshape(bs, KV_WIDTH, wb)
    ks3, vs3, kw3, vw3 = (a.reshape(bs, ts, KV_WIDTH) for a in (ks, vs, kw, vw))
    o_sel, o_win = _sample_attention(qrot.reshape(bs, ts, -1), pool(cache_sel_k), pool(cache_sel_v), page_table,
                                     ks3, vs3, sel_t, wk, wv, kw3, vw3)
    mem_ks = cache_mem_k[0].reshape(bs, cache_mem_k.shape[2], mem_w)
    mem_vs = cache_mem_v[0].reshape(bs, cache_mem_v.shape[2], mem_w)
    y_s = _finish(x_sample.reshape(ns_, d), o_cmp.reshape(ns_, -1), o_sel.reshape(ns_, -1), o_win.reshape(ns_, -1),
                  gates, u, v, mem_ks, mem_vs, bs, lw, ts, min(ns_, 128))
    def slide(cache_t, new):
        win = jnp.concatenate([cache_t, new.transpose(0, 2, 1)], axis=2)[:, :, -wb:]
        return win.reshape(bs, KV_HEADS, HEAD_DIM, wb).transpose(0, 3, 1, 2)[None]

    win_k_s = slide(wk, kw3)
    win_v_s = slide(wv, vw3)
    outs_s = (kv5(kc, bs, ts), kv5(vc, bs, ts), kv5(ks, bs, ts), kv5(vs, bs, ts), win_k_s, win_v_s,
              v.reshape(1, bs, ts, -1))

    return (y_p.reshape(bp, tp, d), y_s.reshape(bs, ts, d)) + outs_p + outs_s
```

```python
import functools

import jax
import jax.numpy as jnp
from jax import lax
from jax.experimental import pallas as pl
from jax.experimental.pallas import tpu as pltpu

F32 = jnp.float32
BF16 = jnp.bfloat16

N_HEADS = 16
HEAD_DIM = 64
KV_HEADS = 4
Q_PER_KV = N_HEADS // KV_HEADS
NSA_WIDTH = N_HEADS * HEAD_DIM
KV_WIDTH = KV_HEADS * HEAD_DIM
ROT_DIM = HEAD_DIM // 4
ROPE_THETA = 500000.0
L_CMP = 32
STRIDE = 16
CMP_R = L_CMP // STRIDE
CMP_HIDDEN = 128
L_SEL = 64
SEL_RATIO = L_SEL // STRIDE
SEL_INNER = (L_SEL - L_CMP) // STRIDE + 1
TOP_N = 16
WINDOW = 512
SGU_GROUPS = 8
SGU_GROUP_DIM = 128
CHUNK = 128
MEM_HEADS = 4
MEM_HEAD_DIM = 128
N_GROUPS = 4
EXPERTS_PER_GROUP = 4
N_EXPERTS = N_GROUPS * EXPERTS_PER_GROUP
EXPERT_FF = 512
EPS = 1e-6
NEG_INF = -1e30
BIG = 1e9
PAD_SCORE = -3e38
CMP_ROW_PITCH = STRIDE + 4
ATTN_SCALE = HEAD_DIM ** -0.5
MEM_SCALE = MEM_HEAD_DIM ** -0.5

LANES = 128
SUBLANES = 8
V7X_VMEM_LIMIT_BYTES = 60000 * 1024

GATE_COLS = LANES
N_GATES = 3 * N_HEADS
_SEG = {}
_off = 0
for _name, _w in (("q", NSA_WIDTH), ("kc", KV_WIDTH), ("vc", KV_WIDTH), ("ks", KV_WIDTH), ("vs", KV_WIDTH),
                  ("kw", KV_WIDTH), ("vw", KV_WIDTH), ("gt", N_GATES), ("u", 1024), ("v", 1024)):
    _SEG[_name] = (_off, _w)
    _off += _w
IN_COLS = _off


def _cparams(sem, vmem=V7X_VMEM_LIMIT_BYTES):
    return pltpu.CompilerParams(dimension_semantics=sem, vmem_limit_bytes=vmem)


def _dot(a, b):
    return jnp.dot(a.astype(BF16), b.astype(BF16), preferred_element_type=F32)


def _dot_nt(a, b):
    return lax.dot_general(a.astype(BF16), b.astype(BF16), (((1,), (1,)), ((), ())), preferred_element_type=F32)


def _dot_tn(a, b):
    return lax.dot_general(a.astype(BF16), b.astype(BF16), (((0,), (0,)), ((), ())), preferred_element_type=F32)


def _rms(x, g):
    return x * lax.rsqrt(jnp.mean(x * x, axis=-1, keepdims=True) + EPS) * g


def _masked_softmax(s, mask, axis=-1):
    s = jnp.where(mask, s, NEG_INF)
    m = jnp.max(s, axis=axis, keepdims=True)
    e = jnp.where(mask, jnp.exp(s - m), 0.0)
    return e * (1.0 / jnp.maximum(jnp.sum(e, axis=axis, keepdims=True), 1e-30))


def _const_spec(shape):
    nd = len(shape)
    return pl.BlockSpec(shape, lambda *_: (0,) * nd, pipeline_mode=pl.Buffered(1))


def _proj_kernel(x_ref, g_ref, w_ref, rc_ref, rs1_ref, rs2_ref, gsv_ref,
                 q_ref, qrot_ref, kc_ref, vc_ref, ks_ref, vs_ref, kw_ref, vw_ref, gates_ref, u_ref, v_ref, *, transpose_kv):
    hb = _rms(x_ref[...], g_ref[...]).astype(BF16)

    def put_kv(ref, val):
        if transpose_kv:
            ref[0] = val.T
        else:
            ref[...] = val

    def seg(name):
        lo, width = _SEG[name]
        return jnp.dot(hb, w_ref[:, lo:lo + width], preferred_element_type=F32)

    def rope(z):
        rc, rs1, rs2 = rc_ref[...], rs1_ref[...], rs2_ref[...]
        half = ROT_DIM // 2
        outs = []
        for c in range(z.shape[1] // LANES):
            zc = z[:, c * LANES:(c + 1) * LANES]
            outs.append(zc * rc + pltpu.roll(zc, LANES - half, 1) * rs1 + pltpu.roll(zc, half, 1) * rs2)
        return jnp.concatenate(outs, axis=1)

    q = seg("q")
    q_ref[...] = q.astype(BF16)
    qrot_ref[...] = rope(q).astype(BF16)
    put_kv(kc_ref, seg("kc"))
    put_kv(vc_ref, seg("vc"))
    put_kv(ks_ref, rope(seg("ks")))
    put_kv(vs_ref, seg("vs"))
    put_kv(kw_ref, rope(seg("kw")))
    put_kv(vw_ref, seg("vw"))
    g0 = _SEG["gt"][0]
    zt = jnp.dot(hb, w_ref[:, g0:IN_COLS], preferred_element_type=F32)
    u0, v0 = _SEG["u"][0] - g0, _SEG["v"][0] - g0
    gates_ref[...] = jax.nn.sigmoid(zt[:, :GATE_COLS])
    u_ref[...] = jax.nn.gelu(zt[:, u0:u0 + 1024])
    v_ref[...] = _rms(jax.nn.gelu(zt[:, v0:v0 + 1024]), gsv_ref[...])


def _project(x2d, g_attn, w_in_p, tables, g_sgu_v, tm, kv_seq=None):
    n, d = x2d.shape
    rc, rs1, rs2 = tables
    tt = rc.shape[0]
    nt = tt // tm
    row = lambda w: pl.BlockSpec((tm, w), lambda i: (i, 0))
    tab = pl.BlockSpec((tm, LANES), lambda i: (i % nt, 0))
    out_shapes = [jax.ShapeDtypeStruct((n, NSA_WIDTH), BF16), jax.ShapeDtypeStruct((n, NSA_WIDTH), BF16)]
    if kv_seq is None:
        out_shapes += [jax.ShapeDtypeStruct((n, KV_WIDTH), F32)] * 6
        kv_spec = row(KV_WIDTH)
    else:
        per = kv_seq // tm
        out_shapes += [jax.ShapeDtypeStruct((n // kv_seq, KV_WIDTH, kv_seq), F32)] * 6
        kv_spec = pl.BlockSpec((1, KV_WIDTH, tm), lambda i: (i // per, 0, i % per))
    out_shapes += [jax.ShapeDtypeStruct((n, GATE_COLS), F32), jax.ShapeDtypeStruct((n, 1024), F32),
                   jax.ShapeDtypeStruct((n, 1024), F32)]
    out_specs = [row(NSA_WIDTH), row(NSA_WIDTH)] + [kv_spec] * 6 + [row(GATE_COLS), row(1024), row(1024)]
    return pl.pallas_call(
        functools.partial(_proj_kernel, transpose_kv=kv_seq is not None),
        grid=(n // tm,),
        in_specs=[row(d), _const_spec((1, d)), _const_spec(w_in_p.shape), tab, tab, tab, _const_spec((1, 1024))],
        out_specs=out_specs,
        out_shape=out_shapes,
        compiler_params=_cparams(("parallel",)),
        name="in_proj",
    )(x2d, g_attn, w_in_p, rc, rs1, rs2, g_sgu_v)


def _rope_tables(pos):
    half = ROT_DIM // 2
    freqs = ROPE_THETA ** (-jnp.arange(half, dtype=F32) / half)
    ang = pos.astype(F32)[:, None] * freqs[None, :]
    cos, sin = jnp.cos(ang), jnp.sin(ang)
    t = pos.shape[0]
    ones = jnp.ones((t, HEAD_DIM - ROT_DIM), F32)
    zeros = jnp.zeros((t, HEAD_DIM - ROT_DIM), F32)
    zh = jnp.zeros((t, half), F32)
    rc = jnp.concatenate([cos, cos, ones], axis=1)
    rs1 = jnp.concatenate([-sin, zh, zeros], axis=1)
    rs2 = jnp.concatenate([zh, sin, zeros], axis=1)
    rep = LANES // HEAD_DIM
    return tuple(jnp.tile(a, (1, rep)) for a in (rc, rs1, rs2))


def _compress_kernel(pt_ref, pool_ref, pe_ref, w1_ref, w2_ref, out_ref, stage, buf, a_ref, sem, *, n_pages, page):
    b = pl.program_id(0)
    nb = pl.num_programs(0)
    n_rows = n_pages * page
    n_sub = n_rows // STRIDE
    slot = b % 2
    n_ct = KV_WIDTH // LANES
    ppl = pool_ref.shape[2] // page

    def page_copy(bb, sl, p):
        q = pt_ref[bb, p]
        src = pool_ref.at[q // ppl, :, pl.ds(pl.multiple_of((q % ppl) * page, page), page)]
        return pltpu.make_async_copy(src, stage.at[sl, :, pl.ds(pl.multiple_of(p * page, page), page)], sem.at[sl])

    def start_all(bb, sl):
        def body(p, carry):
            page_copy(bb, sl, p).start()
            return carry
        lax.fori_loop(0, n_pages, body, 0)

    def wait_all(bb, sl):
        def body(p, carry):
            page_copy(bb, sl, p).wait()
            return carry
        lax.fori_loop(0, n_pages, body, 0)

    @pl.when(b == 0)
    def _():
        start_all(0, 0)

    @pl.when(b + 1 < nb)
    def _():
        start_all(b + 1, 1 - slot)

    wait_all(b, slot)

    tch = min(n_rows, 512)
    for c in range(n_ct):
        for j in range(n_rows // tch):
            rows_t = stage[slot, c * LANES:(c + 1) * LANES, j * tch:(j + 1) * tch].T
            for n in range(tch // STRIDE):
                r0 = (j * (tch // STRIDE) + n) * CMP_ROW_PITCH
                buf[c, r0:r0 + STRIDE, :] = rows_t[n * STRIDE:(n + 1) * STRIDE, :]

    pev = _dot(pe_ref[...], w1_ref[...])
    low = lax.broadcasted_iota(jnp.int32, (n_sub, LANES), 1) < HEAD_DIM
    row = lax.broadcasted_iota(jnp.int32, (n_sub, 4 * HEAD_DIM), 0)
    for c in range(n_ct):
        src = buf.at[c]
        for sp in range(STRIDE // 2):
            x0 = src[pl.ds(2 * sp, n_sub, stride=CMP_ROW_PITCH), :]
            x1 = src[pl.ds(2 * sp + 1, n_sub, stride=CMP_ROW_PITCH), :]
            a_ref[0, :, sp * LANES:(sp + 1) * LANES] = jnp.where(low, x0, pltpu.roll(x1, HEAD_DIM, 1)).astype(BF16)
            a_ref[1, :, sp * LANES:(sp + 1) * LANES] = jnp.where(low, pltpu.roll(x0, HEAD_DIM, 1), x1).astype(BF16)
        for k in range(2):
            pm = jnp.dot(a_ref[k], w1_ref[...], preferred_element_type=F32)
            part0 = pm[:, :CMP_HIDDEN] + pev[0:1, :CMP_HIDDEN]
            part1 = pm[:, CMP_HIDDEN:] + pev[1:2, CMP_HIDDEN:]
            hsum = part0 + pltpu.roll(part1, n_sub - 1, 0)
            o = _dot(jax.nn.silu(hsum), w2_ref[...])
            out_ref[0, 2 * c + k] = jnp.where(row < n_sub - 1, o, 0.0).astype(BF16)


def _compress(pool, page_table, page, pe, w1, w2):
    bsz, n_pages = page_table.shape
    n_rows = n_pages * page
    n_sub = n_rows // STRIDE
    kdim = STRIDE * HEAD_DIM
    n_ct = KV_WIDTH // LANES
    pe8 = jnp.zeros((SUBLANES, kdim), F32).at[:CMP_R].set(pe.reshape(CMP_R, kdim))
    w1c = w1.reshape(CMP_R, kdim, CMP_HIDDEN).transpose(1, 0, 2).reshape(kdim, CMP_R * CMP_HIDDEN).astype(BF16)
    w2t = jnp.tile(w2, (1, 4)).astype(BF16)
    grid_spec = pltpu.PrefetchScalarGridSpec(
        num_scalar_prefetch=1,
        grid=(bsz,),
        in_specs=[pl.BlockSpec(memory_space=pl.ANY),
                  pl.BlockSpec((SUBLANES, kdim), lambda b, pt: (0, 0)),
                  pl.BlockSpec(w1c.shape, lambda b, pt: (0, 0)),
                  pl.BlockSpec(w2t.shape, lambda b, pt: (0, 0))],
        out_specs=pl.BlockSpec((1, KV_HEADS, n_sub, 4 * HEAD_DIM), lambda b, pt: (b, 0, 0, 0)),
        scratch_shapes=[pltpu.VMEM((2, KV_WIDTH, n_rows), F32), pltpu.VMEM((n_ct, n_sub * CMP_ROW_PITCH, LANES), F32),
                        pltpu.VMEM((2, n_sub, kdim), BF16), pltpu.SemaphoreType.DMA((2,))],
    )
    return pl.pallas_call(
        functools.partial(_compress_kernel, n_pages=n_pages, page=page),
        grid_spec=grid_spec,
        out_shape=jax.ShapeDtypeStruct((bsz, KV_HEADS, n_sub, 4 * HEAD_DIM), BF16),
        compiler_params=_cparams(("arbitrary",)),
        name="compress",
    )(page_table, pool, pe8, w1c, w2t)


def _head_blockdiag(x, n_heads_in_lanes=Q_PER_KV):
    n = x.shape[0]
    lane_head = lax.broadcasted_iota(jnp.int32, (n, n_heads_in_lanes * HEAD_DIM), 1) // HEAD_DIM
    xf = x.astype(F32)
    return jnp.concatenate([jnp.where(lane_head == qi, xf, 0.0) for qi in range(n_heads_in_lanes)], axis=0).astype(BF16)


def _cmpsel_kernel(q_ref, k_ref, v_ref, st_ref, o_ref, sc_ref, *, pos0, nc, ns, tq):
    i = pl.program_id(2)
    q = q_ref[0]
    n_sub = k_ref.shape[2]
    ns_pad = st_ref.shape[0]
    kbd = _head_blockdiag(k_ref[0, 0])
    vbd = _head_blockdiag(v_ref[0, 0])
    s_all = _dot_nt(q, kbd) * ATTN_SCALE
    qpos = pos0 + i * tq + lax.broadcasted_iota(jnp.int32, (tq, 1), 0)
    kidx = lax.broadcasted_iota(jnp.int32, (1, n_sub), 1)
    mask = (kidx * STRIDE + (L_CMP - 1) <= qpos) & (kidx < nc)
    probs = []
    for qi in range(Q_PER_KV):
        probs.append(_masked_softmax(s_all[:, qi * n_sub:(qi + 1) * n_sub], mask))
    o_ref[0] = _dot(jnp.concatenate(probs, axis=1), vbd)
    pg = probs[0]
    for p in probs[1:]:
        pg = pg + p
    pg_hi = pg.astype(BF16)
    pg_lo = (pg - pg_hi.astype(F32)).astype(BF16)
    st = st_ref[...]
    ps_t = _dot_nt(st, pg_hi) + _dot_nt(st, pg_lo)
    blk = lax.broadcasted_iota(jnp.int32, (ns_pad, tq), 0)
    qpos_t = pos0 + i * tq + lax.broadcasted_iota(jnp.int32, (ns_pad, tq), 1)
    cur = qpos_t // L_SEL
    forced = (blk == 0) | (blk == cur) | (blk == cur - 1)
    future = blk * L_SEL > qpos_t
    sc = jnp.where(forced, BIG, jnp.where(future, -BIG, ps_t))
    sc_ref[0, 0] = jnp.where(blk < ns, sc, PAD_SCORE)


def _rank_kernel(sc_ref, sel_ref, *, ns, topn):
    sc = sc_ref[0, 0]
    blk = lax.broadcasted_iota(jnp.int32, sc.shape, 0)
    sel_ref[0, 0] = jnp.zeros(sc.shape, F32)

    def body(j, c):
        row = sc_ref[0, 0, pl.ds(j, 1), :]
        ahead = (sc > row) | ((sc == row) & (blk < j))
        rank = jnp.sum(ahead.astype(F32), axis=0, keepdims=True)
        sel_ref[0, 0, pl.ds(j, 1), :] = (rank < topn).astype(F32)
        return c

    lax.fori_loop(0, ns, body, 0)


def _cmp_select(q, kcmp, vcmp, pos0, n_keys):
    bsz, t, _ = q.shape
    n_sub = kcmp.shape[2]
    nc = n_sub - CMP_R + 1
    ns = -(-n_keys // L_SEL)
    ns_pad = -(-ns // SUBLANES) * SUBLANES
    topn = min(TOP_N, ns)
    tq = min(t, 512)
    gw = Q_PER_KV * HEAD_DIM
    cidx = jnp.arange(n_sub)
    st = ((cidx[None, :] // SEL_RATIO == jnp.arange(ns_pad)[:, None]) & (cidx[None, :] % SEL_RATIO < SEL_INNER)
          & (cidx[None, :] < nc)).astype(BF16)
    o_cmp, sc = pl.pallas_call(
        functools.partial(_cmpsel_kernel, pos0=pos0, nc=nc, ns=ns, tq=tq),
        grid=(bsz, KV_HEADS, t // tq),
        in_specs=[pl.BlockSpec((1, tq, gw), lambda b, g, i: (b, i, g)),
                  pl.BlockSpec((1, 1, n_sub, gw), lambda b, g, i: (b, g, 0, 0)),
                  pl.BlockSpec((1, 1, n_sub, gw), lambda b, g, i: (b, g, 0, 0)),
                  pl.BlockSpec((ns_pad, n_sub), lambda b, g, i: (0, 0))],
        out_specs=[pl.BlockSpec((1, tq, gw), lambda b, g, i: (b, i, g)),
                   pl.BlockSpec((1, 1, ns_pad, tq), lambda b, g, i: (b, g, 0, i))],
        out_shape=[jax.ShapeDtypeStruct((bsz, t, NSA_WIDTH), F32),
                   jax.ShapeDtypeStruct((bsz, KV_HEADS, ns_pad, t), F32)],
        compiler_params=_cparams(("parallel", "parallel", "parallel")),
        name="cmp_scores",
    )(q, kcmp, vcmp, st)
    fold = t < LANES
    if fold:
        sc = sc.transpose(2, 0, 1, 3).reshape(1, 1, ns_pad, bsz * KV_HEADS * t)
    nb, ng, _, width = sc.shape
    tl = min(width, 2048)
    sel = pl.pallas_call(
        functools.partial(_rank_kernel, ns=ns, topn=topn),
        grid=(nb, ng, width // tl),
        in_specs=[pl.BlockSpec((1, 1, ns_pad, tl), lambda b, g, i: (b, g, 0, i))],
        out_specs=pl.BlockSpec((1, 1, ns_pad, tl), lambda b, g, i: (b, g, 0, i)),
        out_shape=jax.ShapeDtypeStruct(sc.shape, F32),
        compiler_params=_cparams(("parallel", "parallel", "parallel")),
        name="rank_select",
    )(sc)
    if fold:
        sel = sel.reshape(ns_pad, bsz, KV_HEADS, t).transpose(1, 2, 0, 3)
    return o_cmp, sel


def _pattn_kernel(*refs, mode, t, tq, ck):
    if mode == "sel":
        q_ref, k_ref, v_ref, sel_ref, et_ref, o_ref, kc_scr, vt_scr, m_scr, l_scr, acc_scr = refs
    else:
        q_ref, k_ref, v_ref, o_ref, kc_scr, vt_scr, m_scr, l_scr, acc_scr = refs
    i = pl.program_id(2)
    n_chunks = t // ck
    last = (i + 1) * (tq // ck) - 1

    @pl.when(i == 0)
    def _():
        zeros = jnp.zeros((ck, LANES - HEAD_DIM), F32)
        for c in range(n_chunks):
            kc_scr[c] = jnp.concatenate([k_ref[0, :, c * ck:(c + 1) * ck].T, zeros], axis=1).astype(BF16)
            vt_scr[c] = v_ref[0, :, c * ck:(c + 1) * ck].astype(BF16)

    qf = q_ref[0].astype(F32) * ATTN_SCALE
    q_heads = []
    for qi in range(Q_PER_KV):
        tile = qf[:, (qi // 2) * LANES:(qi // 2 + 1) * LANES]
        q_heads.append((tile if qi % 2 == 0 else pltpu.roll(tile, HEAD_DIM, 1)).astype(BF16))
    q_all = jnp.concatenate(q_heads, axis=0)
    qpos = i * tq + lax.broadcasted_iota(jnp.int32, (1, tq), 1)
    m_scr[...] = jnp.full(m_scr.shape, NEG_INF, F32)
    l_scr[...] = jnp.zeros(l_scr.shape, F32)
    acc_scr[...] = jnp.zeros(acc_scr.shape, F32)
    if mode == "sel":
        sel_b = sel_ref[0, 0].astype(BF16)
        n_steps = last + 1
    else:
        n_steps = last - jnp.maximum(i * tq - WINDOW, 0) // ck + 1

    def chunk(c):
        kpos = c * ck + lax.broadcasted_iota(jnp.int32, (ck, 1), 0)
        if mode == "sel":
            chosen = jnp.dot(et_ref[c], sel_b, preferred_element_type=F32)
            ok = (chosen > 0.5) & (kpos <= qpos)
        else:
            ok = (kpos <= qpos) & (kpos > qpos - WINDOW)
        bias = jnp.where(ok, 0.0, NEG_INF)
        s = _dot_nt(kc_scr[c], q_all) + jnp.concatenate([bias] * Q_PER_KV, axis=1)
        m_old = m_scr[...]
        m_new = jnp.maximum(m_old, jnp.max(s, axis=0, keepdims=True))
        e = jnp.exp(s - m_new)
        alpha = jnp.exp(m_old - m_new)
        l_scr[...] = l_scr[...] * alpha + jnp.sum(e, axis=0, keepdims=True)
        m_scr[...] = m_new
        acc_scr[...] = acc_scr[...] * alpha + jnp.dot(vt_scr[c], e.astype(BF16), preferred_element_type=F32)

    def body(step, carry):
        chunk(last - step)
        return carry

    lax.fori_loop(0, n_steps, body, 0)
    out = acc_scr[...] * (1.0 / jnp.maximum(l_scr[...], 1e-30))
    o_ref[0] = jnp.concatenate([out[:, qi * tq:(qi + 1) * tq] for qi in range(Q_PER_KV)], axis=0).T


def _prompt_attention(qrot, k, v, sel_t=None):
    bsz, t, _ = qrot.shape
    tq, ck = 256, 256
    gw = Q_PER_KV * HEAD_DIM
    mode = "win" if sel_t is None else "sel"
    assert t % tq == 0 and tq % ck == 0 and WINDOW % ck == 0 and ck % L_SEL == 0
    in_specs = [pl.BlockSpec((1, tq, gw), lambda b, g, i: (b, i, g)),
                pl.BlockSpec((1, HEAD_DIM, t), lambda b, g, i: (b, g, 0)),
                pl.BlockSpec((1, HEAD_DIM, t), lambda b, g, i: (b, g, 0))]
    args = [qrot, k, v]
    if mode == "sel":
        ns_pad = sel_t.shape[2]
        et = (jnp.arange(t)[:, None] // L_SEL == jnp.arange(ns_pad)[None, :]).astype(BF16).reshape(t // ck, ck, ns_pad)
        in_specs += [pl.BlockSpec((1, 1, ns_pad, tq), lambda b, g, i: (b, g, 0, i)),
                     pl.BlockSpec((t // ck, ck, ns_pad), lambda b, g, i: (0, 0, 0))]
        args += [sel_t, et]
    return pl.pallas_call(
        functools.partial(_pattn_kernel, mode=mode, t=t, tq=tq, ck=ck),
        grid=(bsz, KV_HEADS, t // tq),
        in_specs=in_specs,
        out_specs=pl.BlockSpec((1, tq, gw), lambda b, g, i: (b, i, g)),
        out_shape=jax.ShapeDtypeStruct((bsz, t, NSA_WIDTH), F32),
        scratch_shapes=[pltpu.VMEM((t // ck, ck, LANES), BF16), pltpu.VMEM((t // ck, HEAD_DIM, ck), BF16),
                        pltpu.VMEM((1, Q_PER_KV * tq), F32), pltpu.VMEM((1, Q_PER_KV * tq), F32),
                        pltpu.VMEM((HEAD_DIM, Q_PER_KV * tq), F32)],
        compiler_params=_cparams(("parallel", "parallel", "arbitrary")),
        name="prompt_attn_" + mode,
    )(*args)


def _sattn_kernel(pt_ref, qbd_ref, kpool_ref, vpool_ref, ksn_ref, vsn_ref, selc_ref, wk_ref, wv_ref, kwn_ref, vwn_ref,
                  osel_ref, owin_ref, kbuf, vbuf, m_scr, l_scr, acc_scr, sem, *, n_pages, ppc, page, tdec):
    b = pl.program_id(0)
    c = pl.program_id(1)
    nb = pl.num_programs(0)
    nch = n_pages // ppc
    step = b * nch + c
    slot = step % 2
    rows = ppc * page
    ncol = qbd_ref.shape[2]

    def copies(bb, cc, sl, p):
        dst_k = kbuf.at[sl, :, pl.ds(pl.multiple_of(p * page, page), page)]
        dst_v = vbuf.at[sl, :, pl.ds(pl.multiple_of(p * page, page), page)]
        pid = pt_ref[bb, cc * ppc + p]
        return (pltpu.make_async_copy(kpool_ref.at[pid], dst_k, sem.at[0, sl]),
                pltpu.make_async_copy(vpool_ref.at[pid], dst_v, sem.at[1, sl]))

    def start_all(bb, cc, sl):
        def body(p, carry):
            ck, cv = copies(bb, cc, sl, p)
            ck.start()
            cv.start()
            return carry
        lax.fori_loop(0, ppc, body, 0)

    def wait_all(bb, cc, sl):
        def body(p, carry):
            ck, cv = copies(bb, cc, sl, p)
            ck.wait()
            cv.wait()
            return carry
        lax.fori_loop(0, ppc, body, 0)

    @pl.when(step == 0)
    def _():
        start_all(0, 0, 0)

    @pl.when(step + 1 < nb * nch)
    def _():
        nxt = step + 1
        start_all(nxt // nch, nxt % nch, 1 - slot)

    wait_all(b, c, slot)

    @pl.when(c == 0)
    def _():
        m_scr[...] = jnp.full(m_scr.shape, NEG_INF, F32)
        l_scr[...] = jnp.zeros(l_scr.shape, F32)
        acc_scr[...] = jnp.zeros(acc_scr.shape, F32)

    qbd = qbd_ref[0]
    nblk = rows // L_SEL
    s3 = (_dot_tn(kbuf[slot], qbd) * ATTN_SCALE).reshape(nblk, L_SEL, ncol)
    blk0 = pl.multiple_of(c * nblk, SUBLANES)
    chosen = (selc_ref[0, pl.ds(blk0, nblk), :] > 0.5)[:, None, :]
    s3 = jnp.where(chosen, s3, NEG_INF)
    m_old = m_scr[...]
    m_new = jnp.maximum(m_old, jnp.max(jnp.max(s3, axis=0), axis=0, keepdims=True))
    e3 = jnp.where(chosen, jnp.exp(s3 - m_new[None]), 0.0)
    alpha = jnp.exp(m_old - m_new)
    l_scr[...] = l_scr[...] * alpha + jnp.sum(jnp.sum(e3, axis=0), axis=0, keepdims=True)
    m_scr[...] = m_new
    acc_scr[...] = acc_scr[...] * alpha + _dot(vbuf[slot], e3.reshape(rows, ncol))

    @pl.when(c == nch - 1)
    def _():
        tcol = lax.broadcasted_iota(jnp.int32, (tdec, ncol), 1) % tdec
        jrow = lax.broadcasted_iota(jnp.int32, (tdec, ncol), 0)
        causal_new = jrow <= tcol
        sel_last = selc_ref[0, pl.ds(nch * nblk, 1), :] > 0.5
        ok_new = causal_new & sel_last
        s_new = jnp.where(ok_new, _dot(ksn_ref[0], qbd) * ATTN_SCALE, NEG_INF)
        m_old2 = m_scr[...]
        m_fin = jnp.maximum(m_old2, jnp.max(s_new, axis=0, keepdims=True))
        e_new = jnp.where(ok_new, jnp.exp(s_new - m_fin), 0.0)
        alpha2 = jnp.exp(m_old2 - m_fin)
        l_fin = l_scr[...] * alpha2 + jnp.sum(e_new, axis=0, keepdims=True)
        inv = 1.0 / jnp.maximum(l_fin, 1e-30)
        acc = acc_scr[...] * alpha2 + _dot_tn(vsn_ref[0], e_new)
        osel_ref[0] = acc * inv

        wb = wk_ref.shape[2]
        jw = lax.broadcasted_iota(jnp.int32, (wb, ncol), 0)
        tw = lax.broadcasted_iota(jnp.int32, (wb, ncol), 1) % tdec
        ok_c = jw + (WINDOW - wb) > tw
        s_c = jnp.where(ok_c, _dot_tn(wk_ref[0], qbd) * ATTN_SCALE, NEG_INF)
        s_n = jnp.where(causal_new, _dot(kwn_ref[0], qbd) * ATTN_SCALE, NEG_INF)
        m_w = jnp.maximum(jnp.max(s_c, axis=0, keepdims=True), jnp.max(s_n, axis=0, keepdims=True))
        e_c = jnp.where(ok_c, jnp.exp(s_c - m_w), 0.0)
        e_n = jnp.where(causal_new, jnp.exp(s_n - m_w), 0.0)
        inv_w = 1.0 / jnp.maximum(jnp.sum(e_c, axis=0, keepdims=True) + jnp.sum(e_n, axis=0, keepdims=True), 1e-30)
        ow = _dot(wv_ref[0], e_c) + _dot_tn(vwn_ref[0], e_n)
        owin_ref[0] = ow * inv_w


def _sample_attention(qrot, kpool, vpool, page_table, ks_new, vs_new, sel_t, wk, wv, kw_new, vw_new):
    bsz, tdec, _ = qrot.shape
    n_pages = page_table.shape[1]
    page = kpool.shape[2]
    ppc = min(n_pages, 32)
    assert n_pages % ppc == 0 and page % L_SEL == 0
    nch = n_pages // ppc
    rows = ppc * page
    ncol = KV_HEADS * Q_PER_KV * tdec
    ns_pad = sel_t.shape[2]
    wb = wk.shape[2]
    q5 = qrot.reshape(bsz, tdec, KV_HEADS, Q_PER_KV, HEAD_DIM).transpose(0, 2, 4, 3, 1)
    eye = jnp.eye(KV_HEADS, dtype=qrot.dtype)
    qbd = (q5[:, :, :, None] * eye[None, :, None, :, None, None]).reshape(bsz, KV_WIDTH, ncol)
    selc = jnp.broadcast_to(sel_t.transpose(0, 2, 1, 3)[:, :, :, None, :], (bsz, ns_pad, KV_HEADS, Q_PER_KV, tdec))
    selc = selc.reshape(bsz, ns_pad, ncol)
    per_b = lambda shape: pl.BlockSpec((1,) + shape, lambda b, c, pt: (b, 0, 0))
    grid_spec = pltpu.PrefetchScalarGridSpec(
        num_scalar_prefetch=1,
        grid=(bsz, nch),
        in_specs=[per_b((KV_WIDTH, ncol)),
                  pl.BlockSpec(memory_space=pl.ANY), pl.BlockSpec(memory_space=pl.ANY),
                  per_b((tdec, KV_WIDTH)), per_b((tdec, KV_WIDTH)),
                  per_b((ns_pad, ncol)),
                  per_b((KV_WIDTH, wb)), per_b((KV_WIDTH, wb)),
                  per_b((tdec, KV_WIDTH)), per_b((tdec, KV_WIDTH))],
        out_specs=[per_b((KV_WIDTH, ncol)), per_b((KV_WIDTH, ncol))],
        scratch_shapes=[pltpu.VMEM((2, KV_WIDTH, rows), F32), pltpu.VMEM((2, KV_WIDTH, rows), F32),
                        pltpu.VMEM((1, ncol), F32), pltpu.VMEM((1, ncol), F32), pltpu.VMEM((KV_WIDTH, ncol), F32),
                        pltpu.SemaphoreType.DMA((2, 2))],
    )
    o_sel, o_win = pl.pallas_call(
        functools.partial(_sattn_kernel, n_pages=n_pages, ppc=ppc, page=page, tdec=tdec),
        grid_spec=grid_spec,
        out_shape=[jax.ShapeDtypeStruct((bsz, KV_WIDTH, ncol), F32)] * 2,
        compiler_params=_cparams(("arbitrary", "arbitrary")),
        name="sample_attn",
    )(page_table, qbd, kpool, vpool, ks_new, vs_new, selc, wk, wv, kw_new, vw_new)

    def unpack(o):
        o6 = o.reshape(bsz, KV_HEADS, HEAD_DIM, KV_HEADS, Q_PER_KV, tdec)
        diag = jnp.stack([o6[:, g, :, g] for g in range(KV_HEADS)], axis=1)
        return diag.transpose(0, 4, 1, 3, 2).reshape(bsz, tdec, NSA_WIDTH)

    return unpack(o_sel), unpack(o_win)


def _mix_kernel(x_ref, oc_ref, os_ref, ow_ref, gt_ref, u_ref, v_ref, wsm_ref, bs_ref, gn_ref, gs_ref, eg_ref, wout_ref,
                o_ref, *, chunk):
    r = x_ref.shape[0]
    g = gt_ref[...]
    g_hi = g.astype(BF16)
    g_lo = (g - g_hi.astype(F32)).astype(BF16)
    onsa = None
    for j, branch in enumerate((oc_ref, os_ref, ow_ref)):
        ge = _dot(g_hi, eg_ref[j]) + _dot(g_lo, eg_ref[j])
        term = ge * branch[...]
        onsa = term if onsa is None else onsa + term
    onsa = _rms(onsa, gn_ref[...])
    ii = lax.broadcasted_iota(jnp.int32, (r, r), 0)
    jj = lax.broadcasted_iota(jnp.int32, (r, r), 1)
    tri = (ii // chunk == jj // chunk) & (jj % chunk <= ii % chunk)
    cols = []
    for gi in range(SGU_GROUPS):
        sl = slice(gi * SGU_GROUP_DIM, (gi + 1) * SGU_GROUP_DIM)
        ws = jnp.where(tri, wsm_ref[gi], 0.0)
        mixed = _dot(ws, v_ref[:, sl]) + bs_ref[:, gi:gi + 1]
        cols.append(u_ref[:, sl] * mixed)
    osgu = _rms(jnp.concatenate(cols, axis=1), gs_ref[...])
    o_ref[...] = x_ref[...] + _dot(jnp.concatenate([onsa, osgu], axis=1), wout_ref[...])


def _gate_expanders():
    c = jnp.arange(GATE_COLS)[None, :, None]
    lane = jnp.arange(NSA_WIDTH)[None, None, :]
    j = jnp.arange(3)[:, None, None]
    return (c == (lane // HEAD_DIM) * 3 + j).astype(BF16)


def _mix(x2d, o_cmp, o_sel, o_win, gates, u, v, w_sgu, b_sgu, g_nsa_out, g_sgu_out, w_out_b, chunk, r):
    n, d = x2d.shape
    rep = r // chunk
    pick = (jnp.arange(r)[:, None] % chunk == jnp.arange(chunk)[None, :]).astype(F32)
    wsm = jnp.einsum("ia,gab,jb->gij", pick, w_sgu[:, :chunk, :chunk], pick, precision=lax.Precision.HIGHEST)
    bs = jnp.tile(b_sgu[:, :chunk].T, (rep, 1))
    row = lambda w: pl.BlockSpec((r, w), lambda i: (i, 0))
    return pl.pallas_call(
        functools.partial(_mix_kernel, chunk=chunk),
        grid=(n // r,),
        in_specs=[row(d), row(NSA_WIDTH), row(NSA_WIDTH), row(NSA_WIDTH), row(GATE_COLS), row(1024), row(1024),
                  _const_spec(wsm.shape), _const_spec(bs.shape), _const_spec((1, NSA_WIDTH)), _const_spec((1, 1024)),
                  _const_spec((3, GATE_COLS, NSA_WIDTH)), _const_spec(w_out_b.shape)],
        out_specs=row(d),
        out_shape=jax.ShapeDtypeStruct((n, d), F32),
        compiler_params=_cparams(("parallel",)),
        name="mix_out_proj",
    )(x2d, o_cmp, o_sel, o_win, gates, u, v, wsm, bs, g_nsa_out, g_sgu_out, _gate_expanders(), w_out_b)


def _rms_matmul_kernel(x_ref, g_ref, w_ref, o_ref):
    o_ref[...] = _dot(_rms(x_ref[...], g_ref[...]), w_ref[...])


def _rms_matmul(x2d, g, w_b, tm):
    n, d = x2d.shape
    m = w_b.shape[1]
    return pl.pallas_call(
        _rms_matmul_kernel,
        grid=(n // tm,),
        in_specs=[pl.BlockSpec((tm, d), lambda i: (i, 0)), _const_spec((1, d)), _const_spec(w_b.shape)],
        out_specs=pl.BlockSpec((tm, m), lambda i: (i, 0)),
        out_shape=jax.ShapeDtypeStruct((n, m), F32),
        compiler_params=_cparams(("parallel",)),
        name="rms_matmul",
    )(x2d, g, w_b)


def _matmul_res_kernel(a_ref, w_ref, r_ref, o_ref):
    o_ref[...] = r_ref[...] + _dot(a_ref[...], w_ref[...])


def _matmul_res(a2d, w_b, res, tm):
    n, k = a2d.shape
    m = w_b.shape[1]
    return pl.pallas_call(
        _matmul_res_kernel,
        grid=(n // tm,),
        in_specs=[pl.BlockSpec((tm, k), lambda i: (i, 0)), _const_spec(w_b.shape), pl.BlockSpec((tm, m), lambda i: (i, 0))],
        out_specs=pl.BlockSpec((tm, m), lambda i: (i, 0)),
        out_shape=jax.ShapeDtypeStruct((n, m), F32),
        compiler_params=_cparams(("parallel",)),
        name="matmul_residual",
    )(a2d, w_b, res)


def _memattn_kernel(q_ref, k_ref, v_ref, o_ref):
    q = q_ref[0]
    outs = []
    for h in range(MEM_HEADS):
        sl = slice(h * MEM_HEAD_DIM, (h + 1) * MEM_HEAD_DIM)
        s = _dot_nt(q[:, sl], k_ref[0, :, h, :]) * MEM_SCALE
        p = _masked_softmax(s, jnp.ones(s.shape, dtype=jnp.bool_))
        outs.append(_dot(p, v_ref[0, :, h, :]))
    o_ref[0] = jnp.concatenate(outs, axis=1)


def _mem_attention(hq, mk, mv):
    bsz, t, w = hq.shape
    m = mk.shape[1]
    tq = min(t, 512)
    return pl.pallas_call(
        _memattn_kernel,
        grid=(bsz, t // tq),
        in_specs=[pl.BlockSpec((1, tq, w), lambda b, i: (b, i, 0)),
                  pl.BlockSpec((1, m, MEM_HEADS, MEM_HEAD_DIM), lambda b, i: (b, 0, 0, 0)),
                  pl.BlockSpec((1, m, MEM_HEADS, MEM_HEAD_DIM), lambda b, i: (b, 0, 0, 0))],
        out_specs=pl.BlockSpec((1, tq, w), lambda b, i: (b, i, 0)),
        out_shape=jax.ShapeDtypeStruct((bsz, t, w), F32),
        compiler_params=_cparams(("parallel", "parallel")),
        name="mem_attn",
    )(hq, mk, mv)


def _router_logits(h, wr_ref):
    h_hi = h.astype(BF16)
    h_lo = (h - h_hi.astype(F32)).astype(BF16)
    w_hi, w_lo = wr_ref[0], wr_ref[1]
    return _dot(h_hi, w_hi) + _dot(h_hi, w_lo) + _dot(h_lo, w_hi)


def _router_kernel(x_ref, g_ref, wr_ref, gtop_ref):
    z = _router_logits(_rms(x_ref[...], g_ref[...]), wr_ref)
    lane = lax.broadcasted_iota(jnp.int32, z.shape, 1)
    zg = jnp.where(lane < N_GROUPS, z, -jnp.inf)
    m = jnp.max(zg, axis=1, keepdims=True)
    first = jnp.min(jnp.where(zg == m, lane.astype(F32), float(LANES)), axis=1, keepdims=True)
    gtop_ref[...] = first.astype(jnp.int32)


def _route_groups(x2d, g_moe, wr, tm):
    n, d = x2d.shape
    return pl.pallas_call(
        _router_kernel,
        grid=(n // tm,),
        in_specs=[pl.BlockSpec((tm, d), lambda i: (i, 0)), _const_spec((1, d)), _const_spec(wr.shape)],
        out_specs=pl.BlockSpec((tm, 1), lambda i: (i, 0)),
        out_shape=jax.ShapeDtypeStruct((n, 1), jnp.int32),
        compiler_params=_cparams(("parallel",)),
        name="route_groups",
    )(x2d, g_moe, wr)


def _moe_kernel(src_ref, tg_ref, tv_ref, nv_ref, widx_ref, x_hbm, gm_ref, gf_ref, wr_ref, wg_ref, wu_ref, wd_ref, y_hbm,
                buf, hb_scr, w4_scr, gsem, ssem, *, tm):
    t = pl.program_id(0)
    e = pl.program_id(1)
    nt = pl.num_programs(0)
    slot = t % 2
    other = 1 - slot
    valid = tv_ref[t] == 1
    prev_valid = (t >= 1) & (tv_ref[jnp.maximum(t - 1, 0)] == 1)
    next_valid = (t + 1 < nt) & (tv_ref[jnp.minimum(t + 1, nt - 1)] == 1)

    def gather_start(tt, sl):
        def body(r, c):
            idx = jnp.maximum(src_ref[tt * tm + r], 0)
            pltpu.make_async_copy(x_hbm.at[pl.ds(idx, 1), :], buf.at[sl, pl.ds(r, 1), :], gsem.at[sl]).start()
            return c
        lax.fori_loop(0, tm, body, 0, unroll=8)

    def gather_wait(sl):
        pltpu.make_async_copy(x_hbm.at[pl.ds(0, tm), :], buf.at[sl], gsem.at[sl]).wait()

    def scatter_copy(tt, sl, r):
        return pltpu.make_async_copy(buf.at[sl, pl.ds(r, 1), :], y_hbm.at[pl.ds(src_ref[tt * tm + r], 1), :], ssem.at[0])

    def scatter_start(tt, sl):
        def body(r, c):
            scatter_copy(tt, sl, r).start()
            return c
        lax.fori_loop(0, nv_ref[tt], body, 0)

    def scatter_wait(tt, sl):
        def body(r, c):
            scatter_copy(tt, sl, r).wait()
            return c
        lax.fori_loop(0, nv_ref[tt], body, 0)

    @pl.when(e == 0)
    def _():
        @pl.when(t == 0)
        def _():
            gather_start(0, 0)

        @pl.when(prev_valid)
        def _():
            scatter_start(t - 1, other)

        @pl.when(valid)
        def _():
            gather_wait(slot)
            h = _rms(buf[slot], gm_ref[...])
            hb_scr[...] = h.astype(BF16)
            z = _router_logits(h, wr_ref)
            lane = lax.broadcasted_iota(jnp.int32, z.shape, 1)
            lanef = lane.astype(F32)
            grp = lane < N_GROUPS
            zg = jnp.where(grp, z, -jnp.inf)
            pg_top = 1.0 / jnp.sum(jnp.where(grp, jnp.exp(zg - jnp.max(zg, axis=1, keepdims=True)), 0.0), axis=1, keepdims=True)
            lo = N_GROUPS + tg_ref[t] * EXPERTS_PER_GROUP
            ing = (lane >= lo) & (lane < lo + EXPERTS_PER_GROUP)
            pf = _masked_softmax(z, ing)
            big = float(2 * LANES)
            m1 = jnp.max(jnp.where(ing, pf, -1.0), axis=1, keepdims=True)
            i1 = jnp.min(jnp.where(ing & (pf == m1), lanef, big), axis=1, keepdims=True)
            rest = ing & (lanef != i1)
            m2 = jnp.max(jnp.where(rest, pf, -1.0), axis=1, keepdims=True)
            i2 = jnp.min(jnp.where(rest & (pf == m2), lanef, big), axis=1, keepdims=True)
            tot = m1 + m2
            w4_scr[...] = jnp.where(lanef == i1, m1 / tot * pg_top, jnp.where(lanef == i2, m2 / tot * pg_top, 0.0))

    @pl.when(e == EXPERTS_PER_GROUP // 2)
    def _():
        @pl.when(prev_valid)
        def _():
            scatter_wait(t - 1, other)

        @pl.when(next_valid)
        def _():
            gather_start(t + 1, other)

    @pl.when(valid)
    def _():
        hb = hb_scr[...]
        lane = lax.broadcasted_iota(jnp.int32, w4_scr.shape, 1)
        col = N_GROUPS + tg_ref[t] * EXPERTS_PER_GROUP + e
        we = jnp.sum(jnp.where(lane == col, w4_scr[...], 0.0), axis=1, keepdims=True)
        act = jax.nn.silu(_dot(hb, wg_ref[0])) * _dot(hb, wu_ref[0])
        buf[slot] += _dot(act * we, wd_ref[0])

    @pl.when(valid & (e == EXPERTS_PER_GROUP - 1))
    def _():
        buf[slot] = _rms(buf[slot], gf_ref[...])


def _moe_final(x2d, g_moe, g_final, wr, w_gate, w_up, w_down, tm):
    n, d = x2d.shape
    g_top = _route_groups(x2d, g_moe, wr, min(n, 512))[:, 0]
    n_tiles = n // tm + N_GROUPS
    onehot = (g_top[:, None] == jnp.arange(N_GROUPS)[None, :]).astype(jnp.int32)
    counts = jnp.sum(onehot, axis=0)
    rank = jnp.sum((jnp.cumsum(onehot, axis=0) - onehot) * onehot, axis=1)
    padded = (counts + tm - 1) // tm * tm
    ends = jnp.cumsum(padded)
    base = ends - padded
    pos = base[g_top] + rank
    src = jnp.full((n_tiles * tm,), -1, jnp.int32).at[pos].set(jnp.arange(n, dtype=jnp.int32))
    tile_start = jnp.arange(n_tiles, dtype=jnp.int32) * tm
    tile_valid = (tile_start < ends[-1]).astype(jnp.int32)
    tile_group = jnp.minimum(jnp.sum((tile_start[:, None] >= ends[None, :]).astype(jnp.int32), axis=1), N_GROUPS - 1)
    tile_rows = jnp.clip(ends[tile_group] - padded[tile_group] + counts[tile_group] - tile_start, 0, tm)
    tile_rows = (tile_rows * tile_valid).astype(jnp.int32)
    n_valid = ends[-1] // tm
    last_group = tile_group[jnp.maximum(n_valid - 1, 0)]
    eidx = tile_group[:, None] * EXPERTS_PER_GROUP + jnp.arange(EXPERTS_PER_GROUP, dtype=jnp.int32)[None, :]
    widx = jnp.where(tile_valid[:, None] == 1, eidx, last_group * EXPERTS_PER_GROUP + EXPERTS_PER_GROUP - 1)
    widx = widx.reshape(-1).astype(jnp.int32)

    wmap = lambda t, e, src, tg, tv, nv, wi: (wi[t * EXPERTS_PER_GROUP + e], 0, 0)
    cmap = lambda t, e, src, tg, tv, nv, wi: (0, 0)
    grid_spec = pltpu.PrefetchScalarGridSpec(
        num_scalar_prefetch=5,
        grid=(n_tiles, EXPERTS_PER_GROUP),
        in_specs=[pl.BlockSpec(memory_space=pl.ANY),
                  pl.BlockSpec((1, d), cmap), pl.BlockSpec((1, d), cmap),
                  pl.BlockSpec(wr.shape, lambda t, e, src, tg, tv, nv, wi: (0, 0, 0)),
                  pl.BlockSpec((1, d, EXPERT_FF), wmap), pl.BlockSpec((1, d, EXPERT_FF), wmap),
                  pl.BlockSpec((1, EXPERT_FF, d), wmap)],
        out_specs=pl.BlockSpec(memory_space=pl.ANY),
        scratch_shapes=[pltpu.VMEM((2, tm, d), F32), pltpu.VMEM((tm, d), BF16), pltpu.VMEM((tm, LANES), F32),
                        pltpu.SemaphoreType.DMA((2,)), pltpu.SemaphoreType.DMA((1,))],
    )
    return pl.pallas_call(
        functools.partial(_moe_kernel, tm=tm),
        grid_spec=grid_spec,
        out_shape=jax.ShapeDtypeStruct((n, d), F32),
        compiler_params=_cparams(("arbitrary", "arbitrary")),
        name="moe_final_norm",
    )(src, tile_group, tile_valid, tile_rows, widx, x2d, g_moe, g_final, wr, w_gate, w_up, w_down)


def _finish(x2d, o_cmp, o_sel, o_win, gates, u, v, mk, mv, bsz, lw, chunk, moe_tm):
    n, d = x2d.shape
    t = n // bsz
    x1 = _mix(x2d, o_cmp, o_sel, o_win, gates, u, v, lw["w_sgu"], lw["b_sgu"], lw["g_nsa_out"], lw["g_sgu_out"],
              lw["w_out"], chunk, min(n, 256))
    hq = _rms_matmul(x1, lw["g_mem_norm"], lw["w_mem_q"], min(n, 512))
    o_m = _mem_attention(hq.reshape(bsz, t, -1), mk, mv)
    x2 = _matmul_res(o_m.reshape(n, -1), lw["w_mem_o"], x1, min(n, 512))
    return _moe_final(x2, lw["g_moe_norm"], lw["g_final"], lw["w_router"], lw["w_exp_gate"], lw["w_exp_up"],
                      lw["w_exp_down"], moe_tm)


def kernel(x_prompt, x_sample, cache_cmp_k, cache_cmp_v, cache_sel_k, cache_sel_v, cache_win_k, cache_win_v, cache_mem_k, cache_mem_v, page_table, mem_prompt, w_in, g_attn_norm, pe_cmp_k, w_cmp_k1, w_cmp_k2, pe_cmp_v, w_cmp_v1, w_cmp_v2, g_sgu_v, w_sgu, b_sgu, g_nsa_out, g_sgu_out, w_out, g_mem_norm, g_mem_src, w_mem_q, w_mem_k, w_mem_v, w_mem_o, g_moe_norm, w_router_group, w_router_expert, w_exp_gate, w_exp_up, w_exp_down, g_final):
    depth = w_in.shape[0]
    assert depth == 1, "single-layer trunk"
    bp, tp, d = x_prompt.shape
    bs, ts, _ = x_sample.shape
    n_pages = page_table.shape[1]
    page = cache_cmp_k.shape[2]
    past = n_pages * page
    assert ts < STRIDE and tp % 256 == 0 and page % STRIDE == 0
    row = lambda a: a[0].reshape(1, -1)

    w_in_p = w_in[0].astype(BF16)
    wr = jnp.concatenate([w_router_group[0], w_router_expert[0],
                          jnp.zeros((d, LANES - N_GROUPS - N_EXPERTS), F32)], axis=1)
    wr_hi = wr.astype(BF16)
    wr = jnp.stack([wr_hi, (wr - wr_hi.astype(F32)).astype(BF16)])
    lw = {
        "w_sgu": w_sgu[0], "b_sgu": b_sgu[0], "g_nsa_out": row(g_nsa_out), "g_sgu_out": row(g_sgu_out),
        "w_out": w_out[0].astype(BF16), "g_mem_norm": row(g_mem_norm), "w_mem_q": w_mem_q[0].astype(BF16),
        "w_mem_o": w_mem_o[0].astype(BF16), "g_moe_norm": row(g_moe_norm), "g_final": g_final.reshape(1, -1),
        "w_router": wr, "w_exp_gate": w_exp_gate[0], "w_exp_up": w_exp_up[0], "w_exp_down": w_exp_down[0],
    }
    g_attn = row(g_attn_norm)
    gsv = row(g_sgu_v)
    kv5 = lambda a, b, t: a.reshape(1, b, t, KV_HEADS, HEAD_DIM)

    np_ = bp * tp
    tabs_p = _rope_tables(jnp.arange(tp, dtype=jnp.int32))
    (q, qrot, kc, vc, ks, vs, kw, vw, gates, u, v) = _project(x_prompt.reshape(np_, d), g_attn, w_in_p, tabs_p, gsv, 256,
                                                              kv_seq=tp)
    pt_p = jnp.arange(np_ // page, dtype=jnp.int32).reshape(bp, tp // page)
    kcmp = _compress(kc, pt_p, page, pe_cmp_k[0], w_cmp_k1[0], w_cmp_k2[0])
    vcmp = _compress(vc, pt_p, page, pe_cmp_v[0], w_cmp_v1[0], w_cmp_v2[0])
    o_cmp, sel_t = _cmp_select(q.reshape(bp, tp, -1), kcmp, vcmp, 0, tp)
    qrot3 = qrot.reshape(bp, tp, -1)
    o_sel = _prompt_attention(qrot3, ks, vs, sel_t)
    o_win = _prompt_attention(qrot3, kw, vw)
    n_mem = mem_prompt.shape[1]
    mem_w = MEM_HEADS * MEM_HEAD_DIM
    mem2d = mem_prompt.reshape(bp * n_mem, d)
    mem4 = lambda a: a.reshape(bp, n_mem, MEM_HEADS, MEM_HEAD_DIM)
    mk_p = mem4(_rms_matmul(mem2d, row(g_mem_src), w_mem_k[0].astype(BF16), min(bp * n_mem, 512)))
    mv_p = mem4(_rms_matmul(mem2d, row(g_mem_src), w_mem_v[0].astype(BF16), min(bp * n_mem, 512)))
    y_p = _finish(x_prompt.reshape(np_, d), o_cmp.reshape(np_, -1), o_sel.reshape(np_, -1), o_win.reshape(np_, -1),
                  gates, u, v, mk_p, mv_p, bp, lw, CHUNK, min(np_, 1024))
    wbp = min(WINDOW, tp)
    kv5t = lambda a: a.reshape(bp, KV_HEADS, HEAD_DIM, a.shape[2]).transpose(0, 3, 1, 2)[None]
    outs_p = (kv5t(kc), kv5t(vc), kv5t(ks), kv5t(vs), kv5t(kw[:, :, -wbp:]), kv5t(vw[:, :, -wbp:]), mk_p[None], mv_p[None])

    ns_ = bs * ts
    pos_s = past + jnp.arange(ts, dtype=jnp.int32)
    tabs_s = tuple(jnp.tile(a, (bs, 1)) for a in _rope_tables(pos_s))
    (q, qrot, kc, vc, ks, vs, kw, vw, gates, u, v) = _project(x_sample.reshape(ns_, d), g_attn, w_in_p, tabs_s, gsv, ns_)
    pool = lambda c: c[0].transpose(0, 2, 3, 1).reshape(c.shape[1], KV_WIDTH, page)
    kcmp = _compress(pool(cache_cmp_k), page_table, page, pe_cmp_k[0], w_cmp_k1[0], w_cmp_k2[0])
    vcmp = _compress(pool(cache_cmp_v), page_table, page, pe_cmp_v[0], w_cmp_v1[0], w_cmp_v2[0])
    o_cmp, sel_t = _cmp_select(q.reshape(bs, ts, -1).astype(F32), kcmp, vcmp, past, past + ts)
    wb = cache_win_k.shape[2]
    wk = cache_win_k[0].transpose(0, 2, 3, 1).reshape(bs, KV_WIDTH, wb)
    wv = cache_win_v[0].transpose(0, 2, 3, 1).reshape(bs, KV_WIDTH, wb)
    ks3, vs3, kw3, vw3 = (a.reshape(bs, ts, KV_WIDTH) for a in (ks, vs, kw, vw))
    o_sel, o_win = _sample_attention(qrot.reshape(bs, ts, -1), pool(cache_sel_k), pool(cache_sel_v), page_table,
                                     ks3, vs3, sel_t, wk, wv, kw3, vw3)
    mem_ks, mem_vs = cache_mem_k[0], cache_mem_v[0]
    y_s = _finish(x_sample.reshape(ns_, d), o_cmp.reshape(ns_, -1), o_sel.reshape(ns_, -1), o_win.reshape(ns_, -1),
                  gates, u, v, mem_ks, mem_vs, bs, lw, ts, min(ns_, 128))
    def slide(cache_t, new):
        win = jnp.concatenate([cache_t, new.transpose(0, 2, 1)], axis=2)[:, :, -wb:]
        return win.reshape(bs, KV_HEADS, HEAD_DIM, wb).transpose(0, 3, 1, 2)[None]

    win_k_s = slide(wk, kw3)
    win_v_s = slide(wv, vw3)
    outs_s = (kv5(kc, bs, ts), kv5(vc, bs, ts), kv5(ks, bs, ts), kv5(vs, bs, ts), win_k_s, win_v_s,
              v.reshape(1, bs, ts, -1))

    return (y_p.reshape(bp, tp, d), y_s.reshape(bs, ts, d)) + outs_p + outs_s
```

```python
import functools

import jax
import jax.numpy as jnp
from jax import lax
from jax.experimental import pallas as pl
from jax.experimental.pallas import tpu as pltpu

F32 = jnp.float32
BF16 = jnp.bfloat16

N_HEADS = 16
HEAD_DIM = 64
KV_HEADS = 4
Q_PER_KV = N_HEADS // KV_HEADS
NSA_WIDTH = N_HEADS * HEAD_DIM
KV_WIDTH = KV_HEADS * HEAD_DIM
ROT_DIM = HEAD_DIM // 4
ROPE_THETA = 500000.0
L_CMP = 32
STRIDE = 16
CMP_R = L_CMP // STRIDE
CMP_HIDDEN = 128
L_SEL = 64
SEL_RATIO = L_SEL // STRIDE
SEL_INNER = (L_SEL - L_CMP) // STRIDE + 1
TOP_N = 16
WINDOW = 512
SGU_GROUPS = 8
SGU_GROUP_DIM = 128
CHUNK = 128
MEM_HEADS = 4
MEM_HEAD_DIM = 128
N_GROUPS = 4
EXPERTS_PER_GROUP = 4
N_EXPERTS = N_GROUPS * EXPERTS_PER_GROUP
EXPERT_FF = 512
EPS = 1e-6
NEG_INF = -1e30
BIG = 1e9
PAD_SCORE = -3e38
CMP_ROW_PITCH = STRIDE + 4
ATTN_SCALE = HEAD_DIM ** -0.5
MEM_SCALE = MEM_HEAD_DIM ** -0.5

LANES = 128
SUBLANES = 8
V7X_VMEM_LIMIT_BYTES = 60000 * 1024

GATE_COLS = LANES
N_GATES = 3 * N_HEADS
_SEG = {}
_off = 0
for _name, _w in (("q", NSA_WIDTH), ("kc", KV_WIDTH), ("vc", KV_WIDTH), ("ks", KV_WIDTH), ("vs", KV_WIDTH),
                  ("kw", KV_WIDTH), ("vw", KV_WIDTH), ("gt", N_GATES), ("u", 1024), ("v", 1024)):
    _SEG[_name] = (_off, _w)
    _off += _w
IN_COLS = _off


def _cparams(sem, vmem=V7X_VMEM_LIMIT_BYTES):
    return pltpu.CompilerParams(dimension_semantics=sem, vmem_limit_bytes=vmem)


def _dot(a, b):
    return jnp.dot(a.astype(BF16), b.astype(BF16), preferred_element_type=F32)


def _dot_nt(a, b):
    return lax.dot_general(a.astype(BF16), b.astype(BF16), (((1,), (1,)), ((), ())), preferred_element_type=F32)


def _dot_tn(a, b):
    return lax.dot_general(a.astype(BF16), b.astype(BF16), (((0,), (0,)), ((), ())), preferred_element_type=F32)


def _rms(x, g):
    return x * lax.rsqrt(jnp.mean(x * x, axis=-1, keepdims=True) + EPS) * g


def _masked_softmax(s, mask, axis=-1):
    s = jnp.where(mask, s, NEG_INF)
    m = jnp.max(s, axis=axis, keepdims=True)
    e = jnp.where(mask, jnp.exp(s - m), 0.0)
    return e * (1.0 / jnp.maximum(jnp.sum(e, axis=axis, keepdims=True), 1e-30))


def _const_spec(shape):
    nd = len(shape)
    return pl.BlockSpec(shape, lambda *_: (0,) * nd, pipeline_mode=pl.Buffered(1))


def _proj_kernel(x_ref, g_ref, w_ref, rc_ref, rs1_ref, rs2_ref, gsv_ref,
                 q_ref, qrot_ref, kc_ref, vc_ref, ks_ref, vs_ref, kw_ref, vw_ref, gates_ref, u_ref, v_ref, *, transpose_kv):
    hb = _rms(x_ref[...], g_ref[...]).astype(BF16)

    def put_kv(ref, val):
        if transpose_kv:
            ref[0] = val.T
        else:
            ref[...] = val

    def seg(name):
        lo, width = _SEG[name]
        return jnp.dot(hb, w_ref[:, lo:lo + width], preferred_element_type=F32)

    def rope(z):
        rc, rs1, rs2 = rc_ref[...], rs1_ref[...], rs2_ref[...]
        half = ROT_DIM // 2
        outs = []
        for c in range(z.shape[1] // LANES):
            zc = z[:, c * LANES:(c + 1) * LANES]
            outs.append(zc * rc + pltpu.roll(zc, LANES - half, 1) * rs1 + pltpu.roll(zc, half, 1) * rs2)
        return jnp.concatenate(outs, axis=1)

    q = seg("q")
    q_ref[...] = q.astype(BF16)
    qrot_ref[...] = rope(q).astype(BF16)
    put_kv(kc_ref, seg("kc"))
    put_kv(vc_ref, seg("vc"))
    put_kv(ks_ref, rope(seg("ks")))
    put_kv(vs_ref, seg("vs"))
    put_kv(kw_ref, rope(seg("kw")))
    put_kv(vw_ref, seg("vw"))
    g0 = _SEG["gt"][0]
    zt = jnp.dot(hb, w_ref[:, g0:IN_COLS], preferred_element_type=F32)
    u0, v0 = _SEG["u"][0] - g0, _SEG["v"][0] - g0
    gates_ref[...] = jax.nn.sigmoid(zt[:, :GATE_COLS])
    u_ref[...] = jax.nn.gelu(zt[:, u0:u0 + 1024])
    v_ref[...] = _rms(jax.nn.gelu(zt[:, v0:v0 + 1024]), gsv_ref[...])


def _project(x2d, g_attn, w_in_p, tables, g_sgu_v, tm, kv_seq=None):
    n, d = x2d.shape
    rc, rs1, rs2 = tables
    tt = rc.shape[0]
    nt = tt // tm
    row = lambda w: pl.BlockSpec((tm, w), lambda i: (i, 0))
    tab = pl.BlockSpec((tm, LANES), lambda i: (i % nt, 0))
    out_shapes = [jax.ShapeDtypeStruct((n, NSA_WIDTH), BF16), jax.ShapeDtypeStruct((n, NSA_WIDTH), BF16)]
    if kv_seq is None:
        out_shapes += [jax.ShapeDtypeStruct((n, KV_WIDTH), F32)] * 6
        kv_spec = row(KV_WIDTH)
    else:
        per = kv_seq // tm
        out_shapes += [jax.ShapeDtypeStruct((n // kv_seq, KV_WIDTH, kv_seq), F32)] * 6
        kv_spec = pl.BlockSpec((1, KV_WIDTH, tm), lambda i: (i // per, 0, i % per))
    out_shapes += [jax.ShapeDtypeStruct((n, GATE_COLS), F32), jax.ShapeDtypeStruct((n, 1024), F32),
                   jax.ShapeDtypeStruct((n, 1024), F32)]
    out_specs = [row(NSA_WIDTH), row(NSA_WIDTH)] + [kv_spec] * 6 + [row(GATE_COLS), row(1024), row(1024)]
    return pl.pallas_call(
        functools.partial(_proj_kernel, transpose_kv=kv_seq is not None),
        grid=(n // tm,),
        in_specs=[row(d), _const_spec((1, d)), _const_spec(w_in_p.shape), tab, tab, tab, _const_spec((1, 1024))],
        out_specs=out_specs,
        out_shape=out_shapes,
        compiler_params=_cparams(("parallel",)),
        name="in_proj",
    )(x2d, g_attn, w_in_p, rc, rs1, rs2, g_sgu_v)


def _rope_tables(pos):
    half = ROT_DIM // 2
    freqs = ROPE_THETA ** (-jnp.arange(half, dtype=F32) / half)
    ang = pos.astype(F32)[:, None] * freqs[None, :]
    cos, sin = jnp.cos(ang), jnp.sin(ang)
    t = pos.shape[0]
    ones = jnp.ones((t, HEAD_DIM - ROT_DIM), F32)
    zeros = jnp.zeros((t, HEAD_DIM - ROT_DIM), F32)
    zh = jnp.zeros((t, half), F32)
    rc = jnp.concatenate([cos, cos, ones], axis=1)
    rs1 = jnp.concatenate([-sin, zh, zeros], axis=1)
    rs2 = jnp.concatenate([zh, sin, zeros], axis=1)
    rep = LANES // HEAD_DIM
    return tuple(jnp.tile(a, (1, rep)) for a in (rc, rs1, rs2))


def _compress_kernel(pt_ref, pool_ref, pe_ref, w1_ref, w2_ref, out_ref, stage, buf, a_ref, sem, *, n_pages, page):
    b = pl.program_id(0)
    nb = pl.num_programs(0)
    n_rows = n_pages * page
    n_sub = n_rows // STRIDE
    slot = b % 2
    n_ct = KV_WIDTH // LANES
    ppl = pool_ref.shape[2] // page

    def page_copy(bb, sl, p):
        q = pt_ref[bb, p]
        src = pool_ref.at[q // ppl, :, pl.ds(pl.multiple_of((q % ppl) * page, page), page)]
        return pltpu.make_async_copy(src, stage.at[sl, :, pl.ds(pl.multiple_of(p * page, page), page)], sem.at[sl])

    def start_all(bb, sl):
        def body(p, carry):
            page_copy(bb, sl, p).start()
            return carry
        lax.fori_loop(0, n_pages, body, 0)

    def wait_all(bb, sl):
        def body(p, carry):
            page_copy(bb, sl, p).wait()
            return carry
        lax.fori_loop(0, n_pages, body, 0)

    @pl.when(b == 0)
    def _():
        start_all(0, 0)

    @pl.when(b + 1 < nb)
    def _():
        start_all(b + 1, 1 - slot)

    wait_all(b, slot)

    tch = min(n_rows, 512)
    for c in range(n_ct):
        for j in range(n_rows // tch):
            rows_t = stage[slot, c * LANES:(c + 1) * LANES, j * tch:(j + 1) * tch].T
            for n in range(tch // STRIDE):
                r0 = (j * (tch // STRIDE) + n) * CMP_ROW_PITCH
                buf[c, r0:r0 + STRIDE, :] = rows_t[n * STRIDE:(n + 1) * STRIDE, :]

    pev = _dot(pe_ref[...], w1_ref[...])
    low = lax.broadcasted_iota(jnp.int32, (n_sub, LANES), 1) < HEAD_DIM
    row = lax.broadcasted_iota(jnp.int32, (n_sub, 4 * HEAD_DIM), 0)
    for c in range(n_ct):
        src = buf.at[c]
        for sp in range(STRIDE // 2):
            x0 = src[pl.ds(2 * sp, n_sub, stride=CMP_ROW_PITCH), :]
            x1 = src[pl.ds(2 * sp + 1, n_sub, stride=CMP_ROW_PITCH), :]
            a_ref[0, :, sp * LANES:(sp + 1) * LANES] = jnp.where(low, x0, pltpu.roll(x1, HEAD_DIM, 1)).astype(BF16)
            a_ref[1, :, sp * LANES:(sp + 1) * LANES] = jnp.where(low, pltpu.roll(x0, HEAD_DIM, 1), x1).astype(BF16)
        for k in range(2):
            pm = jnp.dot(a_ref[k], w1_ref[...], preferred_element_type=F32)
            part0 = pm[:, :CMP_HIDDEN] + pev[0:1, :CMP_HIDDEN]
            part1 = pm[:, CMP_HIDDEN:] + pev[1:2, CMP_HIDDEN:]
            hsum = part0 + pltpu.roll(part1, n_sub - 1, 0)
            o = _dot(jax.nn.silu(hsum), w2_ref[...])
            out_ref[0, 2 * c + k] = jnp.where(row < n_sub - 1, o, 0.0).astype(BF16)


def _compress(pool, page_table, page, pe, w1, w2):
    bsz, n_pages = page_table.shape
    n_rows = n_pages * page
    n_sub = n_rows // STRIDE
    kdim = STRIDE * HEAD_DIM
    n_ct = KV_WIDTH // LANES
    pe8 = jnp.zeros((SUBLANES, kdim), F32).at[:CMP_R].set(pe.reshape(CMP_R, kdim))
    w1c = w1.reshape(CMP_R, kdim, CMP_HIDDEN).transpose(1, 0, 2).reshape(kdim, CMP_R * CMP_HIDDEN).astype(BF16)
    w2t = jnp.tile(w2, (1, 4)).astype(BF16)
    grid_spec = pltpu.PrefetchScalarGridSpec(
        num_scalar_prefetch=1,
        grid=(bsz,),
        in_specs=[pl.BlockSpec(memory_space=pl.ANY),
                  pl.BlockSpec((SUBLANES, kdim), lambda b, pt: (0, 0)),
                  pl.BlockSpec(w1c.shape, lambda b, pt: (0, 0)),
                  pl.BlockSpec(w2t.shape, lambda b, pt: (0, 0))],
        out_specs=pl.BlockSpec((1, KV_HEADS, n_sub, 4 * HEAD_DIM), lambda b, pt: (b, 0, 0, 0)),
        scratch_shapes=[pltpu.VMEM((2, KV_WIDTH, n_rows), F32), pltpu.VMEM((n_ct, n_sub * CMP_ROW_PITCH, LANES), F32),
                        pltpu.VMEM((2, n_sub, kdim), BF16), pltpu.SemaphoreType.DMA((2,))],
    )
    return pl.pallas_call(
        functools.partial(_compress_kernel, n_pages=n_pages, page=page),
        grid_spec=grid_spec,
        out_shape=jax.ShapeDtypeStruct((bsz, KV_HEADS, n_sub, 4 * HEAD_DIM), BF16),
        compiler_params=_cparams(("arbitrary",)),
        name="compress",
    )(page_table, pool, pe8, w1c, w2t)


def _cmpsel_kernel(q_ref, k_ref, v_ref, st_ref, o_ref, sc_ref, *, pos0, nc, ns, tq):
    i = pl.program_id(2)
    n_sub = k_ref.shape[2]
    ns_pad = st_ref.shape[0]
    qf = q_ref[0].astype(F32)
    lane_head = lax.broadcasted_iota(jnp.int32, (tq, Q_PER_KV * HEAD_DIM), 1) // HEAD_DIM
    q_rows = jnp.concatenate([jnp.where(lane_head == qi, qf, 0.0) for qi in range(Q_PER_KV)], axis=0)
    s = _dot_nt(q_rows, k_ref[0, 0]) * ATTN_SCALE
    qpos = pos0 + i * tq + lax.broadcasted_iota(jnp.int32, (tq, 1), 0)
    kidx = lax.broadcasted_iota(jnp.int32, (1, n_sub), 1)
    mask = (kidx * STRIDE + (L_CMP - 1) <= qpos) & (kidx < nc)
    p = _masked_softmax(s, jnp.concatenate([mask] * Q_PER_KV, axis=0))
    o_rows = _dot(p, v_ref[0, 0])
    o = None
    pg = None
    for qi in range(Q_PER_KV):
        part = jnp.where(lane_head == qi, o_rows[qi * tq:(qi + 1) * tq], 0.0)
        o = part if o is None else o + part
        pq = p[qi * tq:(qi + 1) * tq]
        pg = pq if pg is None else pg + pq
    o_ref[0] = o
    pg_hi = pg.astype(BF16)
    pg_lo = (pg - pg_hi.astype(F32)).astype(BF16)
    st = st_ref[...]
    ps_t = _dot_nt(st, pg_hi) + _dot_nt(st, pg_lo)
    blk = lax.broadcasted_iota(jnp.int32, (ns_pad, tq), 0)
    qpos_t = pos0 + i * tq + lax.broadcasted_iota(jnp.int32, (ns_pad, tq), 1)
    cur = qpos_t // L_SEL
    forced = (blk == 0) | (blk == cur) | (blk == cur - 1)
    future = blk * L_SEL > qpos_t
    sc = jnp.where(forced, BIG, jnp.where(future, -BIG, ps_t))
    sc_ref[0, 0] = jnp.where(blk < ns, sc, PAD_SCORE)


def _rank_kernel(sc_ref, sel_ref, *, ns, topn):
    sc = sc_ref[0, 0]
    blk = lax.broadcasted_iota(jnp.int32, sc.shape, 0)
    sel_ref[0, 0] = jnp.zeros(sc.shape, F32)

    def body(j, c):
        row = sc_ref[0, 0, pl.ds(j, 1), :]
        ahead = (sc > row) | ((sc == row) & (blk < j))
        rank = jnp.sum(ahead.astype(F32), axis=0, keepdims=True)
        sel_ref[0, 0, pl.ds(j, 1), :] = (rank < topn).astype(F32)
        return c

    lax.fori_loop(0, ns, body, 0)


def _cmp_select(q, kcmp, vcmp, pos0, n_keys):
    bsz, t, _ = q.shape
    n_sub = kcmp.shape[2]
    nc = n_sub - CMP_R + 1
    ns = -(-n_keys // L_SEL)
    ns_pad = -(-ns // SUBLANES) * SUBLANES
    topn = min(TOP_N, ns)
    tq = min(t, 512)
    gw = Q_PER_KV * HEAD_DIM
    cidx = jnp.arange(n_sub)
    st = ((cidx[None, :] // SEL_RATIO == jnp.arange(ns_pad)[:, None]) & (cidx[None, :] % SEL_RATIO < SEL_INNER)
          & (cidx[None, :] < nc)).astype(BF16)
    o_cmp, sc = pl.pallas_call(
        functools.partial(_cmpsel_kernel, pos0=pos0, nc=nc, ns=ns, tq=tq),
        grid=(bsz, KV_HEADS, t // tq),
        in_specs=[pl.BlockSpec((1, tq, gw), lambda b, g, i: (b, i, g)),
                  pl.BlockSpec((1, 1, n_sub, gw), lambda b, g, i: (b, g, 0, 0)),
                  pl.BlockSpec((1, 1, n_sub, gw), lambda b, g, i: (b, g, 0, 0)),
                  pl.BlockSpec((ns_pad, n_sub), lambda b, g, i: (0, 0))],
        out_specs=[pl.BlockSpec((1, tq, gw), lambda b, g, i: (b, i, g)),
                   pl.BlockSpec((1, 1, ns_pad, tq), lambda b, g, i: (b, g, 0, i))],
        out_shape=[jax.ShapeDtypeStruct((bsz, t, NSA_WIDTH), F32),
                   jax.ShapeDtypeStruct((bsz, KV_HEADS, ns_pad, t), F32)],
        compiler_params=_cparams(("parallel", "parallel", "parallel")),
        name="cmp_scores",
    )(q, kcmp, vcmp, st)
    fold = t < LANES
    if fold:
        sc = sc.transpose(2, 0, 1, 3).reshape(1, 1, ns_pad, bsz * KV_HEADS * t)
    nb, ng, _, width = sc.shape
    tl = min(width, 2048)
    sel = pl.pallas_call(
        functools.partial(_rank_kernel, ns=ns, topn=topn),
        grid=(nb, ng, width // tl),
        in_specs=[pl.BlockSpec((1, 1, ns_pad, tl), lambda b, g, i: (b, g, 0, i))],
        out_specs=pl.BlockSpec((1, 1, ns_pad, tl), lambda b, g, i: (b, g, 0, i)),
        out_shape=jax.ShapeDtypeStruct(sc.shape, F32),
        compiler_params=_cparams(("parallel", "parallel", "parallel")),
        name="rank_select",
    )(sc)
    if fold:
        sel = sel.reshape(ns_pad, bsz, KV_HEADS, t).transpose(1, 2, 0, 3)
    return o_cmp, sel


def _pattn_kernel(*refs, mode, t, tq, ck):
    if mode == "sel":
        q_ref, k_ref, v_ref, sel_ref, et_ref, o_ref, kc_scr, vt_scr, m_scr, l_scr, acc_scr = refs
    else:
        q_ref, k_ref, v_ref, o_ref, kc_scr, vt_scr, m_scr, l_scr, acc_scr = refs
    i = pl.program_id(2)
    n_chunks = t // ck
    last = (i + 1) * (tq // ck) - 1

    @pl.when(i == 0)
    def _():
        zeros = jnp.zeros((ck, LANES - HEAD_DIM), F32)
        for c in range(n_chunks):
            kc_scr[c] = jnp.concatenate([k_ref[0, :, c * ck:(c + 1) * ck].T, zeros], axis=1).astype(BF16)
            vt_scr[c] = v_ref[0, :, c * ck:(c + 1) * ck].astype(BF16)

    qf = q_ref[0].astype(F32) * ATTN_SCALE
    q_heads = []
    for qi in range(Q_PER_KV):
        tile = qf[:, (qi // 2) * LANES:(qi // 2 + 1) * LANES]
        q_heads.append((tile if qi % 2 == 0 else pltpu.roll(tile, HEAD_DIM, 1)).astype(BF16))
    q_all = jnp.concatenate(q_heads, axis=0)
    qpos = i * tq + lax.broadcasted_iota(jnp.int32, (1, tq), 1)
    m_scr[...] = jnp.full(m_scr.shape, NEG_INF, F32)
    l_scr[...] = jnp.zeros(l_scr.shape, F32)
    acc_scr[...] = jnp.zeros(acc_scr.shape, F32)
    if mode == "sel":
        sel_b = sel_ref[0, 0].astype(BF16)
        n_steps = last + 1
    else:
        n_steps = last - jnp.maximum(i * tq - WINDOW, 0) // ck + 1

    def chunk(c):
        kpos = c * ck + lax.broadcasted_iota(jnp.int32, (ck, 1), 0)
        if mode == "sel":
            chosen = jnp.dot(et_ref[c], sel_b, preferred_element_type=F32)
            ok = (chosen > 0.5) & (kpos <= qpos)
        else:
            ok = (kpos <= qpos) & (kpos > qpos - WINDOW)
        bias = jnp.where(ok, 0.0, NEG_INF)
        s = _dot_nt(kc_scr[c], q_all) + jnp.concatenate([bias] * Q_PER_KV, axis=1)
        m_old = m_scr[...]
        m_new = jnp.maximum(m_old, jnp.max(s, axis=0, keepdims=True))
        e = jnp.exp(s - m_new)
        alpha = jnp.exp(m_old - m_new)
        l_scr[...] = l_scr[...] * alpha + jnp.sum(e, axis=0, keepdims=True)
        m_scr[...] = m_new
        acc_scr[...] = acc_scr[...] * alpha + jnp.dot(vt_scr[c], e.astype(BF16), preferred_element_type=F32)

    def body(step, carry):
        chunk(last - step)
        return carry

    lax.fori_loop(0, n_steps, body, 0)
    out = acc_scr[...] * (1.0 / jnp.maximum(l_scr[...], 1e-30))
    o_ref[0] = jnp.concatenate([out[:, qi * tq:(qi + 1) * tq] for qi in range(Q_PER_KV)], axis=0).T


def _prompt_attention(qrot, k, v, sel_t=None):
    bsz, t, _ = qrot.shape
    tq, ck = 512, 512
    gw = Q_PER_KV * HEAD_DIM
    mode = "win" if sel_t is None else "sel"
    assert t % tq == 0 and tq % ck == 0 and WINDOW % ck == 0 and ck % L_SEL == 0
    in_specs = [pl.BlockSpec((1, tq, gw), lambda b, g, i: (b, i, g)),
                pl.BlockSpec((1, HEAD_DIM, t), lambda b, g, i: (b, g, 0)),
                pl.BlockSpec((1, HEAD_DIM, t), lambda b, g, i: (b, g, 0))]
    args = [qrot, k, v]
    if mode == "sel":
        ns_pad = sel_t.shape[2]
        et = (jnp.arange(t)[:, None] // L_SEL == jnp.arange(ns_pad)[None, :]).astype(BF16).reshape(t // ck, ck, ns_pad)
        in_specs += [pl.BlockSpec((1, 1, ns_pad, tq), lambda b, g, i: (b, g, 0, i)),
                     pl.BlockSpec((t // ck, ck, ns_pad), lambda b, g, i: (0, 0, 0))]
        args += [sel_t, et]
    return pl.pallas_call(
        functools.partial(_pattn_kernel, mode=mode, t=t, tq=tq, ck=ck),
        grid=(bsz, KV_HEADS, t // tq),
        in_specs=in_specs,
        out_specs=pl.BlockSpec((1, tq, gw), lambda b, g, i: (b, i, g)),
        out_shape=jax.ShapeDtypeStruct((bsz, t, NSA_WIDTH), F32),
        scratch_shapes=[pltpu.VMEM((t // ck, ck, LANES), BF16), pltpu.VMEM((t // ck, HEAD_DIM, ck), BF16),
                        pltpu.VMEM((1, Q_PER_KV * tq), F32), pltpu.VMEM((1, Q_PER_KV * tq), F32),
                        pltpu.VMEM((HEAD_DIM, Q_PER_KV * tq), F32)],
        compiler_params=_cparams(("parallel", "parallel", "arbitrary")),
        name="prompt_attn_" + mode,
    )(*args)


def _sattn_kernel(pt_ref, qbd_ref, kpool_ref, vpool_ref, ksn_ref, vsn_ref, selc_ref, wk_ref, wv_ref, kwn_ref, vwn_ref,
                  osel_ref, owin_ref, kbuf, vbuf, m_scr, l_scr, acc_scr, sem, *, n_pages, ppc, page, tdec):
    b = pl.program_id(0)
    c = pl.program_id(1)
    nb = pl.num_programs(0)
    nch = n_pages // ppc
    step = b * nch + c
    slot = step % 2
    rows = ppc * page
    ncol = qbd_ref.shape[2]

    def copies(bb, cc, sl, p):
        dst_k = kbuf.at[sl, :, pl.ds(pl.multiple_of(p * page, page), page)]
        dst_v = vbuf.at[sl, :, pl.ds(pl.multiple_of(p * page, page), page)]
        pid = pt_ref[bb, cc * ppc + p]
        return (pltpu.make_async_copy(kpool_ref.at[pid], dst_k, sem.at[0, sl]),
                pltpu.make_async_copy(vpool_ref.at[pid], dst_v, sem.at[1, sl]))

    def start_all(bb, cc, sl):
        def body(p, carry):
            ck, cv = copies(bb, cc, sl, p)
            ck.start()
            cv.start()
            return carry
        lax.fori_loop(0, ppc, body, 0)

    def wait_all(bb, cc, sl):
        def body(p, carry):
            ck, cv = copies(bb, cc, sl, p)
            ck.wait()
            cv.wait()
            return carry
        lax.fori_loop(0, ppc, body, 0)

    @pl.when(step == 0)
    def _():
        start_all(0, 0, 0)

    @pl.when(step + 1 < nb * nch)
    def _():
        nxt = step + 1
        start_all(nxt // nch, nxt % nch, 1 - slot)

    wait_all(b, c, slot)

    @pl.when(c == 0)
    def _():
        m_scr[...] = jnp.full(m_scr.shape, NEG_INF, F32)
        l_scr[...] = jnp.zeros(l_scr.shape, F32)
        acc_scr[...] = jnp.zeros(acc_scr.shape, F32)

    qbd = qbd_ref[0]
    nblk = rows // L_SEL
    s3 = (_dot_tn(kbuf[slot], qbd) * ATTN_SCALE).reshape(nblk, L_SEL, ncol)
    blk0 = pl.multiple_of(c * nblk, SUBLANES)
    chosen = (selc_ref[0, pl.ds(blk0, nblk), :] > 0.5)[:, None, :]
    s3 = jnp.where(chosen, s3, NEG_INF)
    m_old = m_scr[...]
    m_new = jnp.maximum(m_old, jnp.max(jnp.max(s3, axis=0), axis=0, keepdims=True))
    e3 = jnp.where(chosen, jnp.exp(s3 - m_new[None]), 0.0)
    alpha = jnp.exp(m_old - m_new)
    l_scr[...] = l_scr[...] * alpha + jnp.sum(jnp.sum(e3, axis=0), axis=0, keepdims=True)
    m_scr[...] = m_new
    acc_scr[...] = acc_scr[...] * alpha + _dot(vbuf[slot], e3.reshape(rows, ncol))

    @pl.when(c == nch - 1)
    def _():
        tcol = lax.broadcasted_iota(jnp.int32, (tdec, ncol), 1) % tdec
        jrow = lax.broadcasted_iota(jnp.int32, (tdec, ncol), 0)
        causal_new = jrow <= tcol
        sel_last = selc_ref[0, pl.ds(nch * nblk, 1), :] > 0.5
        ok_new = causal_new & sel_last
        s_new = jnp.where(ok_new, _dot(ksn_ref[0], qbd) * ATTN_SCALE, NEG_INF)
        m_old2 = m_scr[...]
        m_fin = jnp.maximum(m_old2, jnp.max(s_new, axis=0, keepdims=True))
        e_new = jnp.where(ok_new, jnp.exp(s_new - m_fin), 0.0)
        alpha2 = jnp.exp(m_old2 - m_fin)
        l_fin = l_scr[...] * alpha2 + jnp.sum(e_new, axis=0, keepdims=True)
        inv = 1.0 / jnp.maximum(l_fin, 1e-30)
        acc = acc_scr[...] * alpha2 + _dot_tn(vsn_ref[0], e_new)
        osel_ref[0] = acc * inv

        wb = wk_ref.shape[2]
        jw = lax.broadcasted_iota(jnp.int32, (wb, ncol), 0)
        tw = lax.broadcasted_iota(jnp.int32, (wb, ncol), 1) % tdec
        ok_c = jw + (WINDOW - wb) > tw
        s_c = jnp.where(ok_c, _dot_tn(wk_ref[0], qbd) * ATTN_SCALE, NEG_INF)
        s_n = jnp.where(causal_new, _dot(kwn_ref[0], qbd) * ATTN_SCALE, NEG_INF)
        m_w = jnp.maximum(jnp.max(s_c, axis=0, keepdims=True), jnp.max(s_n, axis=0, keepdims=True))
        e_c = jnp.where(ok_c, jnp.exp(s_c - m_w), 0.0)
        e_n = jnp.where(causal_new, jnp.exp(s_n - m_w), 0.0)
        inv_w = 1.0 / jnp.maximum(jnp.sum(e_c, axis=0, keepdims=True) + jnp.sum(e_n, axis=0, keepdims=True), 1e-30)
        ow = _dot(wv_ref[0], e_c) + _dot_tn(vwn_ref[0], e_n)
        owin_ref[0] = ow * inv_w


def _sample_attention(qrot, kpool, vpool, page_table, ks_new, vs_new, sel_t, wk, wv, kw_new, vw_new):
    bsz, tdec, _ = qrot.shape
    n_pages = page_table.shape[1]
    page = kpool.shape[2]
    ppc = min(n_pages, 32)
    assert n_pages % ppc == 0 and page % L_SEL == 0
    nch = n_pages // ppc
    rows = ppc * page
    ncol = KV_HEADS * Q_PER_KV * tdec
    ns_pad = sel_t.shape[2]
    wb = wk.shape[2]
    q5 = qrot.reshape(bsz, tdec, KV_HEADS, Q_PER_KV, HEAD_DIM).transpose(0, 2, 4, 3, 1)
    eye = jnp.eye(KV_HEADS, dtype=qrot.dtype)
    qbd = (q5[:, :, :, None] * eye[None, :, None, :, None, None]).reshape(bsz, KV_WIDTH, ncol)
    selc = jnp.broadcast_to(sel_t.transpose(0, 2, 1, 3)[:, :, :, None, :], (bsz, ns_pad, KV_HEADS, Q_PER_KV, tdec))
    selc = selc.reshape(bsz, ns_pad, ncol)
    per_b = lambda shape: pl.BlockSpec((1,) + shape, lambda b, c, pt: (b, 0, 0))
    grid_spec = pltpu.PrefetchScalarGridSpec(
        num_scalar_prefetch=1,
        grid=(bsz, nch),
        in_specs=[per_b((KV_WIDTH, ncol)),
                  pl.BlockSpec(memory_space=pl.ANY), pl.BlockSpec(memory_space=pl.ANY),
                  per_b((tdec, KV_WIDTH)), per_b((tdec, KV_WIDTH)),
                  per_b((ns_pad, ncol)),
                  per_b((KV_WIDTH, wb)), per_b((KV_WIDTH, wb)),
                  per_b((tdec, KV_WIDTH)), per_b((tdec, KV_WIDTH))],
        out_specs=[per_b((KV_WIDTH, ncol)), per_b((KV_WIDTH, ncol))],
        scratch_shapes=[pltpu.VMEM((2, KV_WIDTH, rows), F32), pltpu.VMEM((2, KV_WIDTH, rows), F32),
                        pltpu.VMEM((1, ncol), F32), pltpu.VMEM((1, ncol), F32), pltpu.VMEM((KV_WIDTH, ncol), F32),
                        pltpu.SemaphoreType.DMA((2, 2))],
    )
    o_sel, o_win = pl.pallas_call(
        functools.partial(_sattn_kernel, n_pages=n_pages, ppc=ppc, page=page, tdec=tdec),
        grid_spec=grid_spec,
        out_shape=[jax.ShapeDtypeStruct((bsz, KV_WIDTH, ncol), F32)] * 2,
        compiler_params=_cparams(("arbitrary", "arbitrary")),
        name="sample_attn",
    )(page_table, qbd, kpool, vpool, ks_new, vs_new, selc, wk, wv, kw_new, vw_new)

    def unpack(o):
        o6 = o.reshape(bsz, KV_HEADS, HEAD_DIM, KV_HEADS, Q_PER_KV, tdec)
        diag = jnp.stack([o6[:, g, :, g] for g in range(KV_HEADS)], axis=1)
        return diag.transpose(0, 4, 1, 3, 2).reshape(bsz, tdec, NSA_WIDTH)

    return unpack(o_sel), unpack(o_win)


def _mix_kernel(x_ref, oc_ref, os_ref, ow_ref, gt_ref, u_ref, v_ref, wsm_ref, bs_ref, gn_ref, gs_ref, eg_ref, wout_ref,
                o_ref, *, chunk):
    r = x_ref.shape[0]
    g = gt_ref[...]
    g_hi = g.astype(BF16)
    g_lo = (g - g_hi.astype(F32)).astype(BF16)
    onsa = None
    for j, branch in enumerate((oc_ref, os_ref, ow_ref)):
        ge = _dot(g_hi, eg_ref[j]) + _dot(g_lo, eg_ref[j])
        term = ge * branch[...]
        onsa = term if onsa is None else onsa + term
    onsa = _rms(onsa, gn_ref[...])
    ii = lax.broadcasted_iota(jnp.int32, (r, r), 0)
    jj = lax.broadcasted_iota(jnp.int32, (r, r), 1)
    tri = (ii // chunk == jj // chunk) & (jj % chunk <= ii % chunk)
    cols = []
    for gi in range(SGU_GROUPS):
        sl = slice(gi * SGU_GROUP_DIM, (gi + 1) * SGU_GROUP_DIM)
        ws = jnp.where(tri, wsm_ref[gi], 0.0)
        mixed = _dot(ws, v_ref[:, sl]) + bs_ref[:, gi:gi + 1]
        cols.append(u_ref[:, sl] * mixed)
    osgu = _rms(jnp.concatenate(cols, axis=1), gs_ref[...])
    o_ref[...] = x_ref[...] + _dot(jnp.concatenate([onsa, osgu], axis=1), wout_ref[...])


def _gate_expanders():
    c = jnp.arange(GATE_COLS)[None, :, None]
    lane = jnp.arange(NSA_WIDTH)[None, None, :]
    j = jnp.arange(3)[:, None, None]
    return (c == (lane // HEAD_DIM) * 3 + j).astype(BF16)


def _mix(x2d, o_cmp, o_sel, o_win, gates, u, v, w_sgu, b_sgu, g_nsa_out, g_sgu_out, w_out_b, chunk, r):
    n, d = x2d.shape
    rep = r // chunk
    pick = (jnp.arange(r)[:, None] % chunk == jnp.arange(chunk)[None, :]).astype(F32)
    wsm = jnp.einsum("ia,gab,jb->gij", pick, w_sgu[:, :chunk, :chunk], pick, precision=lax.Precision.HIGHEST)
    bs = jnp.tile(b_sgu[:, :chunk].T, (rep, 1))
    row = lambda w: pl.BlockSpec((r, w), lambda i: (i, 0))
    return pl.pallas_call(
        functools.partial(_mix_kernel, chunk=chunk),
        grid=(n // r,),
        in_specs=[row(d), row(NSA_WIDTH), row(NSA_WIDTH), row(NSA_WIDTH), row(GATE_COLS), row(1024), row(1024),
                  _const_spec(wsm.shape), _const_spec(bs.shape), _const_spec((1, NSA_WIDTH)), _const_spec((1, 1024)),
                  _const_spec((3, GATE_COLS, NSA_WIDTH)), _const_spec(w_out_b.shape)],
        out_specs=row(d),
        out_shape=jax.ShapeDtypeStruct((n, d), F32),
        compiler_params=_cparams(("parallel",)),
        name="mix_out_proj",
    )(x2d, o_cmp, o_sel, o_win, gates, u, v, wsm, bs, g_nsa_out, g_sgu_out, _gate_expanders(), w_out_b)


def _rms_matmul_kernel(x_ref, g_ref, w_ref, o_ref):
    o_ref[...] = _dot(_rms(x_ref[...], g_ref[...]), w_ref[...])


def _rms_matmul(x2d, g, w_b, tm):
    n, d = x2d.shape
    m = w_b.shape[1]
    return pl.pallas_call(
        _rms_matmul_kernel,
        grid=(n // tm,),
        in_specs=[pl.BlockSpec((tm, d), lambda i: (i, 0)), _const_spec((1, d)), _const_spec(w_b.shape)],
        out_specs=pl.BlockSpec((tm, m), lambda i: (i, 0)),
        out_shape=jax.ShapeDtypeStruct((n, m), F32),
        compiler_params=_cparams(("parallel",)),
        name="rms_matmul",
    )(x2d, g, w_b)


def _matmul_res_kernel(a_ref, w_ref, r_ref, o_ref):
    o_ref[...] = r_ref[...] + _dot(a_ref[...], w_ref[...])


def _matmul_res(a2d, w_b, res, tm):
    n, k = a2d.shape
    m = w_b.shape[1]
    return pl.pallas_call(
        _matmul_res_kernel,
        grid=(n // tm,),
        in_specs=[pl.BlockSpec((tm, k), lambda i: (i, 0)), _const_spec(w_b.shape), pl.BlockSpec((tm, m), lambda i: (i, 0))],
        out_specs=pl.BlockSpec((tm, m), lambda i: (i, 0)),
        out_shape=jax.ShapeDtypeStruct((n, m), F32),
        compiler_params=_cparams(("parallel",)),
        name="matmul_residual",
    )(a2d, w_b, res)


def _memattn_kernel(q_ref, k_ref, v_ref, o_ref):
    q = q_ref[0]
    outs = []
    for h in range(MEM_HEADS):
        sl = slice(h * MEM_HEAD_DIM, (h + 1) * MEM_HEAD_DIM)
        s = _dot_nt(q[:, sl], k_ref[0, :, h, :]) * MEM_SCALE
        p = _masked_softmax(s, jnp.ones(s.shape, dtype=jnp.bool_))
        outs.append(_dot(p, v_ref[0, :, h, :]))
    o_ref[0] = jnp.concatenate(outs, axis=1)


def _mem_attention(hq, mk, mv):
    bsz, t, w = hq.shape
    m = mk.shape[1]
    tq = min(t, 512)
    return pl.pallas_call(
        _memattn_kernel,
        grid=(bsz, t // tq),
        in_specs=[pl.BlockSpec((1, tq, w), lambda b, i: (b, i, 0)),
                  pl.BlockSpec((1, m, MEM_HEADS, MEM_HEAD_DIM), lambda b, i: (b, 0, 0, 0)),
                  pl.BlockSpec((1, m, MEM_HEADS, MEM_HEAD_DIM), lambda b, i: (b, 0, 0, 0))],
        out_specs=pl.BlockSpec((1, tq, w), lambda b, i: (b, i, 0)),
        out_shape=jax.ShapeDtypeStruct((bsz, t, w), F32),
        compiler_params=_cparams(("parallel", "parallel")),
        name="mem_attn",
    )(hq, mk, mv)


def _router_logits(h, wr_ref):
    h_hi = h.astype(BF16)
    h_lo = (h - h_hi.astype(F32)).astype(BF16)
    w_hi, w_lo = wr_ref[0], wr_ref[1]
    return _dot(h_hi, w_hi) + _dot(h_hi, w_lo) + _dot(h_lo, w_hi)


def _router_kernel(x_ref, g_ref, wr_ref, gtop_ref):
    z = _router_logits(_rms(x_ref[...], g_ref[...]), wr_ref)
    lane = lax.broadcasted_iota(jnp.int32, z.shape, 1)
    zg = jnp.where(lane < N_GROUPS, z, -jnp.inf)
    m = jnp.max(zg, axis=1, keepdims=True)
    first = jnp.min(jnp.where(zg == m, lane.astype(F32), float(LANES)), axis=1, keepdims=True)
    gtop_ref[...] = first.astype(jnp.int32)


def _route_groups(x2d, g_moe, wr, tm):
    n, d = x2d.shape
    return pl.pallas_call(
        _router_kernel,
        grid=(n // tm,),
        in_specs=[pl.BlockSpec((tm, d), lambda i: (i, 0)), _const_spec((1, d)), _const_spec(wr.shape)],
        out_specs=pl.BlockSpec((tm, 1), lambda i: (i, 0)),
        out_shape=jax.ShapeDtypeStruct((n, 1), jnp.int32),
        compiler_params=_cparams(("parallel",)),
        name="route_groups",
    )(x2d, g_moe, wr)


def _moe_kernel(src_ref, tg_ref, tv_ref, nv_ref, widx_ref, x_hbm, gm_ref, gf_ref, wr_ref, wg_ref, wu_ref, wd_ref, y_hbm,
                buf, hb_scr, w4_scr, gsem, ssem, *, tm):
    t = pl.program_id(0)
    e = pl.program_id(1)
    nt = pl.num_programs(0)
    slot = t % 2
    other = 1 - slot
    valid = tv_ref[t] == 1
    prev_valid = (t >= 1) & (tv_ref[jnp.maximum(t - 1, 0)] == 1)
    next_valid = (t + 1 < nt) & (tv_ref[jnp.minimum(t + 1, nt - 1)] == 1)

    def gather_start(tt, sl):
        def body(r, c):
            idx = jnp.maximum(src_ref[tt * tm + r], 0)
            pltpu.make_async_copy(x_hbm.at[pl.ds(idx, 1), :], buf.at[sl, pl.ds(r, 1), :], gsem.at[sl]).start()
            return c
        lax.fori_loop(0, tm, body, 0, unroll=8)

    def gather_wait(sl):
        pltpu.make_async_copy(x_hbm.at[pl.ds(0, tm), :], buf.at[sl], gsem.at[sl]).wait()

    def scatter_copy(tt, sl, r):
        return pltpu.make_async_copy(buf.at[sl, pl.ds(r, 1), :], y_hbm.at[pl.ds(src_ref[tt * tm + r], 1), :], ssem.at[0])

    def scatter_start(tt, sl):
        def body(r, c):
            scatter_copy(tt, sl, r).start()
            return c
        lax.fori_loop(0, nv_ref[tt], body, 0)

    def scatter_wait(tt, sl):
        def body(r, c):
            scatter_copy(tt, sl, r).wait()
            return c
        lax.fori_loop(0, nv_ref[tt], body, 0)

    @pl.when(e == 0)
    def _():
        @pl.when(t == 0)
        def _():
            gather_start(0, 0)

        @pl.when(prev_valid)
        def _():
            scatter_start(t - 1, other)

        @pl.when(valid)
        def _():
            gather_wait(slot)
            h = _rms(buf[slot], gm_ref[...])
            hb_scr[...] = h.astype(BF16)
            z = _router_logits(h, wr_ref)
            lane = lax.broadcasted_iota(jnp.int32, z.shape, 1)
            lanef = lane.astype(F32)
            grp = lane < N_GROUPS
            zg = jnp.where(grp, z, -jnp.inf)
            pg_top = 1.0 / jnp.sum(jnp.where(grp, jnp.exp(zg - jnp.max(zg, axis=1, keepdims=True)), 0.0), axis=1, keepdims=True)
            lo = N_GROUPS + tg_ref[t] * EXPERTS_PER_GROUP
            ing = (lane >= lo) & (lane < lo + EXPERTS_PER_GROUP)
            pf = _masked_softmax(z, ing)
            big = float(2 * LANES)
            m1 = jnp.max(jnp.where(ing, pf, -1.0), axis=1, keepdims=True)
            i1 = jnp.min(jnp.where(ing & (pf == m1), lanef, big), axis=1, keepdims=True)
            rest = ing & (lanef != i1)
            m2 = jnp.max(jnp.where(rest, pf, -1.0), axis=1, keepdims=True)
            i2 = jnp.min(jnp.where(rest & (pf == m2), lanef, big), axis=1, keepdims=True)
            tot = m1 + m2
            w4_scr[...] = jnp.where(lanef == i1, m1 / tot * pg_top, jnp.where(lanef == i2, m2 / tot * pg_top, 0.0))

    @pl.when(e == EXPERTS_PER_GROUP // 2)
    def _():
        @pl.when(prev_valid)
        def _():
            scatter_wait(t - 1, other)

        @pl.when(next_valid)
        def _():
            gather_start(t + 1, other)

    @pl.when(valid)
    def _():
        hb = hb_scr[...]
        lane = lax.broadcasted_iota(jnp.int32, w4_scr.shape, 1)
        col = N_GROUPS + tg_ref[t] * EXPERTS_PER_GROUP + e
        we = jnp.sum(jnp.where(lane == col, w4_scr[...], 0.0), axis=1, keepdims=True)
        act = jax.nn.silu(_dot(hb, wg_ref[0])) * _dot(hb, wu_ref[0])
        buf[slot] += _dot(act * we, wd_ref[0])

    @pl.when(valid & (e == EXPERTS_PER_GROUP - 1))
    def _():
        buf[slot] = _rms(buf[slot], gf_ref[...])


def _moe_final(x2d, g_moe, g_final, wr, w_gate, w_up, w_down, tm):
    n, d = x2d.shape
    g_top = _route_groups(x2d, g_moe, wr, min(n, 512))[:, 0]
    n_tiles = n // tm + N_GROUPS
    onehot = (g_top[:, None] == jnp.arange(N_GROUPS)[None, :]).astype(jnp.int32)
    counts = jnp.sum(onehot, axis=0)
    rank = jnp.sum((jnp.cumsum(onehot, axis=0) - onehot) * onehot, axis=1)
    padded = (counts + tm - 1) // tm * tm
    ends = jnp.cumsum(padded)
    base = ends - padded
    pos = base[g_top] + rank
    src = jnp.full((n_tiles * tm,), -1, jnp.int32).at[pos].set(jnp.arange(n, dtype=jnp.int32))
    tile_start = jnp.arange(n_tiles, dtype=jnp.int32) * tm
    tile_valid = (tile_start < ends[-1]).astype(jnp.int32)
    tile_group = jnp.minimum(jnp.sum((tile_start[:, None] >= ends[None, :]).astype(jnp.int32), axis=1), N_GROUPS - 1)
    tile_rows = jnp.clip(ends[tile_group] - padded[tile_group] + counts[tile_group] - tile_start, 0, tm)
    tile_rows = (tile_rows * tile_valid).astype(jnp.int32)
    n_valid = ends[-1] // tm
    last_group = tile_group[jnp.maximum(n_valid - 1, 0)]
    eidx = tile_group[:, None] * EXPERTS_PER_GROUP + jnp.arange(EXPERTS_PER_GROUP, dtype=jnp.int32)[None, :]
    widx = jnp.where(tile_valid[:, None] == 1, eidx, last_group * EXPERTS_PER_GROUP + EXPERTS_PER_GROUP - 1)
    widx = widx.reshape(-1).astype(jnp.int32)

    wmap = lambda t, e, src, tg, tv, nv, wi: (wi[t * EXPERTS_PER_GROUP + e], 0, 0)
    cmap = lambda t, e, src, tg, tv, nv, wi: (0, 0)
    grid_spec = pltpu.PrefetchScalarGridSpec(
        num_scalar_prefetch=5,
        grid=(n_tiles, EXPERTS_PER_GROUP),
        in_specs=[pl.BlockSpec(memory_space=pl.ANY),
                  pl.BlockSpec((1, d), cmap), pl.BlockSpec((1, d), cmap),
                  pl.BlockSpec(wr.shape, lambda t, e, src, tg, tv, nv, wi: (0, 0, 0)),
                  pl.BlockSpec((1, d, EXPERT_FF), wmap), pl.BlockSpec((1, d, EXPERT_FF), wmap),
                  pl.BlockSpec((1, EXPERT_FF, d), wmap)],
        out_specs=pl.BlockSpec(memory_space=pl.ANY),
        scratch_shapes=[pltpu.VMEM((2, tm, d), F32), pltpu.VMEM((tm, d), BF16), pltpu.VMEM((tm, LANES), F32),
                        pltpu.SemaphoreType.DMA((2,)), pltpu.SemaphoreType.DMA((1,))],
    )
    return pl.pallas_call(
        functools.partial(_moe_kernel, tm=tm),
        grid_spec=grid_spec,
        out_shape=jax.ShapeDtypeStruct((n, d), F32),
        compiler_params=_cparams(("arbitrary", "arbitrary")),
        name="moe_final_norm",
    )(src, tile_group, tile_valid, tile_rows, widx, x2d, g_moe, g_final, wr, w_gate, w_up, w_down)


def _finish(x2d, o_cmp, o_sel, o_win, gates, u, v, mk, mv, bsz, lw, chunk, moe_tm):
    n, d = x2d.shape
    t = n // bsz
    x1 = _mix(x2d, o_cmp, o_sel, o_win, gates, u, v, lw["w_sgu"], lw["b_sgu"], lw["g_nsa_out"], lw["g_sgu_out"],
              lw["w_out"], chunk, min(n, 256))
    hq = _rms_matmul(x1, lw["g_mem_norm"], lw["w_mem_q"], min(n, 512))
    o_m = _mem_attention(hq.reshape(bsz, t, -1), mk, mv)
    x2 = _matmul_res(o_m.reshape(n, -1), lw["w_mem_o"], x1, min(n, 512))
    return _moe_final(x2, lw["g_moe_norm"], lw["g_final"], lw["w_router"], lw["w_exp_gate"], lw["w_exp_up"],
                      lw["w_exp_down"], moe_tm)


def kernel(x_prompt, x_sample, cache_cmp_k, cache_cmp_v, cache_sel_k, cache_sel_v, cache_win_k, cache_win_v, cache_mem_k, cache_mem_v, page_table, mem_prompt, w_in, g_attn_norm, pe_cmp_k, w_cmp_k1, w_cmp_k2, pe_cmp_v, w_cmp_v1, w_cmp_v2, g_sgu_v, w_sgu, b_sgu, g_nsa_out, g_sgu_out, w_out, g_mem_norm, g_mem_src, w_mem_q, w_mem_k, w_mem_v, w_mem_o, g_moe_norm, w_router_group, w_router_expert, w_exp_gate, w_exp_up, w_exp_down, g_final):
    depth = w_in.shape[0]
    assert depth == 1, "single-layer trunk"
    bp, tp, d = x_prompt.shape
    bs, ts, _ = x_sample.shape
    n_pages = page_table.shape[1]
    page = cache_cmp_k.shape[2]
    past = n_pages * page
    assert ts < STRIDE and tp % 256 == 0 and page % STRIDE == 0
    row = lambda a: a[0].reshape(1, -1)

    w_in_p = w_in[0].astype(BF16)
    wr = jnp.concatenate([w_router_group[0], w_router_expert[0],
                          jnp.zeros((d, LANES - N_GROUPS - N_EXPERTS), F32)], axis=1)
    wr_hi = wr.astype(BF16)
    wr = jnp.stack([wr_hi, (wr - wr_hi.astype(F32)).astype(BF16)])
    lw = {
        "w_sgu": w_sgu[0], "b_sgu": b_sgu[0], "g_nsa_out": row(g_nsa_out), "g_sgu_out": row(g_sgu_out),
        "w_out": w_out[0].astype(BF16), "g_mem_norm": row(g_mem_norm), "w_mem_q": w_mem_q[0].astype(BF16),
        "w_mem_o": w_mem_o[0].astype(BF16), "g_moe_norm": row(g_moe_norm), "g_final": g_final.reshape(1, -1),
        "w_router": wr, "w_exp_gate": w_exp_gate[0], "w_exp_up": w_exp_up[0], "w_exp_down": w_exp_down[0],
    }
    g_attn = row(g_attn_norm)
    gsv = row(g_sgu_v)
    kv5 = lambda a, b, t: a.reshape(1, b, t, KV_HEADS, HEAD_DIM)

    np_ = bp * tp
    tabs_p = _rope_tables(jnp.arange(tp, dtype=jnp.int32))
    (q, qrot, kc, vc, ks, vs, kw, vw, gates, u, v) = _project(x_prompt.reshape(np_, d), g_attn, w_in_p, tabs_p, gsv, 256,
                                                              kv_seq=tp)
    pt_p = jnp.arange(np_ // page, dtype=jnp.int32).reshape(bp, tp // page)
    kcmp = _compress(kc, pt_p, page, pe_cmp_k[0], w_cmp_k1[0], w_cmp_k2[0])
    vcmp = _compress(vc, pt_p, page, pe_cmp_v[0], w_cmp_v1[0], w_cmp_v2[0])
    o_cmp, sel_t = _cmp_select(q.reshape(bp, tp, -1), kcmp, vcmp, 0, tp)
    qrot3 = qrot.reshape(bp, tp, -1)
    o_sel = _prompt_attention(qrot3, ks, vs, sel_t)
    o_win = _prompt_attention(qrot3, kw, vw)
    n_mem = mem_prompt.shape[1]
    mem_w = MEM_HEADS * MEM_HEAD_DIM
    mem2d = mem_prompt.reshape(bp * n_mem, d)
    mem4 = lambda a: a.reshape(bp, n_mem, MEM_HEADS, MEM_HEAD_DIM)
    mk_p = mem4(_rms_matmul(mem2d, row(g_mem_src), w_mem_k[0].astype(BF16), min(bp * n_mem, 512)))
    mv_p = mem4(_rms_matmul(mem2d, row(g_mem_src), w_mem_v[0].astype(BF16), min(bp * n_mem, 512)))
    y_p = _finish(x_prompt.reshape(np_, d), o_cmp.reshape(np_, -1), o_sel.reshape(np_, -1), o_win.reshape(np_, -1),
                  gates, u, v, mk_p, mv_p, bp, lw, CHUNK, min(np_, 1024))
    wbp = min(WINDOW, tp)
    kv5t = lambda a: a.reshape(bp, KV_HEADS, HEAD_DIM, a.shape[2]).transpose(0, 3, 1, 2)[None]
    outs_p = (kv5t(kc), kv5t(vc), kv5t(ks), kv5t(vs), kv5t(kw[:, :, -wbp:]), kv5t(vw[:, :, -wbp:]), mk_p[None], mv_p[None])

    ns_ = bs * ts
    pos_s = past + jnp.arange(ts, dtype=jnp.int32)
    tabs_s = tuple(jnp.tile(a, (bs, 1)) for a in _rope_tables(pos_s))
    (q, qrot, kc, vc, ks, vs, kw, vw, gates, u, v) = _project(x_sample.reshape(ns_, d), g_attn, w_in_p, tabs_s, gsv, ns_)
    pool = lambda c: c[0].transpose(0, 2, 3, 1).reshape(c.shape[1], KV_WIDTH, page)
    kcmp = _compress(pool(cache_cmp_k), page_table, page, pe_cmp_k[0], w_cmp_k1[0], w_cmp_k2[0])
    vcmp = _compress(pool(cache_cmp_v), page_table, page, pe_cmp_v[0], w_cmp_v1[0], w_cmp_v2[0])
    o_cmp, sel_t = _cmp_select(q.reshape(bs, ts, -1).astype(F32), kcmp, vcmp, past, past + ts)
    wb = cache_win_k.shape[2]
    wk = cache_win_k[0].transpose(0, 2, 3, 1).reshape(bs, KV_WIDTH, wb)
    wv = cache_win_v[0].transpose(0, 2, 3, 1).reshape(bs, KV_WIDTH, wb)
    ks3, vs3, kw3, vw3 = (a.reshape(bs, ts, KV_WIDTH) for a in (ks, vs, kw, vw))
    o_sel, o_win = _sample_attention(qrot.reshape(bs, ts, -1), pool(cache_sel_k), pool(cache_sel_v), page_table,
                                     ks3, vs3, sel_t, wk, wv, kw3, vw3)
    mem_ks, mem_vs = cache_mem_k[0], cache_mem_v[0]
    y_s = _finish(x_sample.reshape(ns_, d), o_cmp.reshape(ns_, -1), o_sel.reshape(ns_, -1), o_win.reshape(ns_, -1),
                  gates, u, v, mem_ks, mem_vs, bs, lw, ts, min(ns_, 128))
    def slide(cache_t, new):
        win = jnp.concatenate([cache_t, new.transpose(0, 2, 1)], axis=2)[:, :, -wb:]
        return win.reshape(bs, KV_HEADS, HEAD_DIM, wb).transpose(0, 3, 1, 2)[None]

    win_k_s = slide(wk, kw3)
    win_v_s = slide(wv, vw3)
    outs_s = (kv5(kc, bs, ts), kv5(vc, bs, ts), kv5(ks, bs, ts), kv5(vs, bs, ts), win_k_s, win_v_s,
              v.reshape(1, bs, ts, -1))

    return (y_p.reshape(bp, tp, d), y_s.reshape(bs, ts, d)) + outs_p + outs_s
```

```python
import functools

import jax
import jax.numpy as jnp
from jax import lax
from jax.experimental import pallas as pl
from jax.experimental.pallas import tpu as pltpu

F32 = jnp.float32
BF16 = jnp.bfloat16

N_HEADS = 16
HEAD_DIM = 64
KV_HEADS = 4
Q_PER_KV = N_HEADS // KV_HEADS
NSA_WIDTH = N_HEADS * HEAD_DIM
KV_WIDTH = KV_HEADS * HEAD_DIM
ROT_DIM = HEAD_DIM // 4
ROPE_THETA = 500000.0
L_CMP = 32
STRIDE = 16
CMP_R = L_CMP // STRIDE
CMP_HIDDEN = 128
L_SEL = 64
SEL_RATIO = L_SEL // STRIDE
SEL_INNER = (L_SEL - L_CMP) // STRIDE + 1
TOP_N = 16
WINDOW = 512
SGU_GROUPS = 8
SGU_GROUP_DIM = 128
CHUNK = 128
MEM_HEADS = 4
MEM_HEAD_DIM = 128
N_GROUPS = 4
EXPERTS_PER_GROUP = 4
N_EXPERTS = N_GROUPS * EXPERTS_PER_GROUP
EXPERT_FF = 512
EPS = 1e-6
NEG_INF = -1e30
BIG = 1e9
PAD_SCORE = -3e38
CMP_ROW_PITCH = STRIDE + 4
ATTN_SCALE = HEAD_DIM ** -0.5
MEM_SCALE = MEM_HEAD_DIM ** -0.5

LANES = 128
SUBLANES = 8
V7X_VMEM_LIMIT_BYTES = 60000 * 1024

GATE_COLS = LANES
N_GATES = 3 * N_HEADS
_SEG = {}
_off = 0
for _name, _w in (("q", NSA_WIDTH), ("kc", KV_WIDTH), ("vc", KV_WIDTH), ("ks", KV_WIDTH), ("vs", KV_WIDTH),
                  ("kw", KV_WIDTH), ("vw", KV_WIDTH), ("gt", N_GATES), ("u", 1024), ("v", 1024)):
    _SEG[_name] = (_off, _w)
    _off += _w
IN_COLS = _off


def _cparams(sem, vmem=V7X_VMEM_LIMIT_BYTES):
    return pltpu.CompilerParams(dimension_semantics=sem, vmem_limit_bytes=vmem)


def _dot(a, b):
    return jnp.dot(a.astype(BF16), b.astype(BF16), preferred_element_type=F32)


def _dot_nt(a, b):
    return lax.dot_general(a.astype(BF16), b.astype(BF16), (((1,), (1,)), ((), ())), preferred_element_type=F32)


def _dot_tn(a, b):
    return lax.dot_general(a.astype(BF16), b.astype(BF16), (((0,), (0,)), ((), ())), preferred_element_type=F32)


def _rms(x, g):
    return x * lax.rsqrt(jnp.mean(x * x, axis=-1, keepdims=True) + EPS) * g


def _masked_softmax(s, mask, axis=-1):
    s = jnp.where(mask, s, NEG_INF)
    m = jnp.max(s, axis=axis, keepdims=True)
    e = jnp.where(mask, jnp.exp(s - m), 0.0)
    return e * (1.0 / jnp.maximum(jnp.sum(e, axis=axis, keepdims=True), 1e-30))


def _const_spec(shape):
    nd = len(shape)
    return pl.BlockSpec(shape, lambda *_: (0,) * nd, pipeline_mode=pl.Buffered(1))


def _proj_kernel(x_ref, g_ref, w_ref, rc_ref, rs1_ref, rs2_ref, gsv_ref,
                 q_ref, qrot_ref, kc_ref, vc_ref, ks_ref, vs_ref, kw_ref, vw_ref, gates_ref, u_ref, v_ref, *, transpose_kv):
    hb = _rms(x_ref[...], g_ref[...]).astype(BF16)

    def put_kv(ref, val):
        if transpose_kv:
            ref[0] = val.T
        else:
            ref[...] = val

    def seg(name):
        lo, width = _SEG[name]
        return jnp.dot(hb, w_ref[:, lo:lo + width], preferred_element_type=F32)

    def rope(z):
        rc, rs1, rs2 = rc_ref[...], rs1_ref[...], rs2_ref[...]
        half = ROT_DIM // 2
        outs = []
        for c in range(z.shape[1] // LANES):
            zc = z[:, c * LANES:(c + 1) * LANES]
            outs.append(zc * rc + pltpu.roll(zc, LANES - half, 1) * rs1 + pltpu.roll(zc, half, 1) * rs2)
        return jnp.concatenate(outs, axis=1)

    q = seg("q")
    q_ref[...] = q.astype(BF16)
    qrot_ref[...] = rope(q).astype(BF16)
    put_kv(kc_ref, seg("kc"))
    put_kv(vc_ref, seg("vc"))
    put_kv(ks_ref, rope(seg("ks")))
    put_kv(vs_ref, seg("vs"))
    put_kv(kw_ref, rope(seg("kw")))
    put_kv(vw_ref, seg("vw"))
    g0 = _SEG["gt"][0]
    zt = jnp.dot(hb, w_ref[:, g0:IN_COLS], preferred_element_type=F32)
    u0, v0 = _SEG["u"][0] - g0, _SEG["v"][0] - g0
    gates_ref[...] = jax.nn.sigmoid(zt[:, :GATE_COLS])
    u_ref[...] = jax.nn.gelu(zt[:, u0:u0 + 1024])
    v_ref[...] = _rms(jax.nn.gelu(zt[:, v0:v0 + 1024]), gsv_ref[...])


def _project(x2d, g_attn, w_in_p, tables, g_sgu_v, tm, kv_seq=None):
    n, d = x2d.shape
    rc, rs1, rs2 = tables
    tt = rc.shape[0]
    nt = tt // tm
    row = lambda w: pl.BlockSpec((tm, w), lambda i: (i, 0))
    tab = pl.BlockSpec((tm, LANES), lambda i: (i % nt, 0))
    out_shapes = [jax.ShapeDtypeStruct((n, NSA_WIDTH), BF16), jax.ShapeDtypeStruct((n, NSA_WIDTH), BF16)]
    if kv_seq is None:
        out_shapes += [jax.ShapeDtypeStruct((n, KV_WIDTH), F32)] * 6
        kv_spec = row(KV_WIDTH)
    else:
        per = kv_seq // tm
        out_shapes += [jax.ShapeDtypeStruct((n // kv_seq, KV_WIDTH, kv_seq), F32)] * 6
        kv_spec = pl.BlockSpec((1, KV_WIDTH, tm), lambda i: (i // per, 0, i % per))
    out_shapes += [jax.ShapeDtypeStruct((n, GATE_COLS), F32), jax.ShapeDtypeStruct((n, 1024), F32),
                   jax.ShapeDtypeStruct((n, 1024), F32)]
    out_specs = [row(NSA_WIDTH), row(NSA_WIDTH)] + [kv_spec] * 6 + [row(GATE_COLS), row(1024), row(1024)]
    return pl.pallas_call(
        functools.partial(_proj_kernel, transpose_kv=kv_seq is not None),
        grid=(n // tm,),
        in_specs=[row(d), _const_spec((1, d)), _const_spec(w_in_p.shape), tab, tab, tab, _const_spec((1, 1024))],
        out_specs=out_specs,
        out_shape=out_shapes,
        compiler_params=_cparams(("parallel",)),
        name="in_proj",
    )(x2d, g_attn, w_in_p, rc, rs1, rs2, g_sgu_v)


def _rope_tables(pos):
    half = ROT_DIM // 2
    freqs = ROPE_THETA ** (-jnp.arange(half, dtype=F32) / half)
    ang = pos.astype(F32)[:, None] * freqs[None, :]
    cos, sin = jnp.cos(ang), jnp.sin(ang)
    t = pos.shape[0]
    ones = jnp.ones((t, HEAD_DIM - ROT_DIM), F32)
    zeros = jnp.zeros((t, HEAD_DIM - ROT_DIM), F32)
    zh = jnp.zeros((t, half), F32)
    rc = jnp.concatenate([cos, cos, ones], axis=1)
    rs1 = jnp.concatenate([-sin, zh, zeros], axis=1)
    rs2 = jnp.concatenate([zh, sin, zeros], axis=1)
    rep = LANES // HEAD_DIM
    return tuple(jnp.tile(a, (1, rep)) for a in (rc, rs1, rs2))


def _compress_kernel(pt_ref, pool_ref, pe_ref, w1_ref, w2_ref, out_ref, stage, buf, a_ref, sem, *, n_pages, page):
    b = pl.program_id(0)
    nb = pl.num_programs(0)
    n_rows = n_pages * page
    n_sub = n_rows // STRIDE
    slot = b % 2
    n_ct = KV_WIDTH // LANES
    ppl = pool_ref.shape[2] // page

    def page_copy(bb, sl, p):
        q = pt_ref[bb, p]
        src = pool_ref.at[q // ppl, :, pl.ds(pl.multiple_of((q % ppl) * page, page), page)]
        return pltpu.make_async_copy(src, stage.at[sl, :, pl.ds(pl.multiple_of(p * page, page), page)], sem.at[sl])

    def start_all(bb, sl):
        def body(p, carry):
            page_copy(bb, sl, p).start()
            return carry
        lax.fori_loop(0, n_pages, body, 0)

    def wait_all(bb, sl):
        def body(p, carry):
            page_copy(bb, sl, p).wait()
            return carry
        lax.fori_loop(0, n_pages, body, 0)

    @pl.when(b == 0)
    def _():
        start_all(0, 0)

    @pl.when(b + 1 < nb)
    def _():
        start_all(b + 1, 1 - slot)

    wait_all(b, slot)

    tch = min(n_rows, 512)
    pev = _dot(pe_ref[...], w1_ref[...])
    low = lax.broadcasted_iota(jnp.int32, (n_sub, LANES), 1) < HEAD_DIM
    row = lax.broadcasted_iota(jnp.int32, (n_sub, 4 * HEAD_DIM), 0)
    for c in range(n_ct):
        for j in range(n_rows // tch):
            rows_t = stage[slot, c * LANES:(c + 1) * LANES, j * tch:(j + 1) * tch].T
            for n in range(tch // STRIDE):
                r0 = (j * (tch // STRIDE) + n) * CMP_ROW_PITCH
                buf[c, r0:r0 + STRIDE, :] = rows_t[n * STRIDE:(n + 1) * STRIDE, :]
        src = buf.at[c]
        for sp in range(STRIDE // 2):
            x0 = src[pl.ds(2 * sp, n_sub, stride=CMP_ROW_PITCH), :]
            x1 = src[pl.ds(2 * sp + 1, n_sub, stride=CMP_ROW_PITCH), :]
            a_ref[0, :, sp * LANES:(sp + 1) * LANES] = jnp.where(low, x0, pltpu.roll(x1, HEAD_DIM, 1)).astype(BF16)
            a_ref[1, :, sp * LANES:(sp + 1) * LANES] = jnp.where(low, pltpu.roll(x0, HEAD_DIM, 1), x1).astype(BF16)
        for k in range(2):
            pm = jnp.dot(a_ref[k], w1_ref[...], preferred_element_type=F32)
            part0 = pm[:, :CMP_HIDDEN] + pev[0:1, :CMP_HIDDEN]
            part1 = pm[:, CMP_HIDDEN:] + pev[1:2, CMP_HIDDEN:]
            hsum = part0 + pltpu.roll(part1, n_sub - 1, 0)
            o = _dot(jax.nn.silu(hsum), w2_ref[...])
            out_ref[0, 2 * c + k] = jnp.where(row < n_sub - 1, o, 0.0).astype(BF16)


def _compress(pool, page_table, page, pe, w1, w2):
    bsz, n_pages = page_table.shape
    n_rows = n_pages * page
    n_sub = n_rows // STRIDE
    n_ct = KV_WIDTH // LANES
    kdim = STRIDE * HEAD_DIM
    pe8 = jnp.zeros((SUBLANES, kdim), F32).at[:CMP_R].set(pe.reshape(CMP_R, kdim))
    w1c = w1.reshape(CMP_R, kdim, CMP_HIDDEN).transpose(1, 0, 2).reshape(kdim, CMP_R * CMP_HIDDEN).astype(BF16)
    w2t = jnp.tile(w2, (1, 4)).astype(BF16)
    grid_spec = pltpu.PrefetchScalarGridSpec(
        num_scalar_prefetch=1,
        grid=(bsz,),
        in_specs=[pl.BlockSpec(memory_space=pl.ANY),
                  pl.BlockSpec((SUBLANES, kdim), lambda b, pt: (0, 0)),
                  pl.BlockSpec(w1c.shape, lambda b, pt: (0, 0)),
                  pl.BlockSpec(w2t.shape, lambda b, pt: (0, 0))],
        out_specs=pl.BlockSpec((1, KV_HEADS, n_sub, 4 * HEAD_DIM), lambda b, pt: (b, 0, 0, 0)),
        scratch_shapes=[pltpu.VMEM((2, KV_WIDTH, n_rows), F32), pltpu.VMEM((n_ct, n_sub * CMP_ROW_PITCH, LANES), F32),
                        pltpu.VMEM((2, n_sub, kdim), BF16), pltpu.SemaphoreType.DMA((2,))],
    )
    return pl.pallas_call(
        functools.partial(_compress_kernel, n_pages=n_pages, page=page),
        grid_spec=grid_spec,
        out_shape=jax.ShapeDtypeStruct((bsz, KV_HEADS, n_sub, 4 * HEAD_DIM), BF16),
        compiler_params=_cparams(("arbitrary",)),
        name="compress",
    )(page_table, pool, pe8, w1c, w2t)


def _cmpsel_kernel(q_ref, k_ref, v_ref, st_ref, o_ref, sc_ref, *, pos0, nc, ns, tq):
    i = pl.program_id(2)
    n_sub = k_ref.shape[2]
    ns_pad = st_ref.shape[0]
    gw = Q_PER_KV * HEAD_DIM
    lane_head = lax.broadcasted_iota(jnp.int32, (tq, gw), 1) // HEAD_DIM
    qpos = pos0 + i * tq + lax.broadcasted_iota(jnp.int32, (tq, 1), 0)
    kidx = lax.broadcasted_iota(jnp.int32, (1, n_sub), 1)
    mask = (kidx * STRIDE + (L_CMP - 1) <= qpos) & (kidx < nc)
    mask_rows = jnp.concatenate([mask] * Q_PER_KV, axis=0)
    blk = lax.broadcasted_iota(jnp.int32, (ns_pad, tq), 0)
    qpos_t = pos0 + i * tq + lax.broadcasted_iota(jnp.int32, (ns_pad, tq), 1)
    cur = qpos_t // L_SEL
    forced = (blk == 0) | (blk == cur) | (blk == cur - 1)
    future = blk * L_SEL > qpos_t
    st = st_ref[...]
    for g in range(k_ref.shape[1]):
        qf = q_ref[0, :, g * gw:(g + 1) * gw].astype(F32)
        q_rows = jnp.concatenate([jnp.where(lane_head == qi, qf, 0.0) for qi in range(Q_PER_KV)], axis=0)
        s = _dot_nt(q_rows, k_ref[0, g]) * ATTN_SCALE
        p = _masked_softmax(s, mask_rows)
        o_rows = _dot(p, v_ref[0, g])
        o = None
        pg = None
        for qi in range(Q_PER_KV):
            part = jnp.where(lane_head == qi, o_rows[qi * tq:(qi + 1) * tq], 0.0)
            o = part if o is None else o + part
            pq = p[qi * tq:(qi + 1) * tq]
            pg = pq if pg is None else pg + pq
        o_ref[0, :, g * gw:(g + 1) * gw] = o
        pg_hi = pg.astype(BF16)
        pg_lo = (pg - pg_hi.astype(F32)).astype(BF16)
        ps_t = _dot_nt(st, pg_hi) + _dot_nt(st, pg_lo)
        sc = jnp.where(forced, BIG, jnp.where(future, -BIG, ps_t))
        sc_ref[0, g] = jnp.where(blk < ns, sc, PAD_SCORE)


def _rank_kernel(sc_ref, sel_ref, *, ns, topn):
    sc = sc_ref[0, 0]
    blk = lax.broadcasted_iota(jnp.int32, sc.shape, 0)
    sel_ref[0, 0] = jnp.zeros(sc.shape, F32)

    def body(j, c):
        row = sc_ref[0, 0, pl.ds(j, 1), :]
        ahead = (sc > row) | ((sc == row) & (blk < j))
        rank = jnp.sum(ahead.astype(F32), axis=0, keepdims=True)
        sel_ref[0, 0, pl.ds(j, 1), :] = (rank < topn).astype(F32)
        return c

    lax.fori_loop(0, ns, body, 0)


def _cmp_select(q, kcmp, vcmp, pos0, n_keys):
    bsz, t, _ = q.shape
    n_sub = kcmp.shape[2]
    nc = n_sub - CMP_R + 1
    ns = -(-n_keys // L_SEL)
    ns_pad = -(-ns // SUBLANES) * SUBLANES
    topn = min(TOP_N, ns)
    tq = min(t, 512)
    gw = Q_PER_KV * HEAD_DIM
    gps = KV_HEADS if t < LANES else 1
    cidx = jnp.arange(n_sub)
    st = ((cidx[None, :] // SEL_RATIO == jnp.arange(ns_pad)[:, None]) & (cidx[None, :] % SEL_RATIO < SEL_INNER)
          & (cidx[None, :] < nc)).astype(BF16)
    o_cmp, sc = pl.pallas_call(
        functools.partial(_cmpsel_kernel, pos0=pos0, nc=nc, ns=ns, tq=tq),
        grid=(bsz, KV_HEADS // gps, t // tq),
        in_specs=[pl.BlockSpec((1, tq, gw * gps), lambda b, g, i: (b, i, g)),
                  pl.BlockSpec((1, gps, n_sub, gw), lambda b, g, i: (b, g, 0, 0)),
                  pl.BlockSpec((1, gps, n_sub, gw), lambda b, g, i: (b, g, 0, 0)),
                  pl.BlockSpec((ns_pad, n_sub), lambda b, g, i: (0, 0))],
        out_specs=[pl.BlockSpec((1, tq, gw * gps), lambda b, g, i: (b, i, g)),
                   pl.BlockSpec((1, gps, ns_pad, tq), lambda b, g, i: (b, g, 0, i))],
        out_shape=[jax.ShapeDtypeStruct((bsz, t, NSA_WIDTH), F32),
                   jax.ShapeDtypeStruct((bsz, KV_HEADS, ns_pad, t), F32)],
        compiler_params=_cparams(("parallel", "parallel", "parallel")),
        name="cmp_scores",
    )(q, kcmp, vcmp, st)
    fold = t < LANES
    if fold:
        sc = sc.transpose(2, 0, 1, 3).reshape(1, 1, ns_pad, bsz * KV_HEADS * t)
    nb, ng, _, width = sc.shape
    tl = min(width, 2048)
    sel = pl.pallas_call(
        functools.partial(_rank_kernel, ns=ns, topn=topn),
        grid=(nb, ng, width // tl),
        in_specs=[pl.BlockSpec((1, 1, ns_pad, tl), lambda b, g, i: (b, g, 0, i))],
        out_specs=pl.BlockSpec((1, 1, ns_pad, tl), lambda b, g, i: (b, g, 0, i)),
        out_shape=jax.ShapeDtypeStruct(sc.shape, F32),
        compiler_params=_cparams(("parallel", "parallel", "parallel")),
        name="rank_select",
    )(sc)
    if fold:
        sel = sel.reshape(ns_pad, bsz, KV_HEADS, t).transpose(1, 2, 0, 3)
    return o_cmp, sel


def _pattn_kernel(*refs, mode, t, tq, ck):
    if mode == "sel":
        q_ref, k_ref, v_ref, sel_ref, et_ref, o_ref, kc_scr, vt_scr, m_scr, l_scr, acc_scr = refs
    else:
        q_ref, k_ref, v_ref, o_ref, kc_scr, vt_scr, m_scr, l_scr, acc_scr = refs
    i = pl.program_id(2)
    n_chunks = t // ck
    last = (i + 1) * (tq // ck) - 1

    @pl.when(i == 0)
    def _():
        zeros = jnp.zeros((ck, LANES - HEAD_DIM), F32)
        for c in range(n_chunks):
            kc_scr[c] = jnp.concatenate([k_ref[0, :, c * ck:(c + 1) * ck].T, zeros], axis=1).astype(BF16)
            vt_scr[c] = v_ref[0, :, c * ck:(c + 1) * ck].astype(BF16)

    qf = q_ref[0].astype(F32) * ATTN_SCALE
    q_heads = []
    for qi in range(Q_PER_KV):
        tile = qf[:, (qi // 2) * LANES:(qi // 2 + 1) * LANES]
        q_heads.append((tile if qi % 2 == 0 else pltpu.roll(tile, HEAD_DIM, 1)).astype(BF16))
    q_all = jnp.concatenate(q_heads, axis=0)
    qpos = i * tq + lax.broadcasted_iota(jnp.int32, (1, tq), 1)
    m_scr[...] = jnp.full(m_scr.shape, NEG_INF, F32)
    l_scr[...] = jnp.zeros(l_scr.shape, F32)
    acc_scr[...] = jnp.zeros(acc_scr.shape, F32)
    if mode == "sel":
        sel_b = sel_ref[0, 0].astype(BF16)
        n_steps = last + 1
    else:
        n_steps = last - jnp.maximum(i * tq - WINDOW, 0) // ck + 1

    def chunk(c):
        kpos = c * ck + lax.broadcasted_iota(jnp.int32, (ck, 1), 0)
        if mode == "sel":
            chosen = jnp.dot(et_ref[c], sel_b, preferred_element_type=F32)
            ok = (chosen > 0.5) & (kpos <= qpos)
        else:
            ok = (kpos <= qpos) & (kpos > qpos - WINDOW)
        bias = jnp.where(ok, 0.0, NEG_INF)
        s = _dot_nt(kc_scr[c], q_all) + jnp.concatenate([bias] * Q_PER_KV, axis=1)
        m_old = m_scr[...]
        m_new = jnp.maximum(m_old, jnp.max(s, axis=0, keepdims=True))
        e = jnp.exp(s - m_new)
        alpha = jnp.exp(m_old - m_new)
        l_scr[...] = l_scr[...] * alpha + jnp.sum(e, axis=0, keepdims=True)
        m_scr[...] = m_new
        acc_scr[...] = acc_scr[...] * alpha + jnp.dot(vt_scr[c], e.astype(BF16), preferred_element_type=F32)

    def body(step, carry):
        chunk(last - step)
        return carry

    lax.fori_loop(0, n_steps, body, 0)
    out = acc_scr[...] * (1.0 / jnp.maximum(l_scr[...], 1e-30))
    o_ref[0] = jnp.concatenate([out[:, qi * tq:(qi + 1) * tq] for qi in range(Q_PER_KV)], axis=0).T


def _prompt_attention(qrot, k, v, sel_t=None):
    bsz, t, _ = qrot.shape
    tq, ck = 512, 512
    gw = Q_PER_KV * HEAD_DIM
    mode = "win" if sel_t is None else "sel"
    assert t % tq == 0 and tq % ck == 0 and WINDOW % ck == 0 and ck % L_SEL == 0
    in_specs = [pl.BlockSpec((1, tq, gw), lambda b, g, i: (b, i, g)),
                pl.BlockSpec((1, HEAD_DIM, t), lambda b, g, i: (b, g, 0)),
                pl.BlockSpec((1, HEAD_DIM, t), lambda b, g, i: (b, g, 0))]
    args = [qrot, k, v]
    if mode == "sel":
        ns_pad = sel_t.shape[2]
        et = (jnp.arange(t)[:, None] // L_SEL == jnp.arange(ns_pad)[None, :]).astype(BF16).reshape(t // ck, ck, ns_pad)
        in_specs += [pl.BlockSpec((1, 1, ns_pad, tq), lambda b, g, i: (b, g, 0, i)),
                     pl.BlockSpec((t // ck, ck, ns_pad), lambda b, g, i: (0, 0, 0))]
        args += [sel_t, et]
    return pl.pallas_call(
        functools.partial(_pattn_kernel, mode=mode, t=t, tq=tq, ck=ck),
        grid=(bsz, KV_HEADS, t // tq),
        in_specs=in_specs,
        out_specs=pl.BlockSpec((1, tq, gw), lambda b, g, i: (b, i, g)),
        out_shape=jax.ShapeDtypeStruct((bsz, t, NSA_WIDTH), F32),
        scratch_shapes=[pltpu.VMEM((t // ck, ck, LANES), BF16), pltpu.VMEM((t // ck, HEAD_DIM, ck), BF16),
                        pltpu.VMEM((1, Q_PER_KV * tq), F32), pltpu.VMEM((1, Q_PER_KV * tq), F32),
                        pltpu.VMEM((HEAD_DIM, Q_PER_KV * tq), F32)],
        compiler_params=_cparams(("parallel", "parallel", "arbitrary")),
        name="prompt_attn_" + mode,
    )(*args)


def _sattn_kernel(pt_ref, qbd_ref, kpool_ref, vpool_ref, ksn_ref, vsn_ref, selc_ref, wk_ref, wv_ref, kwn_ref, vwn_ref,
                  osel_ref, owin_ref, kbuf, vbuf, m_scr, l_scr, acc_scr, sem, *, n_pages, ppc, page, tdec):
    b = pl.program_id(0)
    c = pl.program_id(1)
    nb = pl.num_programs(0)
    nch = n_pages // ppc
    step = b * nch + c
    slot = step % 2
    rows = ppc * page
    ncol = qbd_ref.shape[2]

    def copies(bb, cc, sl, p):
        dst_k = kbuf.at[sl, :, pl.ds(pl.multiple_of(p * page, page), page)]
        dst_v = vbuf.at[sl, :, pl.ds(pl.multiple_of(p * page, page), page)]
        pid = pt_ref[bb, cc * ppc + p]
        return (pltpu.make_async_copy(kpool_ref.at[pid], dst_k, sem.at[0, sl]),
                pltpu.make_async_copy(vpool_ref.at[pid], dst_v, sem.at[1, sl]))

    def start_all(bb, cc, sl):
        def body(p, carry):
            ck, cv = copies(bb, cc, sl, p)
            ck.start()
            cv.start()
            return carry
        lax.fori_loop(0, ppc, body, 0)

    def wait_all(bb, cc, sl):
        def body(p, carry):
            ck, cv = copies(bb, cc, sl, p)
            ck.wait()
            cv.wait()
            return carry
        lax.fori_loop(0, ppc, body, 0)

    @pl.when(step == 0)
    def _():
        start_all(0, 0, 0)

    @pl.when(step + 1 < nb * nch)
    def _():
        nxt = step + 1
        start_all(nxt // nch, nxt % nch, 1 - slot)

    wait_all(b, c, slot)

    @pl.when(c == 0)
    def _():
        m_scr[...] = jnp.full(m_scr.shape, NEG_INF, F32)
        l_scr[...] = jnp.zeros(l_scr.shape, F32)
        acc_scr[...] = jnp.zeros(acc_scr.shape, F32)

    qbd = qbd_ref[0]
    nblk = rows // L_SEL
    s3 = (_dot_tn(kbuf[slot], qbd) * ATTN_SCALE).reshape(nblk, L_SEL, ncol)
    blk0 = pl.multiple_of(c * nblk, SUBLANES)
    chosen = (selc_ref[0, pl.ds(blk0, nblk), :] > 0.5)[:, None, :]
    s3 = jnp.where(chosen, s3, NEG_INF)
    m_old = m_scr[...]
    m_new = jnp.maximum(m_old, jnp.max(jnp.max(s3, axis=0), axis=0, keepdims=True))
    e3 = jnp.where(chosen, jnp.exp(s3 - m_new[None]), 0.0)
    alpha = jnp.exp(m_old - m_new)
    l_scr[...] = l_scr[...] * alpha + jnp.sum(jnp.sum(e3, axis=0), axis=0, keepdims=True)
    m_scr[...] = m_new
    acc_scr[...] = acc_scr[...] * alpha + _dot(vbuf[slot], e3.reshape(rows, ncol))

    @pl.when(c == nch - 1)
    def _():
        tcol = lax.broadcasted_iota(jnp.int32, (tdec, ncol), 1) % tdec
        jrow = lax.broadcasted_iota(jnp.int32, (tdec, ncol), 0)
        causal_new = jrow <= tcol
        sel_last = selc_ref[0, pl.ds(nch * nblk, 1), :] > 0.5
        ok_new = causal_new & sel_last
        s_new = jnp.where(ok_new, _dot(ksn_ref[0], qbd) * ATTN_SCALE, NEG_INF)
        m_old2 = m_scr[...]
        m_fin = jnp.maximum(m_old2, jnp.max(s_new, axis=0, keepdims=True))
        e_new = jnp.where(ok_new, jnp.exp(s_new - m_fin), 0.0)
        alpha2 = jnp.exp(m_old2 - m_fin)
        l_fin = l_scr[...] * alpha2 + jnp.sum(e_new, axis=0, keepdims=True)
        inv = 1.0 / jnp.maximum(l_fin, 1e-30)
        acc = acc_scr[...] * alpha2 + _dot_tn(vsn_ref[0], e_new)
        osel_ref[0] = acc * inv

        wb = wk_ref.shape[2]
        jw = lax.broadcasted_iota(jnp.int32, (wb, ncol), 0)
        tw = lax.broadcasted_iota(jnp.int32, (wb, ncol), 1) % tdec
        ok_c = jw + (WINDOW - wb) > tw
        s_c = jnp.where(ok_c, _dot_tn(wk_ref[0], qbd) * ATTN_SCALE, NEG_INF)
        s_n = jnp.where(causal_new, _dot(kwn_ref[0], qbd) * ATTN_SCALE, NEG_INF)
        m_w = jnp.maximum(jnp.max(s_c, axis=0, keepdims=True), jnp.max(s_n, axis=0, keepdims=True))
        e_c = jnp.where(ok_c, jnp.exp(s_c - m_w), 0.0)
        e_n = jnp.where(causal_new, jnp.exp(s_n - m_w), 0.0)
        inv_w = 1.0 / jnp.maximum(jnp.sum(e_c, axis=0, keepdims=True) + jnp.sum(e_n, axis=0, keepdims=True), 1e-30)
        ow = _dot(wv_ref[0], e_c) + _dot_tn(vwn_ref[0], e_n)
        owin_ref[0] = ow * inv_w


def _sample_attention(qrot, kpool, vpool, page_table, ks_new, vs_new, sel_t, wk, wv, kw_new, vw_new):
    bsz, tdec, _ = qrot.shape
    n_pages = page_table.shape[1]
    page = kpool.shape[2]
    ppc = min(n_pages, 32)
    assert n_pages % ppc == 0 and page % L_SEL == 0
    nch = n_pages // ppc
    rows = ppc * page
    ncol = KV_HEADS * Q_PER_KV * tdec
    ns_pad = sel_t.shape[2]
    wb = wk.shape[2]
    q5 = qrot.reshape(bsz, tdec, KV_HEADS, Q_PER_KV, HEAD_DIM).transpose(0, 2, 4, 3, 1)
    eye = jnp.eye(KV_HEADS, dtype=qrot.dtype)
    qbd = (q5[:, :, :, None] * eye[None, :, None, :, None, None]).reshape(bsz, KV_WIDTH, ncol)
    selc = jnp.broadcast_to(sel_t.transpose(0, 2, 1, 3)[:, :, :, None, :], (bsz, ns_pad, KV_HEADS, Q_PER_KV, tdec))
    selc = selc.reshape(bsz, ns_pad, ncol)
    per_b = lambda shape: pl.BlockSpec((1,) + shape, lambda b, c, pt: (b, 0, 0))
    grid_spec = pltpu.PrefetchScalarGridSpec(
        num_scalar_prefetch=1,
        grid=(bsz, nch),
        in_specs=[per_b((KV_WIDTH, ncol)),
                  pl.BlockSpec(memory_space=pl.ANY), pl.BlockSpec(memory_space=pl.ANY),
                  per_b((tdec, KV_WIDTH)), per_b((tdec, KV_WIDTH)),
                  per_b((ns_pad, ncol)),
                  per_b((KV_WIDTH, wb)), per_b((KV_WIDTH, wb)),
                  per_b((tdec, KV_WIDTH)), per_b((tdec, KV_WIDTH))],
        out_specs=[per_b((KV_WIDTH, ncol)), per_b((KV_WIDTH, ncol))],
        scratch_shapes=[pltpu.VMEM((2, KV_WIDTH, rows), F32), pltpu.VMEM((2, KV_WIDTH, rows), F32),
                        pltpu.VMEM((1, ncol), F32), pltpu.VMEM((1, ncol), F32), pltpu.VMEM((KV_WIDTH, ncol), F32),
                        pltpu.SemaphoreType.DMA((2, 2))],
    )
    o_sel, o_win = pl.pallas_call(
        functools.partial(_sattn_kernel, n_pages=n_pages, ppc=ppc, page=page, tdec=tdec),
        grid_spec=grid_spec,
        out_shape=[jax.ShapeDtypeStruct((bsz, KV_WIDTH, ncol), F32)] * 2,
        compiler_params=_cparams(("arbitrary", "arbitrary")),
        name="sample_attn",
    )(page_table, qbd, kpool, vpool, ks_new, vs_new, selc, wk, wv, kw_new, vw_new)

    def unpack(o):
        o6 = o.reshape(bsz, KV_HEADS, HEAD_DIM, KV_HEADS, Q_PER_KV, tdec)
        diag = jnp.stack([o6[:, g, :, g] for g in range(KV_HEADS)], axis=1)
        return diag.transpose(0, 4, 1, 3, 2).reshape(bsz, tdec, NSA_WIDTH)

    return unpack(o_sel), unpack(o_win)


def _mix_kernel(x_ref, oc_ref, os_ref, ow_ref, gt_ref, u_ref, v_ref, wsm_ref, bs_ref, gn_ref, gs_ref, eg_ref, wout_ref,
                o_ref, *, chunk):
    r = x_ref.shape[0]
    g = gt_ref[...]
    g_hi = g.astype(BF16)
    g_lo = (g - g_hi.astype(F32)).astype(BF16)
    onsa = None
    for j, branch in enumerate((oc_ref, os_ref, ow_ref)):
        ge = _dot(g_hi, eg_ref[j]) + _dot(g_lo, eg_ref[j])
        term = ge * branch[...]
        onsa = term if onsa is None else onsa + term
    onsa = _rms(onsa, gn_ref[...])
    ii = lax.broadcasted_iota(jnp.int32, (r, r), 0)
    jj = lax.broadcasted_iota(jnp.int32, (r, r), 1)
    tri = (ii // chunk == jj // chunk) & (jj % chunk <= ii % chunk)
    cols = []
    for gi in range(SGU_GROUPS):
        sl = slice(gi * SGU_GROUP_DIM, (gi + 1) * SGU_GROUP_DIM)
        ws = jnp.where(tri, wsm_ref[gi], 0.0)
        mixed = _dot(ws, v_ref[:, sl]) + bs_ref[:, gi:gi + 1]
        cols.append(u_ref[:, sl] * mixed)
    osgu = _rms(jnp.concatenate(cols, axis=1), gs_ref[...])
    o_ref[...] = x_ref[...] + _dot(jnp.concatenate([onsa, osgu], axis=1), wout_ref[...])


def _gate_expanders():
    c = jnp.arange(GATE_COLS)[None, :, None]
    lane = jnp.arange(NSA_WIDTH)[None, None, :]
    j = jnp.arange(3)[:, None, None]
    return (c == (lane // HEAD_DIM) * 3 + j).astype(BF16)


def _mix(x2d, o_cmp, o_sel, o_win, gates, u, v, w_sgu, b_sgu, g_nsa_out, g_sgu_out, w_out_b, chunk, r):
    n, d = x2d.shape
    rep = r // chunk
    pick = (jnp.arange(r)[:, None] % chunk == jnp.arange(chunk)[None, :]).astype(F32)
    wsm = jnp.einsum("ia,gab,jb->gij", pick, w_sgu[:, :chunk, :chunk], pick, precision=lax.Precision.HIGHEST)
    bs = jnp.tile(b_sgu[:, :chunk].T, (rep, 1))
    row = lambda w: pl.BlockSpec((r, w), lambda i: (i, 0))
    return pl.pallas_call(
        functools.partial(_mix_kernel, chunk=chunk),
        grid=(n // r,),
        in_specs=[row(d), row(NSA_WIDTH), row(NSA_WIDTH), row(NSA_WIDTH), row(GATE_COLS), row(1024), row(1024),
                  _const_spec(wsm.shape), _const_spec(bs.shape), _const_spec((1, NSA_WIDTH)), _const_spec((1, 1024)),
                  _const_spec((3, GATE_COLS, NSA_WIDTH)), _const_spec(w_out_b.shape)],
        out_specs=row(d),
        out_shape=jax.ShapeDtypeStruct((n, d), F32),
        compiler_params=_cparams(("parallel",)),
        name="mix_out_proj",
    )(x2d, o_cmp, o_sel, o_win, gates, u, v, wsm, bs, g_nsa_out, g_sgu_out, _gate_expanders(), w_out_b)


def _rms_matmul_kernel(x_ref, g_ref, w_ref, o_ref):
    o_ref[...] = _dot(_rms(x_ref[...], g_ref[...]), w_ref[...])


def _rms_matmul(x2d, g, w_b, tm):
    n, d = x2d.shape
    m = w_b.shape[1]
    return pl.pallas_call(
        _rms_matmul_kernel,
        grid=(n // tm,),
        in_specs=[pl.BlockSpec((tm, d), lambda i: (i, 0)), _const_spec((1, d)), _const_spec(w_b.shape)],
        out_specs=pl.BlockSpec((tm, m), lambda i: (i, 0)),
        out_shape=jax.ShapeDtypeStruct((n, m), F32),
        compiler_params=_cparams(("parallel",)),
        name="rms_matmul",
    )(x2d, g, w_b)


def _mem_heads(q, k_ref, v_ref):
    outs = []
    for h in range(MEM_HEADS):
        sl = slice(h * MEM_HEAD_DIM, (h + 1) * MEM_HEAD_DIM)
        s = _dot_nt(q[:, sl], k_ref[0, :, h, :]) * MEM_SCALE
        p = _masked_softmax(s, jnp.ones(s.shape, dtype=jnp.bool_))
        outs.append(_dot(p, v_ref[0, :, h, :]))
    return jnp.concatenate(outs, axis=1)


def _memattn_kernel(q_ref, k_ref, v_ref, o_ref):
    o_ref[0] = _mem_heads(q_ref[0], k_ref, v_ref)


def _mem_block_kernel(x_ref, g_ref, wq_ref, k_ref, v_ref, wo_ref, gm_ref, wr_ref, o_ref, gtop_ref):
    x = x_ref[0]
    hq = _dot(_rms(x, g_ref[...]), wq_ref[...])
    x2 = x + _dot(_mem_heads(hq, k_ref, v_ref), wo_ref[...])
    o_ref[0] = x2
    gtop_ref[0] = _top_group(x2, gm_ref, wr_ref)


def _mem_block(x3, g_mem, w_q, mk, mv, w_o, g_moe, wr):
    bsz, t, d = x3.shape
    m = mk.shape[1]
    tq = min(t, 512)
    kv_spec = pl.BlockSpec((1, m, MEM_HEADS, MEM_HEAD_DIM), lambda b, i: (b, 0, 0, 0))
    return pl.pallas_call(
        _mem_block_kernel,
        grid=(bsz, t // tq),
        in_specs=[pl.BlockSpec((1, tq, d), lambda b, i: (b, i, 0)), _const_spec((1, d)), _const_spec(w_q.shape),
                  kv_spec, kv_spec, _const_spec(w_o.shape), _const_spec((1, d)), _const_spec(wr.shape)],
        out_specs=[pl.BlockSpec((1, tq, d), lambda b, i: (b, i, 0)), pl.BlockSpec((1, tq, 1), lambda b, i: (b, i, 0))],
        out_shape=[jax.ShapeDtypeStruct((bsz, t, d), F32), jax.ShapeDtypeStruct((bsz, t, 1), jnp.int32)],
        compiler_params=_cparams(("parallel", "parallel")),
        name="mem_block",
    )(x3, g_mem, w_q, mk, mv, w_o, g_moe, wr)


def _mem_attention(hq, mk, mv):
    bsz, t, w = hq.shape
    m = mk.shape[1]
    tq = min(t, 512)
    return pl.pallas_call(
        _memattn_kernel,
        grid=(bsz, t // tq),
        in_specs=[pl.BlockSpec((1, tq, w), lambda b, i: (b, i, 0)),
                  pl.BlockSpec((1, m, MEM_HEADS, MEM_HEAD_DIM), lambda b, i: (b, 0, 0, 0)),
                  pl.BlockSpec((1, m, MEM_HEADS, MEM_HEAD_DIM), lambda b, i: (b, 0, 0, 0))],
        out_specs=pl.BlockSpec((1, tq, w), lambda b, i: (b, i, 0)),
        out_shape=jax.ShapeDtypeStruct((bsz, t, w), F32),
        compiler_params=_cparams(("parallel", "parallel")),
        name="mem_attn",
    )(hq, mk, mv)


def _router_logits(h, wr_ref):
    h_hi = h.astype(BF16)
    h_lo = (h - h_hi.astype(F32)).astype(BF16)
    w_hi, w_lo = wr_ref[0], wr_ref[1]
    return _dot(h_hi, w_hi) + _dot(h_hi, w_lo) + _dot(h_lo, w_hi)


def _top_group(x, g_ref, wr_ref):
    z = _router_logits(_rms(x, g_ref[...]), wr_ref)
    lane = lax.broadcasted_iota(jnp.int32, z.shape, 1)
    zg = jnp.where(lane < N_GROUPS, z, -jnp.inf)
    m = jnp.max(zg, axis=1, keepdims=True)
    first = jnp.min(jnp.where(zg == m, lane.astype(F32), float(LANES)), axis=1, keepdims=True)
    return first.astype(jnp.int32)


def _matmul_res_route_kernel(a_ref, w_ref, r_ref, g_ref, wr_ref, o_ref, gtop_ref):
    x = r_ref[...] + _dot(a_ref[...], w_ref[...])
    o_ref[...] = x
    gtop_ref[...] = _top_group(x, g_ref, wr_ref)


def _matmul_res_route(a2d, w_b, res, g_moe, wr, tm):
    n, k = a2d.shape
    m = w_b.shape[1]
    return pl.pallas_call(
        _matmul_res_route_kernel,
        grid=(n // tm,),
        in_specs=[pl.BlockSpec((tm, k), lambda i: (i, 0)), _const_spec(w_b.shape), pl.BlockSpec((tm, m), lambda i: (i, 0)),
                  _const_spec((1, m)), _const_spec(wr.shape)],
        out_specs=[pl.BlockSpec((tm, m), lambda i: (i, 0)), pl.BlockSpec((tm, 1), lambda i: (i, 0))],
        out_shape=[jax.ShapeDtypeStruct((n, m), F32), jax.ShapeDtypeStruct((n, 1), jnp.int32)],
        compiler_params=_cparams(("parallel",)),
        name="matmul_residual_route",
    )(a2d, w_b, res, g_moe, wr)


def _moe_kernel(src_ref, tg_ref, tv_ref, nv_ref, widx_ref, x_hbm, gm_ref, gf_ref, wr_ref, wg_ref, wu_ref, wd_ref, y_hbm,
                buf, hb_scr, w4_scr, gsem, ssem, *, tm):
    t = pl.program_id(0)
    e = pl.program_id(1)
    nt = pl.num_programs(0)
    slot = t % 2
    other = 1 - slot
    valid = tv_ref[t] == 1
    prev_valid = (t >= 1) & (tv_ref[jnp.maximum(t - 1, 0)] == 1)
    next_valid = (t + 1 < nt) & (tv_ref[jnp.minimum(t + 1, nt - 1)] == 1)

    def gather_start(tt, sl):
        def body(r, c):
            idx = jnp.maximum(src_ref[tt * tm + r], 0)
            pltpu.make_async_copy(x_hbm.at[pl.ds(idx, 1), :], buf.at[sl, pl.ds(r, 1), :], gsem.at[sl]).start()
            return c
        lax.fori_loop(0, tm, body, 0, unroll=8)

    def gather_wait(sl):
        pltpu.make_async_copy(x_hbm.at[pl.ds(0, tm), :], buf.at[sl], gsem.at[sl]).wait()

    def scatter_copy(tt, sl, r):
        return pltpu.make_async_copy(buf.at[sl, pl.ds(r, 1), :], y_hbm.at[pl.ds(src_ref[tt * tm + r], 1), :], ssem.at[0])

    def scatter_start(tt, sl):
        def body(r, c):
            scatter_copy(tt, sl, r).start()
            return c
        lax.fori_loop(0, nv_ref[tt], body, 0)

    def scatter_wait(tt, sl):
        def body(r, c):
            scatter_copy(tt, sl, r).wait()
            return c
        lax.fori_loop(0, nv_ref[tt], body, 0)

    @pl.when(e == 0)
    def _():
        @pl.when(t == 0)
        def _():
            gather_start(0, 0)

        @pl.when(prev_valid)
        def _():
            scatter_start(t - 1, other)

        @pl.when(valid)
        def _():
            gather_wait(slot)
            h = _rms(buf[slot], gm_ref[...])
            hb_scr[...] = h.astype(BF16)
            z = _router_logits(h, wr_ref)
            lane = lax.broadcasted_iota(jnp.int32, z.shape, 1)
            lanef = lane.astype(F32)
            grp = lane < N_GROUPS
            zg = jnp.where(grp, z, -jnp.inf)
            pg_top = 1.0 / jnp.sum(jnp.where(grp, jnp.exp(zg - jnp.max(zg, axis=1, keepdims=True)), 0.0), axis=1, keepdims=True)
            lo = N_GROUPS + tg_ref[t] * EXPERTS_PER_GROUP
            ing = (lane >= lo) & (lane < lo + EXPERTS_PER_GROUP)
            pf = _masked_softmax(z, ing)
            big = float(2 * LANES)
            m1 = jnp.max(jnp.where(ing, pf, -1.0), axis=1, keepdims=True)
            i1 = jnp.min(jnp.where(ing & (pf == m1), lanef, big), axis=1, keepdims=True)
            rest = ing & (lanef != i1)
            m2 = jnp.max(jnp.where(rest, pf, -1.0), axis=1, keepdims=True)
            i2 = jnp.min(jnp.where(rest & (pf == m2), lanef, big), axis=1, keepdims=True)
            tot = m1 + m2
            w4_scr[...] = jnp.where(lanef == i1, m1 / tot * pg_top, jnp.where(lanef == i2, m2 / tot * pg_top, 0.0))

    @pl.when(e == EXPERTS_PER_GROUP // 2)
    def _():
        @pl.when(prev_valid)
        def _():
            scatter_wait(t - 1, other)

        @pl.when(next_valid)
        def _():
            gather_start(t + 1, other)

    @pl.when(valid)
    def _():
        hb = hb_scr[...]
        lane = lax.broadcasted_iota(jnp.int32, w4_scr.shape, 1)
        col = N_GROUPS + tg_ref[t] * EXPERTS_PER_GROUP + e
        we = jnp.sum(jnp.where(lane == col, w4_scr[...], 0.0), axis=1, keepdims=True)
        act = jax.nn.silu(_dot(hb, wg_ref[0])) * _dot(hb, wu_ref[0])
        buf[slot] += _dot(act * we, wd_ref[0])

    @pl.when(valid & (e == EXPERTS_PER_GROUP - 1))
    def _():
        buf[slot] = _rms(buf[slot], gf_ref[...])


def _moe_final(x2d, g_top, g_moe, g_final, wr, w_gate, w_up, w_down, tm):
    n, d = x2d.shape
    n_tiles = n // tm + N_GROUPS
    onehot = (g_top[:, None] == jnp.arange(N_GROUPS)[None, :]).astype(jnp.int32)
    counts = jnp.sum(onehot, axis=0)
    rank = jnp.sum((jnp.cumsum(onehot, axis=0) - onehot) * onehot, axis=1)
    padded = (counts + tm - 1) // tm * tm
    ends = jnp.cumsum(padded)
    base = ends - padded
    pos = base[g_top] + rank
    src = jnp.full((n_tiles * tm,), -1, jnp.int32).at[pos].set(jnp.arange(n, dtype=jnp.int32))
    tile_start = jnp.arange(n_tiles, dtype=jnp.int32) * tm
    tile_valid = (tile_start < ends[-1]).astype(jnp.int32)
    tile_group = jnp.minimum(jnp.sum((tile_start[:, None] >= ends[None, :]).astype(jnp.int32), axis=1), N_GROUPS - 1)
    tile_rows = jnp.clip(ends[tile_group] - padded[tile_group] + counts[tile_group] - tile_start, 0, tm)
    tile_rows = (tile_rows * tile_valid).astype(jnp.int32)
    n_valid = ends[-1] // tm
    last_group = tile_group[jnp.maximum(n_valid - 1, 0)]
    eidx = tile_group[:, None] * EXPERTS_PER_GROUP + jnp.arange(EXPERTS_PER_GROUP, dtype=jnp.int32)[None, :]
    widx = jnp.where(tile_valid[:, None] == 1, eidx, last_group * EXPERTS_PER_GROUP + EXPERTS_PER_GROUP - 1)
    widx = widx.reshape(-1).astype(jnp.int32)

    wmap = lambda t, e, src, tg, tv, nv, wi: (wi[t * EXPERTS_PER_GROUP + e], 0, 0)
    cmap = lambda t, e, src, tg, tv, nv, wi: (0, 0)
    grid_spec = pltpu.PrefetchScalarGridSpec(
        num_scalar_prefetch=5,
        grid=(n_tiles, EXPERTS_PER_GROUP),
        in_specs=[pl.BlockSpec(memory_space=pl.ANY),
                  pl.BlockSpec((1, d), cmap), pl.BlockSpec((1, d), cmap),
                  pl.BlockSpec(wr.shape, lambda t, e, src, tg, tv, nv, wi: (0, 0, 0)),
                  pl.BlockSpec((1, d, EXPERT_FF), wmap), pl.BlockSpec((1, d, EXPERT_FF), wmap),
                  pl.BlockSpec((1, EXPERT_FF, d), wmap)],
        out_specs=pl.BlockSpec(memory_space=pl.ANY),
        scratch_shapes=[pltpu.VMEM((2, tm, d), F32), pltpu.VMEM((tm, d), BF16), pltpu.VMEM((tm, LANES), F32),
                        pltpu.SemaphoreType.DMA((2,)), pltpu.SemaphoreType.DMA((1,))],
    )
    return pl.pallas_call(
        functools.partial(_moe_kernel, tm=tm),
        grid_spec=grid_spec,
        out_shape=jax.ShapeDtypeStruct((n, d), F32),
        compiler_params=_cparams(("arbitrary", "arbitrary")),
        name="moe_final_norm",
    )(src, tile_group, tile_valid, tile_rows, widx, x2d, g_moe, g_final, wr, w_gate, w_up, w_down)


def _finish(x2d, o_cmp, o_sel, o_win, gates, u, v, mk, mv, bsz, lw, chunk, moe_tm):
    n, d = x2d.shape
    t = n // bsz
    x1 = _mix(x2d, o_cmp, o_sel, o_win, gates, u, v, lw["w_sgu"], lw["b_sgu"], lw["g_nsa_out"], lw["g_sgu_out"],
              lw["w_out"], chunk, min(n, 256))
    if t >= LANES:
        x2, g_top = _mem_block(x1.reshape(bsz, t, d), lw["g_mem_norm"], lw["w_mem_q"], mk, mv, lw["w_mem_o"],
                               lw["g_moe_norm"], lw["w_router"])
        x2, g_top = x2.reshape(n, d), g_top.reshape(n, 1)
    else:
        hq = _rms_matmul(x1, lw["g_mem_norm"], lw["w_mem_q"], min(n, 512))
        o_m = _mem_attention(hq.reshape(bsz, t, -1), mk, mv)
        x2, g_top = _matmul_res_route(o_m.reshape(n, -1), lw["w_mem_o"], x1, lw["g_moe_norm"], lw["w_router"], min(n, 512))
    return _moe_final(x2, g_top[:, 0], lw["g_moe_norm"], lw["g_final"], lw["w_router"], lw["w_exp_gate"], lw["w_exp_up"],
                      lw["w_exp_down"], moe_tm)


def kernel(x_prompt, x_sample, cache_cmp_k, cache_cmp_v, cache_sel_k, cache_sel_v, cache_win_k, cache_win_v, cache_mem_k, cache_mem_v, page_table, mem_prompt, w_in, g_attn_norm, pe_cmp_k, w_cmp_k1, w_cmp_k2, pe_cmp_v, w_cmp_v1, w_cmp_v2, g_sgu_v, w_sgu, b_sgu, g_nsa_out, g_sgu_out, w_out, g_mem_norm, g_mem_src, w_mem_q, w_mem_k, w_mem_v, w_mem_o, g_moe_norm, w_router_group, w_router_expert, w_exp_gate, w_exp_up, w_exp_down, g_final):
    depth = w_in.shape[0]
    assert depth == 1, "single-layer trunk"
    bp, tp, d = x_prompt.shape
    bs, ts, _ = x_sample.shape
    n_pages = page_table.shape[1]
    page = cache_cmp_k.shape[2]
    past = n_pages * page
    assert ts < STRIDE and tp % 256 == 0 and page % STRIDE == 0
    row = lambda a: a[0].reshape(1, -1)

    w_in_p = w_in[0].astype(BF16)
    wr = jnp.concatenate([w_router_group[0], w_router_expert[0],
                          jnp.zeros((d, LANES - N_GROUPS - N_EXPERTS), F32)], axis=1)
    wr_hi = wr.astype(BF16)
    wr = jnp.stack([wr_hi, (wr - wr_hi.astype(F32)).astype(BF16)])
    lw = {
        "w_sgu": w_sgu[0], "b_sgu": b_sgu[0], "g_nsa_out": row(g_nsa_out), "g_sgu_out": row(g_sgu_out),
        "w_out": w_out[0].astype(BF16), "g_mem_norm": row(g_mem_norm), "w_mem_q": w_mem_q[0].astype(BF16),
        "w_mem_o": w_mem_o[0].astype(BF16), "g_moe_norm": row(g_moe_norm), "g_final": g_final.reshape(1, -1),
        "w_router": wr, "w_exp_gate": w_exp_gate[0], "w_exp_up": w_exp_up[0], "w_exp_down": w_exp_down[0],
    }
    g_attn = row(g_attn_norm)
    gsv = row(g_sgu_v)
    kv5 = lambda a, b, t: a.reshape(1, b, t, KV_HEADS, HEAD_DIM)

    np_ = bp * tp
    tabs_p = _rope_tables(jnp.arange(tp, dtype=jnp.int32))
    (q, qrot, kc, vc, ks, vs, kw, vw, gates, u, v) = _project(x_prompt.reshape(np_, d), g_attn, w_in_p, tabs_p, gsv, 256,
                                                              kv_seq=tp)
    pt_p = jnp.arange(np_ // page, dtype=jnp.int32).reshape(bp, tp // page)
    kcmp = _compress(kc, pt_p, page, pe_cmp_k[0], w_cmp_k1[0], w_cmp_k2[0])
    vcmp = _compress(vc, pt_p, page, pe_cmp_v[0], w_cmp_v1[0], w_cmp_v2[0])
    o_cmp, sel_t = _cmp_select(q.reshape(bp, tp, -1), kcmp, vcmp, 0, tp)
    qrot3 = qrot.reshape(bp, tp, -1)
    o_sel = _prompt_attention(qrot3, ks, vs, sel_t)
    o_win = _prompt_attention(qrot3, kw, vw)
    n_mem = mem_prompt.shape[1]
    mem_w = MEM_HEADS * MEM_HEAD_DIM
    mem2d = mem_prompt.reshape(bp * n_mem, d)
    mem4 = lambda a: a.reshape(bp, n_mem, MEM_HEADS, MEM_HEAD_DIM)
    mk_p = mem4(_rms_matmul(mem2d, row(g_mem_src), w_mem_k[0].astype(BF16), min(bp * n_mem, 512)))
    mv_p = mem4(_rms_matmul(mem2d, row(g_mem_src), w_mem_v[0].astype(BF16), min(bp * n_mem, 512)))
    y_p = _finish(x_prompt.reshape(np_, d), o_cmp.reshape(np_, -1), o_sel.reshape(np_, -1), o_win.reshape(np_, -1),
                  gates, u, v, mk_p, mv_p, bp, lw, CHUNK, min(np_, 1024))
    wbp = min(WINDOW, tp)
    kv5t = lambda a: a.reshape(bp, KV_HEADS, HEAD_DIM, a.shape[2]).transpose(0, 3, 1, 2)[None]
    outs_p = (kv5t(kc), kv5t(vc), kv5t(ks), kv5t(vs), kv5t(kw[:, :, -wbp:]), kv5t(vw[:, :, -wbp:]), mk_p[None], mv_p[None])

    ns_ = bs * ts
    pos_s = past + jnp.arange(ts, dtype=jnp.int32)
    tabs_s = tuple(jnp.tile(a, (bs, 1)) for a in _rope_tables(pos_s))
    (q, qrot, kc, vc, ks, vs, kw, vw, gates, u, v) = _project(x_sample.reshape(ns_, d), g_attn, w_in_p, tabs_s, gsv, ns_)
    pool = lambda c: c[0].transpose(0, 2, 3, 1).reshape(c.shape[1], KV_WIDTH, page)
    kcmp = _compress(pool(cache_cmp_k), page_table, page, pe_cmp_k[0], w_cmp_k1[0], w_cmp_k2[0])
    vcmp = _compress(pool(cache_cmp_v), page_table, page, pe_cmp_v[0], w_cmp_v1[0], w_cmp_v2[0])
    o_cmp, sel_t = _cmp_select(q.reshape(bs, ts, -1).astype(F32), kcmp, vcmp, past, past + ts)
    wb = cache_win_k.shape[2]
    wk = cache_win_k[0].transpose(0, 2, 3, 1).reshape(bs, KV_WIDTH, wb)
    wv = cache_win_v[0].transpose(0, 2, 3, 1).reshape(bs, KV_WIDTH, wb)
    ks3, vs3, kw3, vw3 = (a.reshape(bs, ts, KV_WIDTH) for a in (ks, vs, kw, vw))
    o_sel, o_win = _sample_attention(qrot.reshape(bs, ts, -1), pool(cache_sel_k), pool(cache_sel_v), page_table,
                                     ks3, vs3, sel_t, wk, wv, kw3, vw3)
    mem_ks, mem_vs = cache_mem_k[0], cache_mem_v[0]
    y_s = _finish(x_sample.reshape(ns_, d), o_cmp.reshape(ns_, -1), o_sel.reshape(ns_, -1), o_win.reshape(ns_, -1),
                  gates, u, v, mem_ks, mem_vs, bs, lw, ts, min(ns_, 128))
    def slide(cache_t, new):
        win = jnp.concatenate([cache_t, new.transpose(0, 2, 1)], axis=2)[:, :, -wb:]
        return win.reshape(bs, KV_HEADS, HEAD_DIM, wb).transpose(0, 3, 1, 2)[None]

    win_k_s = slide(wk, kw3)
    win_v_s = slide(wv, vw3)
    outs_s = (kv5(kc, bs, ts), kv5(vc, bs, ts), kv5(ks, bs, ts), kv5(vs, bs, ts), win_k_s, win_v_s,
              v.reshape(1, bs, ts, -1))

    return (y_p.reshape(bp, tp, d), y_s.reshape(bs, ts, d)) + outs_p + outs_s
```

```python
import functools

import jax
import jax.numpy as jnp
from jax import lax
from jax.experimental import pallas as pl
from jax.experimental.pallas import tpu as pltpu

F32 = jnp.float32
BF16 = jnp.bfloat16

N_HEADS = 16
HEAD_DIM = 64
KV_HEADS = 4
Q_PER_KV = N_HEADS // KV_HEADS
NSA_WIDTH = N_HEADS * HEAD_DIM
KV_WIDTH = KV_HEADS * HEAD_DIM
ROT_DIM = HEAD_DIM // 4
ROPE_THETA = 500000.0
L_CMP = 32
STRIDE = 16
CMP_R = L_CMP // STRIDE
CMP_HIDDEN = 128
L_SEL = 64
SEL_RATIO = L_SEL // STRIDE
SEL_INNER = (L_SEL - L_CMP) // STRIDE + 1
TOP_N = 16
WINDOW = 512
SGU_GROUPS = 8
SGU_GROUP_DIM = 128
CHUNK = 128
MEM_HEADS = 4
MEM_HEAD_DIM = 128
N_GROUPS = 4
EXPERTS_PER_GROUP = 4
N_EXPERTS = N_GROUPS * EXPERTS_PER_GROUP
EXPERT_FF = 512
EPS = 1e-6
NEG_INF = -1e30
BIG = 1e9
PAD_SCORE = -3e38
CMP_ROW_PITCH = STRIDE + 4
ATTN_SCALE = HEAD_DIM ** -0.5
MEM_SCALE = MEM_HEAD_DIM ** -0.5

LANES = 128
SUBLANES = 8
V7X_VMEM_LIMIT_BYTES = 60000 * 1024

GATE_COLS = LANES
N_GATES = 3 * N_HEADS
_SEG = {}
_off = 0
for _name, _w in (("q", NSA_WIDTH), ("kc", KV_WIDTH), ("vc", KV_WIDTH), ("ks", KV_WIDTH), ("vs", KV_WIDTH),
                  ("kw", KV_WIDTH), ("vw", KV_WIDTH), ("gt", N_GATES), ("u", 1024), ("v", 1024)):
    _SEG[_name] = (_off, _w)
    _off += _w
IN_COLS = _off


def _cparams(sem, vmem=V7X_VMEM_LIMIT_BYTES):
    return pltpu.CompilerParams(dimension_semantics=sem, vmem_limit_bytes=vmem)


def _dot(a, b):
    return jnp.dot(a.astype(BF16), b.astype(BF16), preferred_element_type=F32)


def _dot_nt(a, b):
    return lax.dot_general(a.astype(BF16), b.astype(BF16), (((1,), (1,)), ((), ())), preferred_element_type=F32)


def _dot_tn(a, b):
    return lax.dot_general(a.astype(BF16), b.astype(BF16), (((0,), (0,)), ((), ())), preferred_element_type=F32)


def _rms(x, g):
    return x * lax.rsqrt(jnp.mean(x * x, axis=-1, keepdims=True) + EPS) * g


def _masked_softmax(s, mask, axis=-1):
    s = jnp.where(mask, s, NEG_INF)
    m = jnp.max(s, axis=axis, keepdims=True)
    e = jnp.where(mask, jnp.exp(s - m), 0.0)
    return e * (1.0 / jnp.maximum(jnp.sum(e, axis=axis, keepdims=True), 1e-30))


def _const_spec(shape):
    nd = len(shape)
    return pl.BlockSpec(shape, lambda *_: (0,) * nd, pipeline_mode=pl.Buffered(1))


def _proj_kernel(x_ref, g_ref, w_ref, rc_ref, rs1_ref, rs2_ref, gsv_ref,
                 q_ref, qrot_ref, kc_ref, vc_ref, ks_ref, vs_ref, kw_ref, vw_ref, gates_ref, u_ref, v_ref, *, transpose_kv):
    hb = _rms(x_ref[...], g_ref[...]).astype(BF16)

    def put_kv(ref, val):
        if transpose_kv:
            ref[0] = val.T
        else:
            ref[...] = val

    def seg(name):
        lo, width = _SEG[name]
        return jnp.dot(hb, w_ref[:, lo:lo + width], preferred_element_type=F32)

    def rope(z):
        rc, rs1, rs2 = rc_ref[...], rs1_ref[...], rs2_ref[...]
        half = ROT_DIM // 2
        outs = []
        for c in range(z.shape[1] // LANES):
            zc = z[:, c * LANES:(c + 1) * LANES]
            outs.append(zc * rc + pltpu.roll(zc, LANES - half, 1) * rs1 + pltpu.roll(zc, half, 1) * rs2)
        return jnp.concatenate(outs, axis=1)

    q = seg("q")
    q_ref[...] = q.astype(BF16)
    qrot_ref[...] = rope(q).astype(BF16)
    put_kv(kc_ref, seg("kc"))
    put_kv(vc_ref, seg("vc"))
    put_kv(ks_ref, rope(seg("ks")))
    put_kv(vs_ref, seg("vs"))
    put_kv(kw_ref, rope(seg("kw")))
    put_kv(vw_ref, seg("vw"))
    g0 = _SEG["gt"][0]
    zt = jnp.dot(hb, w_ref[:, g0:IN_COLS], preferred_element_type=F32)
    u0, v0 = _SEG["u"][0] - g0, _SEG["v"][0] - g0
    gates_ref[...] = jax.nn.sigmoid(zt[:, :GATE_COLS])
    u_ref[...] = jax.nn.gelu(zt[:, u0:u0 + 1024])
    v_ref[...] = _rms(jax.nn.gelu(zt[:, v0:v0 + 1024]), gsv_ref[...])


def _project(x2d, g_attn, w_in_p, tables, g_sgu_v, tm, kv_seq=None):
    n, d = x2d.shape
    rc, rs1, rs2 = tables
    tt = rc.shape[0]
    nt = tt // tm
    row = lambda w: pl.BlockSpec((tm, w), lambda i: (i, 0))
    tab = pl.BlockSpec((tm, LANES), lambda i: (i % nt, 0))
    out_shapes = [jax.ShapeDtypeStruct((n, NSA_WIDTH), BF16), jax.ShapeDtypeStruct((n, NSA_WIDTH), BF16)]
    if kv_seq is None:
        out_shapes += [jax.ShapeDtypeStruct((n, KV_WIDTH), F32)] * 6
        kv_spec = row(KV_WIDTH)
    else:
        per = kv_seq // tm
        out_shapes += [jax.ShapeDtypeStruct((n // kv_seq, KV_WIDTH, kv_seq), F32)] * 6
        kv_spec = pl.BlockSpec((1, KV_WIDTH, tm), lambda i: (i // per, 0, i % per))
    out_shapes += [jax.ShapeDtypeStruct((n, GATE_COLS), F32), jax.ShapeDtypeStruct((n, 1024), F32),
                   jax.ShapeDtypeStruct((n, 1024), F32)]
    out_specs = [row(NSA_WIDTH), row(NSA_WIDTH)] + [kv_spec] * 6 + [row(GATE_COLS), row(1024), row(1024)]
    return pl.pallas_call(
        functools.partial(_proj_kernel, transpose_kv=kv_seq is not None),
        grid=(n // tm,),
        in_specs=[row(d), _const_spec((1, d)), _const_spec(w_in_p.shape), tab, tab, tab, _const_spec((1, 1024))],
        out_specs=out_specs,
        out_shape=out_shapes,
        compiler_params=_cparams(("parallel",)),
        name="in_proj",
    )(x2d, g_attn, w_in_p, rc, rs1, rs2, g_sgu_v)


def _rope_tables(pos):
    half = ROT_DIM // 2
    freqs = ROPE_THETA ** (-jnp.arange(half, dtype=F32) / half)
    ang = pos.astype(F32)[:, None] * freqs[None, :]
    cos, sin = jnp.cos(ang), jnp.sin(ang)
    t = pos.shape[0]
    ones = jnp.ones((t, HEAD_DIM - ROT_DIM), F32)
    zeros = jnp.zeros((t, HEAD_DIM - ROT_DIM), F32)
    zh = jnp.zeros((t, half), F32)
    rc = jnp.concatenate([cos, cos, ones], axis=1)
    rs1 = jnp.concatenate([-sin, zh, zeros], axis=1)
    rs2 = jnp.concatenate([zh, sin, zeros], axis=1)
    rep = LANES // HEAD_DIM
    return tuple(jnp.tile(a, (1, rep)) for a in (rc, rs1, rs2))


def _compress_kernel(pt_ref, pool_ref, pe_ref, w1_ref, w2_ref, out_ref, stage, buf, a_ref, sem, *, n_pages, page):
    b = pl.program_id(0)
    nb = pl.num_programs(0)
    n_rows = n_pages * page
    n_sub = n_rows // STRIDE
    slot = b % 2
    n_ct = KV_WIDTH // LANES
    ppl = pool_ref.shape[2] // page

    def page_copy(bb, sl, p):
        q = pt_ref[bb, p]
        src = pool_ref.at[q // ppl, :, pl.ds(pl.multiple_of((q % ppl) * page, page), page)]
        return pltpu.make_async_copy(src, stage.at[sl, :, pl.ds(pl.multiple_of(p * page, page), page)], sem.at[sl])

    def start_all(bb, sl):
        def body(p, carry):
            page_copy(bb, sl, p).start()
            return carry
        lax.fori_loop(0, n_pages, body, 0)

    def wait_all(bb, sl):
        def body(p, carry):
            page_copy(bb, sl, p).wait()
            return carry
        lax.fori_loop(0, n_pages, body, 0)

    @pl.when(b == 0)
    def _():
        start_all(0, 0)

    @pl.when(b + 1 < nb)
    def _():
        start_all(b + 1, 1 - slot)

    wait_all(b, slot)

    tch = min(n_rows, 512)
    for c in range(n_ct):
        for j in range(n_rows // tch):
            rows_t = stage[slot, c * LANES:(c + 1) * LANES, j * tch:(j + 1) * tch].T
            for n in range(tch // STRIDE):
                r0 = (j * (tch // STRIDE) + n) * CMP_ROW_PITCH
                buf[c, r0:r0 + STRIDE, :] = rows_t[n * STRIDE:(n + 1) * STRIDE, :]

    pev = _dot(pe_ref[...], w1_ref[...])
    low = lax.broadcasted_iota(jnp.int32, (n_sub, LANES), 1) < HEAD_DIM
    row = lax.broadcasted_iota(jnp.int32, (n_sub, 4 * HEAD_DIM), 0)
    for c in range(n_ct):
        src = buf.at[c]
        for sp in range(STRIDE // 2):
            x0 = src[pl.ds(2 * sp, n_sub, stride=CMP_ROW_PITCH), :]
            x1 = src[pl.ds(2 * sp + 1, n_sub, stride=CMP_ROW_PITCH), :]
            a_ref[0, :, sp * LANES:(sp + 1) * LANES] = jnp.where(low, x0, pltpu.roll(x1, HEAD_DIM, 1)).astype(BF16)
            a_ref[1, :, sp * LANES:(sp + 1) * LANES] = jnp.where(low, pltpu.roll(x0, HEAD_DIM, 1), x1).astype(BF16)
        for k in range(2):
            pm = jnp.dot(a_ref[k], w1_ref[...], preferred_element_type=F32)
            part0 = pm[:, :CMP_HIDDEN] + pev[0:1, :CMP_HIDDEN]
            part1 = pm[:, CMP_HIDDEN:] + pev[1:2, CMP_HIDDEN:]
            hsum = part0 + pltpu.roll(part1, n_sub - 1, 0)
            o = _dot(jax.nn.silu(hsum), w2_ref[...])
            out_ref[0, 2 * c + k] = jnp.where(row < n_sub - 1, o, 0.0).astype(BF16)


def _compress(pool, page_table, page, pe, w1, w2):
    bsz, n_pages = page_table.shape
    n_rows = n_pages * page
    n_sub = n_rows // STRIDE
    n_ct = KV_WIDTH // LANES
    kdim = STRIDE * HEAD_DIM
    pe8 = jnp.zeros((SUBLANES, kdim), F32).at[:CMP_R].set(pe.reshape(CMP_R, kdim))
    w1c = w1.reshape(CMP_R, kdim, CMP_HIDDEN).transpose(1, 0, 2).reshape(kdim, CMP_R * CMP_HIDDEN).astype(BF16)
    w2t = jnp.tile(w2, (1, 4)).astype(BF16)
    grid_spec = pltpu.PrefetchScalarGridSpec(
        num_scalar_prefetch=1,
        grid=(bsz,),
        in_specs=[pl.BlockSpec(memory_space=pl.ANY),
                  pl.BlockSpec((SUBLANES, kdim), lambda b, pt: (0, 0)),
                  pl.BlockSpec(w1c.shape, lambda b, pt: (0, 0)),
                  pl.BlockSpec(w2t.shape, lambda b, pt: (0, 0))],
        out_specs=pl.BlockSpec((1, KV_HEADS, n_sub, 4 * HEAD_DIM), lambda b, pt: (b, 0, 0, 0)),
        scratch_shapes=[pltpu.VMEM((2, KV_WIDTH, n_rows), F32), pltpu.VMEM((n_ct, n_sub * CMP_ROW_PITCH, LANES), F32),
                        pltpu.VMEM((2, n_sub, kdim), BF16), pltpu.SemaphoreType.DMA((2,))],
    )
    return pl.pallas_call(
        functools.partial(_compress_kernel, n_pages=n_pages, page=page),
        grid_spec=grid_spec,
        out_shape=jax.ShapeDtypeStruct((bsz, KV_HEADS, n_sub, 4 * HEAD_DIM), BF16),
        compiler_params=_cparams(("arbitrary",)),
        name="compress",
    )(page_table, pool, pe8, w1c, w2t)


def _cmpsel_kernel(q_ref, k_ref, v_ref, st_ref, o_ref, sc_ref, *, pos0, nc, ns, tq):
    i = pl.program_id(2)
    n_sub = k_ref.shape[2]
    ns_pad = st_ref.shape[0]
    gw = Q_PER_KV * HEAD_DIM
    lane_head = lax.broadcasted_iota(jnp.int32, (tq, gw), 1) // HEAD_DIM
    qpos = pos0 + i * tq + lax.broadcasted_iota(jnp.int32, (tq, 1), 0)
    kidx = lax.broadcasted_iota(jnp.int32, (1, n_sub), 1)
    mask = (kidx * STRIDE + (L_CMP - 1) <= qpos) & (kidx < nc)
    mask_rows = jnp.concatenate([mask] * Q_PER_KV, axis=0)
    blk = lax.broadcasted_iota(jnp.int32, (ns_pad, tq), 0)
    qpos_t = pos0 + i * tq + lax.broadcasted_iota(jnp.int32, (ns_pad, tq), 1)
    cur = qpos_t // L_SEL
    forced = (blk == 0) | (blk == cur) | (blk == cur - 1)
    future = blk * L_SEL > qpos_t
    st = st_ref[...]
    for g in range(k_ref.shape[1]):
        qf = q_ref[0, :, g * gw:(g + 1) * gw].astype(F32)
        q_rows = jnp.concatenate([jnp.where(lane_head == qi, qf, 0.0) for qi in range(Q_PER_KV)], axis=0)
        s = _dot_nt(q_rows, k_ref[0, g]) * ATTN_SCALE
        p = _masked_softmax(s, mask_rows)
        o_rows = _dot(p, v_ref[0, g])
        o = None
        pg = None
        for qi in range(Q_PER_KV):
            part = jnp.where(lane_head == qi, o_rows[qi * tq:(qi + 1) * tq], 0.0)
            o = part if o is None else o + part
            pq = p[qi * tq:(qi + 1) * tq]
            pg = pq if pg is None else pg + pq
        o_ref[0, :, g * gw:(g + 1) * gw] = o
        pg_hi = pg.astype(BF16)
        pg_lo = (pg - pg_hi.astype(F32)).astype(BF16)
        ps_t = _dot_nt(st, pg_hi) + _dot_nt(st, pg_lo)
        sc = jnp.where(forced, BIG, jnp.where(future, -BIG, ps_t))
        sc_ref[0, g] = jnp.where(blk < ns, sc, PAD_SCORE)


def _rank_kernel(sc_ref, sel_ref, *, ns, topn):
    sc = sc_ref[0, 0]
    blk = lax.broadcasted_iota(jnp.int32, sc.shape, 0)
    sel_ref[0, 0] = jnp.zeros(sc.shape, F32)

    def body(j, c):
        row = sc_ref[0, 0, pl.ds(j, 1), :]
        ahead = (sc > row) | ((sc == row) & (blk < j))
        rank = jnp.sum(ahead.astype(F32), axis=0, keepdims=True)
        sel_ref[0, 0, pl.ds(j, 1), :] = (rank < topn).astype(F32)
        return c

    lax.fori_loop(0, ns, body, 0)


def _cmp_select(q, kcmp, vcmp, pos0, n_keys):
    bsz, t, _ = q.shape
    n_sub = kcmp.shape[2]
    nc = n_sub - CMP_R + 1
    ns = -(-n_keys // L_SEL)
    ns_pad = -(-ns // SUBLANES) * SUBLANES
    topn = min(TOP_N, ns)
    tq = min(t, 512)
    gw = Q_PER_KV * HEAD_DIM
    gps = KV_HEADS if t < LANES else 1
    cidx = jnp.arange(n_sub)
    st = ((cidx[None, :] // SEL_RATIO == jnp.arange(ns_pad)[:, None]) & (cidx[None, :] % SEL_RATIO < SEL_INNER)
          & (cidx[None, :] < nc)).astype(BF16)
    o_cmp, sc = pl.pallas_call(
        functools.partial(_cmpsel_kernel, pos0=pos0, nc=nc, ns=ns, tq=tq),
        grid=(bsz, KV_HEADS // gps, t // tq),
        in_specs=[pl.BlockSpec((1, tq, gw * gps), lambda b, g, i: (b, i, g)),
                  pl.BlockSpec((1, gps, n_sub, gw), lambda b, g, i: (b, g, 0, 0)),
                  pl.BlockSpec((1, gps, n_sub, gw), lambda b, g, i: (b, g, 0, 0)),
                  pl.BlockSpec((ns_pad, n_sub), lambda b, g, i: (0, 0))],
        out_specs=[pl.BlockSpec((1, tq, gw * gps), lambda b, g, i: (b, i, g)),
                   pl.BlockSpec((1, gps, ns_pad, tq), lambda b, g, i: (b, g, 0, i))],
        out_shape=[jax.ShapeDtypeStruct((bsz, t, NSA_WIDTH), F32),
                   jax.ShapeDtypeStruct((bsz, KV_HEADS, ns_pad, t), F32)],
        compiler_params=_cparams(("parallel", "parallel", "parallel")),
        name="cmp_scores",
    )(q, kcmp, vcmp, st)
    fold = t < LANES
    if fold:
        sc = sc.transpose(2, 0, 1, 3).reshape(1, 1, ns_pad, bsz * KV_HEADS * t)
    nb, ng, _, width = sc.shape
    tl = min(width, 2048)
    sel = pl.pallas_call(
        functools.partial(_rank_kernel, ns=ns, topn=topn),
        grid=(nb, ng, width // tl),
        in_specs=[pl.BlockSpec((1, 1, ns_pad, tl), lambda b, g, i: (b, g, 0, i))],
        out_specs=pl.BlockSpec((1, 1, ns_pad, tl), lambda b, g, i: (b, g, 0, i)),
        out_shape=jax.ShapeDtypeStruct(sc.shape, F32),
        compiler_params=_cparams(("parallel", "parallel", "parallel")),
        name="rank_select",
    )(sc)
    if fold:
        sel = sel.reshape(ns_pad, bsz, KV_HEADS, t).transpose(1, 2, 0, 3)
    return o_cmp, sel


def _pattn_kernel(*refs, mode, t, tq, ck):
    if mode == "sel":
        q_ref, k_ref, v_ref, sel_ref, et_ref, o_ref, kc_scr, vt_scr, m_scr, l_scr, acc_scr = refs
    else:
        q_ref, k_ref, v_ref, o_ref, kc_scr, vt_scr, m_scr, l_scr, acc_scr = refs
    i = pl.program_id(2)
    n_chunks = t // ck
    last = (i + 1) * (tq // ck) - 1

    @pl.when(i == 0)
    def _():
        zeros = jnp.zeros((ck, LANES - HEAD_DIM), F32)
        for c in range(n_chunks):
            kc_scr[c] = jnp.concatenate([k_ref[0, :, c * ck:(c + 1) * ck].T, zeros], axis=1).astype(BF16)
            vt_scr[c] = v_ref[0, :, c * ck:(c + 1) * ck].astype(BF16)

    qf = q_ref[0].astype(F32) * ATTN_SCALE
    q_heads = []
    for qi in range(Q_PER_KV):
        tile = qf[:, (qi // 2) * LANES:(qi // 2 + 1) * LANES]
        q_heads.append((tile if qi % 2 == 0 else pltpu.roll(tile, HEAD_DIM, 1)).astype(BF16))
    q_all = jnp.concatenate(q_heads, axis=0)
    qpos = i * tq + lax.broadcasted_iota(jnp.int32, (1, tq), 1)
    m_scr[...] = jnp.full(m_scr.shape, NEG_INF, F32)
    l_scr[...] = jnp.zeros(l_scr.shape, F32)
    acc_scr[...] = jnp.zeros(acc_scr.shape, F32)
    if mode == "sel":
        sel_b = sel_ref[0, 0].astype(BF16)
        n_steps = last + 1
    else:
        n_steps = last - jnp.maximum(i * tq - WINDOW, 0) // ck + 1

    def chunk(c):
        kpos = c * ck + lax.broadcasted_iota(jnp.int32, (ck, 1), 0)
        if mode == "sel":
            chosen = jnp.dot(et_ref[c], sel_b, preferred_element_type=F32)
            ok = (chosen > 0.5) & (kpos <= qpos)
        else:
            ok = (kpos <= qpos) & (kpos > qpos - WINDOW)
        bias = jnp.where(ok, 0.0, NEG_INF)
        s = _dot_nt(kc_scr[c], q_all) + jnp.concatenate([bias] * Q_PER_KV, axis=1)
        m_old = m_scr[...]
        m_new = jnp.maximum(m_old, jnp.max(s, axis=0, keepdims=True))
        e = jnp.exp(s - m_new)
        alpha = jnp.exp(m_old - m_new)
        l_scr[...] = l_scr[...] * alpha + jnp.sum(e, axis=0, keepdims=True)
        m_scr[...] = m_new
        acc_scr[...] = acc_scr[...] * alpha + jnp.dot(vt_scr[c], e.astype(BF16), preferred_element_type=F32)

    def body(step, carry):
        chunk(last - step)
        return carry

    lax.fori_loop(0, n_steps, body, 0)
    out = acc_scr[...] * (1.0 / jnp.maximum(l_scr[...], 1e-30))
    o_ref[0] = jnp.concatenate([out[:, qi * tq:(qi + 1) * tq] for qi in range(Q_PER_KV)], axis=0).T


def _prompt_attention(qrot, k, v, sel_t=None):
    bsz, t, _ = qrot.shape
    tq, ck = 512, 512
    gw = Q_PER_KV * HEAD_DIM
    mode = "win" if sel_t is None else "sel"
    assert t % tq == 0 and tq % ck == 0 and WINDOW % ck == 0 and ck % L_SEL == 0
    in_specs = [pl.BlockSpec((1, tq, gw), lambda b, g, i: (b, i, g)),
                pl.BlockSpec((1, HEAD_DIM, t), lambda b, g, i: (b, g, 0)),
                pl.BlockSpec((1, HEAD_DIM, t), lambda b, g, i: (b, g, 0))]
    args = [qrot, k, v]
    if mode == "sel":
        ns_pad = sel_t.shape[2]
        et = (jnp.arange(t)[:, None] // L_SEL == jnp.arange(ns_pad)[None, :]).astype(BF16).reshape(t // ck, ck, ns_pad)
        in_specs += [pl.BlockSpec((1, 1, ns_pad, tq), lambda b, g, i: (b, g, 0, i)),
                     pl.BlockSpec((t // ck, ck, ns_pad), lambda b, g, i: (0, 0, 0))]
        args += [sel_t, et]
    return pl.pallas_call(
        functools.partial(_pattn_kernel, mode=mode, t=t, tq=tq, ck=ck),
        grid=(bsz, KV_HEADS, t // tq),
        in_specs=in_specs,
        out_specs=pl.BlockSpec((1, tq, gw), lambda b, g, i: (b, i, g)),
        out_shape=jax.ShapeDtypeStruct((bsz, t, NSA_WIDTH), F32),
        scratch_shapes=[pltpu.VMEM((t // ck, ck, LANES), BF16), pltpu.VMEM((t // ck, HEAD_DIM, ck), BF16),
                        pltpu.VMEM((1, Q_PER_KV * tq), F32), pltpu.VMEM((1, Q_PER_KV * tq), F32),
                        pltpu.VMEM((HEAD_DIM, Q_PER_KV * tq), F32)],
        compiler_params=_cparams(("parallel", "parallel", "arbitrary")),
        name="prompt_attn_" + mode,
    )(*args)


def _sattn_kernel(pt_ref, qbd_ref, kpool_ref, vpool_ref, ksn_ref, vsn_ref, selc_ref, wk_ref, wv_ref, kwn_ref, vwn_ref,
                  osel_ref, owin_ref, kbuf, vbuf, m_scr, l_scr, acc_scr, sem, *, n_pages, ppc, page, tdec):
    b = pl.program_id(0)
    c = pl.program_id(1)
    nb = pl.num_programs(0)
    nch = n_pages // ppc
    step = b * nch + c
    slot = step % 2
    rows = ppc * page
    ncol = qbd_ref.shape[2]

    def copies(bb, cc, sl, p):
        dst_k = kbuf.at[sl, :, pl.ds(pl.multiple_of(p * page, page), page)]
        dst_v = vbuf.at[sl, :, pl.ds(pl.multiple_of(p * page, page), page)]
        pid = pt_ref[bb, cc * ppc + p]
        return (pltpu.make_async_copy(kpool_ref.at[pid], dst_k, sem.at[0, sl]),
                pltpu.make_async_copy(vpool_ref.at[pid], dst_v, sem.at[1, sl]))

    def start_all(bb, cc, sl):
        def body(p, carry):
            ck, cv = copies(bb, cc, sl, p)
            ck.start()
            cv.start()
            return carry
        lax.fori_loop(0, ppc, body, 0)

    def wait_all(bb, cc, sl):
        def body(p, carry):
            ck, cv = copies(bb, cc, sl, p)
            ck.wait()
            cv.wait()
            return carry
        lax.fori_loop(0, ppc, body, 0)

    @pl.when(step == 0)
    def _():
        start_all(0, 0, 0)

    @pl.when(step + 1 < nb * nch)
    def _():
        nxt = step + 1
        start_all(nxt // nch, nxt % nch, 1 - slot)

    wait_all(b, c, slot)

    @pl.when(c == 0)
    def _():
        m_scr[...] = jnp.full(m_scr.shape, NEG_INF, F32)
        l_scr[...] = jnp.zeros(l_scr.shape, F32)
        acc_scr[...] = jnp.zeros(acc_scr.shape, F32)

    qbd = qbd_ref[0]
    nblk = rows // L_SEL
    s3 = (_dot_tn(kbuf[slot], qbd) * ATTN_SCALE).reshape(nblk, L_SEL, ncol)
    blk0 = pl.multiple_of(c * nblk, SUBLANES)
    chosen = (selc_ref[0, pl.ds(blk0, nblk), :] > 0.5)[:, None, :]
    s3 = jnp.where(chosen, s3, NEG_INF)
    m_old = m_scr[...]
    m_new = jnp.maximum(m_old, jnp.max(jnp.max(s3, axis=0), axis=0, keepdims=True))
    e3 = jnp.where(chosen, jnp.exp(s3 - m_new[None]), 0.0)
    alpha = jnp.exp(m_old - m_new)
    l_scr[...] = l_scr[...] * alpha + jnp.sum(jnp.sum(e3, axis=0), axis=0, keepdims=True)
    m_scr[...] = m_new
    acc_scr[...] = acc_scr[...] * alpha + _dot(vbuf[slot], e3.reshape(rows, ncol))

    @pl.when(c == nch - 1)
    def _():
        tcol = lax.broadcasted_iota(jnp.int32, (tdec, ncol), 1) % tdec
        jrow = lax.broadcasted_iota(jnp.int32, (tdec, ncol), 0)
        causal_new = jrow <= tcol
        sel_last = selc_ref[0, pl.ds(nch * nblk, 1), :] > 0.5
        ok_new = causal_new & sel_last
        s_new = jnp.where(ok_new, _dot(ksn_ref[0], qbd) * ATTN_SCALE, NEG_INF)
        m_old2 = m_scr[...]
        m_fin = jnp.maximum(m_old2, jnp.max(s_new, axis=0, keepdims=True))
        e_new = jnp.where(ok_new, jnp.exp(s_new - m_fin), 0.0)
        alpha2 = jnp.exp(m_old2 - m_fin)
        l_fin = l_scr[...] * alpha2 + jnp.sum(e_new, axis=0, keepdims=True)
        inv = 1.0 / jnp.maximum(l_fin, 1e-30)
        acc = acc_scr[...] * alpha2 + _dot_tn(vsn_ref[0], e_new)
        osel_ref[0] = acc * inv

        wb = wk_ref.shape[2]
        jw = lax.broadcasted_iota(jnp.int32, (wb, ncol), 0)
        tw = lax.broadcasted_iota(jnp.int32, (wb, ncol), 1) % tdec
        ok_c = jw + (WINDOW - wb) > tw
        s_c = jnp.where(ok_c, _dot_tn(wk_ref[0], qbd) * ATTN_SCALE, NEG_INF)
        s_n = jnp.where(causal_new, _dot(kwn_ref[0], qbd) * ATTN_SCALE, NEG_INF)
        m_w = jnp.maximum(jnp.max(s_c, axis=0, keepdims=True), jnp.max(s_n, axis=0, keepdims=True))
        e_c = jnp.where(ok_c, jnp.exp(s_c - m_w), 0.0)
        e_n = jnp.where(causal_new, jnp.exp(s_n - m_w), 0.0)
        inv_w = 1.0 / jnp.maximum(jnp.sum(e_c, axis=0, keepdims=True) + jnp.sum(e_n, axis=0, keepdims=True), 1e-30)
        ow = _dot(wv_ref[0], e_c) + _dot_tn(vwn_ref[0], e_n)
        owin_ref[0] = ow * inv_w


def _sample_attention(qrot, kpool, vpool, page_table, ks_new, vs_new, sel_t, wk, wv, kw_new, vw_new):
    bsz, tdec, _ = qrot.shape
    n_pages = page_table.shape[1]
    page = kpool.shape[2]
    ppc = min(n_pages, 32)
    assert n_pages % ppc == 0 and page % L_SEL == 0
    nch = n_pages // ppc
    rows = ppc * page
    ncol = KV_HEADS * Q_PER_KV * tdec
    ns_pad = sel_t.shape[2]
    wb = wk.shape[2]
    q5 = qrot.reshape(bsz, tdec, KV_HEADS, Q_PER_KV, HEAD_DIM).transpose(0, 2, 4, 3, 1)
    eye = jnp.eye(KV_HEADS, dtype=qrot.dtype)
    qbd = (q5[:, :, :, None] * eye[None, :, None, :, None, None]).reshape(bsz, KV_WIDTH, ncol)
    selc = jnp.broadcast_to(sel_t.transpose(0, 2, 1, 3)[:, :, :, None, :], (bsz, ns_pad, KV_HEADS, Q_PER_KV, tdec))
    selc = selc.reshape(bsz, ns_pad, ncol)
    per_b = lambda shape: pl.BlockSpec((1,) + shape, lambda b, c, pt: (b, 0, 0))
    grid_spec = pltpu.PrefetchScalarGridSpec(
        num_scalar_prefetch=1,
        grid=(bsz, nch),
        in_specs=[per_b((KV_WIDTH, ncol)),
                  pl.BlockSpec(memory_space=pl.ANY), pl.BlockSpec(memory_space=pl.ANY),
                  per_b((tdec, KV_WIDTH)), per_b((tdec, KV_WIDTH)),
                  per_b((ns_pad, ncol)),
                  per_b((KV_WIDTH, wb)), per_b((KV_WIDTH, wb)),
                  per_b((tdec, KV_WIDTH)), per_b((tdec, KV_WIDTH))],
        out_specs=[per_b((KV_WIDTH, ncol)), per_b((KV_WIDTH, ncol))],
        scratch_shapes=[pltpu.VMEM((2, KV_WIDTH, rows), F32), pltpu.VMEM((2, KV_WIDTH, rows), F32),
                        pltpu.VMEM((1, ncol), F32), pltpu.VMEM((1, ncol), F32), pltpu.VMEM((KV_WIDTH, ncol), F32),
                        pltpu.SemaphoreType.DMA((2, 2))],
    )
    o_sel, o_win = pl.pallas_call(
        functools.partial(_sattn_kernel, n_pages=n_pages, ppc=ppc, page=page, tdec=tdec),
        grid_spec=grid_spec,
        out_shape=[jax.ShapeDtypeStruct((bsz, KV_WIDTH, ncol), F32)] * 2,
        compiler_params=_cparams(("arbitrary", "arbitrary")),
        name="sample_attn",
    )(page_table, qbd, kpool, vpool, ks_new, vs_new, selc, wk, wv, kw_new, vw_new)

    def unpack(o):
        o6 = o.reshape(bsz, KV_HEADS, HEAD_DIM, KV_HEADS, Q_PER_KV, tdec)
        diag = jnp.stack([o6[:, g, :, g] for g in range(KV_HEADS)], axis=1)
        return diag.transpose(0, 4, 1, 3, 2).reshape(bsz, tdec, NSA_WIDTH)

    return unpack(o_sel), unpack(o_win)


def _mix_kernel(x_ref, oc_ref, os_ref, ow_ref, gt_ref, u_ref, v_ref, wsm_ref, bs_ref, gn_ref, gs_ref, eg_ref, wout_ref,
                o_ref, *, chunk):
    r = x_ref.shape[0]
    g = gt_ref[...]
    g_hi = g.astype(BF16)
    g_lo = (g - g_hi.astype(F32)).astype(BF16)
    onsa = None
    for j, branch in enumerate((oc_ref, os_ref, ow_ref)):
        ge = _dot(g_hi, eg_ref[j]) + _dot(g_lo, eg_ref[j])
        term = ge * branch[...]
        onsa = term if onsa is None else onsa + term
    onsa = _rms(onsa, gn_ref[...])
    ii = lax.broadcasted_iota(jnp.int32, (r, r), 0)
    jj = lax.broadcasted_iota(jnp.int32, (r, r), 1)
    tri = (ii // chunk == jj // chunk) & (jj % chunk <= ii % chunk)
    cols = []
    for gi in range(SGU_GROUPS):
        sl = slice(gi * SGU_GROUP_DIM, (gi + 1) * SGU_GROUP_DIM)
        ws = jnp.where(tri, wsm_ref[gi], 0.0)
        mixed = _dot(ws, v_ref[:, sl]) + bs_ref[:, gi:gi + 1]
        cols.append(u_ref[:, sl] * mixed)
    osgu = _rms(jnp.concatenate(cols, axis=1), gs_ref[...])
    o_ref[...] = x_ref[...] + _dot(jnp.concatenate([onsa, osgu], axis=1), wout_ref[...])


def _gate_expanders():
    c = jnp.arange(GATE_COLS)[None, :, None]
    lane = jnp.arange(NSA_WIDTH)[None, None, :]
    j = jnp.arange(3)[:, None, None]
    return (c == (lane // HEAD_DIM) * 3 + j).astype(BF16)


def _mix(x2d, o_cmp, o_sel, o_win, gates, u, v, w_sgu, b_sgu, g_nsa_out, g_sgu_out, w_out_b, chunk, r):
    n, d = x2d.shape
    rep = r // chunk
    pick = (jnp.arange(r)[:, None] % chunk == jnp.arange(chunk)[None, :]).astype(F32)
    wsm = jnp.einsum("ia,gab,jb->gij", pick, w_sgu[:, :chunk, :chunk], pick, precision=lax.Precision.HIGHEST)
    bs = jnp.tile(b_sgu[:, :chunk].T, (rep, 1))
    row = lambda w: pl.BlockSpec((r, w), lambda i: (i, 0))
    return pl.pallas_call(
        functools.partial(_mix_kernel, chunk=chunk),
        grid=(n // r,),
        in_specs=[row(d), row(NSA_WIDTH), row(NSA_WIDTH), row(NSA_WIDTH), row(GATE_COLS), row(1024), row(1024),
                  _const_spec(wsm.shape), _const_spec(bs.shape), _const_spec((1, NSA_WIDTH)), _const_spec((1, 1024)),
                  _const_spec((3, GATE_COLS, NSA_WIDTH)), _const_spec(w_out_b.shape)],
        out_specs=row(d),
        out_shape=jax.ShapeDtypeStruct((n, d), F32),
        compiler_params=_cparams(("parallel",)),
        name="mix_out_proj",
    )(x2d, o_cmp, o_sel, o_win, gates, u, v, wsm, bs, g_nsa_out, g_sgu_out, _gate_expanders(), w_out_b)


def _rms_matmul_kernel(x_ref, g_ref, w_ref, o_ref):
    o_ref[...] = _dot(_rms(x_ref[...], g_ref[...]), w_ref[...])


def _rms_matmul(x2d, g, w_b, tm):
    n, d = x2d.shape
    m = w_b.shape[1]
    return pl.pallas_call(
        _rms_matmul_kernel,
        grid=(n // tm,),
        in_specs=[pl.BlockSpec((tm, d), lambda i: (i, 0)), _const_spec((1, d)), _const_spec(w_b.shape)],
        out_specs=pl.BlockSpec((tm, m), lambda i: (i, 0)),
        out_shape=jax.ShapeDtypeStruct((n, m), F32),
        compiler_params=_cparams(("parallel",)),
        name="rms_matmul",
    )(x2d, g, w_b)


def _mem_heads(q, k_ref, v_ref):
    outs = []
    for h in range(MEM_HEADS):
        sl = slice(h * MEM_HEAD_DIM, (h + 1) * MEM_HEAD_DIM)
        s = _dot_nt(q[:, sl], k_ref[0, :, h, :]) * MEM_SCALE
        p = _masked_softmax(s, jnp.ones(s.shape, dtype=jnp.bool_))
        outs.append(_dot(p, v_ref[0, :, h, :]))
    return jnp.concatenate(outs, axis=1)


def _memattn_kernel(q_ref, k_ref, v_ref, o_ref):
    o_ref[0] = _mem_heads(q_ref[0], k_ref, v_ref)


def _mem_block_kernel(x_ref, g_ref, wq_ref, k_ref, v_ref, wo_ref, gm_ref, wr_ref, o_ref, gtop_ref):
    x = x_ref[0]
    hq = _dot(_rms(x, g_ref[...]), wq_ref[...])
    x2 = x + _dot(_mem_heads(hq, k_ref, v_ref), wo_ref[...])
    o_ref[0] = x2
    gtop_ref[0] = _top_group(x2, gm_ref, wr_ref)


def _mem_block(x3, g_mem, w_q, mk, mv, w_o, g_moe, wr):
    bsz, t, d = x3.shape
    m = mk.shape[1]
    tq = min(t, 512)
    kv_spec = pl.BlockSpec((1, m, MEM_HEADS, MEM_HEAD_DIM), lambda b, i: (b, 0, 0, 0))
    return pl.pallas_call(
        _mem_block_kernel,
        grid=(bsz, t // tq),
        in_specs=[pl.BlockSpec((1, tq, d), lambda b, i: (b, i, 0)), _const_spec((1, d)), _const_spec(w_q.shape),
                  kv_spec, kv_spec, _const_spec(w_o.shape), _const_spec((1, d)), _const_spec(wr.shape)],
        out_specs=[pl.BlockSpec((1, tq, d), lambda b, i: (b, i, 0)), pl.BlockSpec((1, tq, 1), lambda b, i: (b, i, 0))],
        out_shape=[jax.ShapeDtypeStruct((bsz, t, d), F32), jax.ShapeDtypeStruct((bsz, t, 1), jnp.int32)],
        compiler_params=_cparams(("parallel", "parallel")),
        name="mem_block",
    )(x3, g_mem, w_q, mk, mv, w_o, g_moe, wr)


def _mem_attention(hq, mk, mv):
    bsz, t, w = hq.shape
    m = mk.shape[1]
    tq = min(t, 512)
    return pl.pallas_call(
        _memattn_kernel,
        grid=(bsz, t // tq),
        in_specs=[pl.BlockSpec((1, tq, w), lambda b, i: (b, i, 0)),
                  pl.BlockSpec((1, m, MEM_HEADS, MEM_HEAD_DIM), lambda b, i: (b, 0, 0, 0)),
                  pl.BlockSpec((1, m, MEM_HEADS, MEM_HEAD_DIM), lambda b, i: (b, 0, 0, 0))],
        out_specs=pl.BlockSpec((1, tq, w), lambda b, i: (b, i, 0)),
        out_shape=jax.ShapeDtypeStruct((bsz, t, w), F32),
        compiler_params=_cparams(("parallel", "parallel")),
        name="mem_attn",
    )(hq, mk, mv)


def _router_logits(h, wr_ref):
    h_hi = h.astype(BF16)
    h_lo = (h - h_hi.astype(F32)).astype(BF16)
    w_hi, w_lo = wr_ref[0], wr_ref[1]
    return _dot(h_hi, w_hi) + _dot(h_hi, w_lo) + _dot(h_lo, w_hi)


def _top_group(x, g_ref, wr_ref):
    z = _router_logits(_rms(x, g_ref[...]), wr_ref)
    lane = lax.broadcasted_iota(jnp.int32, z.shape, 1)
    zg = jnp.where(lane < N_GROUPS, z, -jnp.inf)
    m = jnp.max(zg, axis=1, keepdims=True)
    first = jnp.min(jnp.where(zg == m, lane.astype(F32), float(LANES)), axis=1, keepdims=True)
    return first.astype(jnp.int32)


def _matmul_res_route_kernel(a_ref, w_ref, r_ref, g_ref, wr_ref, o_ref, gtop_ref):
    x = r_ref[...] + _dot(a_ref[...], w_ref[...])
    o_ref[...] = x
    gtop_ref[...] = _top_group(x, g_ref, wr_ref)


def _matmul_res_route(a2d, w_b, res, g_moe, wr, tm):
    n, k = a2d.shape
    m = w_b.shape[1]
    return pl.pallas_call(
        _matmul_res_route_kernel,
        grid=(n // tm,),
        in_specs=[pl.BlockSpec((tm, k), lambda i: (i, 0)), _const_spec(w_b.shape), pl.BlockSpec((tm, m), lambda i: (i, 0)),
                  _const_spec((1, m)), _const_spec(wr.shape)],
        out_specs=[pl.BlockSpec((tm, m), lambda i: (i, 0)), pl.BlockSpec((tm, 1), lambda i: (i, 0))],
        out_shape=[jax.ShapeDtypeStruct((n, m), F32), jax.ShapeDtypeStruct((n, 1), jnp.int32)],
        compiler_params=_cparams(("parallel",)),
        name="matmul_residual_route",
    )(a2d, w_b, res, g_moe, wr)


def _moe_kernel(src_ref, tg_ref, tv_ref, nv_ref, widx_ref, x_hbm, gm_ref, gf_ref, wr_ref, wg_ref, wu_ref, wd_ref, y_hbm,
                buf, hb_scr, w4_scr, gsem, ssem, *, tm):
    t = pl.program_id(0)
    e = pl.program_id(1)
    nt = pl.num_programs(0)
    slot = t % 2
    other = 1 - slot
    valid = tv_ref[t] == 1
    prev_valid = (t >= 1) & (tv_ref[jnp.maximum(t - 1, 0)] == 1)
    next_valid = (t + 1 < nt) & (tv_ref[jnp.minimum(t + 1, nt - 1)] == 1)

    def gather_start(tt, sl):
        def body(r, c):
            idx = jnp.maximum(src_ref[tt * tm + r], 0)
            pltpu.make_async_copy(x_hbm.at[pl.ds(idx, 1), :], buf.at[sl, pl.ds(r, 1), :], gsem.at[sl]).start()
            return c
        lax.fori_loop(0, tm, body, 0, unroll=8)

    def gather_wait(sl):
        pltpu.make_async_copy(x_hbm.at[pl.ds(0, tm), :], buf.at[sl], gsem.at[sl]).wait()

    def scatter_copy(tt, sl, r):
        return pltpu.make_async_copy(buf.at[sl, pl.ds(r, 1), :], y_hbm.at[pl.ds(src_ref[tt * tm + r], 1), :], ssem.at[0])

    def scatter_start(tt, sl):
        def body(r, c):
            scatter_copy(tt, sl, r).start()
            return c
        lax.fori_loop(0, nv_ref[tt], body, 0)

    def scatter_wait(tt, sl):
        def body(r, c):
            scatter_copy(tt, sl, r).wait()
            return c
        lax.fori_loop(0, nv_ref[tt], body, 0)

    @pl.when(e == 0)
    def _():
        @pl.when(t == 0)
        def _():
            gather_start(0, 0)

        @pl.when(prev_valid)
        def _():
            scatter_start(t - 1, other)

        @pl.when(valid)
        def _():
            gather_wait(slot)
            h = _rms(buf[slot], gm_ref[...])
            hb_scr[...] = h.astype(BF16)
            z = _router_logits(h, wr_ref)
            lane = lax.broadcasted_iota(jnp.int32, z.shape, 1)
            lanef = lane.astype(F32)
            grp = lane < N_GROUPS
            zg = jnp.where(grp, z, -jnp.inf)
            pg_top = 1.0 / jnp.sum(jnp.where(grp, jnp.exp(zg - jnp.max(zg, axis=1, keepdims=True)), 0.0), axis=1, keepdims=True)
            lo = N_GROUPS + tg_ref[t] * EXPERTS_PER_GROUP
            ing = (lane >= lo) & (lane < lo + EXPERTS_PER_GROUP)
            pf = _masked_softmax(z, ing)
            big = float(2 * LANES)
            m1 = jnp.max(jnp.where(ing, pf, -1.0), axis=1, keepdims=True)
            i1 = jnp.min(jnp.where(ing & (pf == m1), lanef, big), axis=1, keepdims=True)
            rest = ing & (lanef != i1)
            m2 = jnp.max(jnp.where(rest, pf, -1.0), axis=1, keepdims=True)
            i2 = jnp.min(jnp.where(rest & (pf == m2), lanef, big), axis=1, keepdims=True)
            tot = m1 + m2
            w4_scr[...] = jnp.where(lanef == i1, m1 / tot * pg_top, jnp.where(lanef == i2, m2 / tot * pg_top, 0.0))

    @pl.when(e == EXPERTS_PER_GROUP // 2)
    def _():
        @pl.when(prev_valid)
        def _():
            scatter_wait(t - 1, other)

        @pl.when(next_valid)
        def _():
            gather_start(t + 1, other)

    @pl.when(valid)
    def _():
        hb = hb_scr[...]
        lane = lax.broadcasted_iota(jnp.int32, w4_scr.shape, 1)
        col = N_GROUPS + tg_ref[t] * EXPERTS_PER_GROUP + e
        we = jnp.sum(jnp.where(lane == col, w4_scr[...], 0.0), axis=1, keepdims=True)
        act = jax.nn.silu(_dot(hb, wg_ref[0])) * _dot(hb, wu_ref[0])
        buf[slot] += _dot(act * we, wd_ref[0])

    @pl.when(valid & (e == EXPERTS_PER_GROUP - 1))
    def _():
        buf[slot] = _rms(buf[slot], gf_ref[...])


def _moe_final(x2d, g_top, g_moe, g_final, wr, w_gate, w_up, w_down, tm):
    n, d = x2d.shape
    n_tiles = n // tm + N_GROUPS
    onehot = (g_top[:, None] == jnp.arange(N_GROUPS)[None, :]).astype(jnp.int32)
    counts = jnp.sum(onehot, axis=0)
    rank = jnp.sum((jnp.cumsum(onehot, axis=0) - onehot) * onehot, axis=1)
    padded = (counts + tm - 1) // tm * tm
    ends = jnp.cumsum(padded)
    base = ends - padded
    pos = base[g_top] + rank
    src = jnp.full((n_tiles * tm,), -1, jnp.int32).at[pos].set(jnp.arange(n, dtype=jnp.int32))
    tile_start = jnp.arange(n_tiles, dtype=jnp.int32) * tm
    tile_valid = (tile_start < ends[-1]).astype(jnp.int32)
    tile_group = jnp.minimum(jnp.sum((tile_start[:, None] >= ends[None, :]).astype(jnp.int32), axis=1), N_GROUPS - 1)
    tile_rows = jnp.clip(ends[tile_group] - padded[tile_group] + counts[tile_group] - tile_start, 0, tm)
    tile_rows = (tile_rows * tile_valid).astype(jnp.int32)
    n_valid = ends[-1] // tm
    last_group = tile_group[jnp.maximum(n_valid - 1, 0)]
    eidx = tile_group[:, None] * EXPERTS_PER_GROUP + jnp.arange(EXPERTS_PER_GROUP, dtype=jnp.int32)[None, :]
    widx = jnp.where(tile_valid[:, None] == 1, eidx, last_group * EXPERTS_PER_GROUP + EXPERTS_PER_GROUP - 1)
    widx = widx.reshape(-1).astype(jnp.int32)

    wmap = lambda t, e, src, tg, tv, nv, wi: (wi[t * EXPERTS_PER_GROUP + e], 0, 0)
    cmap = lambda t, e, src, tg, tv, nv, wi: (0, 0)
    grid_spec = pltpu.PrefetchScalarGridSpec(
        num_scalar_prefetch=5,
        grid=(n_tiles, EXPERTS_PER_GROUP),
        in_specs=[pl.BlockSpec(memory_space=pl.ANY),
                  pl.BlockSpec((1, d), cmap), pl.BlockSpec((1, d), cmap),
                  pl.BlockSpec(wr.shape, lambda t, e, src, tg, tv, nv, wi: (0, 0, 0)),
                  pl.BlockSpec((1, d, EXPERT_FF), wmap), pl.BlockSpec((1, d, EXPERT_FF), wmap),
                  pl.BlockSpec((1, EXPERT_FF, d), wmap)],
        out_specs=pl.BlockSpec(memory_space=pl.ANY),
        scratch_shapes=[pltpu.VMEM((2, tm, d), F32), pltpu.VMEM((tm, d), BF16), pltpu.VMEM((tm, LANES), F32),
                        pltpu.SemaphoreType.DMA((2,)), pltpu.SemaphoreType.DMA((1,))],
    )
    return pl.pallas_call(
        functools.partial(_moe_kernel, tm=tm),
        grid_spec=grid_spec,
        out_shape=jax.ShapeDtypeStruct((n, d), F32),
        compiler_params=_cparams(("arbitrary", "arbitrary")),
        name="moe_final_norm",
    )(src, tile_group, tile_valid, tile_rows, widx, x2d, g_moe, g_final, wr, w_gate, w_up, w_down)


def _finish(x2d, o_cmp, o_sel, o_win, gates, u, v, mk, mv, bsz, lw, chunk, moe_tm):
    n, d = x2d.shape
    t = n // bsz
    x1 = _mix(x2d, o_cmp, o_sel, o_win, gates, u, v, lw["w_sgu"], lw["b_sgu"], lw["g_nsa_out"], lw["g_sgu_out"],
              lw["w_out"], chunk, min(n, 256))
    if t >= LANES:
        x2, g_top = _mem_block(x1.reshape(bsz, t, d), lw["g_mem_norm"], lw["w_mem_q"], mk, mv, lw["w_mem_o"],
                               lw["g_moe_norm"], lw["w_router"])
        x2, g_top = x2.reshape(n, d), g_top.reshape(n, 1)
    else:
        hq = _rms_matmul(x1, lw["g_mem_norm"], lw["w_mem_q"], min(n, 512))
        o_m = _mem_attention(hq.reshape(bsz, t, -1), mk, mv)
        x2, g_top = _matmul_res_route(o_m.reshape(n, -1), lw["w_mem_o"], x1, lw["g_moe_norm"], lw["w_router"], min(n, 512))
    return _moe_final(x2, g_top[:, 0], lw["g_moe_norm"], lw["g_final"], lw["w_router"], lw["w_exp_gate"], lw["w_exp_up"],
                      lw["w_exp_down"], moe_tm)


def kernel(x_prompt, x_sample, cache_cmp_k, cache_cmp_v, cache_sel_k, cache_sel_v, cache_win_k, cache_win_v, cache_mem_k, cache_mem_v, page_table, mem_prompt, w_in, g_attn_norm, pe_cmp_k, w_cmp_k1, w_cmp_k2, pe_cmp_v, w_cmp_v1, w_cmp_v2, g_sgu_v, w_sgu, b_sgu, g_nsa_out, g_sgu_out, w_out, g_mem_norm, g_mem_src, w_mem_q, w_mem_k, w_mem_v, w_mem_o, g_moe_norm, w_router_group, w_router_expert, w_exp_gate, w_exp_up, w_exp_down, g_final):
    depth = w_in.shape[0]
    assert depth == 1, "single-layer trunk"
    bp, tp, d = x_prompt.shape
    bs, ts, _ = x_sample.shape
    n_pages = page_table.shape[1]
    page = cache_cmp_k.shape[2]
    past = n_pages * page
    assert ts < STRIDE and tp % 256 == 0 and page % STRIDE == 0
    row = lambda a: a[0].reshape(1, -1)

    w_in_p = w_in[0].astype(BF16)
    wr = jnp.concatenate([w_router_group[0], w_router_expert[0],
                          jnp.zeros((d, LANES - N_GROUPS - N_EXPERTS), F32)], axis=1)
    wr_hi = wr.astype(BF16)
    wr = jnp.stack([wr_hi, (wr - wr_hi.astype(F32)).astype(BF16)])
    lw = {
        "w_sgu": w_sgu[0], "b_sgu": b_sgu[0], "g_nsa_out": row(g_nsa_out), "g_sgu_out": row(g_sgu_out),
        "w_out": w_out[0].astype(BF16), "g_mem_norm": row(g_mem_norm), "w_mem_q": w_mem_q[0].astype(BF16),
        "w_mem_o": w_mem_o[0].astype(BF16), "g_moe_norm": row(g_moe_norm), "g_final": g_final.reshape(1, -1),
        "w_router": wr, "w_exp_gate": w_exp_gate[0], "w_exp_up": w_exp_up[0], "w_exp_down": w_exp_down[0],
    }
    g_attn = row(g_attn_norm)
    gsv = row(g_sgu_v)
    kv5 = lambda a, b, t: a.reshape(1, b, t, KV_HEADS, HEAD_DIM)

    np_ = bp * tp
    tabs_p = _rope_tables(jnp.arange(tp, dtype=jnp.int32))
    (q, qrot, kc, vc, ks, vs, kw, vw, gates, u, v) = _project(x_prompt.reshape(np_, d), g_attn, w_in_p, tabs_p, gsv, 256,
                                                              kv_seq=tp)
    pt_p = jnp.arange(np_ // page, dtype=jnp.int32).reshape(bp, tp // page)
    kcmp = _compress(kc, pt_p, page, pe_cmp_k[0], w_cmp_k1[0], w_cmp_k2[0])
    vcmp = _compress(vc, pt_p, page, pe_cmp_v[0], w_cmp_v1[0], w_cmp_v2[0])
    o_cmp, sel_t = _cmp_select(q.reshape(bp, tp, -1), kcmp, vcmp, 0, tp)
    qrot3 = qrot.reshape(bp, tp, -1)
    o_sel = _prompt_attention(qrot3, ks, vs, sel_t)
    o_win = _prompt_attention(qrot3, kw, vw)
    n_mem = mem_prompt.shape[1]
    mem_w = MEM_HEADS * MEM_HEAD_DIM
    mem2d = mem_prompt.reshape(bp * n_mem, d)
    mem4 = lambda a: a.reshape(bp, n_mem, MEM_HEADS, MEM_HEAD_DIM)
    mk_p = mem4(_rms_matmul(mem2d, row(g_mem_src), w_mem_k[0].astype(BF16), min(bp * n_mem, 512)))
    mv_p = mem4(_rms_matmul(mem2d, row(g_mem_src), w_mem_v[0].astype(BF16), min(bp * n_mem, 512)))
    y_p = _finish(x_prompt.reshape(np_, d), o_cmp.reshape(np_, -1), o_sel.reshape(np_, -1), o_win.reshape(np_, -1),
                  gates, u, v, mk_p, mv_p, bp, lw, CHUNK, min(np_, 512))
    wbp = min(WINDOW, tp)
    kv5t = lambda a: a.reshape(bp, KV_HEADS, HEAD_DIM, a.shape[2]).transpose(0, 3, 1, 2)[None]
    outs_p = (kv5t(kc), kv5t(vc), kv5t(ks), kv5t(vs), kv5t(kw[:, :, -wbp:]), kv5t(vw[:, :, -wbp:]), mk_p[None], mv_p[None])

    ns_ = bs * ts
    pos_s = past + jnp.arange(ts, dtype=jnp.int32)
    tabs_s = tuple(jnp.tile(a, (bs, 1)) for a in _rope_tables(pos_s))
    (q, qrot, kc, vc, ks, vs, kw, vw, gates, u, v) = _project(x_sample.reshape(ns_, d), g_attn, w_in_p, tabs_s, gsv, ns_)
    pool = lambda c: c[0].transpose(0, 2, 3, 1).reshape(c.shape[1], KV_WIDTH, page)
    kcmp = _compress(pool(cache_cmp_k), page_table, page, pe_cmp_k[0], w_cmp_k1[0], w_cmp_k2[0])
    vcmp = _compress(pool(cache_cmp_v), page_table, page, pe_cmp_v[0], w_cmp_v1[0], w_cmp_v2[0])
    o_cmp, sel_t = _cmp_select(q.reshape(bs, ts, -1).astype(F32), kcmp, vcmp, past, past + ts)
    wb = cache_win_k.shape[2]
    wk = cache_win_k[0].transpose(0, 2, 3, 1).reshape(bs, KV_WIDTH, wb)
    wv = cache_win_v[0].transpose(0, 2, 3, 1).reshape(bs, KV_WIDTH, wb)
    ks3, vs3, kw3, vw3 = (a.reshape(bs, ts, KV_WIDTH) for a in (ks, vs, kw, vw))
    o_sel, o_win = _sample_attention(qrot.reshape(bs, ts, -1), pool(cache_sel_k), pool(cache_sel_v), page_table,
                                     ks3, vs3, sel_t, wk, wv, kw3, vw3)
    mem_ks, mem_vs = cache_mem_k[0], cache_mem_v[0]
    y_s = _finish(x_sample.reshape(ns_, d), o_cmp.reshape(ns_, -1), o_sel.reshape(ns_, -1), o_win.reshape(ns_, -1),
                  gates, u, v, mem_ks, mem_vs, bs, lw, ts, min(ns_, 128))
    def slide(cache_t, new):
        win = jnp.concatenate([cache_t, new.transpose(0, 2, 1)], axis=2)[:, :, -wb:]
        return win.reshape(bs, KV_HEADS, HEAD_DIM, wb).transpose(0, 3, 1, 2)[None]

    win_k_s = slide(wk, kw3)
    win_v_s = slide(wv, vw3)
    outs_s = (kv5(kc, bs, ts), kv5(vc, bs, ts), kv5(ks, bs, ts), kv5(vs, bs, ts), win_k_s, win_v_s,
              v.reshape(1, bs, ts, -1))

    return (y_p.reshape(bp, tp, d), y_s.reshape(bs, ts, d)) + outs_p + outs_s
```

```python
import functools

import jax
import jax.numpy as jnp
from jax import lax
from jax.experimental import pallas as pl
from jax.experimental.pallas import tpu as pltpu

F32 = jnp.float32
BF16 = jnp.bfloat16

N_HEADS = 16
HEAD_DIM = 64
KV_HEADS = 4
Q_PER_KV = N_HEADS // KV_HEADS
NSA_WIDTH = N_HEADS * HEAD_DIM
KV_WIDTH = KV_HEADS * HEAD_DIM
ROT_DIM = HEAD_DIM // 4
ROPE_THETA = 500000.0
L_CMP = 32
STRIDE = 16
CMP_R = L_CMP // STRIDE
CMP_HIDDEN = 128
L_SEL = 64
SEL_RATIO = L_SEL // STRIDE
SEL_INNER = (L_SEL - L_CMP) // STRIDE + 1
TOP_N = 16
WINDOW = 512
SGU_GROUPS = 8
SGU_GROUP_DIM = 128
CHUNK = 128
MEM_HEADS = 4
MEM_HEAD_DIM = 128
N_GROUPS = 4
EXPERTS_PER_GROUP = 4
N_EXPERTS = N_GROUPS * EXPERTS_PER_GROUP
EXPERT_FF = 512
EPS = 1e-6
NEG_INF = -1e30
BIG = 1e9
PAD_SCORE = -3e38
CMP_ROW_PITCH = STRIDE + 4
SOFTMAX_STRIP = 128
ATTN_SCALE = HEAD_DIM ** -0.5
MEM_SCALE = MEM_HEAD_DIM ** -0.5

LANES = 128
SUBLANES = 8
V7X_VMEM_LIMIT_BYTES = 60000 * 1024

GATE_COLS = LANES
N_GATES = 3 * N_HEADS
_SEG = {}
_off = 0
for _name, _w in (("q", NSA_WIDTH), ("kc", KV_WIDTH), ("vc", KV_WIDTH), ("ks", KV_WIDTH), ("vs", KV_WIDTH),
                  ("kw", KV_WIDTH), ("vw", KV_WIDTH), ("gt", N_GATES), ("u", 1024), ("v", 1024)):
    _SEG[_name] = (_off, _w)
    _off += _w
IN_COLS = _off


def _cparams(sem, vmem=V7X_VMEM_LIMIT_BYTES):
    return pltpu.CompilerParams(dimension_semantics=sem, vmem_limit_bytes=vmem)


def _dot(a, b):
    return jnp.dot(a.astype(BF16), b.astype(BF16), preferred_element_type=F32)


def _dot_nt(a, b):
    return lax.dot_general(a.astype(BF16), b.astype(BF16), (((1,), (1,)), ((), ())), preferred_element_type=F32)


def _dot_tn(a, b):
    return lax.dot_general(a.astype(BF16), b.astype(BF16), (((0,), (0,)), ((), ())), preferred_element_type=F32)


def _rms(x, g):
    return x * lax.rsqrt(jnp.mean(x * x, axis=-1, keepdims=True) + EPS) * g


def _masked_softmax(s, mask, axis=-1):
    s = jnp.where(mask, s, NEG_INF)
    m = jnp.max(s, axis=axis, keepdims=True)
    e = jnp.where(mask, jnp.exp(s - m), 0.0)
    return e * (1.0 / jnp.maximum(jnp.sum(e, axis=axis, keepdims=True), 1e-30))


def _const_spec(shape):
    nd = len(shape)
    return pl.BlockSpec(shape, lambda *_: (0,) * nd, pipeline_mode=pl.Buffered(1))


def _proj_kernel(x_ref, g_ref, w_ref, rc_ref, rs1_ref, rs2_ref, gsv_ref,
                 q_ref, qrot_ref, kc_ref, vc_ref, ks_ref, vs_ref, kw_ref, vw_ref, gates_ref, u_ref, v_ref, *, transpose_kv):
    hb = _rms(x_ref[...], g_ref[...]).astype(BF16)

    def put_kv(ref, val):
        if transpose_kv:
            ref[0] = val.T
        else:
            ref[...] = val

    def seg(name):
        lo, width = _SEG[name]
        return jnp.dot(hb, w_ref[:, lo:lo + width], preferred_element_type=F32)

    def rope(z):
        rc, rs1, rs2 = rc_ref[...], rs1_ref[...], rs2_ref[...]
        half = ROT_DIM // 2
        outs = []
        for c in range(z.shape[1] // LANES):
            zc = z[:, c * LANES:(c + 1) * LANES]
            outs.append(zc * rc + pltpu.roll(zc, LANES - half, 1) * rs1 + pltpu.roll(zc, half, 1) * rs2)
        return jnp.concatenate(outs, axis=1)

    q = seg("q")
    q_ref[...] = q.astype(BF16)
    qrot_ref[...] = rope(q).astype(BF16)
    put_kv(kc_ref, seg("kc"))
    put_kv(vc_ref, seg("vc"))
    put_kv(ks_ref, rope(seg("ks")))
    put_kv(vs_ref, seg("vs"))
    put_kv(kw_ref, rope(seg("kw")))
    put_kv(vw_ref, seg("vw"))
    g0 = _SEG["gt"][0]
    zt = jnp.dot(hb, w_ref[:, g0:IN_COLS], preferred_element_type=F32)
    u0, v0 = _SEG["u"][0] - g0, _SEG["v"][0] - g0
    gates_ref[...] = jax.nn.sigmoid(zt[:, :GATE_COLS])
    u_ref[...] = jax.nn.gelu(zt[:, u0:u0 + 1024])
    v_ref[...] = _rms(jax.nn.gelu(zt[:, v0:v0 + 1024]), gsv_ref[...])


def _project(x2d, g_attn, w_in_p, tables, g_sgu_v, tm, kv_seq=None):
    n, d = x2d.shape
    rc, rs1, rs2 = tables
    tt = rc.shape[0]
    nt = tt // tm
    row = lambda w: pl.BlockSpec((tm, w), lambda i: (i, 0))
    tab = pl.BlockSpec((tm, LANES), lambda i: (i % nt, 0))
    out_shapes = [jax.ShapeDtypeStruct((n, NSA_WIDTH), BF16), jax.ShapeDtypeStruct((n, NSA_WIDTH), BF16)]
    if kv_seq is None:
        out_shapes += [jax.ShapeDtypeStruct((n, KV_WIDTH), F32)] * 6
        kv_spec = row(KV_WIDTH)
    else:
        per = kv_seq // tm
        out_shapes += [jax.ShapeDtypeStruct((n // kv_seq, KV_WIDTH, kv_seq), F32)] * 6
        kv_spec = pl.BlockSpec((1, KV_WIDTH, tm), lambda i: (i // per, 0, i % per))
    out_shapes += [jax.ShapeDtypeStruct((n, GATE_COLS), F32), jax.ShapeDtypeStruct((n, 1024), F32),
                   jax.ShapeDtypeStruct((n, 1024), F32)]
    out_specs = [row(NSA_WIDTH), row(NSA_WIDTH)] + [kv_spec] * 6 + [row(GATE_COLS), row(1024), row(1024)]
    return pl.pallas_call(
        functools.partial(_proj_kernel, transpose_kv=kv_seq is not None),
        grid=(n // tm,),
        in_specs=[row(d), _const_spec((1, d)), _const_spec(w_in_p.shape), tab, tab, tab, _const_spec((1, 1024))],
        out_specs=out_specs,
        out_shape=out_shapes,
        compiler_params=_cparams(("parallel",)),
        name="in_proj",
    )(x2d, g_attn, w_in_p, rc, rs1, rs2, g_sgu_v)


def _rope_tables(pos):
    half = ROT_DIM // 2
    freqs = ROPE_THETA ** (-jnp.arange(half, dtype=F32) / half)
    ang = pos.astype(F32)[:, None] * freqs[None, :]
    cos, sin = jnp.cos(ang), jnp.sin(ang)
    t = pos.shape[0]
    ones = jnp.ones((t, HEAD_DIM - ROT_DIM), F32)
    zeros = jnp.zeros((t, HEAD_DIM - ROT_DIM), F32)
    zh = jnp.zeros((t, half), F32)
    rc = jnp.concatenate([cos, cos, ones], axis=1)
    rs1 = jnp.concatenate([-sin, zh, zeros], axis=1)
    rs2 = jnp.concatenate([zh, sin, zeros], axis=1)
    rep = LANES // HEAD_DIM
    return tuple(jnp.tile(a, (1, rep)) for a in (rc, rs1, rs2))


def _compress_kernel(pt_ref, pool_ref, pe_ref, w1_ref, w2_ref, out_ref, stage, buf, a_ref, sem, *, n_pages, page):
    b = pl.program_id(0)
    nb = pl.num_programs(0)
    n_rows = n_pages * page
    n_sub = n_rows // STRIDE
    slot = b % 2
    n_ct = KV_WIDTH // LANES
    ppl = pool_ref.shape[2] // page

    def page_copy(bb, sl, p):
        q = pt_ref[bb, p]
        src = pool_ref.at[q // ppl, :, pl.ds(pl.multiple_of((q % ppl) * page, page), page)]
        return pltpu.make_async_copy(src, stage.at[sl, :, pl.ds(pl.multiple_of(p * page, page), page)], sem.at[sl])

    def start_all(bb, sl):
        def body(p, carry):
            page_copy(bb, sl, p).start()
            return carry
        lax.fori_loop(0, n_pages, body, 0)

    def wait_all(bb, sl):
        def body(p, carry):
            page_copy(bb, sl, p).wait()
            return carry
        lax.fori_loop(0, n_pages, body, 0)

    @pl.when(b == 0)
    def _():
        start_all(0, 0)

    @pl.when(b + 1 < nb)
    def _():
        start_all(b + 1, 1 - slot)

    wait_all(b, slot)

    tch = min(n_rows, 512)
    for c in range(n_ct):
        for j in range(n_rows // tch):
            rows_t = stage[slot, c * LANES:(c + 1) * LANES, j * tch:(j + 1) * tch].T
            for n in range(tch // STRIDE):
                r0 = (j * (tch // STRIDE) + n) * CMP_ROW_PITCH
                buf[c, r0:r0 + STRIDE, :] = rows_t[n * STRIDE:(n + 1) * STRIDE, :]

    pev = _dot(pe_ref[...], w1_ref[...])
    low = lax.broadcasted_iota(jnp.int32, (n_sub, LANES), 1) < HEAD_DIM
    row = lax.broadcasted_iota(jnp.int32, (n_sub, 4 * HEAD_DIM), 0)
    for c in range(n_ct):
        src = buf.at[c]
        for sp in range(STRIDE // 2):
            x0 = src[pl.ds(2 * sp, n_sub, stride=CMP_ROW_PITCH), :]
            x1 = src[pl.ds(2 * sp + 1, n_sub, stride=CMP_ROW_PITCH), :]
            a_ref[0, :, sp * LANES:(sp + 1) * LANES] = jnp.where(low, x0, pltpu.roll(x1, HEAD_DIM, 1)).astype(BF16)
            a_ref[1, :, sp * LANES:(sp + 1) * LANES] = jnp.where(low, pltpu.roll(x0, HEAD_DIM, 1), x1).astype(BF16)
        for k in range(2):
            pm = jnp.dot(a_ref[k], w1_ref[...], preferred_element_type=F32)
            part0 = pm[:, :CMP_HIDDEN] + pev[0:1, :CMP_HIDDEN]
            part1 = pm[:, CMP_HIDDEN:] + pev[1:2, CMP_HIDDEN:]
            hsum = part0 + pltpu.roll(part1, n_sub - 1, 0)
            o = _dot(jax.nn.silu(hsum), w2_ref[...])
            out_ref[0, 2 * c + k] = jnp.where(row < n_sub - 1, o, 0.0).astype(BF16)


def _compress(pool, page_table, page, pe, w1, w2):
    bsz, n_pages = page_table.shape
    n_rows = n_pages * page
    n_sub = n_rows // STRIDE
    n_ct = KV_WIDTH // LANES
    kdim = STRIDE * HEAD_DIM
    pe8 = jnp.zeros((SUBLANES, kdim), F32).at[:CMP_R].set(pe.reshape(CMP_R, kdim))
    w1c = w1.reshape(CMP_R, kdim, CMP_HIDDEN).transpose(1, 0, 2).reshape(kdim, CMP_R * CMP_HIDDEN).astype(BF16)
    w2t = jnp.tile(w2, (1, 4)).astype(BF16)
    grid_spec = pltpu.PrefetchScalarGridSpec(
        num_scalar_prefetch=1,
        grid=(bsz,),
        in_specs=[pl.BlockSpec(memory_space=pl.ANY),
                  pl.BlockSpec((SUBLANES, kdim), lambda b, pt: (0, 0)),
                  pl.BlockSpec(w1c.shape, lambda b, pt: (0, 0)),
                  pl.BlockSpec(w2t.shape, lambda b, pt: (0, 0))],
        out_specs=pl.BlockSpec((1, KV_HEADS, n_sub, 4 * HEAD_DIM), lambda b, pt: (b, 0, 0, 0)),
        scratch_shapes=[pltpu.VMEM((2, KV_WIDTH, n_rows), F32), pltpu.VMEM((n_ct, n_sub * CMP_ROW_PITCH, LANES), F32),
                        pltpu.VMEM((2, n_sub, kdim), BF16), pltpu.SemaphoreType.DMA((2,))],
    )
    return pl.pallas_call(
        functools.partial(_compress_kernel, n_pages=n_pages, page=page),
        grid_spec=grid_spec,
        out_shape=jax.ShapeDtypeStruct((bsz, KV_HEADS, n_sub, 4 * HEAD_DIM), BF16),
        compiler_params=_cparams(("arbitrary",)),
        name="compress",
    )(page_table, pool, pe8, w1c, w2t)


def _cmpsel_kernel(q_ref, k_ref, v_ref, st_ref, o_ref, sc_ref, *, pos0, nc, ns, tq):
    i = pl.program_id(2)
    n_sub = k_ref.shape[2]
    ns_pad = st_ref.shape[0]
    gw = Q_PER_KV * HEAD_DIM
    lane_head = lax.broadcasted_iota(jnp.int32, (tq, gw), 1) // HEAD_DIM
    qpos = pos0 + i * tq + lax.broadcasted_iota(jnp.int32, (tq, 1), 0)
    kidx = lax.broadcasted_iota(jnp.int32, (1, n_sub), 1)
    mask = (kidx * STRIDE + (L_CMP - 1) <= qpos) & (kidx < nc)
    mask_rows = jnp.concatenate([mask] * Q_PER_KV, axis=0)
    blk = lax.broadcasted_iota(jnp.int32, (ns_pad, tq), 0)
    qpos_t = pos0 + i * tq + lax.broadcasted_iota(jnp.int32, (ns_pad, tq), 1)
    cur = qpos_t // L_SEL
    forced = (blk == 0) | (blk == cur) | (blk == cur - 1)
    future = blk * L_SEL > qpos_t
    st = st_ref[...]
    for g in range(k_ref.shape[1]):
        qf = q_ref[0, :, g * gw:(g + 1) * gw].astype(F32)
        q_rows = jnp.concatenate([jnp.where(lane_head == qi, qf, 0.0) for qi in range(Q_PER_KV)], axis=0)
        s = _dot_nt(q_rows, k_ref[0, g]) * ATTN_SCALE
        p = _masked_softmax(s, mask_rows)
        o_rows = _dot(p, v_ref[0, g])
        o = None
        pg = None
        for qi in range(Q_PER_KV):
            part = jnp.where(lane_head == qi, o_rows[qi * tq:(qi + 1) * tq], 0.0)
            o = part if o is None else o + part
            pq = p[qi * tq:(qi + 1) * tq]
            pg = pq if pg is None else pg + pq
        o_ref[0, :, g * gw:(g + 1) * gw] = o
        pg_hi = pg.astype(BF16)
        pg_lo = (pg - pg_hi.astype(F32)).astype(BF16)
        ps_t = _dot_nt(st, pg_hi) + _dot_nt(st, pg_lo)
        sc = jnp.where(forced, BIG, jnp.where(future, -BIG, ps_t))
        sc_ref[0, g] = jnp.where(blk < ns, sc, PAD_SCORE)


def _rank_kernel(sc_ref, sel_ref, *, ns, topn):
    sc = sc_ref[0, 0]
    blk = lax.broadcasted_iota(jnp.int32, sc.shape, 0)
    sel_ref[0, 0] = jnp.zeros(sc.shape, F32)

    def body(j, c):
        row = sc_ref[0, 0, pl.ds(j, 1), :]
        ahead = (sc > row) | ((sc == row) & (blk < j))
        rank = jnp.sum(ahead.astype(F32), axis=0, keepdims=True)
        sel_ref[0, 0, pl.ds(j, 1), :] = (rank < topn).astype(F32)
        return c

    lax.fori_loop(0, ns, body, 0)


def _cmp_select(q, kcmp, vcmp, pos0, n_keys):
    bsz, t, _ = q.shape
    n_sub = kcmp.shape[2]
    nc = n_sub - CMP_R + 1
    ns = -(-n_keys // L_SEL)
    ns_pad = -(-ns // SUBLANES) * SUBLANES
    topn = min(TOP_N, ns)
    tq = min(t, 512)
    gw = Q_PER_KV * HEAD_DIM
    gps = KV_HEADS if t < LANES else 1
    cidx = jnp.arange(n_sub)
    st = ((cidx[None, :] // SEL_RATIO == jnp.arange(ns_pad)[:, None]) & (cidx[None, :] % SEL_RATIO < SEL_INNER)
          & (cidx[None, :] < nc)).astype(BF16)
    o_cmp, sc = pl.pallas_call(
        functools.partial(_cmpsel_kernel, pos0=pos0, nc=nc, ns=ns, tq=tq),
        grid=(bsz, KV_HEADS // gps, t // tq),
        in_specs=[pl.BlockSpec((1, tq, gw * gps), lambda b, g, i: (b, i, g)),
                  pl.BlockSpec((1, gps, n_sub, gw), lambda b, g, i: (b, g, 0, 0)),
                  pl.BlockSpec((1, gps, n_sub, gw), lambda b, g, i: (b, g, 0, 0)),
                  pl.BlockSpec((ns_pad, n_sub), lambda b, g, i: (0, 0))],
        out_specs=[pl.BlockSpec((1, tq, gw * gps), lambda b, g, i: (b, i, g)),
                   pl.BlockSpec((1, gps, ns_pad, tq), lambda b, g, i: (b, g, 0, i))],
        out_shape=[jax.ShapeDtypeStruct((bsz, t, NSA_WIDTH), F32),
                   jax.ShapeDtypeStruct((bsz, KV_HEADS, ns_pad, t), F32)],
        compiler_params=_cparams(("parallel", "parallel", "parallel")),
        name="cmp_scores",
    )(q, kcmp, vcmp, st)
    fold = t < LANES
    if fold:
        sc = sc.transpose(2, 0, 1, 3).reshape(1, 1, ns_pad, bsz * KV_HEADS * t)
    nb, ng, _, width = sc.shape
    tl = min(width, 2048)
    sel = pl.pallas_call(
        functools.partial(_rank_kernel, ns=ns, topn=topn),
        grid=(nb, ng, width // tl),
        in_specs=[pl.BlockSpec((1, 1, ns_pad, tl), lambda b, g, i: (b, g, 0, i))],
        out_specs=pl.BlockSpec((1, 1, ns_pad, tl), lambda b, g, i: (b, g, 0, i)),
        out_shape=jax.ShapeDtypeStruct(sc.shape, F32),
        compiler_params=_cparams(("parallel", "parallel", "parallel")),
        name="rank_select",
    )(sc)
    if fold:
        sel = sel.reshape(ns_pad, bsz, KV_HEADS, t).transpose(1, 2, 0, 3)
    return o_cmp, sel


def _pattn_kernel(*refs, mode, t, tq, ck):
    if mode == "sel":
        q_ref, k_ref, v_ref, sel_ref, et_ref, o_ref, kc_scr, vt_scr, m_scr, l_scr, acc_scr, s_scr, e_scr, a_scr = refs
    else:
        q_ref, k_ref, v_ref, o_ref, kc_scr, vt_scr, m_scr, l_scr, acc_scr, s_scr, e_scr, a_scr = refs
    i = pl.program_id(2)
    n_chunks = t // ck
    last = (i + 1) * (tq // ck) - 1

    @pl.when(i == 0)
    def _():
        zeros = jnp.zeros((ck, LANES - HEAD_DIM), F32)
        for c in range(n_chunks):
            kc_scr[c] = jnp.concatenate([k_ref[0, :, c * ck:(c + 1) * ck].T, zeros], axis=1).astype(BF16)
            vt_scr[c] = v_ref[0, :, c * ck:(c + 1) * ck].astype(BF16)

    qf = q_ref[0].astype(F32) * ATTN_SCALE
    q_heads = []
    for qi in range(Q_PER_KV):
        tile = qf[:, (qi // 2) * LANES:(qi // 2 + 1) * LANES]
        q_heads.append((tile if qi % 2 == 0 else pltpu.roll(tile, HEAD_DIM, 1)).astype(BF16))
    q_all = jnp.concatenate(q_heads, axis=0)
    qpos = i * tq + lax.broadcasted_iota(jnp.int32, (1, tq), 1)
    m_scr[...] = jnp.full(m_scr.shape, NEG_INF, F32)
    l_scr[...] = jnp.zeros(l_scr.shape, F32)
    acc_scr[...] = jnp.zeros(acc_scr.shape, F32)
    if mode == "sel":
        sel_b = sel_ref[0, 0].astype(BF16)
        n_steps = last + 1
    else:
        n_steps = last - jnp.maximum(i * tq - WINDOW, 0) // ck + 1

    def chunk(c):
        kpos = c * ck + lax.broadcasted_iota(jnp.int32, (ck, 1), 0)
        if mode == "sel":
            chosen = jnp.dot(et_ref[c], sel_b, preferred_element_type=F32)
            ok = (chosen > 0.5) & (kpos <= qpos)
        else:
            ok = (kpos <= qpos) & (kpos > qpos - WINDOW)
        bias = jnp.where(ok, 0.0, NEG_INF)
        s_scr[...] = _dot_nt(kc_scr[c], q_all) + jnp.concatenate([bias] * Q_PER_KV, axis=1)
        for j in range(Q_PER_KV * tq // SOFTMAX_STRIP):
            cols = slice(j * SOFTMAX_STRIP, (j + 1) * SOFTMAX_STRIP)
            m_old = m_scr[:, cols]
            m_new = jnp.maximum(m_old, jnp.max(s_scr[:, cols], axis=0, keepdims=True))
            e = jnp.exp(s_scr[:, cols] - m_new)
            alpha = jnp.exp(m_old - m_new)
            l_scr[:, cols] = l_scr[:, cols] * alpha + jnp.sum(e, axis=0, keepdims=True)
            m_scr[:, cols] = m_new
            a_scr[:, cols] = alpha
            e_scr[:, cols] = e.astype(BF16)
        acc_scr[...] = acc_scr[...] * a_scr[...] + jnp.dot(vt_scr[c], e_scr[...], preferred_element_type=F32)

    def body(step, carry):
        chunk(last - step)
        return carry

    lax.fori_loop(0, n_steps, body, 0)
    out = acc_scr[...] * (1.0 / jnp.maximum(l_scr[...], 1e-30))
    o_ref[0] = jnp.concatenate([out[:, qi * tq:(qi + 1) * tq] for qi in range(Q_PER_KV)], axis=0).T


def _prompt_attention(qrot, k, v, sel_t=None):
    bsz, t, _ = qrot.shape
    tq, ck = 512, 512
    gw = Q_PER_KV * HEAD_DIM
    mode = "win" if sel_t is None else "sel"
    assert t % tq == 0 and tq % ck == 0 and WINDOW % ck == 0 and ck % L_SEL == 0
    in_specs = [pl.BlockSpec((1, tq, gw), lambda b, g, i: (b, i, g)),
                pl.BlockSpec((1, HEAD_DIM, t), lambda b, g, i: (b, g, 0)),
                pl.BlockSpec((1, HEAD_DIM, t), lambda b, g, i: (b, g, 0))]
    args = [qrot, k, v]
    if mode == "sel":
        ns_pad = sel_t.shape[2]
        et = (jnp.arange(t)[:, None] // L_SEL == jnp.arange(ns_pad)[None, :]).astype(BF16).reshape(t // ck, ck, ns_pad)
        in_specs += [pl.BlockSpec((1, 1, ns_pad, tq), lambda b, g, i: (b, g, 0, i)),
                     pl.BlockSpec((t // ck, ck, ns_pad), lambda b, g, i: (0, 0, 0))]
        args += [sel_t, et]
    return pl.pallas_call(
        functools.partial(_pattn_kernel, mode=mode, t=t, tq=tq, ck=ck),
        grid=(bsz, KV_HEADS, t // tq),
        in_specs=in_specs,
        out_specs=pl.BlockSpec((1, tq, gw), lambda b, g, i: (b, i, g)),
        out_shape=jax.ShapeDtypeStruct((bsz, t, NSA_WIDTH), F32),
        scratch_shapes=[pltpu.VMEM((t // ck, ck, LANES), BF16), pltpu.VMEM((t // ck, HEAD_DIM, ck), BF16),
                        pltpu.VMEM((1, Q_PER_KV * tq), F32), pltpu.VMEM((1, Q_PER_KV * tq), F32),
                        pltpu.VMEM((HEAD_DIM, Q_PER_KV * tq), F32),
                        pltpu.VMEM((ck, Q_PER_KV * tq), F32), pltpu.VMEM((ck, Q_PER_KV * tq), BF16),
                        pltpu.VMEM((1, Q_PER_KV * tq), F32)],
        compiler_params=_cparams(("parallel", "parallel", "arbitrary")),
        name="prompt_attn_" + mode,
    )(*args)


def _sattn_kernel(pt_ref, qbd_ref, kpool_ref, vpool_ref, ksn_ref, vsn_ref, selc_ref, wk_ref, wv_ref, kwn_ref, vwn_ref,
                  osel_ref, owin_ref, kbuf, vbuf, m_scr, l_scr, acc_scr, sem, *, n_pages, ppc, page, tdec):
    b = pl.program_id(0)
    c = pl.program_id(1)
    nb = pl.num_programs(0)
    nch = n_pages // ppc
    step = b * nch + c
    slot = step % 2
    rows = ppc * page
    ncol = qbd_ref.shape[2]

    def copies(bb, cc, sl, p):
        dst_k = kbuf.at[sl, :, pl.ds(pl.multiple_of(p * page, page), page)]
        dst_v = vbuf.at[sl, :, pl.ds(pl.multiple_of(p * page, page), page)]
        pid = pt_ref[bb, cc * ppc + p]
        return (pltpu.make_async_copy(kpool_ref.at[pid], dst_k, sem.at[0, sl]),
                pltpu.make_async_copy(vpool_ref.at[pid], dst_v, sem.at[1, sl]))

    def start_all(bb, cc, sl):
        def body(p, carry):
            ck, cv = copies(bb, cc, sl, p)
            ck.start()
            cv.start()
            return carry
        lax.fori_loop(0, ppc, body, 0)

    def wait_all(bb, cc, sl):
        def body(p, carry):
            ck, cv = copies(bb, cc, sl, p)
            ck.wait()
            cv.wait()
            return carry
        lax.fori_loop(0, ppc, body, 0)

    @pl.when(step == 0)
    def _():
        start_all(0, 0, 0)

    @pl.when(step + 1 < nb * nch)
    def _():
        nxt = step + 1
        start_all(nxt // nch, nxt % nch, 1 - slot)

    wait_all(b, c, slot)

    @pl.when(c == 0)
    def _():
        m_scr[...] = jnp.full(m_scr.shape, NEG_INF, F32)
        l_scr[...] = jnp.zeros(l_scr.shape, F32)
        acc_scr[...] = jnp.zeros(acc_scr.shape, F32)

    qbd = qbd_ref[0]
    nblk = rows // L_SEL
    s3 = (_dot_tn(kbuf[slot], qbd) * ATTN_SCALE).reshape(nblk, L_SEL, ncol)
    blk0 = pl.multiple_of(c * nblk, SUBLANES)
    chosen = (selc_ref[0, pl.ds(blk0, nblk), :] > 0.5)[:, None, :]
    s3 = jnp.where(chosen, s3, NEG_INF)
    m_old = m_scr[...]
    m_new = jnp.maximum(m_old, jnp.max(jnp.max(s3, axis=0), axis=0, keepdims=True))
    e3 = jnp.where(chosen, jnp.exp(s3 - m_new[None]), 0.0)
    alpha = jnp.exp(m_old - m_new)
    l_scr[...] = l_scr[...] * alpha + jnp.sum(jnp.sum(e3, axis=0), axis=0, keepdims=True)
    m_scr[...] = m_new
    acc_scr[...] = acc_scr[...] * alpha + _dot(vbuf[slot], e3.reshape(rows, ncol))

    @pl.when(c == nch - 1)
    def _():
        tcol = lax.broadcasted_iota(jnp.int32, (tdec, ncol), 1) % tdec
        jrow = lax.broadcasted_iota(jnp.int32, (tdec, ncol), 0)
        causal_new = jrow <= tcol
        sel_last = selc_ref[0, pl.ds(nch * nblk, 1), :] > 0.5
        ok_new = causal_new & sel_last
        s_new = jnp.where(ok_new, _dot(ksn_ref[0], qbd) * ATTN_SCALE, NEG_INF)
        m_old2 = m_scr[...]
        m_fin = jnp.maximum(m_old2, jnp.max(s_new, axis=0, keepdims=True))
        e_new = jnp.where(ok_new, jnp.exp(s_new - m_fin), 0.0)
        alpha2 = jnp.exp(m_old2 - m_fin)
        l_fin = l_scr[...] * alpha2 + jnp.sum(e_new, axis=0, keepdims=True)
        inv = 1.0 / jnp.maximum(l_fin, 1e-30)
        acc = acc_scr[...] * alpha2 + _dot_tn(vsn_ref[0], e_new)
        osel_ref[0] = acc * inv

        wb = wk_ref.shape[2]
        jw = lax.broadcasted_iota(jnp.int32, (wb, ncol), 0)
        tw = lax.broadcasted_iota(jnp.int32, (wb, ncol), 1) % tdec
        ok_c = jw + (WINDOW - wb) > tw
        s_c = jnp.where(ok_c, _dot_tn(wk_ref[0], qbd) * ATTN_SCALE, NEG_INF)
        s_n = jnp.where(causal_new, _dot(kwn_ref[0], qbd) * ATTN_SCALE, NEG_INF)
        m_w = jnp.maximum(jnp.max(s_c, axis=0, keepdims=True), jnp.max(s_n, axis=0, keepdims=True))
        e_c = jnp.where(ok_c, jnp.exp(s_c - m_w), 0.0)
        e_n = jnp.where(causal_new, jnp.exp(s_n - m_w), 0.0)
        inv_w = 1.0 / jnp.maximum(jnp.sum(e_c, axis=0, keepdims=True) + jnp.sum(e_n, axis=0, keepdims=True), 1e-30)
        ow = _dot(wv_ref[0], e_c) + _dot_tn(vwn_ref[0], e_n)
        owin_ref[0] = ow * inv_w


def _sample_attention(qrot, kpool, vpool, page_table, ks_new, vs_new, sel_t, wk, wv, kw_new, vw_new):
    bsz, tdec, _ = qrot.shape
    n_pages = page_table.shape[1]
    page = kpool.shape[2]
    ppc = min(n_pages, 32)
    assert n_pages % ppc == 0 and page % L_SEL == 0
    nch = n_pages // ppc
    rows = ppc * page
    ncol = KV_HEADS * Q_PER_KV * tdec
    ns_pad = sel_t.shape[2]
    wb = wk.shape[2]
    q5 = qrot.reshape(bsz, tdec, KV_HEADS, Q_PER_KV, HEAD_DIM).transpose(0, 2, 4, 3, 1)
    eye = jnp.eye(KV_HEADS, dtype=qrot.dtype)
    qbd = (q5[:, :, :, None] * eye[None, :, None, :, None, None]).reshape(bsz, KV_WIDTH, ncol)
    selc = jnp.broadcast_to(sel_t.transpose(0, 2, 1, 3)[:, :, :, None, :], (bsz, ns_pad, KV_HEADS, Q_PER_KV, tdec))
    selc = selc.reshape(bsz, ns_pad, ncol)
    per_b = lambda shape: pl.BlockSpec((1,) + shape, lambda b, c, pt: (b, 0, 0))
    grid_spec = pltpu.PrefetchScalarGridSpec(
        num_scalar_prefetch=1,
        grid=(bsz, nch),
        in_specs=[per_b((KV_WIDTH, ncol)),
                  pl.BlockSpec(memory_space=pl.ANY), pl.BlockSpec(memory_space=pl.ANY),
                  per_b((tdec, KV_WIDTH)), per_b((tdec, KV_WIDTH)),
                  per_b((ns_pad, ncol)),
                  per_b((KV_WIDTH, wb)), per_b((KV_WIDTH, wb)),
                  per_b((tdec, KV_WIDTH)), per_b((tdec, KV_WIDTH))],
        out_specs=[per_b((KV_WIDTH, ncol)), per_b((KV_WIDTH, ncol))],
        scratch_shapes=[pltpu.VMEM((2, KV_WIDTH, rows), F32), pltpu.VMEM((2, KV_WIDTH, rows), F32),
                        pltpu.VMEM((1, ncol), F32), pltpu.VMEM((1, ncol), F32), pltpu.VMEM((KV_WIDTH, ncol), F32),
                        pltpu.SemaphoreType.DMA((2, 2))],
    )
    o_sel, o_win = pl.pallas_call(
        functools.partial(_sattn_kernel, n_pages=n_pages, ppc=ppc, page=page, tdec=tdec),
        grid_spec=grid_spec,
        out_shape=[jax.ShapeDtypeStruct((bsz, KV_WIDTH, ncol), F32)] * 2,
        compiler_params=_cparams(("arbitrary", "arbitrary")),
        name="sample_attn",
    )(page_table, qbd, kpool, vpool, ks_new, vs_new, selc, wk, wv, kw_new, vw_new)

    def unpack(o):
        o6 = o.reshape(bsz, KV_HEADS, HEAD_DIM, KV_HEADS, Q_PER_KV, tdec)
        diag = jnp.stack([o6[:, g, :, g] for g in range(KV_HEADS)], axis=1)
        return diag.transpose(0, 4, 1, 3, 2).reshape(bsz, tdec, NSA_WIDTH)

    return unpack(o_sel), unpack(o_win)


def _mix_kernel(x_ref, oc_ref, os_ref, ow_ref, gt_ref, u_ref, v_ref, wsm_ref, bs_ref, gn_ref, gs_ref, eg_ref, wout_ref,
                o_ref, *, chunk):
    r = x_ref.shape[0]
    g = gt_ref[...]
    g_hi = g.astype(BF16)
    g_lo = (g - g_hi.astype(F32)).astype(BF16)
    onsa = None
    for j, branch in enumerate((oc_ref, os_ref, ow_ref)):
        ge = _dot(g_hi, eg_ref[j]) + _dot(g_lo, eg_ref[j])
        term = ge * branch[...]
        onsa = term if onsa is None else onsa + term
    onsa = _rms(onsa, gn_ref[...])
    ii = lax.broadcasted_iota(jnp.int32, (r, r), 0)
    jj = lax.broadcasted_iota(jnp.int32, (r, r), 1)
    tri = (ii // chunk == jj // chunk) & (jj % chunk <= ii % chunk)
    cols = []
    for gi in range(SGU_GROUPS):
        sl = slice(gi * SGU_GROUP_DIM, (gi + 1) * SGU_GROUP_DIM)
        ws = jnp.where(tri, wsm_ref[gi], 0.0)
        mixed = _dot(ws, v_ref[:, sl]) + bs_ref[:, gi:gi + 1]
        cols.append(u_ref[:, sl] * mixed)
    osgu = _rms(jnp.concatenate(cols, axis=1), gs_ref[...])
    o_ref[...] = x_ref[...] + _dot(jnp.concatenate([onsa, osgu], axis=1), wout_ref[...])


def _gate_expanders():
    c = jnp.arange(GATE_COLS)[None, :, None]
    lane = jnp.arange(NSA_WIDTH)[None, None, :]
    j = jnp.arange(3)[:, None, None]
    return (c == (lane // HEAD_DIM) * 3 + j).astype(BF16)


def _mix(x2d, o_cmp, o_sel, o_win, gates, u, v, w_sgu, b_sgu, g_nsa_out, g_sgu_out, w_out_b, chunk, r):
    n, d = x2d.shape
    rep = r // chunk
    pick = (jnp.arange(r)[:, None] % chunk == jnp.arange(chunk)[None, :]).astype(F32)
    wsm = jnp.einsum("ia,gab,jb->gij", pick, w_sgu[:, :chunk, :chunk], pick, precision=lax.Precision.HIGHEST)
    bs = jnp.tile(b_sgu[:, :chunk].T, (rep, 1))
    row = lambda w: pl.BlockSpec((r, w), lambda i: (i, 0))
    return pl.pallas_call(
        functools.partial(_mix_kernel, chunk=chunk),
        grid=(n // r,),
        in_specs=[row(d), row(NSA_WIDTH), row(NSA_WIDTH), row(NSA_WIDTH), row(GATE_COLS), row(1024), row(1024),
                  _const_spec(wsm.shape), _const_spec(bs.shape), _const_spec((1, NSA_WIDTH)), _const_spec((1, 1024)),
                  _const_spec((3, GATE_COLS, NSA_WIDTH)), _const_spec(w_out_b.shape)],
        out_specs=row(d),
        out_shape=jax.ShapeDtypeStruct((n, d), F32),
        compiler_params=_cparams(("parallel",)),
        name="mix_out_proj",
    )(x2d, o_cmp, o_sel, o_win, gates, u, v, wsm, bs, g_nsa_out, g_sgu_out, _gate_expanders(), w_out_b)


def _rms_matmul_kernel(x_ref, g_ref, w_ref, o_ref):
    o_ref[...] = _dot(_rms(x_ref[...], g_ref[...]), w_ref[...])


def _rms_matmul(x2d, g, w_b, tm):
    n, d = x2d.shape
    m = w_b.shape[1]
    return pl.pallas_call(
        _rms_matmul_kernel,
        grid=(n // tm,),
        in_specs=[pl.BlockSpec((tm, d), lambda i: (i, 0)), _const_spec((1, d)), _const_spec(w_b.shape)],
        out_specs=pl.BlockSpec((tm, m), lambda i: (i, 0)),
        out_shape=jax.ShapeDtypeStruct((n, m), F32),
        compiler_params=_cparams(("parallel",)),
        name="rms_matmul",
    )(x2d, g, w_b)


def _mem_heads(q, k_ref, v_ref, i=0):
    outs = []
    for h in range(MEM_HEADS):
        sl = slice(h * MEM_HEAD_DIM, (h + 1) * MEM_HEAD_DIM)
        s = _dot_nt(q[:, sl], k_ref[i, :, h, :]) * MEM_SCALE
        p = _masked_softmax(s, jnp.ones(s.shape, dtype=jnp.bool_))
        outs.append(_dot(p, v_ref[i, :, h, :]))
    return jnp.concatenate(outs, axis=1)


def _memattn_kernel(q_ref, k_ref, v_ref, o_ref):
    for i in range(q_ref.shape[0]):
        o_ref[i] = _mem_heads(q_ref[i], k_ref, v_ref, i)


def _mem_block_kernel(x_ref, g_ref, wq_ref, k_ref, v_ref, wo_ref, gm_ref, wr_ref, o_ref, gtop_ref):
    x = x_ref[0]
    hq = _dot(_rms(x, g_ref[...]), wq_ref[...])
    x2 = x + _dot(_mem_heads(hq, k_ref, v_ref), wo_ref[...])
    o_ref[0] = x2
    gtop_ref[0] = _top_group(x2, gm_ref, wr_ref)


def _mem_block(x3, g_mem, w_q, mk, mv, w_o, g_moe, wr):
    bsz, t, d = x3.shape
    m = mk.shape[1]
    tq = min(t, 512)
    kv_spec = pl.BlockSpec((1, m, MEM_HEADS, MEM_HEAD_DIM), lambda b, i: (b, 0, 0, 0))
    return pl.pallas_call(
        _mem_block_kernel,
        grid=(bsz, t // tq),
        in_specs=[pl.BlockSpec((1, tq, d), lambda b, i: (b, i, 0)), _const_spec((1, d)), _const_spec(w_q.shape),
                  kv_spec, kv_spec, _const_spec(w_o.shape), _const_spec((1, d)), _const_spec(wr.shape)],
        out_specs=[pl.BlockSpec((1, tq, d), lambda b, i: (b, i, 0)), pl.BlockSpec((1, tq, 1), lambda b, i: (b, i, 0))],
        out_shape=[jax.ShapeDtypeStruct((bsz, t, d), F32), jax.ShapeDtypeStruct((bsz, t, 1), jnp.int32)],
        compiler_params=_cparams(("parallel", "parallel")),
        name="mem_block",
    )(x3, g_mem, w_q, mk, mv, w_o, g_moe, wr)


def _mem_attention(hq, mk, mv):
    bsz, t, w = hq.shape
    m = mk.shape[1]
    nb = 4 if bsz % 4 == 0 else 1
    kv_spec = pl.BlockSpec((nb, m, MEM_HEADS, MEM_HEAD_DIM), lambda b: (b, 0, 0, 0))
    return pl.pallas_call(
        _memattn_kernel,
        grid=(bsz // nb,),
        in_specs=[pl.BlockSpec((nb, t, w), lambda b: (b, 0, 0)), kv_spec, kv_spec],
        out_specs=pl.BlockSpec((nb, t, w), lambda b: (b, 0, 0)),
        out_shape=jax.ShapeDtypeStruct((bsz, t, w), F32),
        compiler_params=_cparams(("parallel",)),
        name="mem_attn",
    )(hq, mk, mv)


def _router_logits(h, wr_ref):
    h_hi = h.astype(BF16)
    h_lo = (h - h_hi.astype(F32)).astype(BF16)
    w_hi, w_lo = wr_ref[0], wr_ref[1]
    return _dot(h_hi, w_hi) + _dot(h_hi, w_lo) + _dot(h_lo, w_hi)


def _top_group(x, g_ref, wr_ref):
    z = _router_logits(_rms(x, g_ref[...]), wr_ref)
    lane = lax.broadcasted_iota(jnp.int32, z.shape, 1)
    zg = jnp.where(lane < N_GROUPS, z, -jnp.inf)
    m = jnp.max(zg, axis=1, keepdims=True)
    first = jnp.min(jnp.where(zg == m, lane.astype(F32), float(LANES)), axis=1, keepdims=True)
    return first.astype(jnp.int32)


def _matmul_res_route_kernel(a_ref, w_ref, r_ref, g_ref, wr_ref, o_ref, gtop_ref):
    x = r_ref[...] + _dot(a_ref[...], w_ref[...])
    o_ref[...] = x
    gtop_ref[...] = _top_group(x, g_ref, wr_ref)


def _matmul_res_route(a2d, w_b, res, g_moe, wr, tm):
    n, k = a2d.shape
    m = w_b.shape[1]
    return pl.pallas_call(
        _matmul_res_route_kernel,
        grid=(n // tm,),
        in_specs=[pl.BlockSpec((tm, k), lambda i: (i, 0)), _const_spec(w_b.shape), pl.BlockSpec((tm, m), lambda i: (i, 0)),
                  _const_spec((1, m)), _const_spec(wr.shape)],
        out_specs=[pl.BlockSpec((tm, m), lambda i: (i, 0)), pl.BlockSpec((tm, 1), lambda i: (i, 0))],
        out_shape=[jax.ShapeDtypeStruct((n, m), F32), jax.ShapeDtypeStruct((n, 1), jnp.int32)],
        compiler_params=_cparams(("parallel",)),
        name="matmul_residual_route",
    )(a2d, w_b, res, g_moe, wr)


def _moe_kernel(src_ref, tg_ref, tv_ref, nv_ref, widx_ref, x_hbm, gm_ref, gf_ref, wr_ref, wg_ref, wu_ref, wd_ref, y_hbm,
                buf, hb_scr, w4_scr, gsem, ssem, *, tm):
    t = pl.program_id(0)
    e = pl.program_id(1)
    nt = pl.num_programs(0)
    slot = t % 2
    other = 1 - slot
    valid = tv_ref[t] == 1
    prev_valid = (t >= 1) & (tv_ref[jnp.maximum(t - 1, 0)] == 1)
    next_valid = (t + 1 < nt) & (tv_ref[jnp.minimum(t + 1, nt - 1)] == 1)

    def gather_start(tt, sl):
        def body(r, c):
            idx = jnp.maximum(src_ref[tt * tm + r], 0)
            pltpu.make_async_copy(x_hbm.at[pl.ds(idx, 1), :], buf.at[sl, pl.ds(r, 1), :], gsem.at[sl]).start()
            return c
        lax.fori_loop(0, tm, body, 0, unroll=8)

    def gather_wait(sl):
        pltpu.make_async_copy(x_hbm.at[pl.ds(0, tm), :], buf.at[sl], gsem.at[sl]).wait()

    def scatter_copy(tt, sl, r):
        return pltpu.make_async_copy(buf.at[sl, pl.ds(r, 1), :], y_hbm.at[pl.ds(src_ref[tt * tm + r], 1), :], ssem.at[0])

    def scatter_start(tt, sl):
        def body(r, c):
            scatter_copy(tt, sl, r).start()
            return c
        lax.fori_loop(0, nv_ref[tt], body, 0)

    def scatter_wait(tt, sl):
        def body(r, c):
            scatter_copy(tt, sl, r).wait()
            return c
        lax.fori_loop(0, nv_ref[tt], body, 0)

    @pl.when(e == 0)
    def _():
        @pl.when(t == 0)
        def _():
            gather_start(0, 0)

        @pl.when(prev_valid)
        def _():
            scatter_start(t - 1, other)

        @pl.when(valid)
        def _():
            gather_wait(slot)
            h = _rms(buf[slot], gm_ref[...])
            hb_scr[...] = h.astype(BF16)
            z = _router_logits(h, wr_ref)
            lane = lax.broadcasted_iota(jnp.int32, z.shape, 1)
            lanef = lane.astype(F32)
            grp = lane < N_GROUPS
            zg = jnp.where(grp, z, -jnp.inf)
            pg_top = 1.0 / jnp.sum(jnp.where(grp, jnp.exp(zg - jnp.max(zg, axis=1, keepdims=True)), 0.0), axis=1, keepdims=True)
            lo = N_GROUPS + tg_ref[t] * EXPERTS_PER_GROUP
            ing = (lane >= lo) & (lane < lo + EXPERTS_PER_GROUP)
            pf = _masked_softmax(z, ing)
            big = float(2 * LANES)
            m1 = jnp.max(jnp.where(ing, pf, -1.0), axis=1, keepdims=True)
            i1 = jnp.min(jnp.where(ing & (pf == m1), lanef, big), axis=1, keepdims=True)
            rest = ing & (lanef != i1)
            m2 = jnp.max(jnp.where(rest, pf, -1.0), axis=1, keepdims=True)
            i2 = jnp.min(jnp.where(rest & (pf == m2), lanef, big), axis=1, keepdims=True)
            tot = m1 + m2
            w4_scr[...] = jnp.where(lanef == i1, m1 / tot * pg_top, jnp.where(lanef == i2, m2 / tot * pg_top, 0.0))

    @pl.when(e == EXPERTS_PER_GROUP // 2)
    def _():
        @pl.when(prev_valid)
        def _():
            scatter_wait(t - 1, other)

        @pl.when(next_valid)
        def _():
            gather_start(t + 1, other)

    @pl.when(valid)
    def _():
        hb = hb_scr[...]
        lane = lax.broadcasted_iota(jnp.int32, w4_scr.shape, 1)
        col = N_GROUPS + tg_ref[t] * EXPERTS_PER_GROUP + e
        we = jnp.sum(jnp.where(lane == col, w4_scr[...], 0.0), axis=1, keepdims=True)
        act = jax.nn.silu(_dot(hb, wg_ref[0])) * _dot(hb, wu_ref[0])
        buf[slot] += _dot(act * we, wd_ref[0])

    @pl.when(valid & (e == EXPERTS_PER_GROUP - 1))
    def _():
        buf[slot] = _rms(buf[slot], gf_ref[...])


def _moe_final(x2d, g_top, g_moe, g_final, wr, w_gate, w_up, w_down, tm):
    n, d = x2d.shape
    n_tiles = n // tm + N_GROUPS
    onehot = (g_top[:, None] == jnp.arange(N_GROUPS)[None, :]).astype(jnp.int32)
    counts = jnp.sum(onehot, axis=0)
    rank = jnp.sum((jnp.cumsum(onehot, axis=0) - onehot) * onehot, axis=1)
    padded = (counts + tm - 1) // tm * tm
    ends = jnp.cumsum(padded)
    base = ends - padded
    pos = base[g_top] + rank
    src = jnp.full((n_tiles * tm,), -1, jnp.int32).at[pos].set(jnp.arange(n, dtype=jnp.int32))
    tile_start = jnp.arange(n_tiles, dtype=jnp.int32) * tm
    tile_valid = (tile_start < ends[-1]).astype(jnp.int32)
    tile_group = jnp.minimum(jnp.sum((tile_start[:, None] >= ends[None, :]).astype(jnp.int32), axis=1), N_GROUPS - 1)
    tile_rows = jnp.clip(ends[tile_group] - padded[tile_group] + counts[tile_group] - tile_start, 0, tm)
    tile_rows = (tile_rows * tile_valid).astype(jnp.int32)
    n_valid = ends[-1] // tm
    last_group = tile_group[jnp.maximum(n_valid - 1, 0)]
    eidx = tile_group[:, None] * EXPERTS_PER_GROUP + jnp.arange(EXPERTS_PER_GROUP, dtype=jnp.int32)[None, :]
    widx = jnp.where(tile_valid[:, None] == 1, eidx, last_group * EXPERTS_PER_GROUP + EXPERTS_PER_GROUP - 1)
    widx = widx.reshape(-1).astype(jnp.int32)

    wmap = lambda t, e, src, tg, tv, nv, wi: (wi[t * EXPERTS_PER_GROUP + e], 0, 0)
    cmap = lambda t, e, src, tg, tv, nv, wi: (0, 0)
    grid_spec = pltpu.PrefetchScalarGridSpec(
        num_scalar_prefetch=5,
        grid=(n_tiles, EXPERTS_PER_GROUP),
        in_specs=[pl.BlockSpec(memory_space=pl.ANY),
                  pl.BlockSpec((1, d), cmap), pl.BlockSpec((1, d), cmap),
                  pl.BlockSpec(wr.shape, lambda t, e, src, tg, tv, nv, wi: (0, 0, 0)),
                  pl.BlockSpec((1, d, EXPERT_FF), wmap), pl.BlockSpec((1, d, EXPERT_FF), wmap),
                  pl.BlockSpec((1, EXPERT_FF, d), wmap)],
        out_specs=pl.BlockSpec(memory_space=pl.ANY),
        scratch_shapes=[pltpu.VMEM((2, tm, d), F32), pltpu.VMEM((tm, d), BF16), pltpu.VMEM((tm, LANES), F32),
                        pltpu.SemaphoreType.DMA((2,)), pltpu.SemaphoreType.DMA((1,))],
    )
    return pl.pallas_call(
        functools.partial(_moe_kernel, tm=tm),
        grid_spec=grid_spec,
        out_shape=jax.ShapeDtypeStruct((n, d), F32),
        compiler_params=_cparams(("arbitrary", "arbitrary")),
        name="moe_final_norm",
    )(src, tile_group, tile_valid, tile_rows, widx, x2d, g_moe, g_final, wr, w_gate, w_up, w_down)


def _finish(x2d, o_cmp, o_sel, o_win, gates, u, v, mk, mv, bsz, lw, chunk, moe_tm):
    n, d = x2d.shape
    t = n // bsz
    x1 = _mix(x2d, o_cmp, o_sel, o_win, gates, u, v, lw["w_sgu"], lw["b_sgu"], lw["g_nsa_out"], lw["g_sgu_out"],
              lw["w_out"], chunk, min(n, 256))
    if t >= LANES:
        x2, g_top = _mem_block(x1.reshape(bsz, t, d), lw["g_mem_norm"], lw["w_mem_q"], mk, mv, lw["w_mem_o"],
                               lw["g_moe_norm"], lw["w_router"])
        x2, g_top = x2.reshape(n, d), g_top.reshape(n, 1)
    else:
        hq = _rms_matmul(x1, lw["g_mem_norm"], lw["w_mem_q"], min(n, 512))
        o_m = _mem_attention(hq.reshape(bsz, t, -1), mk, mv)
        x2, g_top = _matmul_res_route(o_m.reshape(n, -1), lw["w_mem_o"], x1, lw["g_moe_norm"], lw["w_router"], min(n, 512))
    return _moe_final(x2, g_top[:, 0], lw["g_moe_norm"], lw["g_final"], lw["w_router"], lw["w_exp_gate"], lw["w_exp_up"],
                      lw["w_exp_down"], moe_tm)


def kernel(x_prompt, x_sample, cache_cmp_k, cache_cmp_v, cache_sel_k, cache_sel_v, cache_win_k, cache_win_v, cache_mem_k, cache_mem_v, page_table, mem_prompt, w_in, g_attn_norm, pe_cmp_k, w_cmp_k1, w_cmp_k2, pe_cmp_v, w_cmp_v1, w_cmp_v2, g_sgu_v, w_sgu, b_sgu, g_nsa_out, g_sgu_out, w_out, g_mem_norm, g_mem_src, w_mem_q, w_mem_k, w_mem_v, w_mem_o, g_moe_norm, w_router_group, w_router_expert, w_exp_gate, w_exp_up, w_exp_down, g_final):
    depth = w_in.shape[0]
    assert depth == 1, "single-layer trunk"
    bp, tp, d = x_prompt.shape
    bs, ts, _ = x_sample.shape
    n_pages = page_table.shape[1]
    page = cache_cmp_k.shape[2]
    past = n_pages * page
    assert ts < STRIDE and tp % 256 == 0 and page % STRIDE == 0
    row = lambda a: a[0].reshape(1, -1)

    w_in_p = w_in[0].astype(BF16)
    wr = jnp.concatenate([w_router_group[0], w_router_expert[0],
                          jnp.zeros((d, LANES - N_GROUPS - N_EXPERTS), F32)], axis=1)
    wr_hi = wr.astype(BF16)
    wr = jnp.stack([wr_hi, (wr - wr_hi.astype(F32)).astype(BF16)])
    lw = {
        "w_sgu": w_sgu[0], "b_sgu": b_sgu[0], "g_nsa_out": row(g_nsa_out), "g_sgu_out": row(g_sgu_out),
        "w_out": w_out[0].astype(BF16), "g_mem_norm": row(g_mem_norm), "w_mem_q": w_mem_q[0].astype(BF16),
        "w_mem_o": w_mem_o[0].astype(BF16), "g_moe_norm": row(g_moe_norm), "g_final": g_final.reshape(1, -1),
        "w_router": wr, "w_exp_gate": w_exp_gate[0], "w_exp_up": w_exp_up[0], "w_exp_down": w_exp_down[0],
    }
    g_attn = row(g_attn_norm)
    gsv = row(g_sgu_v)
    kv5 = lambda a, b, t: a.reshape(1, b, t, KV_HEADS, HEAD_DIM)

    np_ = bp * tp
    tabs_p = _rope_tables(jnp.arange(tp, dtype=jnp.int32))
    (q, qrot, kc, vc, ks, vs, kw, vw, gates, u, v) = _project(x_prompt.reshape(np_, d), g_attn, w_in_p, tabs_p, gsv, 256,
                                                              kv_seq=tp)
    pt_p = jnp.arange(np_ // page, dtype=jnp.int32).reshape(bp, tp // page)
    kcmp = _compress(kc, pt_p, page, pe_cmp_k[0], w_cmp_k1[0], w_cmp_k2[0])
    vcmp = _compress(vc, pt_p, page, pe_cmp_v[0], w_cmp_v1[0], w_cmp_v2[0])
    o_cmp, sel_t = _cmp_select(q.reshape(bp, tp, -1), kcmp, vcmp, 0, tp)
    qrot3 = qrot.reshape(bp, tp, -1)
    o_sel = _prompt_attention(qrot3, ks, vs, sel_t)
    o_win = _prompt_attention(qrot3, kw, vw)
    n_mem = mem_prompt.shape[1]
    mem_w = MEM_HEADS * MEM_HEAD_DIM
    mem2d = mem_prompt.reshape(bp * n_mem, d)
    mem4 = lambda a: a.reshape(bp, n_mem, MEM_HEADS, MEM_HEAD_DIM)
    mk_p = mem4(_rms_matmul(mem2d, row(g_mem_src), w_mem_k[0].astype(BF16), min(bp * n_mem, 512)))
    mv_p = mem4(_rms_matmul(mem2d, row(g_mem_src), w_mem_v[0].astype(BF16), min(bp * n_mem, 512)))
    y_p = _finish(x_prompt.reshape(np_, d), o_cmp.reshape(np_, -1), o_sel.reshape(np_, -1), o_win.reshape(np_, -1),
                  gates, u, v, mk_p, mv_p, bp, lw, CHUNK, min(np_, 512))
    wbp = min(WINDOW, tp)
    kv5t = lambda a: a.reshape(bp, KV_HEADS, HEAD_DIM, a.shape[2]).transpose(0, 3, 1, 2)[None]
    outs_p = (kv5t(kc), kv5t(vc), kv5t(ks), kv5t(vs), kv5t(kw[:, :, -wbp:]), kv5t(vw[:, :, -wbp:]), mk_p[None], mv_p[None])

    ns_ = bs * ts
    pos_s = past + jnp.arange(ts, dtype=jnp.int32)
    tabs_s = tuple(jnp.tile(a, (bs, 1)) for a in _rope_tables(pos_s))
    (q, qrot, kc, vc, ks, vs, kw, vw, gates, u, v) = _project(x_sample.reshape(ns_, d), g_attn, w_in_p, tabs_s, gsv, ns_)
    pool = lambda c: c[0].transpose(0, 2, 3, 1).reshape(c.shape[1], KV_WIDTH, page)
    kcmp = _compress(pool(cache_cmp_k), page_table, page, pe_cmp_k[0], w_cmp_k1[0], w_cmp_k2[0])
    vcmp = _compress(pool(cache_cmp_v), page_table, page, pe_cmp_v[0], w_cmp_v1[0], w_cmp_v2[0])
    o_cmp, sel_t = _cmp_select(q.reshape(bs, ts, -1).astype(F32), kcmp, vcmp, past, past + ts)
    wb = cache_win_k.shape[2]
    wk = cache_win_k[0].transpose(0, 2, 3, 1).reshape(bs, KV_WIDTH, wb)
    wv = cache_win_v[0].transpose(0, 2, 3, 1).reshape(bs, KV_WIDTH, wb)
    ks3, vs3, kw3, vw3 = (a.reshape(bs, ts, KV_WIDTH) for a in (ks, vs, kw, vw))
    o_sel, o_win = _sample_attention(qrot.reshape(bs, ts, -1), pool(cache_sel_k), pool(cache_sel_v), page_table,
                                     ks3, vs3, sel_t, wk, wv, kw3, vw3)
    mem_ks, mem_vs = cache_mem_k[0], cache_mem_v[0]
    y_s = _finish(x_sample.reshape(ns_, d), o_cmp.reshape(ns_, -1), o_sel.reshape(ns_, -1), o_win.reshape(ns_, -1),
                  gates, u, v, mem_ks, mem_vs, bs, lw, ts, min(ns_, 128))
    def slide(cache_t, new):
        win = jnp.concatenate([cache_t, new.transpose(0, 2, 1)], axis=2)[:, :, -wb:]
        return win.reshape(bs, KV_HEADS, HEAD_DIM, wb).transpose(0, 3, 1, 2)[None]

    win_k_s = slide(wk, kw3)
    win_v_s = slide(wv, vw3)
    outs_s = (kv5(kc, bs, ts), kv5(vc, bs, ts), kv5(ks, bs, ts), kv5(vs, bs, ts), win_k_s, win_v_s,
              v.reshape(1, bs, ts, -1))

    return (y_p.reshape(bp, tp, d), y_s.reshape(bs, ts, d)) + outs_p + outs_s
```

```python
import functools

import jax
import jax.numpy as jnp
from jax import lax
from jax.experimental import pallas as pl
from jax.experimental.pallas import tpu as pltpu

F32 = jnp.float32
BF16 = jnp.bfloat16

N_HEADS = 16
HEAD_DIM = 64
KV_HEADS = 4
Q_PER_KV = N_HEADS // KV_HEADS
NSA_WIDTH = N_HEADS * HEAD_DIM
KV_WIDTH = KV_HEADS * HEAD_DIM
ROT_DIM = HEAD_DIM // 4
ROPE_THETA = 500000.0
L_CMP = 32
STRIDE = 16
CMP_R = L_CMP // STRIDE
CMP_HIDDEN = 128
L_SEL = 64
SEL_RATIO = L_SEL // STRIDE
SEL_INNER = (L_SEL - L_CMP) // STRIDE + 1
TOP_N = 16
WINDOW = 512
SGU_GROUPS = 8
SGU_GROUP_DIM = 128
CHUNK = 128
MEM_HEADS = 4
MEM_HEAD_DIM = 128
N_GROUPS = 4
EXPERTS_PER_GROUP = 4
N_EXPERTS = N_GROUPS * EXPERTS_PER_GROUP
EXPERT_FF = 512
EPS = 1e-6
NEG_INF = -1e30
BIG = 1e9
PAD_SCORE = -3e38
CMP_ROW_PITCH = STRIDE + 4
ATTN_SCALE = HEAD_DIM ** -0.5
MEM_SCALE = MEM_HEAD_DIM ** -0.5

LANES = 128
SUBLANES = 8
V7X_VMEM_LIMIT_BYTES = 60000 * 1024

GATE_COLS = LANES
N_GATES = 3 * N_HEADS
_SEG = {}
_off = 0
for _name, _w in (("q", NSA_WIDTH), ("kc", KV_WIDTH), ("vc", KV_WIDTH), ("ks", KV_WIDTH), ("vs", KV_WIDTH),
                  ("kw", KV_WIDTH), ("vw", KV_WIDTH), ("gt", N_GATES), ("u", 1024), ("v", 1024)):
    _SEG[_name] = (_off, _w)
    _off += _w
IN_COLS = _off


def _cparams(sem, vmem=V7X_VMEM_LIMIT_BYTES):
    return pltpu.CompilerParams(dimension_semantics=sem, vmem_limit_bytes=vmem)


def _dot(a, b):
    return jnp.dot(a.astype(BF16), b.astype(BF16), preferred_element_type=F32)


def _dot_nt(a, b):
    return lax.dot_general(a.astype(BF16), b.astype(BF16), (((1,), (1,)), ((), ())), preferred_element_type=F32)


def _dot_tn(a, b):
    return lax.dot_general(a.astype(BF16), b.astype(BF16), (((0,), (0,)), ((), ())), preferred_element_type=F32)


def _rms(x, g):
    return x * lax.rsqrt(jnp.mean(x * x, axis=-1, keepdims=True) + EPS) * g


def _masked_softmax(s, mask, axis=-1):
    s = jnp.where(mask, s, NEG_INF)
    m = jnp.max(s, axis=axis, keepdims=True)
    e = jnp.where(mask, jnp.exp(s - m), 0.0)
    return e * (1.0 / jnp.maximum(jnp.sum(e, axis=axis, keepdims=True), 1e-30))


def _const_spec(shape):
    nd = len(shape)
    return pl.BlockSpec(shape, lambda *_: (0,) * nd, pipeline_mode=pl.Buffered(1))


def _proj_kernel(x_ref, g_ref, w_ref, rc_ref, rs1_ref, rs2_ref, gsv_ref,
                 q_ref, qrot_ref, kc_ref, vc_ref, ks_ref, vs_ref, kw_ref, vw_ref, gates_ref, u_ref, v_ref, *, transpose_kv):
    hb = _rms(x_ref[...], g_ref[...]).astype(BF16)

    def put_kv(ref, val):
        if transpose_kv:
            ref[0] = val.T
        else:
            ref[...] = val

    def seg(name):
        lo, width = _SEG[name]
        return jnp.dot(hb, w_ref[:, lo:lo + width], preferred_element_type=F32)

    def rope(z):
        rc, rs1, rs2 = rc_ref[...], rs1_ref[...], rs2_ref[...]
        half = ROT_DIM // 2
        outs = []
        for c in range(z.shape[1] // LANES):
            zc = z[:, c * LANES:(c + 1) * LANES]
            outs.append(zc * rc + pltpu.roll(zc, LANES - half, 1) * rs1 + pltpu.roll(zc, half, 1) * rs2)
        return jnp.concatenate(outs, axis=1)

    q = seg("q")
    q_ref[...] = q.astype(BF16)
    qrot_ref[...] = rope(q).astype(BF16)
    put_kv(kc_ref, seg("kc"))
    put_kv(vc_ref, seg("vc"))
    put_kv(ks_ref, rope(seg("ks")))
    put_kv(vs_ref, seg("vs"))
    put_kv(kw_ref, rope(seg("kw")))
    put_kv(vw_ref, seg("vw"))
    g0 = _SEG["gt"][0]
    zt = jnp.dot(hb, w_ref[:, g0:IN_COLS], preferred_element_type=F32)
    u0, v0 = _SEG["u"][0] - g0, _SEG["v"][0] - g0
    gates_ref[...] = jax.nn.sigmoid(zt[:, :GATE_COLS])
    u_ref[...] = jax.nn.gelu(zt[:, u0:u0 + 1024])
    v_ref[...] = _rms(jax.nn.gelu(zt[:, v0:v0 + 1024]), gsv_ref[...])


def _project(x2d, g_attn, w_in_p, tables, g_sgu_v, tm, kv_seq=None):
    n, d = x2d.shape
    rc, rs1, rs2 = tables
    tt = rc.shape[0]
    nt = tt // tm
    row = lambda w: pl.BlockSpec((tm, w), lambda i: (i, 0))
    tab = pl.BlockSpec((tm, LANES), lambda i: (i % nt, 0))
    out_shapes = [jax.ShapeDtypeStruct((n, NSA_WIDTH), BF16), jax.ShapeDtypeStruct((n, NSA_WIDTH), BF16)]
    if kv_seq is None:
        out_shapes += [jax.ShapeDtypeStruct((n, KV_WIDTH), F32)] * 6
        kv_spec = row(KV_WIDTH)
    else:
        per = kv_seq // tm
        out_shapes += [jax.ShapeDtypeStruct((n // kv_seq, KV_WIDTH, kv_seq), F32)] * 6
        kv_spec = pl.BlockSpec((1, KV_WIDTH, tm), lambda i: (i // per, 0, i % per))
    out_shapes += [jax.ShapeDtypeStruct((n, GATE_COLS), F32), jax.ShapeDtypeStruct((n, 1024), F32),
                   jax.ShapeDtypeStruct((n, 1024), F32)]
    out_specs = [row(NSA_WIDTH), row(NSA_WIDTH)] + [kv_spec] * 6 + [row(GATE_COLS), row(1024), row(1024)]
    return pl.pallas_call(
        functools.partial(_proj_kernel, transpose_kv=kv_seq is not None),
        grid=(n // tm,),
        in_specs=[row(d), _const_spec((1, d)), _const_spec(w_in_p.shape), tab, tab, tab, _const_spec((1, 1024))],
        out_specs=out_specs,
        out_shape=out_shapes,
        compiler_params=_cparams(("parallel",)),
        name="in_proj",
    )(x2d, g_attn, w_in_p, rc, rs1, rs2, g_sgu_v)


def _rope_tables(pos):
    half = ROT_DIM // 2
    freqs = ROPE_THETA ** (-jnp.arange(half, dtype=F32) / half)
    ang = pos.astype(F32)[:, None] * freqs[None, :]
    cos, sin = jnp.cos(ang), jnp.sin(ang)
    t = pos.shape[0]
    ones = jnp.ones((t, HEAD_DIM - ROT_DIM), F32)
    zeros = jnp.zeros((t, HEAD_DIM - ROT_DIM), F32)
    zh = jnp.zeros((t, half), F32)
    rc = jnp.concatenate([cos, cos, ones], axis=1)
    rs1 = jnp.concatenate([-sin, zh, zeros], axis=1)
    rs2 = jnp.concatenate([zh, sin, zeros], axis=1)
    rep = LANES // HEAD_DIM
    return tuple(jnp.tile(a, (1, rep)) for a in (rc, rs1, rs2))


def _compress_kernel(pt_ref, pool_ref, pe_ref, w1_ref, w2_ref, out_ref, stage, buf, a_ref, sem, *, n_pages, page):
    b = pl.program_id(0)
    nb = pl.num_programs(0)
    n_rows = n_pages * page
    n_sub = n_rows // STRIDE
    slot = b % 2
    n_ct = KV_WIDTH // LANES
    ppl = pool_ref.shape[2] // page

    def page_copy(bb, sl, p):
        q = pt_ref[bb, p]
        src = pool_ref.at[q // ppl, :, pl.ds(pl.multiple_of((q % ppl) * page, page), page)]
        return pltpu.make_async_copy(src, stage.at[sl, :, pl.ds(pl.multiple_of(p * page, page), page)], sem.at[sl])

    def start_all(bb, sl):
        def body(p, carry):
            page_copy(bb, sl, p).start()
            return carry
        lax.fori_loop(0, n_pages, body, 0)

    def wait_all(bb, sl):
        def body(p, carry):
            page_copy(bb, sl, p).wait()
            return carry
        lax.fori_loop(0, n_pages, body, 0)

    @pl.when(b == 0)
    def _():
        start_all(0, 0)

    @pl.when(b + 1 < nb)
    def _():
        start_all(b + 1, 1 - slot)

    wait_all(b, slot)

    tch = min(n_rows, 512)
    for c in range(n_ct):
        for j in range(n_rows // tch):
            rows_t = stage[slot, c * LANES:(c + 1) * LANES, j * tch:(j + 1) * tch].T
            for n in range(tch // STRIDE):
                r0 = (j * (tch // STRIDE) + n) * CMP_ROW_PITCH
                buf[c, r0:r0 + STRIDE, :] = rows_t[n * STRIDE:(n + 1) * STRIDE, :]

    pev = _dot(pe_ref[...], w1_ref[...])
    low = lax.broadcasted_iota(jnp.int32, (n_sub, LANES), 1) < HEAD_DIM
    row = lax.broadcasted_iota(jnp.int32, (n_sub, 4 * HEAD_DIM), 0)
    for c in range(n_ct):
        src = buf.at[c]
        for sp in range(STRIDE // 2):
            x0 = src[pl.ds(2 * sp, n_sub, stride=CMP_ROW_PITCH), :]
            x1 = src[pl.ds(2 * sp + 1, n_sub, stride=CMP_ROW_PITCH), :]
            a_ref[0, :, sp * LANES:(sp + 1) * LANES] = jnp.where(low, x0, pltpu.roll(x1, HEAD_DIM, 1)).astype(BF16)
            a_ref[1, :, sp * LANES:(sp + 1) * LANES] = jnp.where(low, pltpu.roll(x0, HEAD_DIM, 1), x1).astype(BF16)
        for k in range(2):
            pm = jnp.dot(a_ref[k], w1_ref[...], preferred_element_type=F32)
            part0 = pm[:, :CMP_HIDDEN] + pev[0:1, :CMP_HIDDEN]
            part1 = pm[:, CMP_HIDDEN:] + pev[1:2, CMP_HIDDEN:]
            hsum = part0 + pltpu.roll(part1, n_sub - 1, 0)
            o = _dot(jax.nn.silu(hsum), w2_ref[...])
            out_ref[0, 2 * c + k] = jnp.where(row < n_sub - 1, o, 0.0).astype(BF16)


def _compress(pool, page_table, page, pe, w1, w2):
    bsz, n_pages = page_table.shape
    n_rows = n_pages * page
    n_sub = n_rows // STRIDE
    n_ct = KV_WIDTH // LANES
    kdim = STRIDE * HEAD_DIM
    pe8 = jnp.zeros((SUBLANES, kdim), F32).at[:CMP_R].set(pe.reshape(CMP_R, kdim))
    w1c = w1.reshape(CMP_R, kdim, CMP_HIDDEN).transpose(1, 0, 2).reshape(kdim, CMP_R * CMP_HIDDEN).astype(BF16)
    w2t = jnp.tile(w2, (1, 4)).astype(BF16)
    grid_spec = pltpu.PrefetchScalarGridSpec(
        num_scalar_prefetch=1,
        grid=(bsz,),
        in_specs=[pl.BlockSpec(memory_space=pl.ANY),
                  pl.BlockSpec((SUBLANES, kdim), lambda b, pt: (0, 0)),
                  pl.BlockSpec(w1c.shape, lambda b, pt: (0, 0)),
                  pl.BlockSpec(w2t.shape, lambda b, pt: (0, 0))],
        out_specs=pl.BlockSpec((1, KV_HEADS, n_sub, 4 * HEAD_DIM), lambda b, pt: (b, 0, 0, 0)),
        scratch_shapes=[pltpu.VMEM((2, KV_WIDTH, n_rows), F32), pltpu.VMEM((n_ct, n_sub * CMP_ROW_PITCH, LANES), F32),
                        pltpu.VMEM((2, n_sub, kdim), BF16), pltpu.SemaphoreType.DMA((2,))],
    )
    return pl.pallas_call(
        functools.partial(_compress_kernel, n_pages=n_pages, page=page),
        grid_spec=grid_spec,
        out_shape=jax.ShapeDtypeStruct((bsz, KV_HEADS, n_sub, 4 * HEAD_DIM), BF16),
        compiler_params=_cparams(("arbitrary",)),
        name="compress",
    )(page_table, pool, pe8, w1c, w2t)


def _cmpsel_kernel(q_ref, k_ref, v_ref, st_ref, o_ref, sc_ref, *, pos0, nc, ns, tq):
    i = pl.program_id(2)
    n_sub = k_ref.shape[2]
    ns_pad = st_ref.shape[0]
    gw = Q_PER_KV * HEAD_DIM
    lane_head = lax.broadcasted_iota(jnp.int32, (tq, gw), 1) // HEAD_DIM
    qpos = pos0 + i * tq + lax.broadcasted_iota(jnp.int32, (tq, 1), 0)
    kidx = lax.broadcasted_iota(jnp.int32, (1, n_sub), 1)
    mask = (kidx * STRIDE + (L_CMP - 1) <= qpos) & (kidx < nc)
    mask_rows = jnp.concatenate([mask] * Q_PER_KV, axis=0)
    blk = lax.broadcasted_iota(jnp.int32, (ns_pad, tq), 0)
    qpos_t = pos0 + i * tq + lax.broadcasted_iota(jnp.int32, (ns_pad, tq), 1)
    cur = qpos_t // L_SEL
    forced = (blk == 0) | (blk == cur) | (blk == cur - 1)
    future = blk * L_SEL > qpos_t
    st = st_ref[...]
    for g in range(k_ref.shape[1]):
        qf = q_ref[0, :, g * gw:(g + 1) * gw].astype(F32)
        q_rows = jnp.concatenate([jnp.where(lane_head == qi, qf, 0.0) for qi in range(Q_PER_KV)], axis=0)
        s = _dot_nt(q_rows, k_ref[0, g]) * ATTN_SCALE
        p = _masked_softmax(s, mask_rows)
        o_rows = _dot(p, v_ref[0, g])
        o = None
        pg = None
        for qi in range(Q_PER_KV):
            part = jnp.where(lane_head == qi, o_rows[qi * tq:(qi + 1) * tq], 0.0)
            o = part if o is None else o + part
            pq = p[qi * tq:(qi + 1) * tq]
            pg = pq if pg is None else pg + pq
        o_ref[0, :, g * gw:(g + 1) * gw] = o
        pg_hi = pg.astype(BF16)
        pg_lo = (pg - pg_hi.astype(F32)).astype(BF16)
        ps_t = _dot_nt(st, pg_hi) + _dot_nt(st, pg_lo)
        sc = jnp.where(forced, BIG, jnp.where(future, -BIG, ps_t))
        sc_ref[0, g] = jnp.where(blk < ns, sc, PAD_SCORE)


def _rank_kernel(sc_ref, sel_ref, *, ns, topn):
    sc = sc_ref[0, 0]
    blk = lax.broadcasted_iota(jnp.int32, sc.shape, 0)
    sel_ref[0, 0] = jnp.zeros(sc.shape, F32)

    def body(j, c):
        row = sc_ref[0, 0, pl.ds(j, 1), :]
        ahead = (sc > row) | ((sc == row) & (blk < j))
        rank = jnp.sum(ahead.astype(F32), axis=0, keepdims=True)
        sel_ref[0, 0, pl.ds(j, 1), :] = (rank < topn).astype(F32)
        return c

    lax.fori_loop(0, ns, body, 0)


def _cmp_select(q, kcmp, vcmp, pos0, n_keys):
    bsz, t, _ = q.shape
    n_sub = kcmp.shape[2]
    nc = n_sub - CMP_R + 1
    ns = -(-n_keys // L_SEL)
    ns_pad = -(-ns // SUBLANES) * SUBLANES
    topn = min(TOP_N, ns)
    tq = min(t, 512)
    gw = Q_PER_KV * HEAD_DIM
    gps = KV_HEADS if t < LANES else 1
    cidx = jnp.arange(n_sub)
    st = ((cidx[None, :] // SEL_RATIO == jnp.arange(ns_pad)[:, None]) & (cidx[None, :] % SEL_RATIO < SEL_INNER)
          & (cidx[None, :] < nc)).astype(BF16)
    o_cmp, sc = pl.pallas_call(
        functools.partial(_cmpsel_kernel, pos0=pos0, nc=nc, ns=ns, tq=tq),
        grid=(bsz, KV_HEADS // gps, t // tq),
        in_specs=[pl.BlockSpec((1, tq, gw * gps), lambda b, g, i: (b, i, g)),
                  pl.BlockSpec((1, gps, n_sub, gw), lambda b, g, i: (b, g, 0, 0)),
                  pl.BlockSpec((1, gps, n_sub, gw), lambda b, g, i: (b, g, 0, 0)),
                  pl.BlockSpec((ns_pad, n_sub), lambda b, g, i: (0, 0))],
        out_specs=[pl.BlockSpec((1, tq, gw * gps), lambda b, g, i: (b, i, g)),
                   pl.BlockSpec((1, gps, ns_pad, tq), lambda b, g, i: (b, g, 0, i))],
        out_shape=[jax.ShapeDtypeStruct((bsz, t, NSA_WIDTH), F32),
                   jax.ShapeDtypeStruct((bsz, KV_HEADS, ns_pad, t), F32)],
        compiler_params=_cparams(("parallel", "parallel", "parallel")),
        name="cmp_scores",
    )(q, kcmp, vcmp, st)
    fold = t < LANES
    if fold:
        sc = sc.transpose(2, 0, 1, 3).reshape(1, 1, ns_pad, bsz * KV_HEADS * t)
    nb, ng, _, width = sc.shape
    tl = min(width, 2048)
    sel = pl.pallas_call(
        functools.partial(_rank_kernel, ns=ns, topn=topn),
        grid=(nb, ng, width // tl),
        in_specs=[pl.BlockSpec((1, 1, ns_pad, tl), lambda b, g, i: (b, g, 0, i))],
        out_specs=pl.BlockSpec((1, 1, ns_pad, tl), lambda b, g, i: (b, g, 0, i)),
        out_shape=jax.ShapeDtypeStruct(sc.shape, F32),
        compiler_params=_cparams(("parallel", "parallel", "parallel")),
        name="rank_select",
    )(sc)
    if fold:
        sel = sel.reshape(ns_pad, bsz, KV_HEADS, t).transpose(1, 2, 0, 3)
    return o_cmp, sel


def _pattn_kernel(*refs, mode, t, tq, ck):
    if mode == "sel":
        q_ref, k_ref, v_ref, sel_ref, et_ref, o_ref, kc_scr, vt_scr, m_scr, l_scr, acc_scr = refs
    else:
        q_ref, k_ref, v_ref, o_ref, kc_scr, vt_scr, m_scr, l_scr, acc_scr = refs
    i = pl.program_id(2)
    n_chunks = t // ck
    last = (i + 1) * (tq // ck) - 1

    @pl.when(i == 0)
    def _():
        zeros = jnp.zeros((ck, LANES - HEAD_DIM), F32)
        for c in range(n_chunks):
            kc_scr[c] = jnp.concatenate([k_ref[0, :, c * ck:(c + 1) * ck].T, zeros], axis=1).astype(BF16)
            vt_scr[c] = v_ref[0, :, c * ck:(c + 1) * ck].astype(BF16)

    qf = q_ref[0].astype(F32) * ATTN_SCALE
    q_heads = []
    for qi in range(Q_PER_KV):
        tile = qf[:, (qi // 2) * LANES:(qi // 2 + 1) * LANES]
        q_heads.append((tile if qi % 2 == 0 else pltpu.roll(tile, HEAD_DIM, 1)).astype(BF16))
    q_all = jnp.concatenate(q_heads, axis=0)
    qpos = i * tq + lax.broadcasted_iota(jnp.int32, (1, tq), 1)
    m_scr[...] = jnp.full(m_scr.shape, NEG_INF, F32)
    l_scr[...] = jnp.zeros(l_scr.shape, F32)
    acc_scr[...] = jnp.zeros(acc_scr.shape, F32)
    if mode == "sel":
        sel_b = sel_ref[0, 0].astype(BF16)
        n_steps = last + 1
    else:
        n_steps = last - jnp.maximum(i * tq - WINDOW, 0) // ck + 1

    def chunk(c):
        kpos = c * ck + lax.broadcasted_iota(jnp.int32, (ck, 1), 0)
        if mode == "sel":
            chosen = jnp.dot(et_ref[c], sel_b, preferred_element_type=F32)
            ok = (chosen > 0.5) & (kpos <= qpos)
        else:
            ok = (kpos <= qpos) & (kpos > qpos - WINDOW)
        bias = jnp.where(ok, 0.0, NEG_INF)
        s = _dot_nt(kc_scr[c], q_all) + jnp.concatenate([bias] * Q_PER_KV, axis=1)
        m_old = m_scr[...]
        m_new = jnp.maximum(m_old, jnp.max(s, axis=0, keepdims=True))
        e = jnp.exp(s - m_new)
        alpha = jnp.exp(m_old - m_new)
        l_scr[...] = l_scr[...] * alpha + jnp.sum(e, axis=0, keepdims=True)
        m_scr[...] = m_new
        acc_scr[...] = acc_scr[...] * alpha + jnp.dot(vt_scr[c], e.astype(BF16), preferred_element_type=F32)

    def body(step, carry):
        chunk(last - step)
        return carry

    lax.fori_loop(0, n_steps, body, 0)
    out = acc_scr[...] * (1.0 / jnp.maximum(l_scr[...], 1e-30))
    o_ref[0] = jnp.concatenate([out[:, qi * tq:(qi + 1) * tq] for qi in range(Q_PER_KV)], axis=0).T


def _prompt_attention(qrot, k, v, sel_t=None):
    bsz, t, _ = qrot.shape
    tq, ck = 512, 512
    gw = Q_PER_KV * HEAD_DIM
    mode = "win" if sel_t is None else "sel"
    assert t % tq == 0 and tq % ck == 0 and WINDOW % ck == 0 and ck % L_SEL == 0
    in_specs = [pl.BlockSpec((1, tq, gw), lambda b, g, i: (b, i, g)),
                pl.BlockSpec((1, HEAD_DIM, t), lambda b, g, i: (b, g, 0)),
                pl.BlockSpec((1, HEAD_DIM, t), lambda b, g, i: (b, g, 0))]
    args = [qrot, k, v]
    if mode == "sel":
        ns_pad = sel_t.shape[2]
        et = (jnp.arange(t)[:, None] // L_SEL == jnp.arange(ns_pad)[None, :]).astype(BF16).reshape(t // ck, ck, ns_pad)
        in_specs += [pl.BlockSpec((1, 1, ns_pad, tq), lambda b, g, i: (b, g, 0, i)),
                     pl.BlockSpec((t // ck, ck, ns_pad), lambda b, g, i: (0, 0, 0))]
        args += [sel_t, et]
    return pl.pallas_call(
        functools.partial(_pattn_kernel, mode=mode, t=t, tq=tq, ck=ck),
        grid=(bsz, KV_HEADS, t // tq),
        in_specs=in_specs,
        out_specs=pl.BlockSpec((1, tq, gw), lambda b, g, i: (b, i, g)),
        out_shape=jax.ShapeDtypeStruct((bsz, t, NSA_WIDTH), F32),
        scratch_shapes=[pltpu.VMEM((t // ck, ck, LANES), BF16), pltpu.VMEM((t // ck, HEAD_DIM, ck), BF16),
                        pltpu.VMEM((1, Q_PER_KV * tq), F32), pltpu.VMEM((1, Q_PER_KV * tq), F32),
                        pltpu.VMEM((HEAD_DIM, Q_PER_KV * tq), F32)],
        compiler_params=_cparams(("parallel", "parallel", "arbitrary")),
        name="prompt_attn_" + mode,
    )(*args)


def _sattn_kernel(pt_ref, qbd_ref, kpool_ref, vpool_ref, ksn_ref, vsn_ref, selc_ref, wk_ref, wv_ref, kwn_ref, vwn_ref,
                  osel_ref, owin_ref, kbuf, vbuf, m_scr, l_scr, acc_scr, sem, *, n_pages, ppc, page, tdec):
    b = pl.program_id(0)
    c = pl.program_id(1)
    nb = pl.num_programs(0)
    nch = n_pages // ppc
    step = b * nch + c
    slot = step % 2
    rows = ppc * page
    ncol = qbd_ref.shape[2]

    def copies(bb, cc, sl, p):
        dst_k = kbuf.at[sl, :, pl.ds(pl.multiple_of(p * page, page), page)]
        dst_v = vbuf.at[sl, :, pl.ds(pl.multiple_of(p * page, page), page)]
        pid = pt_ref[bb, cc * ppc + p]
        return (pltpu.make_async_copy(kpool_ref.at[pid], dst_k, sem.at[0, sl]),
                pltpu.make_async_copy(vpool_ref.at[pid], dst_v, sem.at[1, sl]))

    def start_all(bb, cc, sl):
        def body(p, carry):
            ck, cv = copies(bb, cc, sl, p)
            ck.start()
            cv.start()
            return carry
        lax.fori_loop(0, ppc, body, 0)

    def wait_all(bb, cc, sl):
        def body(p, carry):
            ck, cv = copies(bb, cc, sl, p)
            ck.wait()
            cv.wait()
            return carry
        lax.fori_loop(0, ppc, body, 0)

    @pl.when(step == 0)
    def _():
        start_all(0, 0, 0)

    @pl.when(step + 1 < nb * nch)
    def _():
        nxt = step + 1
        start_all(nxt // nch, nxt % nch, 1 - slot)

    wait_all(b, c, slot)

    @pl.when(c == 0)
    def _():
        m_scr[...] = jnp.full(m_scr.shape, NEG_INF, F32)
        l_scr[...] = jnp.zeros(l_scr.shape, F32)
        acc_scr[...] = jnp.zeros(acc_scr.shape, F32)

    qbd = qbd_ref[0]
    nblk = rows // L_SEL
    s3 = (_dot_tn(kbuf[slot], qbd) * ATTN_SCALE).reshape(nblk, L_SEL, ncol)
    blk0 = pl.multiple_of(c * nblk, SUBLANES)
    chosen = (selc_ref[0, pl.ds(blk0, nblk), :] > 0.5)[:, None, :]
    s3 = jnp.where(chosen, s3, NEG_INF)
    m_old = m_scr[...]
    m_new = jnp.maximum(m_old, jnp.max(jnp.max(s3, axis=0), axis=0, keepdims=True))
    e3 = jnp.where(chosen, jnp.exp(s3 - m_new[None]), 0.0)
    alpha = jnp.exp(m_old - m_new)
    l_scr[...] = l_scr[...] * alpha + jnp.sum(jnp.sum(e3, axis=0), axis=0, keepdims=True)
    m_scr[...] = m_new
    acc_scr[...] = acc_scr[...] * alpha + _dot(vbuf[slot], e3.reshape(rows, ncol))

    @pl.when(c == nch - 1)
    def _():
        tcol = lax.broadcasted_iota(jnp.int32, (tdec, ncol), 1) % tdec
        jrow = lax.broadcasted_iota(jnp.int32, (tdec, ncol), 0)
        causal_new = jrow <= tcol
        sel_last = selc_ref[0, pl.ds(nch * nblk, 1), :] > 0.5
        ok_new = causal_new & sel_last
        s_new = jnp.where(ok_new, _dot(ksn_ref[0], qbd) * ATTN_SCALE, NEG_INF)
        m_old2 = m_scr[...]
        m_fin = jnp.maximum(m_old2, jnp.max(s_new, axis=0, keepdims=True))
        e_new = jnp.where(ok_new, jnp.exp(s_new - m_fin), 0.0)
        alpha2 = jnp.exp(m_old2 - m_fin)
        l_fin = l_scr[...] * alpha2 + jnp.sum(e_new, axis=0, keepdims=True)
        inv = 1.0 / jnp.maximum(l_fin, 1e-30)
        acc = acc_scr[...] * alpha2 + _dot_tn(vsn_ref[0], e_new)
        osel_ref[0] = acc * inv

        wb = wk_ref.shape[2]
        jw = lax.broadcasted_iota(jnp.int32, (wb, ncol), 0)
        tw = lax.broadcasted_iota(jnp.int32, (wb, ncol), 1) % tdec
        ok_c = jw + (WINDOW - wb) > tw
        s_c = jnp.where(ok_c, _dot_tn(wk_ref[0], qbd) * ATTN_SCALE, NEG_INF)
        s_n = jnp.where(causal_new, _dot(kwn_ref[0], qbd) * ATTN_SCALE, NEG_INF)
        m_w = jnp.maximum(jnp.max(s_c, axis=0, keepdims=True), jnp.max(s_n, axis=0, keepdims=True))
        e_c = jnp.where(ok_c, jnp.exp(s_c - m_w), 0.0)
        e_n = jnp.where(causal_new, jnp.exp(s_n - m_w), 0.0)
        inv_w = 1.0 / jnp.maximum(jnp.sum(e_c, axis=0, keepdims=True) + jnp.sum(e_n, axis=0, keepdims=True), 1e-30)
        ow = _dot(wv_ref[0], e_c) + _dot_tn(vwn_ref[0], e_n)
        owin_ref[0] = ow * inv_w


def _sample_attention(qrot, kpool, vpool, page_table, ks_new, vs_new, sel_t, wk, wv, kw_new, vw_new):
    bsz, tdec, _ = qrot.shape
    n_pages = page_table.shape[1]
    page = kpool.shape[2]
    ppc = min(n_pages, 32)
    assert n_pages % ppc == 0 and page % L_SEL == 0
    nch = n_pages // ppc
    rows = ppc * page
    ncol = KV_HEADS * Q_PER_KV * tdec
    ns_pad = sel_t.shape[2]
    wb = wk.shape[2]
    q5 = qrot.reshape(bsz, tdec, KV_HEADS, Q_PER_KV, HEAD_DIM).transpose(0, 2, 4, 3, 1)
    eye = jnp.eye(KV_HEADS, dtype=qrot.dtype)
    qbd = (q5[:, :, :, None] * eye[None, :, None, :, None, None]).reshape(bsz, KV_WIDTH, ncol)
    selc = jnp.broadcast_to(sel_t.transpose(0, 2, 1, 3)[:, :, :, None, :], (bsz, ns_pad, KV_HEADS, Q_PER_KV, tdec))
    selc = selc.reshape(bsz, ns_pad, ncol)
    per_b = lambda shape: pl.BlockSpec((1,) + shape, lambda b, c, pt: (b, 0, 0))
    grid_spec = pltpu.PrefetchScalarGridSpec(
        num_scalar_prefetch=1,
        grid=(bsz, nch),
        in_specs=[per_b((KV_WIDTH, ncol)),
                  pl.BlockSpec(memory_space=pl.ANY), pl.BlockSpec(memory_space=pl.ANY),
                  per_b((tdec, KV_WIDTH)), per_b((tdec, KV_WIDTH)),
                  per_b((ns_pad, ncol)),
                  per_b((KV_WIDTH, wb)), per_b((KV_WIDTH, wb)),
                  per_b((tdec, KV_WIDTH)), per_b((tdec, KV_WIDTH))],
        out_specs=[per_b((KV_WIDTH, ncol)), per_b((KV_WIDTH, ncol))],
        scratch_shapes=[pltpu.VMEM((2, KV_WIDTH, rows), F32), pltpu.VMEM((2, KV_WIDTH, rows), F32),
                        pltpu.VMEM((1, ncol), F32), pltpu.VMEM((1, ncol), F32), pltpu.VMEM((KV_WIDTH, ncol), F32),
                        pltpu.SemaphoreType.DMA((2, 2))],
    )
    o_sel, o_win = pl.pallas_call(
        functools.partial(_sattn_kernel, n_pages=n_pages, ppc=ppc, page=page, tdec=tdec),
        grid_spec=grid_spec,
        out_shape=[jax.ShapeDtypeStruct((bsz, KV_WIDTH, ncol), F32)] * 2,
        compiler_params=_cparams(("arbitrary", "arbitrary")),
        name="sample_attn",
    )(page_table, qbd, kpool, vpool, ks_new, vs_new, selc, wk, wv, kw_new, vw_new)

    def unpack(o):
        o6 = o.reshape(bsz, KV_HEADS, HEAD_DIM, KV_HEADS, Q_PER_KV, tdec)
        diag = jnp.stack([o6[:, g, :, g] for g in range(KV_HEADS)], axis=1)
        return diag.transpose(0, 4, 1, 3, 2).reshape(bsz, tdec, NSA_WIDTH)

    return unpack(o_sel), unpack(o_win)


def _mix_kernel(x_ref, oc_ref, os_ref, ow_ref, gt_ref, u_ref, v_ref, wsm_ref, bs_ref, gn_ref, gs_ref, eg_ref, wout_ref,
                o_ref, *, chunk):
    r = x_ref.shape[0]
    g = gt_ref[...]
    g_hi = g.astype(BF16)
    g_lo = (g - g_hi.astype(F32)).astype(BF16)
    onsa = None
    for j, branch in enumerate((oc_ref, os_ref, ow_ref)):
        ge = _dot(g_hi, eg_ref[j]) + _dot(g_lo, eg_ref[j])
        term = ge * branch[...]
        onsa = term if onsa is None else onsa + term
    onsa = _rms(onsa, gn_ref[...])
    ii = lax.broadcasted_iota(jnp.int32, (r, r), 0)
    jj = lax.broadcasted_iota(jnp.int32, (r, r), 1)
    tri = (ii // chunk == jj // chunk) & (jj % chunk <= ii % chunk)
    cols = []
    for gi in range(SGU_GROUPS):
        sl = slice(gi * SGU_GROUP_DIM, (gi + 1) * SGU_GROUP_DIM)
        ws = jnp.where(tri, wsm_ref[gi], 0.0)
        mixed = _dot(ws, v_ref[:, sl]) + bs_ref[:, gi:gi + 1]
        cols.append(u_ref[:, sl] * mixed)
    osgu = _rms(jnp.concatenate(cols, axis=1), gs_ref[...])
    o_ref[...] = x_ref[...] + _dot(jnp.concatenate([onsa, osgu], axis=1), wout_ref[...])


def _gate_expanders():
    c = jnp.arange(GATE_COLS)[None, :, None]
    lane = jnp.arange(NSA_WIDTH)[None, None, :]
    j = jnp.arange(3)[:, None, None]
    return (c == (lane // HEAD_DIM) * 3 + j).astype(BF16)


def _mix(x2d, o_cmp, o_sel, o_win, gates, u, v, w_sgu, b_sgu, g_nsa_out, g_sgu_out, w_out_b, chunk, r):
    n, d = x2d.shape
    rep = r // chunk
    pick = (jnp.arange(r)[:, None] % chunk == jnp.arange(chunk)[None, :]).astype(F32)
    wsm = jnp.einsum("ia,gab,jb->gij", pick, w_sgu[:, :chunk, :chunk], pick, precision=lax.Precision.HIGHEST)
    bs = jnp.tile(b_sgu[:, :chunk].T, (rep, 1))
    row = lambda w: pl.BlockSpec((r, w), lambda i: (i, 0))
    return pl.pallas_call(
        functools.partial(_mix_kernel, chunk=chunk),
        grid=(n // r,),
        in_specs=[row(d), row(NSA_WIDTH), row(NSA_WIDTH), row(NSA_WIDTH), row(GATE_COLS), row(1024), row(1024),
                  _const_spec(wsm.shape), _const_spec(bs.shape), _const_spec((1, NSA_WIDTH)), _const_spec((1, 1024)),
                  _const_spec((3, GATE_COLS, NSA_WIDTH)), _const_spec(w_out_b.shape)],
        out_specs=row(d),
        out_shape=jax.ShapeDtypeStruct((n, d), F32),
        compiler_params=_cparams(("parallel",)),
        name="mix_out_proj",
    )(x2d, o_cmp, o_sel, o_win, gates, u, v, wsm, bs, g_nsa_out, g_sgu_out, _gate_expanders(), w_out_b)


def _rms_matmul_kernel(x_ref, g_ref, w_ref, o_ref):
    o_ref[...] = _dot(_rms(x_ref[...], g_ref[...]), w_ref[...])


def _rms_matmul(x2d, g, w_b, tm):
    n, d = x2d.shape
    m = w_b.shape[1]
    return pl.pallas_call(
        _rms_matmul_kernel,
        grid=(n // tm,),
        in_specs=[pl.BlockSpec((tm, d), lambda i: (i, 0)), _const_spec((1, d)), _const_spec(w_b.shape)],
        out_specs=pl.BlockSpec((tm, m), lambda i: (i, 0)),
        out_shape=jax.ShapeDtypeStruct((n, m), F32),
        compiler_params=_cparams(("parallel",)),
        name="rms_matmul",
    )(x2d, g, w_b)


def _mem_heads(q, k_ref, v_ref):
    outs = []
    for h in range(MEM_HEADS):
        sl = slice(h * MEM_HEAD_DIM, (h + 1) * MEM_HEAD_DIM)
        s = _dot_nt(q[:, sl], k_ref[0, :, h, :]) * MEM_SCALE
        p = _masked_softmax(s, jnp.ones(s.shape, dtype=jnp.bool_))
        outs.append(_dot(p, v_ref[0, :, h, :]))
    return jnp.concatenate(outs, axis=1)


def _memattn_kernel(q_ref, k_ref, v_ref, o_ref):
    o_ref[0] = _mem_heads(q_ref[0], k_ref, v_ref)


def _mem_block_kernel(x_ref, g_ref, wq_ref, k_ref, v_ref, wo_ref, gm_ref, wr_ref, o_ref, gtop_ref):
    x = x_ref[0]
    hq = _dot(_rms(x, g_ref[...]), wq_ref[...])
    x2 = x + _dot(_mem_heads(hq, k_ref, v_ref), wo_ref[...])
    o_ref[0] = x2
    gtop_ref[0] = _top_group(x2, gm_ref, wr_ref)


def _mem_block(x3, g_mem, w_q, mk, mv, w_o, g_moe, wr):
    bsz, t, d = x3.shape
    m = mk.shape[1]
    tq = min(t, 512)
    kv_spec = pl.BlockSpec((1, m, MEM_HEADS, MEM_HEAD_DIM), lambda b, i: (b, 0, 0, 0))
    return pl.pallas_call(
        _mem_block_kernel,
        grid=(bsz, t // tq),
        in_specs=[pl.BlockSpec((1, tq, d), lambda b, i: (b, i, 0)), _const_spec((1, d)), _const_spec(w_q.shape),
                  kv_spec, kv_spec, _const_spec(w_o.shape), _const_spec((1, d)), _const_spec(wr.shape)],
        out_specs=[pl.BlockSpec((1, tq, d), lambda b, i: (b, i, 0)), pl.BlockSpec((1, tq, 1), lambda b, i: (b, i, 0))],
        out_shape=[jax.ShapeDtypeStruct((bsz, t, d), F32), jax.ShapeDtypeStruct((bsz, t, 1), jnp.int32)],
        compiler_params=_cparams(("parallel", "parallel")),
        name="mem_block",
    )(x3, g_mem, w_q, mk, mv, w_o, g_moe, wr)


def _mem_attention(hq, mk, mv):
    bsz, t, w = hq.shape
    m = mk.shape[1]
    tq = min(t, 512)
    return pl.pallas_call(
        _memattn_kernel,
        grid=(bsz, t // tq),
        in_specs=[pl.BlockSpec((1, tq, w), lambda b, i: (b, i, 0)),
                  pl.BlockSpec((1, m, MEM_HEADS, MEM_HEAD_DIM), lambda b, i: (b, 0, 0, 0)),
                  pl.BlockSpec((1, m, MEM_HEADS, MEM_HEAD_DIM), lambda b, i: (b, 0, 0, 0))],
        out_specs=pl.BlockSpec((1, tq, w), lambda b, i: (b, i, 0)),
        out_shape=jax.ShapeDtypeStruct((bsz, t, w), F32),
        compiler_params=_cparams(("parallel", "parallel")),
        name="mem_attn",
    )(hq, mk, mv)


def _router_logits(h, wr_ref):
    h_hi = h.astype(BF16)
    h_lo = (h - h_hi.astype(F32)).astype(BF16)
    w_hi, w_lo = wr_ref[0], wr_ref[1]
    return _dot(h_hi, w_hi) + _dot(h_hi, w_lo) + _dot(h_lo, w_hi)


def _top_group(x, g_ref, wr_ref):
    z = _router_logits(_rms(x, g_ref[...]), wr_ref)
    lane = lax.broadcasted_iota(jnp.int32, z.shape, 1)
    zg = jnp.where(lane < N_GROUPS, z, -jnp.inf)
    m = jnp.max(zg, axis=1, keepdims=True)
    first = jnp.min(jnp.where(zg == m, lane.astype(F32), float(LANES)), axis=1, keepdims=True)
    return first.astype(jnp.int32)


def _matmul_res_route_kernel(a_ref, w_ref, r_ref, g_ref, wr_ref, o_ref, gtop_ref):
    x = r_ref[...] + _dot(a_ref[...], w_ref[...])
    o_ref[...] = x
    gtop_ref[...] = _top_group(x, g_ref, wr_ref)


def _matmul_res_route(a2d, w_b, res, g_moe, wr, tm):
    n, k = a2d.shape
    m = w_b.shape[1]
    return pl.pallas_call(
        _matmul_res_route_kernel,
        grid=(n // tm,),
        in_specs=[pl.BlockSpec((tm, k), lambda i: (i, 0)), _const_spec(w_b.shape), pl.BlockSpec((tm, m), lambda i: (i, 0)),
                  _const_spec((1, m)), _const_spec(wr.shape)],
        out_specs=[pl.BlockSpec((tm, m), lambda i: (i, 0)), pl.BlockSpec((tm, 1), lambda i: (i, 0))],
        out_shape=[jax.ShapeDtypeStruct((n, m), F32), jax.ShapeDtypeStruct((n, 1), jnp.int32)],
        compiler_params=_cparams(("parallel",)),
        name="matmul_residual_route",
    )(a2d, w_b, res, g_moe, wr)


def _moe_kernel(src_ref, tg_ref, tv_ref, nv_ref, widx_ref, x_hbm, gm_ref, gf_ref, wr_ref, wg_ref, wu_ref, wd_ref, y_hbm,
                buf, hb_scr, w4_scr, gsem, ssem, *, tm):
    t = pl.program_id(0)
    e = pl.program_id(1)
    nt = pl.num_programs(0)
    slot = t % 2
    other = 1 - slot
    valid = tv_ref[t] == 1
    prev_valid = (t >= 1) & (tv_ref[jnp.maximum(t - 1, 0)] == 1)
    next_valid = (t + 1 < nt) & (tv_ref[jnp.minimum(t + 1, nt - 1)] == 1)

    def gather_start(tt, sl):
        def body(r, c):
            idx = jnp.maximum(src_ref[tt * tm + r], 0)
            pltpu.make_async_copy(x_hbm.at[pl.ds(idx, 1), :], buf.at[sl, pl.ds(r, 1), :], gsem.at[sl]).start()
            return c
        lax.fori_loop(0, tm, body, 0, unroll=8)

    def gather_wait(sl):
        pltpu.make_async_copy(x_hbm.at[pl.ds(0, tm), :], buf.at[sl], gsem.at[sl]).wait()

    def scatter_copy(tt, sl, r):
        return pltpu.make_async_copy(buf.at[sl, pl.ds(r, 1), :], y_hbm.at[pl.ds(src_ref[tt * tm + r], 1), :], ssem.at[0])

    def scatter_start(tt, sl):
        def body(r, c):
            scatter_copy(tt, sl, r).start()
            return c

        @pl.when(nv_ref[tt] == tm)
        def _():
            lax.fori_loop(0, tm, body, 0, unroll=8)

        @pl.when(nv_ref[tt] != tm)
        def _():
            lax.fori_loop(0, nv_ref[tt], body, 0)

    def scatter_wait(tt, sl):
        def body(r, c):
            scatter_copy(tt, sl, r).wait()
            return c

        @pl.when(nv_ref[tt] == tm)
        def _():
            pltpu.make_async_copy(buf.at[sl], y_hbm.at[pl.ds(0, tm), :], ssem.at[0]).wait()

        @pl.when(nv_ref[tt] != tm)
        def _():
            lax.fori_loop(0, nv_ref[tt], body, 0)

    @pl.when(e == 0)
    def _():
        @pl.when(t == 0)
        def _():
            gather_start(0, 0)

        @pl.when(prev_valid)
        def _():
            scatter_start(t - 1, other)

        @pl.when(valid)
        def _():
            gather_wait(slot)
            h = _rms(buf[slot], gm_ref[...])
            hb_scr[...] = h.astype(BF16)
            z = _router_logits(h, wr_ref)
            lane = lax.broadcasted_iota(jnp.int32, z.shape, 1)
            lanef = lane.astype(F32)
            grp = lane < N_GROUPS
            zg = jnp.where(grp, z, -jnp.inf)
            pg_top = 1.0 / jnp.sum(jnp.where(grp, jnp.exp(zg - jnp.max(zg, axis=1, keepdims=True)), 0.0), axis=1, keepdims=True)
            lo = N_GROUPS + tg_ref[t] * EXPERTS_PER_GROUP
            ing = (lane >= lo) & (lane < lo + EXPERTS_PER_GROUP)
            pf = _masked_softmax(z, ing)
            big = float(2 * LANES)
            m1 = jnp.max(jnp.where(ing, pf, -1.0), axis=1, keepdims=True)
            i1 = jnp.min(jnp.where(ing & (pf == m1), lanef, big), axis=1, keepdims=True)
            rest = ing & (lanef != i1)
            m2 = jnp.max(jnp.where(rest, pf, -1.0), axis=1, keepdims=True)
            i2 = jnp.min(jnp.where(rest & (pf == m2), lanef, big), axis=1, keepdims=True)
            tot = m1 + m2
            w4_scr[...] = jnp.where(lanef == i1, m1 / tot * pg_top, jnp.where(lanef == i2, m2 / tot * pg_top, 0.0))

    @pl.when(e == EXPERTS_PER_GROUP // 2)
    def _():
        @pl.when(prev_valid)
        def _():
            scatter_wait(t - 1, other)

        @pl.when(next_valid)
        def _():
            gather_start(t + 1, other)

    @pl.when(valid)
    def _():
        hb = hb_scr[...]
        lane = lax.broadcasted_iota(jnp.int32, w4_scr.shape, 1)
        col = N_GROUPS + tg_ref[t] * EXPERTS_PER_GROUP + e
        we = jnp.sum(jnp.where(lane == col, w4_scr[...], 0.0), axis=1, keepdims=True)
        act = jax.nn.silu(_dot(hb, wg_ref[0])) * _dot(hb, wu_ref[0])
        buf[slot] += _dot(act * we, wd_ref[0])

    @pl.when(valid & (e == EXPERTS_PER_GROUP - 1))
    def _():
        buf[slot] = _rms(buf[slot], gf_ref[...])


def _moe_final(x2d, g_top, g_moe, g_final, wr, w_gate, w_up, w_down, tm):
    n, d = x2d.shape
    n_tiles = n // tm + N_GROUPS
    onehot = (g_top[:, None] == jnp.arange(N_GROUPS)[None, :]).astype(jnp.int32)
    counts = jnp.sum(onehot, axis=0)
    rank = jnp.sum((jnp.cumsum(onehot, axis=0) - onehot) * onehot, axis=1)
    padded = (counts + tm - 1) // tm * tm
    ends = jnp.cumsum(padded)
    base = ends - padded
    pos = base[g_top] + rank
    src = jnp.full((n_tiles * tm,), -1, jnp.int32).at[pos].set(jnp.arange(n, dtype=jnp.int32))
    tile_start = jnp.arange(n_tiles, dtype=jnp.int32) * tm
    tile_valid = (tile_start < ends[-1]).astype(jnp.int32)
    tile_group = jnp.minimum(jnp.sum((tile_start[:, None] >= ends[None, :]).astype(jnp.int32), axis=1), N_GROUPS - 1)
    tile_rows = jnp.clip(ends[tile_group] - padded[tile_group] + counts[tile_group] - tile_start, 0, tm)
    tile_rows = (tile_rows * tile_valid).astype(jnp.int32)
    n_valid = ends[-1] // tm
    last_group = tile_group[jnp.maximum(n_valid - 1, 0)]
    eidx = tile_group[:, None] * EXPERTS_PER_GROUP + jnp.arange(EXPERTS_PER_GROUP, dtype=jnp.int32)[None, :]
    widx = jnp.where(tile_valid[:, None] == 1, eidx, last_group * EXPERTS_PER_GROUP + EXPERTS_PER_GROUP - 1)
    widx = widx.reshape(-1).astype(jnp.int32)

    wmap = lambda t, e, src, tg, tv, nv, wi: (wi[t * EXPERTS_PER_GROUP + e], 0, 0)
    cmap = lambda t, e, src, tg, tv, nv, wi: (0, 0)
    grid_spec = pltpu.PrefetchScalarGridSpec(
        num_scalar_prefetch=5,
        grid=(n_tiles, EXPERTS_PER_GROUP),
        in_specs=[pl.BlockSpec(memory_space=pl.ANY),
                  pl.BlockSpec((1, d), cmap), pl.BlockSpec((1, d), cmap),
                  pl.BlockSpec(wr.shape, lambda t, e, src, tg, tv, nv, wi: (0, 0, 0)),
                  pl.BlockSpec((1, d, EXPERT_FF), wmap), pl.BlockSpec((1, d, EXPERT_FF), wmap),
                  pl.BlockSpec((1, EXPERT_FF, d), wmap)],
        out_specs=pl.BlockSpec(memory_space=pl.ANY),
        scratch_shapes=[pltpu.VMEM((2, tm, d), F32), pltpu.VMEM((tm, d), BF16), pltpu.VMEM((tm, LANES), F32),
                        pltpu.SemaphoreType.DMA((2,)), pltpu.SemaphoreType.DMA((1,))],
    )
    return pl.pallas_call(
        functools.partial(_moe_kernel, tm=tm),
        grid_spec=grid_spec,
        out_shape=jax.ShapeDtypeStruct((n, d), F32),
        compiler_params=_cparams(("arbitrary", "arbitrary")),
        name="moe_final_norm",
    )(src, tile_group, tile_valid, tile_rows, widx, x2d, g_moe, g_final, wr, w_gate, w_up, w_down)


def _finish(x2d, o_cmp, o_sel, o_win, gates, u, v, mk, mv, bsz, lw, chunk, moe_tm):
    n, d = x2d.shape
    t = n // bsz
    x1 = _mix(x2d, o_cmp, o_sel, o_win, gates, u, v, lw["w_sgu"], lw["b_sgu"], lw["g_nsa_out"], lw["g_sgu_out"],
              lw["w_out"], chunk, min(n, 256))
    if t >= LANES:
        x2, g_top = _mem_block(x1.reshape(bsz, t, d), lw["g_mem_norm"], lw["w_mem_q"], mk, mv, lw["w_mem_o"],
                               lw["g_moe_norm"], lw["w_router"])
        x2, g_top = x2.reshape(n, d), g_top.reshape(n, 1)
    else:
        hq = _rms_matmul(x1, lw["g_mem_norm"], lw["w_mem_q"], min(n, 512))
        o_m = _mem_attention(hq.reshape(bsz, t, -1), mk, mv)
        x2, g_top = _matmul_res_route(o_m.reshape(n, -1), lw["w_mem_o"], x1, lw["g_moe_norm"], lw["w_router"], min(n, 512))
    return _moe_final(x2, g_top[:, 0], lw["g_moe_norm"], lw["g_final"], lw["w_router"], lw["w_exp_gate"], lw["w_exp_up"],
                      lw["w_exp_down"], moe_tm)


def kernel(x_prompt, x_sample, cache_cmp_k, cache_cmp_v, cache_sel_k, cache_sel_v, cache_win_k, cache_win_v, cache_mem_k, cache_mem_v, page_table, mem_prompt, w_in, g_attn_norm, pe_cmp_k, w_cmp_k1, w_cmp_k2, pe_cmp_v, w_cmp_v1, w_cmp_v2, g_sgu_v, w_sgu, b_sgu, g_nsa_out, g_sgu_out, w_out, g_mem_norm, g_mem_src, w_mem_q, w_mem_k, w_mem_v, w_mem_o, g_moe_norm, w_router_group, w_router_expert, w_exp_gate, w_exp_up, w_exp_down, g_final):
    depth = w_in.shape[0]
    assert depth == 1, "single-layer trunk"
    bp, tp, d = x_prompt.shape
    bs, ts, _ = x_sample.shape
    n_pages = page_table.shape[1]
    page = cache_cmp_k.shape[2]
    past = n_pages * page
    assert ts < STRIDE and tp % 256 == 0 and page % STRIDE == 0
    row = lambda a: a[0].reshape(1, -1)

    w_in_p = w_in[0].astype(BF16)
    wr = jnp.concatenate([w_router_group[0], w_router_expert[0],
                          jnp.zeros((d, LANES - N_GROUPS - N_EXPERTS), F32)], axis=1)
    wr_hi = wr.astype(BF16)
    wr = jnp.stack([wr_hi, (wr - wr_hi.astype(F32)).astype(BF16)])
    lw = {
        "w_sgu": w_sgu[0], "b_sgu": b_sgu[0], "g_nsa_out": row(g_nsa_out), "g_sgu_out": row(g_sgu_out),
        "w_out": w_out[0].astype(BF16), "g_mem_norm": row(g_mem_norm), "w_mem_q": w_mem_q[0].astype(BF16),
        "w_mem_o": w_mem_o[0].astype(BF16), "g_moe_norm": row(g_moe_norm), "g_final": g_final.reshape(1, -1),
        "w_router": wr, "w_exp_gate": w_exp_gate[0], "w_exp_up": w_exp_up[0], "w_exp_down": w_exp_down[0],
    }
    g_attn = row(g_attn_norm)
    gsv = row(g_sgu_v)
    kv5 = lambda a, b, t: a.reshape(1, b, t, KV_HEADS, HEAD_DIM)

    np_ = bp * tp
    tabs_p = _rope_tables(jnp.arange(tp, dtype=jnp.int32))
    (q, qrot, kc, vc, ks, vs, kw, vw, gates, u, v) = _project(x_prompt.reshape(np_, d), g_attn, w_in_p, tabs_p, gsv, 256,
                                                              kv_seq=tp)
    pt_p = jnp.arange(np_ // page, dtype=jnp.int32).reshape(bp, tp // page)
    kcmp = _compress(kc, pt_p, page, pe_cmp_k[0], w_cmp_k1[0], w_cmp_k2[0])
    vcmp = _compress(vc, pt_p, page, pe_cmp_v[0], w_cmp_v1[0], w_cmp_v2[0])
    o_cmp, sel_t = _cmp_select(q.reshape(bp, tp, -1), kcmp, vcmp, 0, tp)
    qrot3 = qrot.reshape(bp, tp, -1)
    o_sel = _prompt_attention(qrot3, ks, vs, sel_t)
    o_win = _prompt_attention(qrot3, kw, vw)
    n_mem = mem_prompt.shape[1]
    mem_w = MEM_HEADS * MEM_HEAD_DIM
    mem2d = mem_prompt.reshape(bp * n_mem, d)
    mem4 = lambda a: a.reshape(bp, n_mem, MEM_HEADS, MEM_HEAD_DIM)
    mk_p = mem4(_rms_matmul(mem2d, row(g_mem_src), w_mem_k[0].astype(BF16), min(bp * n_mem, 512)))
    mv_p = mem4(_rms_matmul(mem2d, row(g_mem_src), w_mem_v[0].astype(BF16), min(bp * n_mem, 512)))
    y_p = _finish(x_prompt.reshape(np_, d), o_cmp.reshape(np_, -1), o_sel.reshape(np_, -1), o_win.reshape(np_, -1),
                  gates, u, v, mk_p, mv_p, bp, lw, CHUNK, min(np_, 512))
    wbp = min(WINDOW, tp)
    kv5t = lambda a: a.reshape(bp, KV_HEADS, HEAD_DIM, a.shape[2]).transpose(0, 3, 1, 2)[None]
    outs_p = (kv5t(kc), kv5t(vc), kv5t(ks), kv5t(vs), kv5t(kw[:, :, -wbp:]), kv5t(vw[:, :, -wbp:]), mk_p[None], mv_p[None])

    ns_ = bs * ts
    pos_s = past + jnp.arange(ts, dtype=jnp.int32)
    tabs_s = tuple(jnp.tile(a, (bs, 1)) for a in _rope_tables(pos_s))
    (q, qrot, kc, vc, ks, vs, kw, vw, gates, u, v) = _project(x_sample.reshape(ns_, d), g_attn, w_in_p, tabs_s, gsv, ns_)
    pool = lambda c: c[0].transpose(0, 2, 3, 1).reshape(c.shape[1], KV_WIDTH, page)
    kcmp = _compress(pool(cache_cmp_k), page_table, page, pe_cmp_k[0], w_cmp_k1[0], w_cmp_k2[0])
    vcmp = _compress(pool(cache_cmp_v), page_table, page, pe_cmp_v[0], w_cmp_v1[0], w_cmp_v2[0])
    o_cmp, sel_t = _cmp_select(q.reshape(bs, ts, -1).astype(F32), kcmp, vcmp, past, past + ts)
    wb = cache_win_k.shape[2]
    wk = cache_win_k[0].transpose(0, 2, 3, 1).reshape(bs, KV_WIDTH, wb)
    wv = cache_win_v[0].transpose(0, 2, 3, 1).reshape(bs, KV_WIDTH, wb)
    ks3, vs3, kw3, vw3 = (a.reshape(bs, ts, KV_WIDTH) for a in (ks, vs, kw, vw))
    o_sel, o_win = _sample_attention(qrot.reshape(bs, ts, -1), pool(cache_sel_k), pool(cache_sel_v), page_table,
                                     ks3, vs3, sel_t, wk, wv, kw3, vw3)
    mem_ks, mem_vs = cache_mem_k[0], cache_mem_v[0]
    y_s = _finish(x_sample.reshape(ns_, d), o_cmp.reshape(ns_, -1), o_sel.reshape(ns_, -1), o_win.reshape(ns_, -1),
                  gates, u, v, mem_ks, mem_vs, bs, lw, ts, min(ns_, 128))
    def slide(cache_t, new):
        win = jnp.concatenate([cache_t, new.transpose(0, 2, 1)], axis=2)[:, :, -wb:]
        return win.reshape(bs, KV_HEADS, HEAD_DIM, wb).transpose(0, 3, 1, 2)[None]

    win_k_s = slide(wk, kw3)
    win_v_s = slide(wv, vw3)
    outs_s = (kv5(kc, bs, ts), kv5(vc, bs, ts), kv5(ks, bs, ts), kv5(vs, bs, ts), win_k_s, win_v_s,
              v.reshape(1, bs, ts, -1))

    return (y_p.reshape(bp, tp, d), y_s.reshape(bs, ts, d)) + outs_p + outs_s
```

```python
import functools

import jax
import jax.numpy as jnp
from jax import lax
from jax.experimental import pallas as pl
from jax.experimental.pallas import tpu as pltpu

F32 = jnp.float32
BF16 = jnp.bfloat16

N_HEADS = 16
HEAD_DIM = 64
KV_HEADS = 4
Q_PER_KV = N_HEADS // KV_HEADS
NSA_WIDTH = N_HEADS * HEAD_DIM
KV_WIDTH = KV_HEADS * HEAD_DIM
ROT_DIM = HEAD_DIM // 4
ROPE_THETA = 500000.0
L_CMP = 32
STRIDE = 16
CMP_R = L_CMP // STRIDE
CMP_HIDDEN = 128
L_SEL = 64
SEL_RATIO = L_SEL // STRIDE
SEL_INNER = (L_SEL - L_CMP) // STRIDE + 1
TOP_N = 16
WINDOW = 512
SGU_GROUPS = 8
SGU_GROUP_DIM = 128
CHUNK = 128
MEM_HEADS = 4
MEM_HEAD_DIM = 128
N_GROUPS = 4
EXPERTS_PER_GROUP = 4
N_EXPERTS = N_GROUPS * EXPERTS_PER_GROUP
EXPERT_FF = 512
EPS = 1e-6
NEG_INF = -1e30
BIG = 1e9
PAD_SCORE = -3e38
CMP_ROW_PITCH = STRIDE + 4
ATTN_SCALE = HEAD_DIM ** -0.5
MEM_SCALE = MEM_HEAD_DIM ** -0.5

LANES = 128
SUBLANES = 8
V7X_VMEM_LIMIT_BYTES = 60000 * 1024

PROJ_ROWS = 256
ROW_TILE = 512
ATTN_TILE = 512
RANK_LANES = 2048
MOE_ROWS = 512
MOE_ROWS_SHORT = 128
PAGES_PER_STEP = 32
TRANSPOSE_COLS = 512

GATE_COLS = LANES
N_GATES = 3 * N_HEADS
_SEG = {}
_off = 0
for _name, _w in (("q", NSA_WIDTH), ("kc", KV_WIDTH), ("vc", KV_WIDTH), ("ks", KV_WIDTH), ("vs", KV_WIDTH),
                  ("kw", KV_WIDTH), ("vw", KV_WIDTH), ("gt", N_GATES), ("u", 1024), ("v", 1024)):
    _SEG[_name] = (_off, _w)
    _off += _w
IN_COLS = _off


def _cparams(sem, vmem=V7X_VMEM_LIMIT_BYTES):
    return pltpu.CompilerParams(dimension_semantics=sem, vmem_limit_bytes=vmem)


def _dot(a, b):
    return jnp.dot(a.astype(BF16), b.astype(BF16), preferred_element_type=F32)


def _dot_nt(a, b):
    return lax.dot_general(a.astype(BF16), b.astype(BF16), (((1,), (1,)), ((), ())), preferred_element_type=F32)


def _dot_tn(a, b):
    return lax.dot_general(a.astype(BF16), b.astype(BF16), (((0,), (0,)), ((), ())), preferred_element_type=F32)


def _rms(x, g):
    return x * lax.rsqrt(jnp.mean(x * x, axis=-1, keepdims=True) + EPS) * g


def _masked_softmax(s, mask, axis=-1):
    s = jnp.where(mask, s, NEG_INF)
    m = jnp.max(s, axis=axis, keepdims=True)
    e = jnp.where(mask, jnp.exp(s - m), 0.0)
    return e * (1.0 / jnp.maximum(jnp.sum(e, axis=axis, keepdims=True), 1e-30))


def _const_spec(shape):
    nd = len(shape)
    return pl.BlockSpec(shape, lambda *_: (0,) * nd, pipeline_mode=pl.Buffered(1))


def _proj_kernel(x_ref, g_ref, w_ref, rc_ref, rs1_ref, rs2_ref, gsv_ref,
                 q_ref, qrot_ref, kc_ref, vc_ref, ks_ref, vs_ref, kw_ref, vw_ref, gates_ref, u_ref, v_ref, *, transpose_kv):
    hb = _rms(x_ref[...], g_ref[...]).astype(BF16)

    def put_kv(ref, val):
        if transpose_kv:
            ref[0] = val.T
        else:
            ref[...] = val

    def seg(name):
        lo, width = _SEG[name]
        return jnp.dot(hb, w_ref[:, lo:lo + width], preferred_element_type=F32)

    def rope(z):
        rc, rs1, rs2 = rc_ref[...], rs1_ref[...], rs2_ref[...]
        half = ROT_DIM // 2
        outs = []
        for c in range(z.shape[1] // LANES):
            zc = z[:, c * LANES:(c + 1) * LANES]
            outs.append(zc * rc + pltpu.roll(zc, LANES - half, 1) * rs1 + pltpu.roll(zc, half, 1) * rs2)
        return jnp.concatenate(outs, axis=1)

    q = seg("q")
    q_ref[...] = q.astype(BF16)
    qrot_ref[...] = rope(q).astype(BF16)
    put_kv(kc_ref, seg("kc"))
    put_kv(vc_ref, seg("vc"))
    put_kv(ks_ref, rope(seg("ks")))
    put_kv(vs_ref, seg("vs"))
    put_kv(kw_ref, rope(seg("kw")))
    put_kv(vw_ref, seg("vw"))
    g0 = _SEG["gt"][0]
    zt = jnp.dot(hb, w_ref[:, g0:IN_COLS], preferred_element_type=F32)
    u0, v0 = _SEG["u"][0] - g0, _SEG["v"][0] - g0
    gates_ref[...] = jax.nn.sigmoid(zt[:, :GATE_COLS])
    u_ref[...] = jax.nn.gelu(zt[:, u0:u0 + 1024])
    v_ref[...] = _rms(jax.nn.gelu(zt[:, v0:v0 + 1024]), gsv_ref[...])


def _project(x2d, g_attn, w_in_p, tables, g_sgu_v, tm, kv_seq=None):
    n, d = x2d.shape
    rc, rs1, rs2 = tables
    tt = rc.shape[0]
    nt = tt // tm
    row = lambda w: pl.BlockSpec((tm, w), lambda i: (i, 0))
    tab = pl.BlockSpec((tm, LANES), lambda i: (i % nt, 0))
    out_shapes = [jax.ShapeDtypeStruct((n, NSA_WIDTH), BF16), jax.ShapeDtypeStruct((n, NSA_WIDTH), BF16)]
    if kv_seq is None:
        out_shapes += [jax.ShapeDtypeStruct((n, KV_WIDTH), F32)] * 6
        kv_spec = row(KV_WIDTH)
    else:
        per = kv_seq // tm
        out_shapes += [jax.ShapeDtypeStruct((n // kv_seq, KV_WIDTH, kv_seq), F32)] * 6
        kv_spec = pl.BlockSpec((1, KV_WIDTH, tm), lambda i: (i // per, 0, i % per))
    out_shapes += [jax.ShapeDtypeStruct((n, GATE_COLS), F32), jax.ShapeDtypeStruct((n, 1024), F32),
                   jax.ShapeDtypeStruct((n, 1024), F32)]
    out_specs = [row(NSA_WIDTH), row(NSA_WIDTH)] + [kv_spec] * 6 + [row(GATE_COLS), row(1024), row(1024)]
    return pl.pallas_call(
        functools.partial(_proj_kernel, transpose_kv=kv_seq is not None),
        grid=(n // tm,),
        in_specs=[row(d), _const_spec((1, d)), _const_spec(w_in_p.shape), tab, tab, tab, _const_spec((1, 1024))],
        out_specs=out_specs,
        out_shape=out_shapes,
        compiler_params=_cparams(("parallel",)),
        name="in_proj",
    )(x2d, g_attn, w_in_p, rc, rs1, rs2, g_sgu_v)


def _rope_tables(pos):
    half = ROT_DIM // 2
    freqs = ROPE_THETA ** (-jnp.arange(half, dtype=F32) / half)
    ang = pos.astype(F32)[:, None] * freqs[None, :]
    cos, sin = jnp.cos(ang), jnp.sin(ang)
    t = pos.shape[0]
    ones = jnp.ones((t, HEAD_DIM - ROT_DIM), F32)
    zeros = jnp.zeros((t, HEAD_DIM - ROT_DIM), F32)
    zh = jnp.zeros((t, half), F32)
    rc = jnp.concatenate([cos, cos, ones], axis=1)
    rs1 = jnp.concatenate([-sin, zh, zeros], axis=1)
    rs2 = jnp.concatenate([zh, sin, zeros], axis=1)
    rep = LANES // HEAD_DIM
    return tuple(jnp.tile(a, (1, rep)) for a in (rc, rs1, rs2))


def _compress_kernel(pt_ref, pool_ref, pe_ref, w1_ref, w2_ref, out_ref, stage, buf, a_ref, sem, *, n_pages, page):
    b = pl.program_id(0)
    nb = pl.num_programs(0)
    n_rows = n_pages * page
    n_sub = n_rows // STRIDE
    slot = b % 2
    n_ct = KV_WIDTH // LANES
    ppl = pool_ref.shape[2] // page

    def page_copy(bb, sl, p):
        q = pt_ref[bb, p]
        src = pool_ref.at[q // ppl, :, pl.ds(pl.multiple_of((q % ppl) * page, page), page)]
        return pltpu.make_async_copy(src, stage.at[sl, :, pl.ds(pl.multiple_of(p * page, page), page)], sem.at[sl])

    def start_all(bb, sl):
        def body(p, carry):
            page_copy(bb, sl, p).start()
            return carry
        lax.fori_loop(0, n_pages, body, 0)

    def wait_all(bb, sl):
        def body(p, carry):
            page_copy(bb, sl, p).wait()
            return carry
        lax.fori_loop(0, n_pages, body, 0)

    @pl.when(b == 0)
    def _():
        start_all(0, 0)

    @pl.when(b + 1 < nb)
    def _():
        start_all(b + 1, 1 - slot)

    wait_all(b, slot)

    tch = min(n_rows, TRANSPOSE_COLS)
    for c in range(n_ct):
        for j in range(n_rows // tch):
            rows_t = stage[slot, c * LANES:(c + 1) * LANES, j * tch:(j + 1) * tch].T
            for n in range(tch // STRIDE):
                r0 = (j * (tch // STRIDE) + n) * CMP_ROW_PITCH
                buf[c, r0:r0 + STRIDE, :] = rows_t[n * STRIDE:(n + 1) * STRIDE, :]

    pev = _dot(pe_ref[...], w1_ref[...])
    low = lax.broadcasted_iota(jnp.int32, (n_sub, LANES), 1) < HEAD_DIM
    row = lax.broadcasted_iota(jnp.int32, (n_sub, 4 * HEAD_DIM), 0)
    for c in range(n_ct):
        src = buf.at[c]
        for sp in range(STRIDE // 2):
            x0 = src[pl.ds(2 * sp, n_sub, stride=CMP_ROW_PITCH), :]
            x1 = src[pl.ds(2 * sp + 1, n_sub, stride=CMP_ROW_PITCH), :]
            a_ref[0, :, sp * LANES:(sp + 1) * LANES] = jnp.where(low, x0, pltpu.roll(x1, HEAD_DIM, 1)).astype(BF16)
            a_ref[1, :, sp * LANES:(sp + 1) * LANES] = jnp.where(low, pltpu.roll(x0, HEAD_DIM, 1), x1).astype(BF16)
        for k in range(2):
            pm = jnp.dot(a_ref[k], w1_ref[...], preferred_element_type=F32)
            part0 = pm[:, :CMP_HIDDEN] + pev[0:1, :CMP_HIDDEN]
            part1 = pm[:, CMP_HIDDEN:] + pev[1:2, CMP_HIDDEN:]
            hsum = part0 + pltpu.roll(part1, n_sub - 1, 0)
            o = _dot(jax.nn.silu(hsum), w2_ref[...])
            out_ref[0, 2 * c + k] = jnp.where(row < n_sub - 1, o, 0.0).astype(BF16)


def _compress(pool, page_table, page, pe, w1, w2):
    bsz, n_pages = page_table.shape
    n_rows = n_pages * page
    n_sub = n_rows // STRIDE
    n_ct = KV_WIDTH // LANES
    kdim = STRIDE * HEAD_DIM
    pe8 = jnp.zeros((SUBLANES, kdim), F32).at[:CMP_R].set(pe.reshape(CMP_R, kdim))
    w1c = w1.reshape(CMP_R, kdim, CMP_HIDDEN).transpose(1, 0, 2).reshape(kdim, CMP_R * CMP_HIDDEN).astype(BF16)
    w2t = jnp.tile(w2, (1, 4)).astype(BF16)
    grid_spec = pltpu.PrefetchScalarGridSpec(
        num_scalar_prefetch=1,
        grid=(bsz,),
        in_specs=[pl.BlockSpec(memory_space=pl.ANY),
                  pl.BlockSpec((SUBLANES, kdim), lambda b, pt: (0, 0)),
                  pl.BlockSpec(w1c.shape, lambda b, pt: (0, 0)),
                  pl.BlockSpec(w2t.shape, lambda b, pt: (0, 0))],
        out_specs=pl.BlockSpec((1, KV_HEADS, n_sub, 4 * HEAD_DIM), lambda b, pt: (b, 0, 0, 0)),
        scratch_shapes=[pltpu.VMEM((2, KV_WIDTH, n_rows), F32), pltpu.VMEM((n_ct, n_sub * CMP_ROW_PITCH, LANES), F32),
                        pltpu.VMEM((2, n_sub, kdim), BF16), pltpu.SemaphoreType.DMA((2,))],
    )
    return pl.pallas_call(
        functools.partial(_compress_kernel, n_pages=n_pages, page=page),
        grid_spec=grid_spec,
        out_shape=jax.ShapeDtypeStruct((bsz, KV_HEADS, n_sub, 4 * HEAD_DIM), BF16),
        compiler_params=_cparams(("arbitrary",)),
        name="compress",
    )(page_table, pool, pe8, w1c, w2t)


def _cmpsel_kernel(q_ref, k_ref, v_ref, st_ref, o_ref, sc_ref, *, pos0, nc, ns, tq):
    i = pl.program_id(2)
    n_sub = k_ref.shape[2]
    ns_pad = st_ref.shape[0]
    gw = Q_PER_KV * HEAD_DIM
    lane_head = lax.broadcasted_iota(jnp.int32, (tq, gw), 1) // HEAD_DIM
    qpos = pos0 + i * tq + lax.broadcasted_iota(jnp.int32, (tq, 1), 0)
    kidx = lax.broadcasted_iota(jnp.int32, (1, n_sub), 1)
    mask = (kidx * STRIDE + (L_CMP - 1) <= qpos) & (kidx < nc)
    mask_rows = jnp.concatenate([mask] * Q_PER_KV, axis=0)
    blk = lax.broadcasted_iota(jnp.int32, (ns_pad, tq), 0)
    qpos_t = pos0 + i * tq + lax.broadcasted_iota(jnp.int32, (ns_pad, tq), 1)
    cur = qpos_t // L_SEL
    forced = (blk == 0) | (blk == cur) | (blk == cur - 1)
    future = blk * L_SEL > qpos_t
    st = st_ref[...]
    for g in range(k_ref.shape[1]):
        qf = q_ref[0, :, g * gw:(g + 1) * gw].astype(F32)
        q_rows = jnp.concatenate([jnp.where(lane_head == qi, qf, 0.0) for qi in range(Q_PER_KV)], axis=0)
        s = _dot_nt(q_rows, k_ref[0, g]) * ATTN_SCALE
        p = _masked_softmax(s, mask_rows)
        o_rows = _dot(p, v_ref[0, g])
        o = None
        pg = None
        for qi in range(Q_PER_KV):
            part = jnp.where(lane_head == qi, o_rows[qi * tq:(qi + 1) * tq], 0.0)
            o = part if o is None else o + part
            pq = p[qi * tq:(qi + 1) * tq]
            pg = pq if pg is None else pg + pq
        o_ref[0, :, g * gw:(g + 1) * gw] = o
        pg_hi = pg.astype(BF16)
        pg_lo = (pg - pg_hi.astype(F32)).astype(BF16)
        ps_t = _dot_nt(st, pg_hi) + _dot_nt(st, pg_lo)
        sc = jnp.where(forced, BIG, jnp.where(future, -BIG, ps_t))
        sc_ref[0, g] = jnp.where(blk < ns, sc, PAD_SCORE)


def _rank_kernel(sc_ref, sel_ref, *, ns, topn):
    sc = sc_ref[0, 0]
    blk = lax.broadcasted_iota(jnp.int32, sc.shape, 0)
    sel_ref[0, 0] = jnp.zeros(sc.shape, F32)

    def body(j, c):
        row = sc_ref[0, 0, pl.ds(j, 1), :]
        ahead = (sc > row) | ((sc == row) & (blk < j))
        rank = jnp.sum(ahead.astype(F32), axis=0, keepdims=True)
        sel_ref[0, 0, pl.ds(j, 1), :] = (rank < topn).astype(F32)
        return c

    lax.fori_loop(0, ns, body, 0)


def _cmp_select(q, kcmp, vcmp, pos0, n_keys):
    bsz, t, _ = q.shape
    n_sub = kcmp.shape[2]
    nc = n_sub - CMP_R + 1
    ns = -(-n_keys // L_SEL)
    ns_pad = -(-ns // SUBLANES) * SUBLANES
    topn = min(TOP_N, ns)
    tq = min(t, ROW_TILE)
    gw = Q_PER_KV * HEAD_DIM
    gps = KV_HEADS if t < LANES else 1
    cidx = jnp.arange(n_sub)
    st = ((cidx[None, :] // SEL_RATIO == jnp.arange(ns_pad)[:, None]) & (cidx[None, :] % SEL_RATIO < SEL_INNER)
          & (cidx[None, :] < nc)).astype(BF16)
    o_cmp, sc = pl.pallas_call(
        functools.partial(_cmpsel_kernel, pos0=pos0, nc=nc, ns=ns, tq=tq),
        grid=(bsz, KV_HEADS // gps, t // tq),
        in_specs=[pl.BlockSpec((1, tq, gw * gps), lambda b, g, i: (b, i, g)),
                  pl.BlockSpec((1, gps, n_sub, gw), lambda b, g, i: (b, g, 0, 0)),
                  pl.BlockSpec((1, gps, n_sub, gw), lambda b, g, i: (b, g, 0, 0)),
                  pl.BlockSpec((ns_pad, n_sub), lambda b, g, i: (0, 0))],
        out_specs=[pl.BlockSpec((1, tq, gw * gps), lambda b, g, i: (b, i, g)),
                   pl.BlockSpec((1, gps, ns_pad, tq), lambda b, g, i: (b, g, 0, i))],
        out_shape=[jax.ShapeDtypeStruct((bsz, t, NSA_WIDTH), F32),
                   jax.ShapeDtypeStruct((bsz, KV_HEADS, ns_pad, t), F32)],
        compiler_params=_cparams(("parallel", "parallel", "parallel")),
        name="cmp_scores",
    )(q, kcmp, vcmp, st)
    fold = t < LANES
    if fold:
        sc = sc.transpose(2, 0, 1, 3).reshape(1, 1, ns_pad, bsz * KV_HEADS * t)
    nb, ng, _, width = sc.shape
    tl = min(width, RANK_LANES)
    sel = pl.pallas_call(
        functools.partial(_rank_kernel, ns=ns, topn=topn),
        grid=(nb, ng, width // tl),
        in_specs=[pl.BlockSpec((1, 1, ns_pad, tl), lambda b, g, i: (b, g, 0, i))],
        out_specs=pl.BlockSpec((1, 1, ns_pad, tl), lambda b, g, i: (b, g, 0, i)),
        out_shape=jax.ShapeDtypeStruct(sc.shape, F32),
        compiler_params=_cparams(("parallel", "parallel", "parallel")),
        name="rank_select",
    )(sc)
    if fold:
        sel = sel.reshape(ns_pad, bsz, KV_HEADS, t).transpose(1, 2, 0, 3)
    return o_cmp, sel


def _pattn_kernel(*refs, mode, t, tq, ck):
    if mode == "sel":
        q_ref, k_ref, v_ref, sel_ref, et_ref, o_ref, kc_scr, vt_scr, m_scr, l_scr, acc_scr = refs
    else:
        q_ref, k_ref, v_ref, o_ref, kc_scr, vt_scr, m_scr, l_scr, acc_scr = refs
    i = pl.program_id(2)
    n_chunks = t // ck
    last = (i + 1) * (tq // ck) - 1

    @pl.when(i == 0)
    def _():
        zeros = jnp.zeros((ck, LANES - HEAD_DIM), F32)
        for c in range(n_chunks):
            kc_scr[c] = jnp.concatenate([k_ref[0, :, c * ck:(c + 1) * ck].T, zeros], axis=1).astype(BF16)
            vt_scr[c] = v_ref[0, :, c * ck:(c + 1) * ck].astype(BF16)

    qf = q_ref[0].astype(F32) * ATTN_SCALE
    q_heads = []
    for qi in range(Q_PER_KV):
        tile = qf[:, (qi // 2) * LANES:(qi // 2 + 1) * LANES]
        q_heads.append((tile if qi % 2 == 0 else pltpu.roll(tile, HEAD_DIM, 1)).astype(BF16))
    q_all = jnp.concatenate(q_heads, axis=0)
    qpos = i * tq + lax.broadcasted_iota(jnp.int32, (1, tq), 1)
    m_scr[...] = jnp.full(m_scr.shape, NEG_INF, F32)
    l_scr[...] = jnp.zeros(l_scr.shape, F32)
    acc_scr[...] = jnp.zeros(acc_scr.shape, F32)
    if mode == "sel":
        sel_b = sel_ref[0, 0].astype(BF16)
        n_steps = last + 1
    else:
        n_steps = last - jnp.maximum(i * tq - WINDOW, 0) // ck + 1

    def chunk(c):
        kpos = c * ck + lax.broadcasted_iota(jnp.int32, (ck, 1), 0)
        if mode == "sel":
            chosen = jnp.dot(et_ref[c], sel_b, preferred_element_type=F32)
            ok = (chosen > 0.5) & (kpos <= qpos)
        else:
            ok = (kpos <= qpos) & (kpos > qpos - WINDOW)
        bias = jnp.where(ok, 0.0, NEG_INF)
        s = _dot_nt(kc_scr[c], q_all) + jnp.concatenate([bias] * Q_PER_KV, axis=1)
        m_old = m_scr[...]
        m_new = jnp.maximum(m_old, jnp.max(s, axis=0, keepdims=True))
        e = jnp.exp(s - m_new)
        alpha = jnp.exp(m_old - m_new)
        l_scr[...] = l_scr[...] * alpha + jnp.sum(e, axis=0, keepdims=True)
        m_scr[...] = m_new
        acc_scr[...] = acc_scr[...] * alpha + jnp.dot(vt_scr[c], e.astype(BF16), preferred_element_type=F32)

    def body(step, carry):
        chunk(last - step)
        return carry

    lax.fori_loop(0, n_steps, body, 0)
    out = acc_scr[...] * (1.0 / jnp.maximum(l_scr[...], 1e-30))
    o_ref[0] = jnp.concatenate([out[:, qi * tq:(qi + 1) * tq] for qi in range(Q_PER_KV)], axis=0).T


def _prompt_attention(qrot, k, v, sel_t=None):
    bsz, t, _ = qrot.shape
    tq, ck = ATTN_TILE, ATTN_TILE
    gw = Q_PER_KV * HEAD_DIM
    mode = "win" if sel_t is None else "sel"
    assert t % tq == 0 and tq % ck == 0 and WINDOW % ck == 0 and ck % L_SEL == 0
    in_specs = [pl.BlockSpec((1, tq, gw), lambda b, g, i: (b, i, g)),
                pl.BlockSpec((1, HEAD_DIM, t), lambda b, g, i: (b, g, 0)),
                pl.BlockSpec((1, HEAD_DIM, t), lambda b, g, i: (b, g, 0))]
    args = [qrot, k, v]
    if mode == "sel":
        ns_pad = sel_t.shape[2]
        et = (jnp.arange(t)[:, None] // L_SEL == jnp.arange(ns_pad)[None, :]).astype(BF16).reshape(t // ck, ck, ns_pad)
        in_specs += [pl.BlockSpec((1, 1, ns_pad, tq), lambda b, g, i: (b, g, 0, i)),
                     pl.BlockSpec((t // ck, ck, ns_pad), lambda b, g, i: (0, 0, 0))]
        args += [sel_t, et]
    return pl.pallas_call(
        functools.partial(_pattn_kernel, mode=mode, t=t, tq=tq, ck=ck),
        grid=(bsz, KV_HEADS, t // tq),
        in_specs=in_specs,
        out_specs=pl.BlockSpec((1, tq, gw), lambda b, g, i: (b, i, g)),
        out_shape=jax.ShapeDtypeStruct((bsz, t, NSA_WIDTH), F32),
        scratch_shapes=[pltpu.VMEM((t // ck, ck, LANES), BF16), pltpu.VMEM((t // ck, HEAD_DIM, ck), BF16),
                        pltpu.VMEM((1, Q_PER_KV * tq), F32), pltpu.VMEM((1, Q_PER_KV * tq), F32),
                        pltpu.VMEM((HEAD_DIM, Q_PER_KV * tq), F32)],
        compiler_params=_cparams(("parallel", "parallel", "arbitrary")),
        name="prompt_attn_" + mode,
    )(*args)


def _sattn_kernel(pt_ref, qbd_ref, kpool_ref, vpool_ref, ksn_ref, vsn_ref, selc_ref, wk_ref, wv_ref, kwn_ref, vwn_ref,
                  osel_ref, owin_ref, kbuf, vbuf, m_scr, l_scr, acc_scr, sem, *, n_pages, ppc, page, tdec):
    b = pl.program_id(0)
    c = pl.program_id(1)
    nb = pl.num_programs(0)
    nch = n_pages // ppc
    step = b * nch + c
    slot = step % 2
    rows = ppc * page
    ncol = qbd_ref.shape[2]

    def copies(bb, cc, sl, p):
        dst_k = kbuf.at[sl, :, pl.ds(pl.multiple_of(p * page, page), page)]
        dst_v = vbuf.at[sl, :, pl.ds(pl.multiple_of(p * page, page), page)]
        pid = pt_ref[bb, cc * ppc + p]
        return (pltpu.make_async_copy(kpool_ref.at[pid], dst_k, sem.at[0, sl]),
                pltpu.make_async_copy(vpool_ref.at[pid], dst_v, sem.at[1, sl]))

    def start_all(bb, cc, sl):
        def body(p, carry):
            ck, cv = copies(bb, cc, sl, p)
            ck.start()
            cv.start()
            return carry
        lax.fori_loop(0, ppc, body, 0)

    def wait_all(bb, cc, sl):
        def body(p, carry):
            ck, cv = copies(bb, cc, sl, p)
            ck.wait()
            cv.wait()
            return carry
        lax.fori_loop(0, ppc, body, 0)

    @pl.when(step == 0)
    def _():
        start_all(0, 0, 0)

    @pl.when(step + 1 < nb * nch)
    def _():
        nxt = step + 1
        start_all(nxt // nch, nxt % nch, 1 - slot)

    wait_all(b, c, slot)

    @pl.when(c == 0)
    def _():
        m_scr[...] = jnp.full(m_scr.shape, NEG_INF, F32)
        l_scr[...] = jnp.zeros(l_scr.shape, F32)
        acc_scr[...] = jnp.zeros(acc_scr.shape, F32)

    qbd = qbd_ref[0]
    nblk = rows // L_SEL
    s3 = (_dot_tn(kbuf[slot], qbd) * ATTN_SCALE).reshape(nblk, L_SEL, ncol)
    blk0 = pl.multiple_of(c * nblk, SUBLANES)
    chosen = (selc_ref[0, pl.ds(blk0, nblk), :] > 0.5)[:, None, :]
    s3 = jnp.where(chosen, s3, NEG_INF)
    m_old = m_scr[...]
    m_new = jnp.maximum(m_old, jnp.max(jnp.max(s3, axis=0), axis=0, keepdims=True))
    e3 = jnp.where(chosen, jnp.exp(s3 - m_new[None]), 0.0)
    alpha = jnp.exp(m_old - m_new)
    l_scr[...] = l_scr[...] * alpha + jnp.sum(jnp.sum(e3, axis=0), axis=0, keepdims=True)
    m_scr[...] = m_new
    acc_scr[...] = acc_scr[...] * alpha + _dot(vbuf[slot], e3.reshape(rows, ncol))

    @pl.when(c == nch - 1)
    def _():
        tcol = lax.broadcasted_iota(jnp.int32, (tdec, ncol), 1) % tdec
        jrow = lax.broadcasted_iota(jnp.int32, (tdec, ncol), 0)
        causal_new = jrow <= tcol
        sel_last = selc_ref[0, pl.ds(nch * nblk, 1), :] > 0.5
        ok_new = causal_new & sel_last
        s_new = jnp.where(ok_new, _dot(ksn_ref[0], qbd) * ATTN_SCALE, NEG_INF)
        m_old2 = m_scr[...]
        m_fin = jnp.maximum(m_old2, jnp.max(s_new, axis=0, keepdims=True))
        e_new = jnp.where(ok_new, jnp.exp(s_new - m_fin), 0.0)
        alpha2 = jnp.exp(m_old2 - m_fin)
        l_fin = l_scr[...] * alpha2 + jnp.sum(e_new, axis=0, keepdims=True)
        inv = 1.0 / jnp.maximum(l_fin, 1e-30)
        acc = acc_scr[...] * alpha2 + _dot_tn(vsn_ref[0], e_new)
        osel_ref[0] = acc * inv

        wb = wk_ref.shape[2]
        jw = lax.broadcasted_iota(jnp.int32, (wb, ncol), 0)
        tw = lax.broadcasted_iota(jnp.int32, (wb, ncol), 1) % tdec
        ok_c = jw + (WINDOW - wb) > tw
        s_c = jnp.where(ok_c, _dot_tn(wk_ref[0], qbd) * ATTN_SCALE, NEG_INF)
        s_n = jnp.where(causal_new, _dot(kwn_ref[0], qbd) * ATTN_SCALE, NEG_INF)
        m_w = jnp.maximum(jnp.max(s_c, axis=0, keepdims=True), jnp.max(s_n, axis=0, keepdims=True))
        e_c = jnp.where(ok_c, jnp.exp(s_c - m_w), 0.0)
        e_n = jnp.where(causal_new, jnp.exp(s_n - m_w), 0.0)
        inv_w = 1.0 / jnp.maximum(jnp.sum(e_c, axis=0, keepdims=True) + jnp.sum(e_n, axis=0, keepdims=True), 1e-30)
        ow = _dot(wv_ref[0], e_c) + _dot_tn(vwn_ref[0], e_n)
        owin_ref[0] = ow * inv_w


def _sample_attention(qrot, kpool, vpool, page_table, ks_new, vs_new, sel_t, wk, wv, kw_new, vw_new):
    bsz, tdec, _ = qrot.shape
    n_pages = page_table.shape[1]
    page = kpool.shape[2]
    ppc = min(n_pages, PAGES_PER_STEP)
    assert n_pages % ppc == 0 and page % L_SEL == 0
    nch = n_pages // ppc
    rows = ppc * page
    ncol = KV_HEADS * Q_PER_KV * tdec
    ns_pad = sel_t.shape[2]
    wb = wk.shape[2]
    q5 = qrot.reshape(bsz, tdec, KV_HEADS, Q_PER_KV, HEAD_DIM).transpose(0, 2, 4, 3, 1)
    eye = jnp.eye(KV_HEADS, dtype=qrot.dtype)
    qbd = (q5[:, :, :, None] * eye[None, :, None, :, None, None]).reshape(bsz, KV_WIDTH, ncol)
    selc = jnp.broadcast_to(sel_t.transpose(0, 2, 1, 3)[:, :, :, None, :], (bsz, ns_pad, KV_HEADS, Q_PER_KV, tdec))
    selc = selc.reshape(bsz, ns_pad, ncol)
    per_b = lambda shape: pl.BlockSpec((1,) + shape, lambda b, c, pt: (b, 0, 0))
    grid_spec = pltpu.PrefetchScalarGridSpec(
        num_scalar_prefetch=1,
        grid=(bsz, nch),
        in_specs=[per_b((KV_WIDTH, ncol)),
                  pl.BlockSpec(memory_space=pl.ANY), pl.BlockSpec(memory_space=pl.ANY),
                  per_b((tdec, KV_WIDTH)), per_b((tdec, KV_WIDTH)),
                  per_b((ns_pad, ncol)),
                  per_b((KV_WIDTH, wb)), per_b((KV_WIDTH, wb)),
                  per_b((tdec, KV_WIDTH)), per_b((tdec, KV_WIDTH))],
        out_specs=[per_b((KV_WIDTH, ncol)), per_b((KV_WIDTH, ncol))],
        scratch_shapes=[pltpu.VMEM((2, KV_WIDTH, rows), F32), pltpu.VMEM((2, KV_WIDTH, rows), F32),
                        pltpu.VMEM((1, ncol), F32), pltpu.VMEM((1, ncol), F32), pltpu.VMEM((KV_WIDTH, ncol), F32),
                        pltpu.SemaphoreType.DMA((2, 2))],
    )
    o_sel, o_win = pl.pallas_call(
        functools.partial(_sattn_kernel, n_pages=n_pages, ppc=ppc, page=page, tdec=tdec),
        grid_spec=grid_spec,
        out_shape=[jax.ShapeDtypeStruct((bsz, KV_WIDTH, ncol), F32)] * 2,
        compiler_params=_cparams(("arbitrary", "arbitrary")),
        name="sample_attn",
    )(page_table, qbd, kpool, vpool, ks_new, vs_new, selc, wk, wv, kw_new, vw_new)

    def unpack(o):
        o6 = o.reshape(bsz, KV_HEADS, HEAD_DIM, KV_HEADS, Q_PER_KV, tdec)
        diag = jnp.stack([o6[:, g, :, g] for g in range(KV_HEADS)], axis=1)
        return diag.transpose(0, 4, 1, 3, 2).reshape(bsz, tdec, NSA_WIDTH)

    return unpack(o_sel), unpack(o_win)


def _mix_kernel(x_ref, oc_ref, os_ref, ow_ref, gt_ref, u_ref, v_ref, wsm_ref, bs_ref, gn_ref, gs_ref, eg_ref, wout_ref,
                o_ref, *, chunk):
    r = x_ref.shape[0]
    g = gt_ref[...]
    g_hi = g.astype(BF16)
    g_lo = (g - g_hi.astype(F32)).astype(BF16)
    onsa = None
    for j, branch in enumerate((oc_ref, os_ref, ow_ref)):
        ge = _dot(g_hi, eg_ref[j]) + _dot(g_lo, eg_ref[j])
        term = ge * branch[...]
        onsa = term if onsa is None else onsa + term
    onsa = _rms(onsa, gn_ref[...])
    ii = lax.broadcasted_iota(jnp.int32, (r, r), 0)
    jj = lax.broadcasted_iota(jnp.int32, (r, r), 1)
    tri = (ii // chunk == jj // chunk) & (jj % chunk <= ii % chunk)
    cols = []
    for gi in range(SGU_GROUPS):
        sl = slice(gi * SGU_GROUP_DIM, (gi + 1) * SGU_GROUP_DIM)
        ws = jnp.where(tri, wsm_ref[gi], 0.0)
        mixed = _dot(ws, v_ref[:, sl]) + bs_ref[:, gi:gi + 1]
        cols.append(u_ref[:, sl] * mixed)
    osgu = _rms(jnp.concatenate(cols, axis=1), gs_ref[...])
    o_ref[...] = x_ref[...] + _dot(jnp.concatenate([onsa, osgu], axis=1), wout_ref[...])


def _gate_expanders():
    c = jnp.arange(GATE_COLS)[None, :, None]
    lane = jnp.arange(NSA_WIDTH)[None, None, :]
    j = jnp.arange(3)[:, None, None]
    return (c == (lane // HEAD_DIM) * 3 + j).astype(BF16)


def _mix(x2d, o_cmp, o_sel, o_win, gates, u, v, w_sgu, b_sgu, g_nsa_out, g_sgu_out, w_out_b, chunk, r):
    n, d = x2d.shape
    rep = r // chunk
    pick = (jnp.arange(r)[:, None] % chunk == jnp.arange(chunk)[None, :]).astype(F32)
    wsm = jnp.einsum("ia,gab,jb->gij", pick, w_sgu[:, :chunk, :chunk], pick, precision=lax.Precision.HIGHEST)
    bs = jnp.tile(b_sgu[:, :chunk].T, (rep, 1))
    row = lambda w: pl.BlockSpec((r, w), lambda i: (i, 0))
    return pl.pallas_call(
        functools.partial(_mix_kernel, chunk=chunk),
        grid=(n // r,),
        in_specs=[row(d), row(NSA_WIDTH), row(NSA_WIDTH), row(NSA_WIDTH), row(GATE_COLS), row(1024), row(1024),
                  _const_spec(wsm.shape), _const_spec(bs.shape), _const_spec((1, NSA_WIDTH)), _const_spec((1, 1024)),
                  _const_spec((3, GATE_COLS, NSA_WIDTH)), _const_spec(w_out_b.shape)],
        out_specs=row(d),
        out_shape=jax.ShapeDtypeStruct((n, d), F32),
        compiler_params=_cparams(("parallel",)),
        name="mix_out_proj",
    )(x2d, o_cmp, o_sel, o_win, gates, u, v, wsm, bs, g_nsa_out, g_sgu_out, _gate_expanders(), w_out_b)


def _rms_matmul_kernel(x_ref, g_ref, w_ref, o_ref):
    o_ref[...] = _dot(_rms(x_ref[...], g_ref[...]), w_ref[...])


def _rms_matmul(x2d, g, w_b, tm):
    n, d = x2d.shape
    m = w_b.shape[1]
    return pl.pallas_call(
        _rms_matmul_kernel,
        grid=(n // tm,),
        in_specs=[pl.BlockSpec((tm, d), lambda i: (i, 0)), _const_spec((1, d)), _const_spec(w_b.shape)],
        out_specs=pl.BlockSpec((tm, m), lambda i: (i, 0)),
        out_shape=jax.ShapeDtypeStruct((n, m), F32),
        compiler_params=_cparams(("parallel",)),
        name="rms_matmul",
    )(x2d, g, w_b)


def _mem_heads(q, k_ref, v_ref):
    outs = []
    for h in range(MEM_HEADS):
        sl = slice(h * MEM_HEAD_DIM, (h + 1) * MEM_HEAD_DIM)
        s = _dot_nt(q[:, sl], k_ref[0, :, h, :]) * MEM_SCALE
        p = _masked_softmax(s, jnp.ones(s.shape, dtype=jnp.bool_))
        outs.append(_dot(p, v_ref[0, :, h, :]))
    return jnp.concatenate(outs, axis=1)


def _memattn_kernel(q_ref, k_ref, v_ref, o_ref):
    o_ref[0] = _mem_heads(q_ref[0], k_ref, v_ref)


def _mem_block_kernel(x_ref, g_ref, wq_ref, k_ref, v_ref, wo_ref, gm_ref, wr_ref, o_ref, gtop_ref):
    x = x_ref[0]
    hq = _dot(_rms(x, g_ref[...]), wq_ref[...])
    x2 = x + _dot(_mem_heads(hq, k_ref, v_ref), wo_ref[...])
    o_ref[0] = x2
    gtop_ref[0] = _top_group(x2, gm_ref, wr_ref)


def _mem_block(x3, g_mem, w_q, mk, mv, w_o, g_moe, wr):
    bsz, t, d = x3.shape
    m = mk.shape[1]
    tq = min(t, ROW_TILE)
    kv_spec = pl.BlockSpec((1, m, MEM_HEADS, MEM_HEAD_DIM), lambda b, i: (b, 0, 0, 0))
    return pl.pallas_call(
        _mem_block_kernel,
        grid=(bsz, t // tq),
        in_specs=[pl.BlockSpec((1, tq, d), lambda b, i: (b, i, 0)), _const_spec((1, d)), _const_spec(w_q.shape),
                  kv_spec, kv_spec, _const_spec(w_o.shape), _const_spec((1, d)), _const_spec(wr.shape)],
        out_specs=[pl.BlockSpec((1, tq, d), lambda b, i: (b, i, 0)), pl.BlockSpec((1, tq, 1), lambda b, i: (b, i, 0))],
        out_shape=[jax.ShapeDtypeStruct((bsz, t, d), F32), jax.ShapeDtypeStruct((bsz, t, 1), jnp.int32)],
        compiler_params=_cparams(("parallel", "parallel")),
        name="mem_block",
    )(x3, g_mem, w_q, mk, mv, w_o, g_moe, wr)


def _mem_attention(hq, mk, mv):
    bsz, t, w = hq.shape
    m = mk.shape[1]
    tq = min(t, ROW_TILE)
    return pl.pallas_call(
        _memattn_kernel,
        grid=(bsz, t // tq),
        in_specs=[pl.BlockSpec((1, tq, w), lambda b, i: (b, i, 0)),
                  pl.BlockSpec((1, m, MEM_HEADS, MEM_HEAD_DIM), lambda b, i: (b, 0, 0, 0)),
                  pl.BlockSpec((1, m, MEM_HEADS, MEM_HEAD_DIM), lambda b, i: (b, 0, 0, 0))],
        out_specs=pl.BlockSpec((1, tq, w), lambda b, i: (b, i, 0)),
        out_shape=jax.ShapeDtypeStruct((bsz, t, w), F32),
        compiler_params=_cparams(("parallel", "parallel")),
        name="mem_attn",
    )(hq, mk, mv)


def _router_logits(h, wr_ref):
    h_hi = h.astype(BF16)
    h_lo = (h - h_hi.astype(F32)).astype(BF16)
    w_hi, w_lo = wr_ref[0], wr_ref[1]
    return _dot(h_hi, w_hi) + _dot(h_hi, w_lo) + _dot(h_lo, w_hi)


def _top_group(x, g_ref, wr_ref):
    z = _router_logits(_rms(x, g_ref[...]), wr_ref)
    lane = lax.broadcasted_iota(jnp.int32, z.shape, 1)
    zg = jnp.where(lane < N_GROUPS, z, -jnp.inf)
    m = jnp.max(zg, axis=1, keepdims=True)
    first = jnp.min(jnp.where(zg == m, lane.astype(F32), float(LANES)), axis=1, keepdims=True)
    return first.astype(jnp.int32)


def _matmul_res_route_kernel(a_ref, w_ref, r_ref, g_ref, wr_ref, o_ref, gtop_ref):
    x = r_ref[...] + _dot(a_ref[...], w_ref[...])
    o_ref[...] = x
    gtop_ref[...] = _top_group(x, g_ref, wr_ref)


def _matmul_res_route(a2d, w_b, res, g_moe, wr, tm):
    n, k = a2d.shape
    m = w_b.shape[1]
    return pl.pallas_call(
        _matmul_res_route_kernel,
        grid=(n // tm,),
        in_specs=[pl.BlockSpec((tm, k), lambda i: (i, 0)), _const_spec(w_b.shape), pl.BlockSpec((tm, m), lambda i: (i, 0)),
                  _const_spec((1, m)), _const_spec(wr.shape)],
        out_specs=[pl.BlockSpec((tm, m), lambda i: (i, 0)), pl.BlockSpec((tm, 1), lambda i: (i, 0))],
        out_shape=[jax.ShapeDtypeStruct((n, m), F32), jax.ShapeDtypeStruct((n, 1), jnp.int32)],
        compiler_params=_cparams(("parallel",)),
        name="matmul_residual_route",
    )(a2d, w_b, res, g_moe, wr)


def _moe_kernel(src_ref, tg_ref, tv_ref, nv_ref, widx_ref, x_hbm, gm_ref, gf_ref, wr_ref, wg_ref, wu_ref, wd_ref, y_hbm,
                buf, hb_scr, w4_scr, gsem, ssem, *, tm):
    t = pl.program_id(0)
    e = pl.program_id(1)
    nt = pl.num_programs(0)
    slot = t % 2
    other = 1 - slot
    valid = tv_ref[t] == 1
    prev_valid = (t >= 1) & (tv_ref[jnp.maximum(t - 1, 0)] == 1)
    next_valid = (t + 1 < nt) & (tv_ref[jnp.minimum(t + 1, nt - 1)] == 1)

    def gather_start(tt, sl):
        def body(r, c):
            idx = jnp.maximum(src_ref[tt * tm + r], 0)
            pltpu.make_async_copy(x_hbm.at[pl.ds(idx, 1), :], buf.at[sl, pl.ds(r, 1), :], gsem.at[sl]).start()
            return c
        lax.fori_loop(0, tm, body, 0, unroll=8)

    def gather_wait(sl):
        pltpu.make_async_copy(x_hbm.at[pl.ds(0, tm), :], buf.at[sl], gsem.at[sl]).wait()

    def scatter_copy(tt, sl, r):
        return pltpu.make_async_copy(buf.at[sl, pl.ds(r, 1), :], y_hbm.at[pl.ds(src_ref[tt * tm + r], 1), :], ssem.at[0])

    def scatter_start(tt, sl):
        def body(r, c):
            scatter_copy(tt, sl, r).start()
            return c

        @pl.when(nv_ref[tt] == tm)
        def _():
            lax.fori_loop(0, tm, body, 0, unroll=8)

        @pl.when(nv_ref[tt] != tm)
        def _():
            lax.fori_loop(0, nv_ref[tt], body, 0)

    def scatter_wait(tt, sl):
        def body(r, c):
            scatter_copy(tt, sl, r).wait()
            return c

        @pl.when(nv_ref[tt] == tm)
        def _():
            pltpu.make_async_copy(buf.at[sl], y_hbm.at[pl.ds(0, tm), :], ssem.at[0]).wait()

        @pl.when(nv_ref[tt] != tm)
        def _():
            lax.fori_loop(0, nv_ref[tt], body, 0)

    @pl.when(e == 0)
    def _():
        @pl.when(t == 0)
        def _():
            gather_start(0, 0)

        @pl.when(prev_valid)
        def _():
            scatter_start(t - 1, other)

        @pl.when(valid)
        def _():
            gather_wait(slot)
            h = _rms(buf[slot], gm_ref[...])
            hb_scr[...] = h.astype(BF16)
            z = _router_logits(h, wr_ref)
            lane = lax.broadcasted_iota(jnp.int32, z.shape, 1)
            lanef = lane.astype(F32)
            grp = lane < N_GROUPS
            zg = jnp.where(grp, z, -jnp.inf)
            pg_top = 1.0 / jnp.sum(jnp.where(grp, jnp.exp(zg - jnp.max(zg, axis=1, keepdims=True)), 0.0), axis=1, keepdims=True)
            lo = N_GROUPS + tg_ref[t] * EXPERTS_PER_GROUP
            ing = (lane >= lo) & (lane < lo + EXPERTS_PER_GROUP)
            pf = _masked_softmax(z, ing)
            big = float(2 * LANES)
            m1 = jnp.max(jnp.where(ing, pf, -1.0), axis=1, keepdims=True)
            i1 = jnp.min(jnp.where(ing & (pf == m1), lanef, big), axis=1, keepdims=True)
            rest = ing & (lanef != i1)
            m2 = jnp.max(jnp.where(rest, pf, -1.0), axis=1, keepdims=True)
            i2 = jnp.min(jnp.where(rest & (pf == m2), lanef, big), axis=1, keepdims=True)
            tot = m1 + m2
            w4_scr[...] = jnp.where(lanef == i1, m1 / tot * pg_top, jnp.where(lanef == i2, m2 / tot * pg_top, 0.0))

    @pl.when(e == EXPERTS_PER_GROUP // 2)
    def _():
        @pl.when(prev_valid)
        def _():
            scatter_wait(t - 1, other)

        @pl.when(next_valid)
        def _():
            gather_start(t + 1, other)

    @pl.when(valid)
    def _():
        hb = hb_scr[...]
        lane = lax.broadcasted_iota(jnp.int32, w4_scr.shape, 1)
        col = N_GROUPS + tg_ref[t] * EXPERTS_PER_GROUP + e
        we = jnp.sum(jnp.where(lane == col, w4_scr[...], 0.0), axis=1, keepdims=True)
        act = jax.nn.silu(_dot(hb, wg_ref[0])) * _dot(hb, wu_ref[0])
        buf[slot] += _dot(act * we, wd_ref[0])

    @pl.when(valid & (e == EXPERTS_PER_GROUP - 1))
    def _():
        buf[slot] = _rms(buf[slot], gf_ref[...])


def _moe_final(x2d, g_top, g_moe, g_final, wr, w_gate, w_up, w_down, tm):
    n, d = x2d.shape
    n_tiles = n // tm + N_GROUPS
    onehot = (g_top[:, None] == jnp.arange(N_GROUPS)[None, :]).astype(jnp.int32)
    counts = jnp.sum(onehot, axis=0)
    rank = jnp.sum((jnp.cumsum(onehot, axis=0) - onehot) * onehot, axis=1)
    padded = (counts + tm - 1) // tm * tm
    ends = jnp.cumsum(padded)
    base = ends - padded
    pos = base[g_top] + rank
    src = jnp.full((n_tiles * tm,), -1, jnp.int32).at[pos].set(jnp.arange(n, dtype=jnp.int32))
    tile_start = jnp.arange(n_tiles, dtype=jnp.int32) * tm
    tile_valid = (tile_start < ends[-1]).astype(jnp.int32)
    tile_group = jnp.minimum(jnp.sum((tile_start[:, None] >= ends[None, :]).astype(jnp.int32), axis=1), N_GROUPS - 1)
    tile_rows = jnp.clip(ends[tile_group] - padded[tile_group] + counts[tile_group] - tile_start, 0, tm)
    tile_rows = (tile_rows * tile_valid).astype(jnp.int32)
    n_valid = ends[-1] // tm
    last_group = tile_group[jnp.maximum(n_valid - 1, 0)]
    eidx = tile_group[:, None] * EXPERTS_PER_GROUP + jnp.arange(EXPERTS_PER_GROUP, dtype=jnp.int32)[None, :]
    widx = jnp.where(tile_valid[:, None] == 1, eidx, last_group * EXPERTS_PER_GROUP + EXPERTS_PER_GROUP - 1)
    widx = widx.reshape(-1).astype(jnp.int32)

    wmap = lambda t, e, src, tg, tv, nv, wi: (wi[t * EXPERTS_PER_GROUP + e], 0, 0)
    cmap = lambda t, e, src, tg, tv, nv, wi: (0, 0)
    grid_spec = pltpu.PrefetchScalarGridSpec(
        num_scalar_prefetch=5,
        grid=(n_tiles, EXPERTS_PER_GROUP),
        in_specs=[pl.BlockSpec(memory_space=pl.ANY),
                  pl.BlockSpec((1, d), cmap), pl.BlockSpec((1, d), cmap),
                  pl.BlockSpec(wr.shape, lambda t, e, src, tg, tv, nv, wi: (0, 0, 0)),
                  pl.BlockSpec((1, d, EXPERT_FF), wmap), pl.BlockSpec((1, d, EXPERT_FF), wmap),
                  pl.BlockSpec((1, EXPERT_FF, d), wmap)],
        out_specs=pl.BlockSpec(memory_space=pl.ANY),
        scratch_shapes=[pltpu.VMEM((2, tm, d), F32), pltpu.VMEM((tm, d), BF16), pltpu.VMEM((tm, LANES), F32),
                        pltpu.SemaphoreType.DMA((2,)), pltpu.SemaphoreType.DMA((1,))],
    )
    return pl.pallas_call(
        functools.partial(_moe_kernel, tm=tm),
        grid_spec=grid_spec,
        out_shape=jax.ShapeDtypeStruct((n, d), F32),
        compiler_params=_cparams(("arbitrary", "arbitrary")),
        name="moe_final_norm",
    )(src, tile_group, tile_valid, tile_rows, widx, x2d, g_moe, g_final, wr, w_gate, w_up, w_down)


def _finish(x2d, o_cmp, o_sel, o_win, gates, u, v, mk, mv, bsz, lw, chunk, moe_tm):
    n, d = x2d.shape
    t = n // bsz
    x1 = _mix(x2d, o_cmp, o_sel, o_win, gates, u, v, lw["w_sgu"], lw["b_sgu"], lw["g_nsa_out"], lw["g_sgu_out"],
              lw["w_out"], chunk, min(n, PROJ_ROWS))
    if t >= LANES:
        x2, g_top = _mem_block(x1.reshape(bsz, t, d), lw["g_mem_norm"], lw["w_mem_q"], mk, mv, lw["w_mem_o"],
                               lw["g_moe_norm"], lw["w_router"])
        x2, g_top = x2.reshape(n, d), g_top.reshape(n, 1)
    else:
        hq = _rms_matmul(x1, lw["g_mem_norm"], lw["w_mem_q"], min(n, ROW_TILE))
        o_m = _mem_attention(hq.reshape(bsz, t, -1), mk, mv)
        x2, g_top = _matmul_res_route(o_m.reshape(n, -1), lw["w_mem_o"], x1, lw["g_moe_norm"], lw["w_router"], min(n, ROW_TILE))
    return _moe_final(x2, g_top[:, 0], lw["g_moe_norm"], lw["g_final"], lw["w_router"], lw["w_exp_gate"], lw["w_exp_up"],
                      lw["w_exp_down"], moe_tm)


def kernel(x_prompt, x_sample, cache_cmp_k, cache_cmp_v, cache_sel_k, cache_sel_v, cache_win_k, cache_win_v, cache_mem_k, cache_mem_v, page_table, mem_prompt, w_in, g_attn_norm, pe_cmp_k, w_cmp_k1, w_cmp_k2, pe_cmp_v, w_cmp_v1, w_cmp_v2, g_sgu_v, w_sgu, b_sgu, g_nsa_out, g_sgu_out, w_out, g_mem_norm, g_mem_src, w_mem_q, w_mem_k, w_mem_v, w_mem_o, g_moe_norm, w_router_group, w_router_expert, w_exp_gate, w_exp_up, w_exp_down, g_final):
    depth = w_in.shape[0]
    assert depth == 1, "single-layer trunk"
    bp, tp, d = x_prompt.shape
    bs, ts, _ = x_sample.shape
    n_pages = page_table.shape[1]
    page = cache_cmp_k.shape[2]
    past = n_pages * page
    assert ts < STRIDE and tp % ATTN_TILE == 0 and page % STRIDE == 0
    row = lambda a: a[0].reshape(1, -1)

    w_in_p = w_in[0].astype(BF16)
    wr = jnp.concatenate([w_router_group[0], w_router_expert[0],
                          jnp.zeros((d, LANES - N_GROUPS - N_EXPERTS), F32)], axis=1)
    wr_hi = wr.astype(BF16)
    wr = jnp.stack([wr_hi, (wr - wr_hi.astype(F32)).astype(BF16)])
    lw = {
        "w_sgu": w_sgu[0], "b_sgu": b_sgu[0], "g_nsa_out": row(g_nsa_out), "g_sgu_out": row(g_sgu_out),
        "w_out": w_out[0].astype(BF16), "g_mem_norm": row(g_mem_norm), "w_mem_q": w_mem_q[0].astype(BF16),
        "w_mem_o": w_mem_o[0].astype(BF16), "g_moe_norm": row(g_moe_norm), "g_final": g_final.reshape(1, -1),
        "w_router": wr, "w_exp_gate": w_exp_gate[0], "w_exp_up": w_exp_up[0], "w_exp_down": w_exp_down[0],
    }
    g_attn = row(g_attn_norm)
    gsv = row(g_sgu_v)
    kv5 = lambda a, b, t: a.reshape(1, b, t, KV_HEADS, HEAD_DIM)

    np_ = bp * tp
    tabs_p = _rope_tables(jnp.arange(tp, dtype=jnp.int32))
    (q, qrot, kc, vc, ks, vs, kw, vw, gates, u, v) = _project(x_prompt.reshape(np_, d), g_attn, w_in_p, tabs_p, gsv, PROJ_ROWS,
                                                              kv_seq=tp)
    pt_p = jnp.arange(np_ // page, dtype=jnp.int32).reshape(bp, tp // page)
    kcmp = _compress(kc, pt_p, page, pe_cmp_k[0], w_cmp_k1[0], w_cmp_k2[0])
    vcmp = _compress(vc, pt_p, page, pe_cmp_v[0], w_cmp_v1[0], w_cmp_v2[0])
    o_cmp, sel_t = _cmp_select(q.reshape(bp, tp, -1), kcmp, vcmp, 0, tp)
    qrot3 = qrot.reshape(bp, tp, -1)
    o_sel = _prompt_attention(qrot3, ks, vs, sel_t)
    o_win = _prompt_attention(qrot3, kw, vw)
    n_mem = mem_prompt.shape[1]
    mem_w = MEM_HEADS * MEM_HEAD_DIM
    mem2d = mem_prompt.reshape(bp * n_mem, d)
    mem4 = lambda a: a.reshape(bp, n_mem, MEM_HEADS, MEM_HEAD_DIM)
    mk_p = mem4(_rms_matmul(mem2d, row(g_mem_src), w_mem_k[0].astype(BF16), min(bp * n_mem, ROW_TILE)))
    mv_p = mem4(_rms_matmul(mem2d, row(g_mem_src), w_mem_v[0].astype(BF16), min(bp * n_mem, ROW_TILE)))
    y_p = _finish(x_prompt.reshape(np_, d), o_cmp.reshape(np_, -1), o_sel.reshape(np_, -1), o_win.reshape(np_, -1),
                  gates, u, v, mk_p, mv_p, bp, lw, CHUNK, min(np_, MOE_ROWS))
    wbp = min(WINDOW, tp)
    kv5t = lambda a: a.reshape(bp, KV_HEADS, HEAD_DIM, a.shape[2]).transpose(0, 3, 1, 2)[None]
    outs_p = (kv5t(kc), kv5t(vc), kv5t(ks), kv5t(vs), kv5t(kw[:, :, -wbp:]), kv5t(vw[:, :, -wbp:]), mk_p[None], mv_p[None])

    ns_ = bs * ts
    pos_s = past + jnp.arange(ts, dtype=jnp.int32)
    tabs_s = tuple(jnp.tile(a, (bs, 1)) for a in _rope_tables(pos_s))
    (q, qrot, kc, vc, ks, vs, kw, vw, gates, u, v) = _project(x_sample.reshape(ns_, d), g_attn, w_in_p, tabs_s, gsv, ns_)
    pool = lambda c: c[0].transpose(0, 2, 3, 1).reshape(c.shape[1], KV_WIDTH, page)
    kcmp = _compress(pool(cache_cmp_k), page_table, page, pe_cmp_k[0], w_cmp_k1[0], w_cmp_k2[0])
    vcmp = _compress(pool(cache_cmp_v), page_table, page, pe_cmp_v[0], w_cmp_v1[0], w_cmp_v2[0])
    o_cmp, sel_t = _cmp_select(q.reshape(bs, ts, -1).astype(F32), kcmp, vcmp, past, past + ts)
    wb = cache_win_k.shape[2]
    wk = cache_win_k[0].transpose(0, 2, 3, 1).reshape(bs, KV_WIDTH, wb)
    wv = cache_win_v[0].transpose(0, 2, 3, 1).reshape(bs, KV_WIDTH, wb)
    ks3, vs3, kw3, vw3 = (a.reshape(bs, ts, KV_WIDTH) for a in (ks, vs, kw, vw))
    o_sel, o_win = _sample_attention(qrot.reshape(bs, ts, -1), pool(cache_sel_k), pool(cache_sel_v), page_table,
                                     ks3, vs3, sel_t, wk, wv, kw3, vw3)
    mem_ks, mem_vs = cache_mem_k[0], cache_mem_v[0]
    y_s = _finish(x_sample.reshape(ns_, d), o_cmp.reshape(ns_, -1), o_sel.reshape(ns_, -1), o_win.reshape(ns_, -1),
                  gates, u, v, mem_ks, mem_vs, bs, lw, ts, min(ns_, MOE_ROWS_SHORT))
    def slide(cache_t, new):
        win = jnp.concatenate([cache_t, new.transpose(0, 2, 1)], axis=2)[:, :, -wb:]
        return win.reshape(bs, KV_HEADS, HEAD_DIM, wb).transpose(0, 3, 1, 2)[None]

    win_k_s = slide(wk, kw3)
    win_v_s = slide(wv, vw3)
    outs_s = (kv5(kc, bs, ts), kv5(vc, bs, ts), kv5(ks, bs, ts), kv5(vs, bs, ts), win_k_s, win_v_s,
              v.reshape(1, bs, ts, -1))

    return (y_p.reshape(bp, tp, d), y_s.reshape(bs, ts, d)) + outs_p + outs_s
```

```python
import functools

import jax
import jax.numpy as jnp
from jax import lax
from jax.experimental import pallas as pl
from jax.experimental.pallas import tpu as pltpu

F32 = jnp.float32
BF16 = jnp.bfloat16

N_HEADS = 16
HEAD_DIM = 64
KV_HEADS = 4
Q_PER_KV = N_HEADS // KV_HEADS
NSA_WIDTH = N_HEADS * HEAD_DIM
KV_WIDTH = KV_HEADS * HEAD_DIM
ROT_DIM = HEAD_DIM // 4
ROPE_THETA = 500000.0
L_CMP = 32
STRIDE = 16
CMP_R = L_CMP // STRIDE
CMP_HIDDEN = 128
L_SEL = 64
SEL_RATIO = L_SEL // STRIDE
SEL_INNER = (L_SEL - L_CMP) // STRIDE + 1
TOP_N = 16
WINDOW = 512
SGU_GROUPS = 8
SGU_GROUP_DIM = 128
CHUNK = 128
MEM_HEADS = 4
MEM_HEAD_DIM = 128
N_GROUPS = 4
EXPERTS_PER_GROUP = 4
N_EXPERTS = N_GROUPS * EXPERTS_PER_GROUP
EXPERT_FF = 512
EPS = 1e-6
NEG_INF = -1e30
BIG = 1e9
PAD_SCORE = -3e38
CMP_ROW_PITCH = STRIDE + 4
ATTN_SCALE = HEAD_DIM ** -0.5
MEM_SCALE = MEM_HEAD_DIM ** -0.5

LANES = 128
SUBLANES = 8
V7X_VMEM_LIMIT_BYTES = 60000 * 1024

PROJ_ROWS = 256
ROW_TILE = 512
ATTN_TILE = 512
RANK_LANES = 2048
MOE_ROWS = 512
MOE_ROWS_SHORT = 128
PAGES_PER_STEP = 32
TRANSPOSE_COLS = 512

GATE_COLS = LANES
N_GATES = 3 * N_HEADS
_SEG = {}
_off = 0
for _name, _w in (("q", NSA_WIDTH), ("kc", KV_WIDTH), ("vc", KV_WIDTH), ("ks", KV_WIDTH), ("vs", KV_WIDTH),
                  ("kw", KV_WIDTH), ("vw", KV_WIDTH), ("gt", N_GATES), ("u", 1024), ("v", 1024)):
    _SEG[_name] = (_off, _w)
    _off += _w
IN_COLS = _off


def _cparams(sem, vmem=V7X_VMEM_LIMIT_BYTES):
    return pltpu.CompilerParams(dimension_semantics=sem, vmem_limit_bytes=vmem)


def _dot(a, b):
    return jnp.dot(a.astype(BF16), b.astype(BF16), preferred_element_type=F32)


def _dot_nt(a, b):
    return lax.dot_general(a.astype(BF16), b.astype(BF16), (((1,), (1,)), ((), ())), preferred_element_type=F32)


def _dot_tn(a, b):
    return lax.dot_general(a.astype(BF16), b.astype(BF16), (((0,), (0,)), ((), ())), preferred_element_type=F32)


def _rms(x, g):
    return x * lax.rsqrt(jnp.mean(x * x, axis=-1, keepdims=True) + EPS) * g


def _masked_softmax(s, mask, axis=-1):
    s = jnp.where(mask, s, NEG_INF)
    m = jnp.max(s, axis=axis, keepdims=True)
    e = jnp.where(mask, jnp.exp(s - m), 0.0)
    return e * (1.0 / jnp.maximum(jnp.sum(e, axis=axis, keepdims=True), 1e-30))


def _const_spec(shape):
    nd = len(shape)
    return pl.BlockSpec(shape, lambda *_: (0,) * nd, pipeline_mode=pl.Buffered(1))


def _proj_kernel(x_ref, g_ref, w_ref, rc_ref, rs1_ref, rs2_ref, gsv_ref,
                 q_ref, qrot_ref, kc_ref, vc_ref, ks_ref, vs_ref, kw_ref, vw_ref, gates_ref, u_ref, v_ref, *, transpose_kv):
    hb = _rms(x_ref[...], g_ref[...]).astype(BF16)

    def put_kv(ref, val):
        if transpose_kv:
            ref[0] = val.T
        else:
            ref[...] = val

    def seg(name):
        lo, width = _SEG[name]
        return jnp.dot(hb, w_ref[:, lo:lo + width], preferred_element_type=F32)

    def rope(z):
        rc, rs1, rs2 = rc_ref[...], rs1_ref[...], rs2_ref[...]
        half = ROT_DIM // 2
        outs = []
        for c in range(z.shape[1] // LANES):
            zc = z[:, c * LANES:(c + 1) * LANES]
            outs.append(zc * rc + pltpu.roll(zc, LANES - half, 1) * rs1 + pltpu.roll(zc, half, 1) * rs2)
        return jnp.concatenate(outs, axis=1)

    q = seg("q")
    q_ref[...] = q.astype(BF16)
    qrot_ref[...] = rope(q).astype(BF16)
    put_kv(kc_ref, seg("kc"))
    put_kv(vc_ref, seg("vc"))
    put_kv(ks_ref, rope(seg("ks")))
    put_kv(vs_ref, seg("vs"))
    put_kv(kw_ref, rope(seg("kw")))
    put_kv(vw_ref, seg("vw"))
    g0 = _SEG["gt"][0]
    zt = jnp.dot(hb, w_ref[:, g0:IN_COLS], preferred_element_type=F32)
    u0, v0 = _SEG["u"][0] - g0, _SEG["v"][0] - g0
    gates_ref[...] = jax.nn.sigmoid(zt[:, :GATE_COLS])
    u_ref[...] = jax.nn.gelu(zt[:, u0:u0 + 1024])
    v_ref[...] = _rms(jax.nn.gelu(zt[:, v0:v0 + 1024]), gsv_ref[...])


def _project(x2d, g_attn, w_in_p, tables, g_sgu_v, tm, kv_seq=None):
    n, d = x2d.shape
    rc, rs1, rs2 = tables
    tt = rc.shape[0]
    nt = tt // tm
    row = lambda w: pl.BlockSpec((tm, w), lambda i: (i, 0))
    tab = pl.BlockSpec((tm, LANES), lambda i: (i % nt, 0))
    out_shapes = [jax.ShapeDtypeStruct((n, NSA_WIDTH), BF16), jax.ShapeDtypeStruct((n, NSA_WIDTH), BF16)]
    if kv_seq is None:
        out_shapes += [jax.ShapeDtypeStruct((n, KV_WIDTH), F32)] * 6
        kv_spec = row(KV_WIDTH)
    else:
        per = kv_seq // tm
        out_shapes += [jax.ShapeDtypeStruct((n // kv_seq, KV_WIDTH, kv_seq), F32)] * 6
        kv_spec = pl.BlockSpec((1, KV_WIDTH, tm), lambda i: (i // per, 0, i % per))
    out_shapes += [jax.ShapeDtypeStruct((n, GATE_COLS), F32), jax.ShapeDtypeStruct((n, 1024), F32),
                   jax.ShapeDtypeStruct((n, 1024), F32)]
    out_specs = [row(NSA_WIDTH), row(NSA_WIDTH)] + [kv_spec] * 6 + [row(GATE_COLS), row(1024), row(1024)]
    return pl.pallas_call(
        functools.partial(_proj_kernel, transpose_kv=kv_seq is not None),
        grid=(n // tm,),
        in_specs=[row(d), _const_spec((1, d)), _const_spec(w_in_p.shape), tab, tab, tab, _const_spec((1, 1024))],
        out_specs=out_specs,
        out_shape=out_shapes,
        compiler_params=_cparams(("parallel",)),
        name="in_proj",
    )(x2d, g_attn, w_in_p, rc, rs1, rs2, g_sgu_v)


def _rope_tables(pos):
    half = ROT_DIM // 2
    freqs = ROPE_THETA ** (-jnp.arange(half, dtype=F32) / half)
    ang = pos.astype(F32)[:, None] * freqs[None, :]
    cos, sin = jnp.cos(ang), jnp.sin(ang)
    t = pos.shape[0]
    ones = jnp.ones((t, HEAD_DIM - ROT_DIM), F32)
    zeros = jnp.zeros((t, HEAD_DIM - ROT_DIM), F32)
    zh = jnp.zeros((t, half), F32)
    rc = jnp.concatenate([cos, cos, ones], axis=1)
    rs1 = jnp.concatenate([-sin, zh, zeros], axis=1)
    rs2 = jnp.concatenate([zh, sin, zeros], axis=1)
    rep = LANES // HEAD_DIM
    return tuple(jnp.tile(a, (1, rep)) for a in (rc, rs1, rs2))


def _compress_kernel(pt_ref, pool_ref, pe_ref, w1_ref, w2_ref, out_ref, stage, buf, a_ref, sem, *, n_pages, page):
    b = pl.program_id(0)
    nb = pl.num_programs(0)
    n_rows = n_pages * page
    n_sub = n_rows // STRIDE
    slot = b % 2
    n_ct = KV_WIDTH // LANES
    ppl = pool_ref.shape[2] // page

    def page_copy(bb, sl, p):
        q = pt_ref[bb, p]
        src = pool_ref.at[q // ppl, :, pl.ds(pl.multiple_of((q % ppl) * page, page), page)]
        return pltpu.make_async_copy(src, stage.at[sl, :, pl.ds(pl.multiple_of(p * page, page), page)], sem.at[sl])

    def start_all(bb, sl):
        def body(p, carry):
            page_copy(bb, sl, p).start()
            return carry
        lax.fori_loop(0, n_pages, body, 0)

    def wait_all(sl):
        pltpu.make_async_copy(stage.at[1 - sl], stage.at[sl], sem.at[sl]).wait()

    @pl.when(b == 0)
    def _():
        start_all(0, 0)

    @pl.when(b + 1 < nb)
    def _():
        start_all(b + 1, 1 - slot)

    wait_all(slot)

    tch = min(n_rows, TRANSPOSE_COLS)
    for c in range(n_ct):
        for j in range(n_rows // tch):
            rows_t = stage[slot, c * LANES:(c + 1) * LANES, j * tch:(j + 1) * tch].T
            for n in range(tch // STRIDE):
                r0 = (j * (tch // STRIDE) + n) * CMP_ROW_PITCH
                buf[c, r0:r0 + STRIDE, :] = rows_t[n * STRIDE:(n + 1) * STRIDE, :]

    pev = _dot(pe_ref[...], w1_ref[...])
    low = lax.broadcasted_iota(jnp.int32, (n_sub, LANES), 1) < HEAD_DIM
    row = lax.broadcasted_iota(jnp.int32, (n_sub, 4 * HEAD_DIM), 0)
    for c in range(n_ct):
        src = buf.at[c]
        for sp in range(STRIDE // 2):
            x0 = src[pl.ds(2 * sp, n_sub, stride=CMP_ROW_PITCH), :]
            x1 = src[pl.ds(2 * sp + 1, n_sub, stride=CMP_ROW_PITCH), :]
            a_ref[0, :, sp * LANES:(sp + 1) * LANES] = jnp.where(low, x0, pltpu.roll(x1, HEAD_DIM, 1)).astype(BF16)
            a_ref[1, :, sp * LANES:(sp + 1) * LANES] = jnp.where(low, pltpu.roll(x0, HEAD_DIM, 1), x1).astype(BF16)
        for k in range(2):
            pm = jnp.dot(a_ref[k], w1_ref[...], preferred_element_type=F32)
            part0 = pm[:, :CMP_HIDDEN] + pev[0:1, :CMP_HIDDEN]
            part1 = pm[:, CMP_HIDDEN:] + pev[1:2, CMP_HIDDEN:]
            hsum = part0 + pltpu.roll(part1, n_sub - 1, 0)
            o = _dot(jax.nn.silu(hsum), w2_ref[...])
            out_ref[0, 2 * c + k] = jnp.where(row < n_sub - 1, o, 0.0).astype(BF16)


def _compress(pool, page_table, page, pe, w1, w2):
    bsz, n_pages = page_table.shape
    n_rows = n_pages * page
    n_sub = n_rows // STRIDE
    n_ct = KV_WIDTH // LANES
    kdim = STRIDE * HEAD_DIM
    pe8 = jnp.zeros((SUBLANES, kdim), F32).at[:CMP_R].set(pe.reshape(CMP_R, kdim))
    w1c = w1.reshape(CMP_R, kdim, CMP_HIDDEN).transpose(1, 0, 2).reshape(kdim, CMP_R * CMP_HIDDEN).astype(BF16)
    w2t = jnp.tile(w2, (1, 4)).astype(BF16)
    grid_spec = pltpu.PrefetchScalarGridSpec(
        num_scalar_prefetch=1,
        grid=(bsz,),
        in_specs=[pl.BlockSpec(memory_space=pl.ANY),
                  pl.BlockSpec((SUBLANES, kdim), lambda b, pt: (0, 0)),
                  pl.BlockSpec(w1c.shape, lambda b, pt: (0, 0)),
                  pl.BlockSpec(w2t.shape, lambda b, pt: (0, 0))],
        out_specs=pl.BlockSpec((1, KV_HEADS, n_sub, 4 * HEAD_DIM), lambda b, pt: (b, 0, 0, 0)),
        scratch_shapes=[pltpu.VMEM((2, KV_WIDTH, n_rows), F32), pltpu.VMEM((n_ct, n_sub * CMP_ROW_PITCH, LANES), F32),
                        pltpu.VMEM((2, n_sub, kdim), BF16), pltpu.SemaphoreType.DMA((2,))],
    )
    return pl.pallas_call(
        functools.partial(_compress_kernel, n_pages=n_pages, page=page),
        grid_spec=grid_spec,
        out_shape=jax.ShapeDtypeStruct((bsz, KV_HEADS, n_sub, 4 * HEAD_DIM), BF16),
        compiler_params=_cparams(("arbitrary",)),
        name="compress",
    )(page_table, pool, pe8, w1c, w2t)


def _cmpsel_kernel(q_ref, k_ref, v_ref, st_ref, o_ref, sc_ref, *, pos0, nc, ns, tq):
    i = pl.program_id(2)
    n_sub = k_ref.shape[2]
    ns_pad = st_ref.shape[0]
    gw = Q_PER_KV * HEAD_DIM
    lane_head = lax.broadcasted_iota(jnp.int32, (tq, gw), 1) // HEAD_DIM
    qpos = pos0 + i * tq + lax.broadcasted_iota(jnp.int32, (tq, 1), 0)
    kidx = lax.broadcasted_iota(jnp.int32, (1, n_sub), 1)
    mask = (kidx * STRIDE + (L_CMP - 1) <= qpos) & (kidx < nc)
    mask_rows = jnp.concatenate([mask] * Q_PER_KV, axis=0)
    blk = lax.broadcasted_iota(jnp.int32, (ns_pad, tq), 0)
    qpos_t = pos0 + i * tq + lax.broadcasted_iota(jnp.int32, (ns_pad, tq), 1)
    cur = qpos_t // L_SEL
    forced = (blk == 0) | (blk == cur) | (blk == cur - 1)
    future = blk * L_SEL > qpos_t
    st = st_ref[...]
    for g in range(k_ref.shape[1]):
        qf = q_ref[0, :, g * gw:(g + 1) * gw].astype(F32)
        q_rows = jnp.concatenate([jnp.where(lane_head == qi, qf, 0.0) for qi in range(Q_PER_KV)], axis=0)
        s = _dot_nt(q_rows, k_ref[0, g]) * ATTN_SCALE
        p = _masked_softmax(s, mask_rows)
        o_rows = _dot(p, v_ref[0, g])
        o = None
        pg = None
        for qi in range(Q_PER_KV):
            part = jnp.where(lane_head == qi, o_rows[qi * tq:(qi + 1) * tq], 0.0)
            o = part if o is None else o + part
            pq = p[qi * tq:(qi + 1) * tq]
            pg = pq if pg is None else pg + pq
        o_ref[0, :, g * gw:(g + 1) * gw] = o
        pg_hi = pg.astype(BF16)
        pg_lo = (pg - pg_hi.astype(F32)).astype(BF16)
        ps_t = _dot_nt(st, pg_hi) + _dot_nt(st, pg_lo)
        sc = jnp.where(forced, BIG, jnp.where(future, -BIG, ps_t))
        sc_ref[0, g] = jnp.where(blk < ns, sc, PAD_SCORE)


def _rank_kernel(sc_ref, sel_ref, *, ns, topn):
    sc = sc_ref[0, 0]
    blk = lax.broadcasted_iota(jnp.int32, sc.shape, 0)
    sel_ref[0, 0] = jnp.zeros(sc.shape, F32)

    def body(j, c):
        row = sc_ref[0, 0, pl.ds(j, 1), :]
        ahead = (sc > row) | ((sc == row) & (blk < j))
        rank = jnp.sum(ahead.astype(F32), axis=0, keepdims=True)
        sel_ref[0, 0, pl.ds(j, 1), :] = (rank < topn).astype(F32)
        return c

    lax.fori_loop(0, ns, body, 0)


def _cmp_select(q, kcmp, vcmp, pos0, n_keys):
    bsz, t, _ = q.shape
    n_sub = kcmp.shape[2]
    nc = n_sub - CMP_R + 1
    ns = -(-n_keys // L_SEL)
    ns_pad = -(-ns // SUBLANES) * SUBLANES
    topn = min(TOP_N, ns)
    tq = min(t, ROW_TILE)
    gw = Q_PER_KV * HEAD_DIM
    gps = KV_HEADS if t < LANES else 1
    cidx = jnp.arange(n_sub)
    st = ((cidx[None, :] // SEL_RATIO == jnp.arange(ns_pad)[:, None]) & (cidx[None, :] % SEL_RATIO < SEL_INNER)
          & (cidx[None, :] < nc)).astype(BF16)
    o_cmp, sc = pl.pallas_call(
        functools.partial(_cmpsel_kernel, pos0=pos0, nc=nc, ns=ns, tq=tq),
        grid=(bsz, KV_HEADS // gps, t // tq),
        in_specs=[pl.BlockSpec((1, tq, gw * gps), lambda b, g, i: (b, i, g)),
                  pl.BlockSpec((1, gps, n_sub, gw), lambda b, g, i: (b, g, 0, 0)),
                  pl.BlockSpec((1, gps, n_sub, gw), lambda b, g, i: (b, g, 0, 0)),
                  pl.BlockSpec((ns_pad, n_sub), lambda b, g, i: (0, 0))],
        out_specs=[pl.BlockSpec((1, tq, gw * gps), lambda b, g, i: (b, i, g)),
                   pl.BlockSpec((1, gps, ns_pad, tq), lambda b, g, i: (b, g, 0, i))],
        out_shape=[jax.ShapeDtypeStruct((bsz, t, NSA_WIDTH), F32),
                   jax.ShapeDtypeStruct((bsz, KV_HEADS, ns_pad, t), F32)],
        compiler_params=_cparams(("parallel", "parallel", "parallel")),
        name="cmp_scores",
    )(q, kcmp, vcmp, st)
    fold = t < LANES
    if fold:
        sc = sc.transpose(2, 0, 1, 3).reshape(1, 1, ns_pad, bsz * KV_HEADS * t)
    nb, ng, _, width = sc.shape
    tl = min(width, RANK_LANES)
    sel = pl.pallas_call(
        functools.partial(_rank_kernel, ns=ns, topn=topn),
        grid=(nb, ng, width // tl),
        in_specs=[pl.BlockSpec((1, 1, ns_pad, tl), lambda b, g, i: (b, g, 0, i))],
        out_specs=pl.BlockSpec((1, 1, ns_pad, tl), lambda b, g, i: (b, g, 0, i)),
        out_shape=jax.ShapeDtypeStruct(sc.shape, F32),
        compiler_params=_cparams(("parallel", "parallel", "parallel")),
        name="rank_select",
    )(sc)
    if fold:
        sel = sel.reshape(ns_pad, bsz, KV_HEADS, t).transpose(1, 2, 0, 3)
    return o_cmp, sel


def _pattn_kernel(*refs, mode, t, tq, ck):
    if mode == "sel":
        q_ref, k_ref, v_ref, sel_ref, et_ref, o_ref, kc_scr, vt_scr, m_scr, l_scr, acc_scr = refs
    else:
        q_ref, k_ref, v_ref, o_ref, kc_scr, vt_scr, m_scr, l_scr, acc_scr = refs
    i = pl.program_id(2)
    n_chunks = t // ck
    last = (i + 1) * (tq // ck) - 1

    @pl.when(i == 0)
    def _():
        zeros = jnp.zeros((ck, LANES - HEAD_DIM), F32)
        for c in range(n_chunks):
            kc_scr[c] = jnp.concatenate([k_ref[0, :, c * ck:(c + 1) * ck].T, zeros], axis=1).astype(BF16)
            vt_scr[c] = v_ref[0, :, c * ck:(c + 1) * ck].astype(BF16)

    qf = q_ref[0].astype(F32) * ATTN_SCALE
    q_heads = []
    for qi in range(Q_PER_KV):
        tile = qf[:, (qi // 2) * LANES:(qi // 2 + 1) * LANES]
        q_heads.append((tile if qi % 2 == 0 else pltpu.roll(tile, HEAD_DIM, 1)).astype(BF16))
    q_all = jnp.concatenate(q_heads, axis=0)
    qpos = i * tq + lax.broadcasted_iota(jnp.int32, (1, tq), 1)
    m_scr[...] = jnp.full(m_scr.shape, NEG_INF, F32)
    l_scr[...] = jnp.zeros(l_scr.shape, F32)
    acc_scr[...] = jnp.zeros(acc_scr.shape, F32)
    if mode == "sel":
        sel_b = sel_ref[0, 0].astype(BF16)
        n_steps = last + 1
    else:
        n_steps = last - jnp.maximum(i * tq - WINDOW, 0) // ck + 1

    def chunk(c):
        kpos = c * ck + lax.broadcasted_iota(jnp.int32, (ck, 1), 0)
        if mode == "sel":
            chosen = jnp.dot(et_ref[c], sel_b, preferred_element_type=F32)
            ok = (chosen > 0.5) & (kpos <= qpos)
        else:
            ok = (kpos <= qpos) & (kpos > qpos - WINDOW)
        bias = jnp.where(ok, 0.0, NEG_INF)
        s = _dot_nt(kc_scr[c], q_all) + jnp.concatenate([bias] * Q_PER_KV, axis=1)
        m_old = m_scr[...]
        m_new = jnp.maximum(m_old, jnp.max(s, axis=0, keepdims=True))
        e = jnp.exp(s - m_new)
        alpha = jnp.exp(m_old - m_new)
        l_scr[...] = l_scr[...] * alpha + jnp.sum(e, axis=0, keepdims=True)
        m_scr[...] = m_new
        acc_scr[...] = acc_scr[...] * alpha + jnp.dot(vt_scr[c], e.astype(BF16), preferred_element_type=F32)

    def body(step, carry):
        chunk(last - step)
        return carry

    lax.fori_loop(0, n_steps, body, 0)
    out = acc_scr[...] * (1.0 / jnp.maximum(l_scr[...], 1e-30))
    o_ref[0] = jnp.concatenate([out[:, qi * tq:(qi + 1) * tq] for qi in range(Q_PER_KV)], axis=0).T


def _prompt_attention(qrot, k, v, sel_t=None):
    bsz, t, _ = qrot.shape
    tq, ck = ATTN_TILE, ATTN_TILE
    gw = Q_PER_KV * HEAD_DIM
    mode = "win" if sel_t is None else "sel"
    assert t % tq == 0 and tq % ck == 0 and WINDOW % ck == 0 and ck % L_SEL == 0
    in_specs = [pl.BlockSpec((1, tq, gw), lambda b, g, i: (b, i, g)),
                pl.BlockSpec((1, HEAD_DIM, t), lambda b, g, i: (b, g, 0)),
                pl.BlockSpec((1, HEAD_DIM, t), lambda b, g, i: (b, g, 0))]
    args = [qrot, k, v]
    if mode == "sel":
        ns_pad = sel_t.shape[2]
        et = (jnp.arange(t)[:, None] // L_SEL == jnp.arange(ns_pad)[None, :]).astype(BF16).reshape(t // ck, ck, ns_pad)
        in_specs += [pl.BlockSpec((1, 1, ns_pad, tq), lambda b, g, i: (b, g, 0, i)),
                     pl.BlockSpec((t // ck, ck, ns_pad), lambda b, g, i: (0, 0, 0))]
        args += [sel_t, et]
    return pl.pallas_call(
        functools.partial(_pattn_kernel, mode=mode, t=t, tq=tq, ck=ck),
        grid=(bsz, KV_HEADS, t // tq),
        in_specs=in_specs,
        out_specs=pl.BlockSpec((1, tq, gw), lambda b, g, i: (b, i, g)),
        out_shape=jax.ShapeDtypeStruct((bsz, t, NSA_WIDTH), F32),
        scratch_shapes=[pltpu.VMEM((t // ck, ck, LANES), BF16), pltpu.VMEM((t // ck, HEAD_DIM, ck), BF16),
                        pltpu.VMEM((1, Q_PER_KV * tq), F32), pltpu.VMEM((1, Q_PER_KV * tq), F32),
                        pltpu.VMEM((HEAD_DIM, Q_PER_KV * tq), F32)],
        compiler_params=_cparams(("parallel", "parallel", "arbitrary")),
        name="prompt_attn_" + mode,
    )(*args)


def _sattn_kernel(pt_ref, qbd_ref, kpool_ref, vpool_ref, ksn_ref, vsn_ref, selc_ref, wk_ref, wv_ref, kwn_ref, vwn_ref,
                  osel_ref, owin_ref, kbuf, vbuf, m_scr, l_scr, acc_scr, sem, *, n_pages, ppc, page, tdec):
    b = pl.program_id(0)
    c = pl.program_id(1)
    nb = pl.num_programs(0)
    nch = n_pages // ppc
    step = b * nch + c
    slot = step % 2
    rows = ppc * page
    ncol = qbd_ref.shape[2]

    def copies(bb, cc, sl, p):
        dst_k = kbuf.at[sl, :, pl.ds(pl.multiple_of(p * page, page), page)]
        dst_v = vbuf.at[sl, :, pl.ds(pl.multiple_of(p * page, page), page)]
        pid = pt_ref[bb, cc * ppc + p]
        return (pltpu.make_async_copy(kpool_ref.at[pid], dst_k, sem.at[0, sl]),
                pltpu.make_async_copy(vpool_ref.at[pid], dst_v, sem.at[1, sl]))

    def start_all(bb, cc, sl):
        def body(p, carry):
            ck, cv = copies(bb, cc, sl, p)
            ck.start()
            cv.start()
            return carry
        lax.fori_loop(0, ppc, body, 0)

    def wait_all(sl):
        pltpu.make_async_copy(kbuf.at[1 - sl], kbuf.at[sl], sem.at[0, sl]).wait()
        pltpu.make_async_copy(vbuf.at[1 - sl], vbuf.at[sl], sem.at[1, sl]).wait()

    @pl.when(step == 0)
    def _():
        start_all(0, 0, 0)

    @pl.when(step + 1 < nb * nch)
    def _():
        nxt = step + 1
        start_all(nxt // nch, nxt % nch, 1 - slot)

    wait_all(slot)

    @pl.when(c == 0)
    def _():
        m_scr[...] = jnp.full(m_scr.shape, NEG_INF, F32)
        l_scr[...] = jnp.zeros(l_scr.shape, F32)
        acc_scr[...] = jnp.zeros(acc_scr.shape, F32)

    qbd = qbd_ref[0]
    nblk = rows // L_SEL
    s3 = (_dot_tn(kbuf[slot], qbd) * ATTN_SCALE).reshape(nblk, L_SEL, ncol)
    blk0 = pl.multiple_of(c * nblk, SUBLANES)
    chosen = (selc_ref[0, pl.ds(blk0, nblk), :] > 0.5)[:, None, :]
    s3 = jnp.where(chosen, s3, NEG_INF)
    m_old = m_scr[...]
    m_new = jnp.maximum(m_old, jnp.max(jnp.max(s3, axis=0), axis=0, keepdims=True))
    e3 = jnp.where(chosen, jnp.exp(s3 - m_new[None]), 0.0)
    alpha = jnp.exp(m_old - m_new)
    l_scr[...] = l_scr[...] * alpha + jnp.sum(jnp.sum(e3, axis=0), axis=0, keepdims=True)
    m_scr[...] = m_new
    acc_scr[...] = acc_scr[...] * alpha + _dot(vbuf[slot], e3.reshape(rows, ncol))

    @pl.when(c == nch - 1)
    def _():
        tcol = lax.broadcasted_iota(jnp.int32, (tdec, ncol), 1) % tdec
        jrow = lax.broadcasted_iota(jnp.int32, (tdec, ncol), 0)
        causal_new = jrow <= tcol
        sel_last = selc_ref[0, pl.ds(nch * nblk, 1), :] > 0.5
        ok_new = causal_new & sel_last
        s_new = jnp.where(ok_new, _dot(ksn_ref[0], qbd) * ATTN_SCALE, NEG_INF)
        m_old2 = m_scr[...]
        m_fin = jnp.maximum(m_old2, jnp.max(s_new, axis=0, keepdims=True))
        e_new = jnp.where(ok_new, jnp.exp(s_new - m_fin), 0.0)
        alpha2 = jnp.exp(m_old2 - m_fin)
        l_fin = l_scr[...] * alpha2 + jnp.sum(e_new, axis=0, keepdims=True)
        inv = 1.0 / jnp.maximum(l_fin, 1e-30)
        acc = acc_scr[...] * alpha2 + _dot_tn(vsn_ref[0], e_new)
        osel_ref[0] = acc * inv

        wb = wk_ref.shape[2]
        jw = lax.broadcasted_iota(jnp.int32, (wb, ncol), 0)
        tw = lax.broadcasted_iota(jnp.int32, (wb, ncol), 1) % tdec
        ok_c = jw + (WINDOW - wb) > tw
        s_c = jnp.where(ok_c, _dot_tn(wk_ref[0], qbd) * ATTN_SCALE, NEG_INF)
        s_n = jnp.where(causal_new, _dot(kwn_ref[0], qbd) * ATTN_SCALE, NEG_INF)
        m_w = jnp.maximum(jnp.max(s_c, axis=0, keepdims=True), jnp.max(s_n, axis=0, keepdims=True))
        e_c = jnp.where(ok_c, jnp.exp(s_c - m_w), 0.0)
        e_n = jnp.where(causal_new, jnp.exp(s_n - m_w), 0.0)
        inv_w = 1.0 / jnp.maximum(jnp.sum(e_c, axis=0, keepdims=True) + jnp.sum(e_n, axis=0, keepdims=True), 1e-30)
        ow = _dot(wv_ref[0], e_c) + _dot_tn(vwn_ref[0], e_n)
        owin_ref[0] = ow * inv_w


def _sample_attention(qrot, kpool, vpool, page_table, ks_new, vs_new, sel_t, wk, wv, kw_new, vw_new):
    bsz, tdec, _ = qrot.shape
    n_pages = page_table.shape[1]
    page = kpool.shape[2]
    ppc = min(n_pages, PAGES_PER_STEP)
    assert n_pages % ppc == 0 and page % L_SEL == 0
    nch = n_pages // ppc
    rows = ppc * page
    ncol = KV_HEADS * Q_PER_KV * tdec
    ns_pad = sel_t.shape[2]
    wb = wk.shape[2]
    q5 = qrot.reshape(bsz, tdec, KV_HEADS, Q_PER_KV, HEAD_DIM).transpose(0, 2, 4, 3, 1)
    eye = jnp.eye(KV_HEADS, dtype=qrot.dtype)
    qbd = (q5[:, :, :, None] * eye[None, :, None, :, None, None]).reshape(bsz, KV_WIDTH, ncol)
    selc = jnp.broadcast_to(sel_t.transpose(0, 2, 1, 3)[:, :, :, None, :], (bsz, ns_pad, KV_HEADS, Q_PER_KV, tdec))
    selc = selc.reshape(bsz, ns_pad, ncol)
    per_b = lambda shape: pl.BlockSpec((1,) + shape, lambda b, c, pt: (b, 0, 0))
    grid_spec = pltpu.PrefetchScalarGridSpec(
        num_scalar_prefetch=1,
        grid=(bsz, nch),
        in_specs=[per_b((KV_WIDTH, ncol)),
                  pl.BlockSpec(memory_space=pl.ANY), pl.BlockSpec(memory_space=pl.ANY),
                  per_b((tdec, KV_WIDTH)), per_b((tdec, KV_WIDTH)),
                  per_b((ns_pad, ncol)),
                  per_b((KV_WIDTH, wb)), per_b((KV_WIDTH, wb)),
                  per_b((tdec, KV_WIDTH)), per_b((tdec, KV_WIDTH))],
        out_specs=[per_b((KV_WIDTH, ncol)), per_b((KV_WIDTH, ncol))],
        scratch_shapes=[pltpu.VMEM((2, KV_WIDTH, rows), F32), pltpu.VMEM((2, KV_WIDTH, rows), F32),
                        pltpu.VMEM((1, ncol), F32), pltpu.VMEM((1, ncol), F32), pltpu.VMEM((KV_WIDTH, ncol), F32),
                        pltpu.SemaphoreType.DMA((2, 2))],
    )
    o_sel, o_win = pl.pallas_call(
        functools.partial(_sattn_kernel, n_pages=n_pages, ppc=ppc, page=page, tdec=tdec),
        grid_spec=grid_spec,
        out_shape=[jax.ShapeDtypeStruct((bsz, KV_WIDTH, ncol), F32)] * 2,
        compiler_params=_cparams(("arbitrary", "arbitrary")),
        name="sample_attn",
    )(page_table, qbd, kpool, vpool, ks_new, vs_new, selc, wk, wv, kw_new, vw_new)

    def unpack(o):
        o6 = o.reshape(bsz, KV_HEADS, HEAD_DIM, KV_HEADS, Q_PER_KV, tdec)
        diag = jnp.stack([o6[:, g, :, g] for g in range(KV_HEADS)], axis=1)
        return diag.transpose(0, 4, 1, 3, 2).reshape(bsz, tdec, NSA_WIDTH)

    return unpack(o_sel), unpack(o_win)


def _mix_kernel(x_ref, oc_ref, os_ref, ow_ref, gt_ref, u_ref, v_ref, wsm_ref, bs_ref, gn_ref, gs_ref, eg_ref, wout_ref,
                o_ref, *, chunk):
    r = x_ref.shape[0]
    g = gt_ref[...]
    g_hi = g.astype(BF16)
    g_lo = (g - g_hi.astype(F32)).astype(BF16)
    onsa = None
    for j, branch in enumerate((oc_ref, os_ref, ow_ref)):
        ge = _dot(g_hi, eg_ref[j]) + _dot(g_lo, eg_ref[j])
        term = ge * branch[...]
        onsa = term if onsa is None else onsa + term
    onsa = _rms(onsa, gn_ref[...])
    ii = lax.broadcasted_iota(jnp.int32, (r, r), 0)
    jj = lax.broadcasted_iota(jnp.int32, (r, r), 1)
    tri = (ii // chunk == jj // chunk) & (jj % chunk <= ii % chunk)
    cols = []
    for gi in range(SGU_GROUPS):
        sl = slice(gi * SGU_GROUP_DIM, (gi + 1) * SGU_GROUP_DIM)
        ws = jnp.where(tri, wsm_ref[gi], 0.0)
        mixed = _dot(ws, v_ref[:, sl]) + bs_ref[:, gi:gi + 1]
        cols.append(u_ref[:, sl] * mixed)
    osgu = _rms(jnp.concatenate(cols, axis=1), gs_ref[...])
    o_ref[...] = x_ref[...] + _dot(jnp.concatenate([onsa, osgu], axis=1), wout_ref[...])


def _gate_expanders():
    c = jnp.arange(GATE_COLS)[None, :, None]
    lane = jnp.arange(NSA_WIDTH)[None, None, :]
    j = jnp.arange(3)[:, None, None]
    return (c == (lane // HEAD_DIM) * 3 + j).astype(BF16)


def _mix(x2d, o_cmp, o_sel, o_win, gates, u, v, w_sgu, b_sgu, g_nsa_out, g_sgu_out, w_out_b, chunk, r):
    n, d = x2d.shape
    rep = r // chunk
    pick = (jnp.arange(r)[:, None] % chunk == jnp.arange(chunk)[None, :]).astype(F32)
    wsm = jnp.einsum("ia,gab,jb->gij", pick, w_sgu[:, :chunk, :chunk], pick, precision=lax.Precision.HIGHEST)
    bs = jnp.tile(b_sgu[:, :chunk].T, (rep, 1))
    row = lambda w: pl.BlockSpec((r, w), lambda i: (i, 0))
    return pl.pallas_call(
        functools.partial(_mix_kernel, chunk=chunk),
        grid=(n // r,),
        in_specs=[row(d), row(NSA_WIDTH), row(NSA_WIDTH), row(NSA_WIDTH), row(GATE_COLS), row(1024), row(1024),
                  _const_spec(wsm.shape), _const_spec(bs.shape), _const_spec((1, NSA_WIDTH)), _const_spec((1, 1024)),
                  _const_spec((3, GATE_COLS, NSA_WIDTH)), _const_spec(w_out_b.shape)],
        out_specs=row(d),
        out_shape=jax.ShapeDtypeStruct((n, d), F32),
        compiler_params=_cparams(("parallel",)),
        name="mix_out_proj",
    )(x2d, o_cmp, o_sel, o_win, gates, u, v, wsm, bs, g_nsa_out, g_sgu_out, _gate_expanders(), w_out_b)


def _rms_matmul_kernel(x_ref, g_ref, w_ref, o_ref):
    o_ref[...] = _dot(_rms(x_ref[...], g_ref[...]), w_ref[...])


def _rms_matmul(x2d, g, w_b, tm):
    n, d = x2d.shape
    m = w_b.shape[1]
    return pl.pallas_call(
        _rms_matmul_kernel,
        grid=(n // tm,),
        in_specs=[pl.BlockSpec((tm, d), lambda i: (i, 0)), _const_spec((1, d)), _const_spec(w_b.shape)],
        out_specs=pl.BlockSpec((tm, m), lambda i: (i, 0)),
        out_shape=jax.ShapeDtypeStruct((n, m), F32),
        compiler_params=_cparams(("parallel",)),
        name="rms_matmul",
    )(x2d, g, w_b)


def _mem_heads(q, k_ref, v_ref):
    outs = []
    for h in range(MEM_HEADS):
        sl = slice(h * MEM_HEAD_DIM, (h + 1) * MEM_HEAD_DIM)
        s = _dot_nt(q[:, sl], k_ref[0, :, h, :]) * MEM_SCALE
        p = _masked_softmax(s, jnp.ones(s.shape, dtype=jnp.bool_))
        outs.append(_dot(p, v_ref[0, :, h, :]))
    return jnp.concatenate(outs, axis=1)


def _memattn_kernel(q_ref, k_ref, v_ref, o_ref):
    o_ref[0] = _mem_heads(q_ref[0], k_ref, v_ref)


def _mem_block_kernel(x_ref, g_ref, wq_ref, k_ref, v_ref, wo_ref, gm_ref, wr_ref, o_ref, gtop_ref):
    x = x_ref[0]
    hq = _dot(_rms(x, g_ref[...]), wq_ref[...])
    x2 = x + _dot(_mem_heads(hq, k_ref, v_ref), wo_ref[...])
    o_ref[0] = x2
    gtop_ref[0] = _top_group(x2, gm_ref, wr_ref)


def _mem_block(x3, g_mem, w_q, mk, mv, w_o, g_moe, wr):
    bsz, t, d = x3.shape
    m = mk.shape[1]
    tq = min(t, ROW_TILE)
    kv_spec = pl.BlockSpec((1, m, MEM_HEADS, MEM_HEAD_DIM), lambda b, i: (b, 0, 0, 0))
    return pl.pallas_call(
        _mem_block_kernel,
        grid=(bsz, t // tq),
        in_specs=[pl.BlockSpec((1, tq, d), lambda b, i: (b, i, 0)), _const_spec((1, d)), _const_spec(w_q.shape),
                  kv_spec, kv_spec, _const_spec(w_o.shape), _const_spec((1, d)), _const_spec(wr.shape)],
        out_specs=[pl.BlockSpec((1, tq, d), lambda b, i: (b, i, 0)), pl.BlockSpec((1, tq, 1), lambda b, i: (b, i, 0))],
        out_shape=[jax.ShapeDtypeStruct((bsz, t, d), F32), jax.ShapeDtypeStruct((bsz, t, 1), jnp.int32)],
        compiler_params=_cparams(("parallel", "parallel")),
        name="mem_block",
    )(x3, g_mem, w_q, mk, mv, w_o, g_moe, wr)


def _mem_attention(hq, mk, mv):
    bsz, t, w = hq.shape
    m = mk.shape[1]
    tq = min(t, ROW_TILE)
    return pl.pallas_call(
        _memattn_kernel,
        grid=(bsz, t // tq),
        in_specs=[pl.BlockSpec((1, tq, w), lambda b, i: (b, i, 0)),
                  pl.BlockSpec((1, m, MEM_HEADS, MEM_HEAD_DIM), lambda b, i: (b, 0, 0, 0)),
                  pl.BlockSpec((1, m, MEM_HEADS, MEM_HEAD_DIM), lambda b, i: (b, 0, 0, 0))],
        out_specs=pl.BlockSpec((1, tq, w), lambda b, i: (b, i, 0)),
        out_shape=jax.ShapeDtypeStruct((bsz, t, w), F32),
        compiler_params=_cparams(("parallel", "parallel")),
        name="mem_attn",
    )(hq, mk, mv)


def _router_logits(h, wr_ref):
    h_hi = h.astype(BF16)
    h_lo = (h - h_hi.astype(F32)).astype(BF16)
    w_hi, w_lo = wr_ref[0], wr_ref[1]
    return _dot(h_hi, w_hi) + _dot(h_hi, w_lo) + _dot(h_lo, w_hi)


def _top_group(x, g_ref, wr_ref):
    z = _router_logits(_rms(x, g_ref[...]), wr_ref)
    lane = lax.broadcasted_iota(jnp.int32, z.shape, 1)
    zg = jnp.where(lane < N_GROUPS, z, -jnp.inf)
    m = jnp.max(zg, axis=1, keepdims=True)
    first = jnp.min(jnp.where(zg == m, lane.astype(F32), float(LANES)), axis=1, keepdims=True)
    return first.astype(jnp.int32)


def _matmul_res_route_kernel(a_ref, w_ref, r_ref, g_ref, wr_ref, o_ref, gtop_ref):
    x = r_ref[...] + _dot(a_ref[...], w_ref[...])
    o_ref[...] = x
    gtop_ref[...] = _top_group(x, g_ref, wr_ref)


def _matmul_res_route(a2d, w_b, res, g_moe, wr, tm):
    n, k = a2d.shape
    m = w_b.shape[1]
    return pl.pallas_call(
        _matmul_res_route_kernel,
        grid=(n // tm,),
        in_specs=[pl.BlockSpec((tm, k), lambda i: (i, 0)), _const_spec(w_b.shape), pl.BlockSpec((tm, m), lambda i: (i, 0)),
                  _const_spec((1, m)), _const_spec(wr.shape)],
        out_specs=[pl.BlockSpec((tm, m), lambda i: (i, 0)), pl.BlockSpec((tm, 1), lambda i: (i, 0))],
        out_shape=[jax.ShapeDtypeStruct((n, m), F32), jax.ShapeDtypeStruct((n, 1), jnp.int32)],
        compiler_params=_cparams(("parallel",)),
        name="matmul_residual_route",
    )(a2d, w_b, res, g_moe, wr)


def _moe_kernel(src_ref, tg_ref, tv_ref, nv_ref, widx_ref, x_hbm, gm_ref, gf_ref, wr_ref, wg_ref, wu_ref, wd_ref, y_hbm,
                buf, hb_scr, w4_scr, gsem, ssem, *, tm):
    t = pl.program_id(0)
    e = pl.program_id(1)
    nt = pl.num_programs(0)
    slot = t % 2
    other = 1 - slot
    valid = tv_ref[t] == 1
    prev_valid = (t >= 1) & (tv_ref[jnp.maximum(t - 1, 0)] == 1)
    next_valid = (t + 1 < nt) & (tv_ref[jnp.minimum(t + 1, nt - 1)] == 1)

    def gather_start(tt, sl):
        def body(r, c):
            idx = jnp.maximum(src_ref[tt * tm + r], 0)
            pltpu.make_async_copy(x_hbm.at[pl.ds(idx, 1), :], buf.at[sl, pl.ds(r, 1), :], gsem.at[sl]).start()
            return c
        lax.fori_loop(0, tm, body, 0, unroll=8)

    def gather_wait(sl):
        pltpu.make_async_copy(x_hbm.at[pl.ds(0, tm), :], buf.at[sl], gsem.at[sl]).wait()

    def scatter_copy(tt, sl, r):
        return pltpu.make_async_copy(buf.at[sl, pl.ds(r, 1), :], y_hbm.at[pl.ds(src_ref[tt * tm + r], 1), :], ssem.at[0])

    def scatter_start(tt, sl):
        def body(r, c):
            scatter_copy(tt, sl, r).start()
            return c

        @pl.when(nv_ref[tt] == tm)
        def _():
            lax.fori_loop(0, tm, body, 0, unroll=8)

        @pl.when(nv_ref[tt] != tm)
        def _():
            lax.fori_loop(0, nv_ref[tt], body, 0)

    def scatter_wait(tt, sl):
        def body(r, c):
            scatter_copy(tt, sl, r).wait()
            return c

        @pl.when(nv_ref[tt] == tm)
        def _():
            pltpu.make_async_copy(buf.at[sl], y_hbm.at[pl.ds(0, tm), :], ssem.at[0]).wait()

        @pl.when(nv_ref[tt] != tm)
        def _():
            lax.fori_loop(0, nv_ref[tt], body, 0)

    @pl.when(e == 0)
    def _():
        @pl.when(t == 0)
        def _():
            gather_start(0, 0)

        @pl.when(prev_valid)
        def _():
            scatter_start(t - 1, other)

        @pl.when(valid)
        def _():
            gather_wait(slot)
            h = _rms(buf[slot], gm_ref[...])
            hb_scr[...] = h.astype(BF16)
            z = _router_logits(h, wr_ref)
            lane = lax.broadcasted_iota(jnp.int32, z.shape, 1)
            lanef = lane.astype(F32)
            grp = lane < N_GROUPS
            zg = jnp.where(grp, z, -jnp.inf)
            pg_top = 1.0 / jnp.sum(jnp.where(grp, jnp.exp(zg - jnp.max(zg, axis=1, keepdims=True)), 0.0), axis=1, keepdims=True)
            lo = N_GROUPS + tg_ref[t] * EXPERTS_PER_GROUP
            ing = (lane >= lo) & (lane < lo + EXPERTS_PER_GROUP)
            pf = _masked_softmax(z, ing)
            big = float(2 * LANES)
            m1 = jnp.max(jnp.where(ing, pf, -1.0), axis=1, keepdims=True)
            i1 = jnp.min(jnp.where(ing & (pf == m1), lanef, big), axis=1, keepdims=True)
            rest = ing & (lanef != i1)
            m2 = jnp.max(jnp.where(rest, pf, -1.0), axis=1, keepdims=True)
            i2 = jnp.min(jnp.where(rest & (pf == m2), lanef, big), axis=1, keepdims=True)
            tot = m1 + m2
            w4_scr[...] = jnp.where(lanef == i1, m1 / tot * pg_top, jnp.where(lanef == i2, m2 / tot * pg_top, 0.0))

    @pl.when(e == EXPERTS_PER_GROUP // 2)
    def _():
        @pl.when(prev_valid)
        def _():
            scatter_wait(t - 1, other)

        @pl.when(next_valid)
        def _():
            gather_start(t + 1, other)

    @pl.when(valid)
    def _():
        hb = hb_scr[...]
        lane = lax.broadcasted_iota(jnp.int32, w4_scr.shape, 1)
        col = N_GROUPS + tg_ref[t] * EXPERTS_PER_GROUP + e
        we = jnp.sum(jnp.where(lane == col, w4_scr[...], 0.0), axis=1, keepdims=True)
        act = jax.nn.silu(_dot(hb, wg_ref[0])) * _dot(hb, wu_ref[0])
        buf[slot] += _dot(act * we, wd_ref[0])

    @pl.when(valid & (e == EXPERTS_PER_GROUP - 1))
    def _():
        buf[slot] = _rms(buf[slot], gf_ref[...])


def _moe_final(x2d, g_top, g_moe, g_final, wr, w_gate, w_up, w_down, tm):
    n, d = x2d.shape
    n_tiles = n // tm + N_GROUPS
    onehot = (g_top[:, None] == jnp.arange(N_GROUPS)[None, :]).astype(jnp.int32)
    counts = jnp.sum(onehot, axis=0)
    rank = jnp.sum((jnp.cumsum(onehot, axis=0) - onehot) * onehot, axis=1)
    padded = (counts + tm - 1) // tm * tm
    ends = jnp.cumsum(padded)
    base = ends - padded
    pos = base[g_top] + rank
    src = jnp.full((n_tiles * tm,), -1, jnp.int32).at[pos].set(jnp.arange(n, dtype=jnp.int32))
    tile_start = jnp.arange(n_tiles, dtype=jnp.int32) * tm
    tile_valid = (tile_start < ends[-1]).astype(jnp.int32)
    tile_group = jnp.minimum(jnp.sum((tile_start[:, None] >= ends[None, :]).astype(jnp.int32), axis=1), N_GROUPS - 1)
    tile_rows = jnp.clip(ends[tile_group] - padded[tile_group] + counts[tile_group] - tile_start, 0, tm)
    tile_rows = (tile_rows * tile_valid).astype(jnp.int32)
    n_valid = ends[-1] // tm
    last_group = tile_group[jnp.maximum(n_valid - 1, 0)]
    eidx = tile_group[:, None] * EXPERTS_PER_GROUP + jnp.arange(EXPERTS_PER_GROUP, dtype=jnp.int32)[None, :]
    widx = jnp.where(tile_valid[:, None] == 1, eidx, last_group * EXPERTS_PER_GROUP + EXPERTS_PER_GROUP - 1)
    widx = widx.reshape(-1).astype(jnp.int32)

    wmap = lambda t, e, src, tg, tv, nv, wi: (wi[t * EXPERTS_PER_GROUP + e], 0, 0)
    cmap = lambda t, e, src, tg, tv, nv, wi: (0, 0)
    grid_spec = pltpu.PrefetchScalarGridSpec(
        num_scalar_prefetch=5,
        grid=(n_tiles, EXPERTS_PER_GROUP),
        in_specs=[pl.BlockSpec(memory_space=pl.ANY),
                  pl.BlockSpec((1, d), cmap), pl.BlockSpec((1, d), cmap),
                  pl.BlockSpec(wr.shape, lambda t, e, src, tg, tv, nv, wi: (0, 0, 0)),
                  pl.BlockSpec((1, d, EXPERT_FF), wmap), pl.BlockSpec((1, d, EXPERT_FF), wmap),
                  pl.BlockSpec((1, EXPERT_FF, d), wmap)],
        out_specs=pl.BlockSpec(memory_space=pl.ANY),
        scratch_shapes=[pltpu.VMEM((2, tm, d), F32), pltpu.VMEM((tm, d), BF16), pltpu.VMEM((tm, LANES), F32),
                        pltpu.SemaphoreType.DMA((2,)), pltpu.SemaphoreType.DMA((1,))],
    )
    return pl.pallas_call(
        functools.partial(_moe_kernel, tm=tm),
        grid_spec=grid_spec,
        out_shape=jax.ShapeDtypeStruct((n, d), F32),
        compiler_params=_cparams(("arbitrary", "arbitrary")),
        name="moe_final_norm",
    )(src, tile_group, tile_valid, tile_rows, widx, x2d, g_moe, g_final, wr, w_gate, w_up, w_down)


def _finish(x2d, o_cmp, o_sel, o_win, gates, u, v, mk, mv, bsz, lw, chunk, moe_tm):
    n, d = x2d.shape
    t = n // bsz
    x1 = _mix(x2d, o_cmp, o_sel, o_win, gates, u, v, lw["w_sgu"], lw["b_sgu"], lw["g_nsa_out"], lw["g_sgu_out"],
              lw["w_out"], chunk, min(n, PROJ_ROWS))
    if t >= LANES:
        x2, g_top = _mem_block(x1.reshape(bsz, t, d), lw["g_mem_norm"], lw["w_mem_q"], mk, mv, lw["w_mem_o"],
                               lw["g_moe_norm"], lw["w_router"])
        x2, g_top = x2.reshape(n, d), g_top.reshape(n, 1)
    else:
        hq = _rms_matmul(x1, lw["g_mem_norm"], lw["w_mem_q"], min(n, ROW_TILE))
        o_m = _mem_attention(hq.reshape(bsz, t, -1), mk, mv)
        x2, g_top = _matmul_res_route(o_m.reshape(n, -1), lw["w_mem_o"], x1, lw["g_moe_norm"], lw["w_router"], min(n, ROW_TILE))
    return _moe_final(x2, g_top[:, 0], lw["g_moe_norm"], lw["g_final"], lw["w_router"], lw["w_exp_gate"], lw["w_exp_up"],
                      lw["w_exp_down"], moe_tm)


def kernel(x_prompt, x_sample, cache_cmp_k, cache_cmp_v, cache_sel_k, cache_sel_v, cache_win_k, cache_win_v, cache_mem_k, cache_mem_v, page_table, mem_prompt, w_in, g_attn_norm, pe_cmp_k, w_cmp_k1, w_cmp_k2, pe_cmp_v, w_cmp_v1, w_cmp_v2, g_sgu_v, w_sgu, b_sgu, g_nsa_out, g_sgu_out, w_out, g_mem_norm, g_mem_src, w_mem_q, w_mem_k, w_mem_v, w_mem_o, g_moe_norm, w_router_group, w_router_expert, w_exp_gate, w_exp_up, w_exp_down, g_final):
    depth = w_in.shape[0]
    assert depth == 1, "single-layer trunk"
    bp, tp, d = x_prompt.shape
    bs, ts, _ = x_sample.shape
    n_pages = page_table.shape[1]
    page = cache_cmp_k.shape[2]
    past = n_pages * page
    assert ts < STRIDE and tp % ATTN_TILE == 0 and page % STRIDE == 0
    row = lambda a: a[0].reshape(1, -1)

    w_in_p = w_in[0].astype(BF16)
    wr = jnp.concatenate([w_router_group[0], w_router_expert[0],
                          jnp.zeros((d, LANES - N_GROUPS - N_EXPERTS), F32)], axis=1)
    wr_hi = wr.astype(BF16)
    wr = jnp.stack([wr_hi, (wr - wr_hi.astype(F32)).astype(BF16)])
    lw = {
        "w_sgu": w_sgu[0], "b_sgu": b_sgu[0], "g_nsa_out": row(g_nsa_out), "g_sgu_out": row(g_sgu_out),
        "w_out": w_out[0].astype(BF16), "g_mem_norm": row(g_mem_norm), "w_mem_q": w_mem_q[0].astype(BF16),
        "w_mem_o": w_mem_o[0].astype(BF16), "g_moe_norm": row(g_moe_norm), "g_final": g_final.reshape(1, -1),
        "w_router": wr, "w_exp_gate": w_exp_gate[0], "w_exp_up": w_exp_up[0], "w_exp_down": w_exp_down[0],
    }
    g_attn = row(g_attn_norm)
    gsv = row(g_sgu_v)
    kv5 = lambda a, b, t: a.reshape(1, b, t, KV_HEADS, HEAD_DIM)

    np_ = bp * tp
    tabs_p = _rope_tables(jnp.arange(tp, dtype=jnp.int32))
    (q, qrot, kc, vc, ks, vs, kw, vw, gates, u, v) = _project(x_prompt.reshape(np_, d), g_attn, w_in_p, tabs_p, gsv, PROJ_ROWS,
                                                              kv_seq=tp)
    pt_p = jnp.arange(np_ // page, dtype=jnp.int32).reshape(bp, tp // page)
    kcmp = _compress(kc, pt_p, page, pe_cmp_k[0], w_cmp_k1[0], w_cmp_k2[0])
    vcmp = _compress(vc, pt_p, page, pe_cmp_v[0], w_cmp_v1[0], w_cmp_v2[0])
    o_cmp, sel_t = _cmp_select(q.reshape(bp, tp, -1), kcmp, vcmp, 0, tp)
    qrot3 = qrot.reshape(bp, tp, -1)
    o_sel = _prompt_attention(qrot3, ks, vs, sel_t)
    o_win = _prompt_attention(qrot3, kw, vw)
    n_mem = mem_prompt.shape[1]
    mem_w = MEM_HEADS * MEM_HEAD_DIM
    mem2d = mem_prompt.reshape(bp * n_mem, d)
    mem4 = lambda a: a.reshape(bp, n_mem, MEM_HEADS, MEM_HEAD_DIM)
    mk_p = mem4(_rms_matmul(mem2d, row(g_mem_src), w_mem_k[0].astype(BF16), min(bp * n_mem, ROW_TILE)))
    mv_p = mem4(_rms_matmul(mem2d, row(g_mem_src), w_mem_v[0].astype(BF16), min(bp * n_mem, ROW_TILE)))
    y_p = _finish(x_prompt.reshape(np_, d), o_cmp.reshape(np_, -1), o_sel.reshape(np_, -1), o_win.reshape(np_, -1),
                  gates, u, v, mk_p, mv_p, bp, lw, CHUNK, min(np_, MOE_ROWS))
    wbp = min(WINDOW, tp)
    kv5t = lambda a: a.reshape(bp, KV_HEADS, HEAD_DIM, a.shape[2]).transpose(0, 3, 1, 2)[None]
    outs_p = (kv5t(kc), kv5t(vc), kv5t(ks), kv5t(vs), kv5t(kw[:, :, -wbp:]), kv5t(vw[:, :, -wbp:]), mk_p[None], mv_p[None])

    ns_ = bs * ts
    pos_s = past + jnp.arange(ts, dtype=jnp.int32)
    tabs_s = tuple(jnp.tile(a, (bs, 1)) for a in _rope_tables(pos_s))
    (q, qrot, kc, vc, ks, vs, kw, vw, gates, u, v) = _project(x_sample.reshape(ns_, d), g_attn, w_in_p, tabs_s, gsv, ns_)
    pool = lambda c: c[0].transpose(0, 2, 3, 1).reshape(c.shape[1], KV_WIDTH, page)
    kcmp = _compress(pool(cache_cmp_k), page_table, page, pe_cmp_k[0], w_cmp_k1[0], w_cmp_k2[0])
    vcmp = _compress(pool(cache_cmp_v), page_table, page, pe_cmp_v[0], w_cmp_v1[0], w_cmp_v2[0])
    o_cmp, sel_t = _cmp_select(q.reshape(bs, ts, -1).astype(F32), kcmp, vcmp, past, past + ts)
    wb = cache_win_k.shape[2]
    wk = cache_win_k[0].transpose(0, 2, 3, 1).reshape(bs, KV_WIDTH, wb)
    wv = cache_win_v[0].transpose(0, 2, 3, 1).reshape(bs, KV_WIDTH, wb)
    ks3, vs3, kw3, vw3 = (a.reshape(bs, ts, KV_WIDTH) for a in (ks, vs, kw, vw))
    o_sel, o_win = _sample_attention(qrot.reshape(bs, ts, -1), pool(cache_sel_k), pool(cache_sel_v), page_table,
                                     ks3, vs3, sel_t, wk, wv, kw3, vw3)
    mem_ks, mem_vs = cache_mem_k[0], cache_mem_v[0]
    y_s = _finish(x_sample.reshape(ns_, d), o_cmp.reshape(ns_, -1), o_sel.reshape(ns_, -1), o_win.reshape(ns_, -1),
                  gates, u, v, mem_ks, mem_vs, bs, lw, ts, min(ns_, MOE_ROWS_SHORT))
    def slide(cache_t, new):
        win = jnp.concatenate([cache_t, new.transpose(0, 2, 1)], axis=2)[:, :, -wb:]
        return win.reshape(bs, KV_HEADS, HEAD_DIM, wb).transpose(0, 3, 1, 2)[None]

    win_k_s = slide(wk, kw3)
    win_v_s = slide(wv, vw3)
    outs_s = (kv5(kc, bs, ts), kv5(vc, bs, ts), kv5(ks, bs, ts), kv5(vs, bs, ts), win_k_s, win_v_s,
              v.reshape(1, bs, ts, -1))

    return (y_p.reshape(bp, tp, d), y_s.reshape(bs, ts, d)) + outs_p + outs_s
```
